```python
import math
import jax, jax.numpy as jnp
from jax import lax
import numpy as np

D_MODEL = 1024
BATCH = 8
SEQ = 2048
DEPTH = 2
DEC_BATCH = 128
DEC_SEQ = 1
PAST_LEN = 16384
PAGE_SIZE = 128

GDN_HEADS = 4
GDN_DK = 128
GDN_DV = 128
CONV_W = 4
GDN_CHUNK = 64
ML_HEADS = 4
ML_DK = 128
ML_DV = 128
ML_CHUNK = 64
S5_GROUP = 16
S5_GROUPS = D_MODEL // S5_GROUP
S5_STATE = 64
N_EXPERTS = 16
N_EXPERT_GROUPS = 4
EXPERTS_PER_GROUP = N_EXPERTS // N_EXPERT_GROUPS
TOP_K = 2
D_FF_EXPERT = 512
DN_ALPHA = (2 * DEPTH) ** 0.25
DN_BETA = (8 * DEPTH) ** -0.25
N_AB_LAYERS = (DEPTH + 1) // 2
N_S5_LAYERS = DEPTH // 2
LN_EPS = 1e-5
RMS_EPS = 1e-6
NEG_BIG = -1e30
A_QK = GDN_HEADS * GDN_DK
A_VW = GDN_HEADS * GDN_DV
A_CONV = 2 * A_QK + A_VW
B_QK = ML_HEADS * ML_DK
B_VW = ML_HEADS * ML_DV
MIX_WIDTH = A_VW + B_VW
IN_SPLITS = (A_CONV, GDN_HEADS, GDN_HEADS, A_VW, B_QK, B_QK, B_VW, ML_HEADS, ML_HEADS, B_VW)
D_IN = A_CONV + 2 * GDN_HEADS + A_VW + 2 * B_QK + 2 * B_VW + 2 * ML_HEADS

kernel_name = 'hybrid_gdn_mlstm_s5_moe_decode_step'

F32 = jnp.float32


def _layer_norm(x, g, b):
    xf = x.astype(F32)
    mu = jnp.mean(xf, -1, keepdims=True)
    var = jnp.mean(jnp.square(xf - mu), -1, keepdims=True)
    return ((xf - mu) * lax.rsqrt(var + LN_EPS) * g.astype(F32) + b.astype(F32)).astype(x.dtype)


def _rms_heads(x, w):
    return x * lax.rsqrt(jnp.mean(jnp.square(x), -1, keepdims=True) + RMS_EPS) * w.astype(F32)


def _l2norm(x):
    return x * lax.rsqrt(jnp.sum(jnp.square(x), -1, keepdims=True) + RMS_EPS)


def _pad_time(t, pad, value=0.0):
    return jnp.pad(t, [(0, 0), (0, pad)] + [(0, 0)] * (t.ndim - 2), constant_values=value)


def _to_chunks(t, n, c):
    bt = t.shape[0]
    t = t.reshape((bt, n, c) + t.shape[2:])
    return t.transpose((1, 0, 3, 2) + tuple(range(4, t.ndim)))


def _from_chunks(t, length):
    n, bt, h, c, d = t.shape
    return t.transpose((1, 0, 3, 2, 4)).reshape(bt, n * c, h, d)[:, :length]


def gated_delta_chunked(q, k, v, g, beta, S0):
    length = q.shape[1]
    c = min(GDN_CHUNK, length)
    pad = (-length) % c
    if pad:
        q, k, v, g, beta = (_pad_time(t, pad) for t in (q, k, v, g, beta))
    n = (length + pad) // c
    qc, kc, vc = (_to_chunks(t, n, c) for t in (q, k, v))
    gc, bc = (_to_chunks(t, n, c) for t in (g, beta))
    G = jnp.cumsum(gc, axis=-1)
    incl = jnp.tril(jnp.ones((c, c), bool))
    strict = jnp.tril(jnp.ones((c, c), bool), -1)
    decay = jnp.where(incl, jnp.exp(jnp.where(incl, G[..., :, None] - G[..., None, :], 0.0)), 0.0)
    kb = kc * bc[..., None]
    vb = vc * bc[..., None]
    a_low = jnp.where(strict, jnp.einsum('nbhik,nbhjk->nbhij', kb, kc) * decay, 0.0)
    m = a_low + jnp.eye(c, dtype=F32)
    u = lax.linalg.triangular_solve(m, vb, left_side=True, lower=True, unit_diagonal=True)
    w = lax.linalg.triangular_solve(m, kb * jnp.exp(G)[..., None], left_side=True, lower=True, unit_diagonal=True)
    attn = jnp.einsum('nbhik,nbhjk->nbhij', qc, kc) * decay
    q_dec = qc * jnp.exp(G)[..., None]
    k_tail = kc * jnp.exp(G[..., -1:] - G)[..., None]
    chunk_dec = jnp.exp(G[..., -1])

    def step(S, xs):
        qd, wi, ui, ai, kt, cd = xs
        v_new = ui - jnp.einsum('bhck,bhkv->bhcv', wi, S)
        o = jnp.einsum('bhck,bhkv->bhcv', qd, S) + jnp.einsum('bhij,bhjv->bhiv', ai, v_new)
        S = S * cd[..., None, None] + jnp.einsum('bhck,bhcv->bhkv', kt, v_new)
        return S, o

    S_fin, o = lax.scan(step, S0, (q_dec, w, u, attn, k_tail, chunk_dec))
    return _from_chunks(o, length), S_fin


def mlstm_chunked(q, k, v, i_pre, logf, C0, n0, m0):
    length = q.shape[1]
    c = min(ML_CHUNK, length)
    pad = (-length) % c
    if pad:
        q, k, v, logf = (_pad_time(t, pad) for t in (q, k, v, logf))
        i_pre = _pad_time(i_pre, pad, NEG_BIG)
    n = (length + pad) // c
    qc, kc, vc = (_to_chunks(t, n, c) for t in (q, k, v))
    ic, fc = (_to_chunks(t, n, c) for t in (i_pre, logf))
    b = jnp.cumsum(fc, axis=-1)
    incl = jnp.tril(jnp.ones((c, c), bool))
    dlog = jnp.where(incl, b[..., :, None] - b[..., None, :] + ic[..., None, :], NEG_BIG)
    dmax = jnp.max(dlog, axis=-1)
    qk = jnp.einsum('nbhik,nbhjk->nbhij', qc, kc)
    b_last = b[..., -1]
    tail = b_last[..., None] - b + ic

    def step(carry, xs):
        C, nv, m = carry
        qi, ki, vi, bi, dl, dm, qki, bl, tl = xs
        inter = bi + m[..., None]
        mt = jnp.maximum(inter, dm)
        wts = jnp.exp(dl - mt[..., None]) * qki
        sc = jnp.exp(inter - mt)
        num = sc[..., None] * jnp.einsum('bhck,bhkv->bhcv', qi, C) + jnp.einsum('bhij,bhjv->bhiv', wts, vi)
        den = sc * jnp.einsum('bhck,bhk->bhc', qi, nv) + jnp.sum(wts, axis=-1)
        h = num / jnp.maximum(jnp.abs(den), jnp.exp(-mt))[..., None]
        m_new = mt[..., -1]
        sd = jnp.exp(bl + m - m_new)
        wk = jnp.exp(tl - m_new[..., None])
        C = sd[..., None, None] * C + jnp.einsum('bhc,bhck,bhcv->bhkv', wk, ki, vi)
        nv = sd[..., None] * nv + jnp.einsum('bhc,bhck->bhk', wk, ki)
        return (C, nv, m_new), h

    (C_fin, n_fin, m_fin), h = lax.scan(step, (C0, n0, m0), (qc, kc, vc, b, dlog, dmax, qk, b_last, tail))
    return _from_chunks(h, length), C_fin, n_fin, m_fin


def ab_mixer(h, conv_buf, S0, C0, n0, m0, w_in, conv_w, A_log, dt_bias, gdn_norm_w,
             b_i, b_f, ml_norm_w, w_out):
    bt, length, _ = h.shape
    proj = h @ w_in
    cuts = np.cumsum(IN_SPLITS)[:-1].tolist()
    qkv_a, a_a, beta_a, z_a, q_b, k_b, v_b, i_b, f_b, o_b = jnp.split(proj, cuts, axis=-1)
    xpad = jnp.concatenate([conv_buf.astype(proj.dtype), qkv_a], axis=1)
    conv = xpad[:, :length] * conv_w[0]
    for w in range(1, CONV_W):
        conv = conv + xpad[:, w:w + length] * conv_w[w]
    new_buf = xpad[:, length:]
    conv = jax.nn.silu(conv.astype(F32))
    qa, ka, va = jnp.split(conv, [A_QK, 2 * A_QK], axis=-1)
    qa = _l2norm(qa.reshape(bt, length, GDN_HEADS, GDN_DK)) * GDN_DK ** -0.5
    ka = _l2norm(ka.reshape(bt, length, GDN_HEADS, GDN_DK))
    va = va.reshape(bt, length, GDN_HEADS, GDN_DV)
    beta = jax.nn.sigmoid(beta_a.astype(F32))
    g = -jnp.exp(A_log.astype(F32)) * jax.nn.softplus(a_a.astype(F32) + dt_bias.astype(F32))
    o_a, S_new = gated_delta_chunked(qa, ka, va, g, beta, S0.astype(F32))
    o_a = _rms_heads(o_a, gdn_norm_w) * jax.nn.silu(z_a.astype(F32).reshape(bt, length, GDN_HEADS, GDN_DV))
    qb = q_b.astype(F32).reshape(bt, length, ML_HEADS, ML_DK)
    kb = k_b.astype(F32).reshape(bt, length, ML_HEADS, ML_DK) * ML_DK ** -0.5
    vb = v_b.astype(F32).reshape(bt, length, ML_HEADS, ML_DV)
    i_pre = i_b.astype(F32) + b_i.astype(F32)
    logf = jax.nn.log_sigmoid(f_b.astype(F32) + b_f.astype(F32))
    h_b, C_new, n_new, m_new = mlstm_chunked(qb, kb, vb, i_pre, logf, C0.astype(F32), n0.astype(F32), m0.astype(F32))
    o_bh = _rms_heads(h_b, ml_norm_w) * jax.nn.sigmoid(o_b.astype(F32).reshape(bt, length, ML_HEADS, ML_DV))
    merged = jnp.concatenate([o_a.reshape(bt, length, A_VW), o_bh.reshape(bt, length, B_VW)], axis=-1).astype(h.dtype)
    return merged @ w_out, new_buf, S_new, C_new, n_new, m_new


def s5_mixer(x, h0_re, h0_im, A_re, A_im, log_dt, B_re, B_im, C_re, C_im, D_skip, w_glu_a, w_glu_b):
    bt, length, _ = x.shape
    u = x.astype(F32).reshape(bt, length, S5_GROUPS, S5_GROUP)
    a_r = A_re.astype(F32)
    a_i = A_im.astype(F32)
    dt = jnp.exp(log_dt.astype(F32))[:, None]
    mag = jnp.exp(dt * a_r)
    ab_re = mag * jnp.cos(dt * a_i)
    ab_im = mag * jnp.sin(dt * a_i)
    den = jnp.square(a_r) + jnp.square(a_i)
    nr = ab_re - 1.0
    z_re = (nr * a_r + ab_im * a_i) / den
    z_im = (ab_im * a_r - nr * a_i) / den
    br = B_re.astype(F32)
    bi = B_im.astype(F32)
    bb_re = z_re[..., None] * br - z_im[..., None] * bi
    bb_im = z_re[..., None] * bi + z_im[..., None] * br
    bu_re = jnp.einsum('blgc,gpc->blgp', u, bb_re)
    bu_im = jnp.einsum('blgc,gpc->blgp', u, bb_im)
    h0r = h0_re.astype(F32)
    h0i = h0_im.astype(F32)
    bu_re = bu_re.at[:, 0].add(ab_re * h0r - ab_im * h0i)
    bu_im = bu_im.at[:, 0].add(ab_re * h0i + ab_im * h0r)
    a_re_b = jnp.broadcast_to(ab_re, bu_re.shape)
    a_im_b = jnp.broadcast_to(ab_im, bu_im.shape)

    def combine(e1, e2):
        a1r, a1i, b1r, b1i = e1
        a2r, a2i, b2r, b2i = e2
        return (a2r * a1r - a2i * a1i, a2r * a1i + a2i * a1r,
                a2r * b1r - a2i * b1i + b2r, a2r * b1i + a2i * b1r + b2i)

    _, _, hr, hi = lax.associative_scan(combine, (a_re_b, a_im_b, bu_re, bu_im), axis=1)
    y = (jnp.einsum('blgp,gcp->blgc', hr, C_re.astype(F32)) - jnp.einsum('blgp,gcp->blgc', hi, C_im.astype(F32))
         + D_skip.astype(F32).reshape(S5_GROUPS, S5_GROUP) * u)
    gy = jax.nn.gelu(y.reshape(bt, length, D_MODEL)).astype(x.dtype)
    out = (gy @ w_glu_a) * jax.nn.sigmoid(gy @ w_glu_b)
    return out, hr[:, -1], hi[:, -1]


def moe(x, router_w, router_b, w_gate, w_up, w_down):
    logits = jnp.einsum('bld,de->ble', x.astype(F32), router_w.astype(F32))
    probs = jax.nn.softmax(logits, axis=-1)
    sel = probs + router_b.astype(F32)
    grp = sel.reshape(sel.shape[:-1] + (N_EXPERT_GROUPS, EXPERTS_PER_GROUP))
    grp_score = jnp.sum(lax.top_k(grp, TOP_K)[0], axis=-1)
    best = jnp.argmax(grp_score, axis=-1)
    in_grp = (jnp.arange(N_EXPERTS) // EXPERTS_PER_GROUP) == best[..., None]
    _, idx = lax.top_k(jnp.where(in_grp, sel, -jnp.inf), TOP_K)
    wts = jnp.take_along_axis(probs, idx, axis=-1)
    wts = wts / jnp.sum(wts, axis=-1, keepdims=True)
    comb = jnp.sum(jax.nn.one_hot(idx, N_EXPERTS, dtype=F32) * wts[..., None], axis=-2)
    hid = jax.nn.silu(jnp.einsum('bld,edf->blef', x, w_gate)) * jnp.einsum('bld,edf->blef', x, w_up)
    hid = hid * comb[..., None].astype(hid.dtype)
    return jnp.einsum('blef,efd->bld', hid, w_down)


def _trunk(x, st_conv, st_S, st_C, st_n, st_m, st_re, st_im, p):
    h = x
    l_conv, l_S, l_C, l_n, l_m, l_re, l_im = [], [], [], [], [], [], []
    for layer in range(DEPTH):
        j = layer // 2
        if layer % 2 == 0:
            mix, c_new, s_new, cm_new, n_new, m_new = ab_mixer(
                h, st_conv[:, j], st_S[:, j], st_C[:, j], st_n[:, j], st_m[:, j],
                p['w_in'][j], p['gdn_conv_w'][j], p['gdn_A_log'][j], p['gdn_dt_bias'][j], p['gdn_norm_w'][j],
                p['ml_b_i'][j], p['ml_b_f'][j], p['ml_norm_w'][j], p['w_out'][j])
            l_conv.append(c_new)
            l_S.append(s_new)
            l_C.append(cm_new)
            l_n.append(n_new)
            l_m.append(m_new)
        else:
            mix, re_new, im_new = s5_mixer(
                h, st_re[:, j], st_im[:, j], p['s5_A_re'][j], p['s5_A_im'][j], p['s5_log_dt'][j],
                p['s5_B_re'][j], p['s5_B_im'][j], p['s5_C_re'][j], p['s5_C_im'][j], p['s5_D'][j],
                p['s5_w_glu_a'][j], p['s5_w_glu_b'][j])
            l_re.append(re_new)
            l_im.append(im_new)
        h = _layer_norm(DN_ALPHA * h + mix, p['ln_mix_g'][layer], p['ln_mix_b'][layer])
        ffn = moe(h, p['router_w'], p['router_b'], p['moe_w_gate'][layer], p['moe_w_up'][layer], p['moe_w_down'][layer])
        h = _layer_norm(DN_ALPHA * h + ffn, p['ln_ffn_g'][layer], p['ln_ffn_b'][layer])
    stk = lambda lst, ref: jnp.stack(lst, axis=1).astype(ref.dtype)
    return (h, stk(l_conv, st_conv), stk(l_S, st_S), stk(l_C, st_C), stk(l_n, st_n), stk(l_m, st_m),
            stk(l_re, st_re), stk(l_im, st_im))


def setup_inputs(seed: int = 0) -> dict:
    key = jax.random.key(seed)
    ks = iter(jax.random.split(key, 64))
    nrm = lambda shape, scale=1.0: jax.random.normal(next(ks), shape, F32) * scale
    uni = lambda shape, lo, hi: jax.random.uniform(next(ks), shape, F32, lo, hi)
    na, ns = N_AB_LAYERS, N_S5_LAYERS
    gdn_dt = jnp.exp(uni((na, GDN_HEADS), math.log(1e-3), math.log(1e-1)))
    s5_a_im = math.pi * jnp.broadcast_to(jnp.arange(S5_STATE, dtype=F32), (ns, S5_GROUPS, S5_STATE))
    return {
        'x_prompt': nrm((BATCH, SEQ, D_MODEL)),
        'x_sample': nrm((DEC_BATCH, DEC_SEQ, D_MODEL)),
        'state_gdn_conv': nrm((DEC_BATCH, na, CONV_W - 1, A_CONV)),
        'state_gdn_S': nrm((DEC_BATCH, na, GDN_HEADS, GDN_DK, GDN_DV), 0.1),
        'state_mlstm_C': nrm((DEC_BATCH, na, ML_HEADS, ML_DK, ML_DV), 0.1),
        'state_mlstm_n': nrm((DEC_BATCH, na, ML_HEADS, ML_DK), 0.1),
        'state_mlstm_m': 1.0 + nrm((DEC_BATCH, na, ML_HEADS), 0.5),
        'state_s5_re': nrm((DEC_BATCH, ns, S5_GROUPS, S5_STATE), 0.5),
        'state_s5_im': nrm((DEC_BATCH, ns, S5_GROUPS, S5_STATE), 0.5),
        'w_in': nrm((na, D_MODEL, D_IN), D_MODEL ** -0.5),
        'gdn_conv_w': nrm((na, CONV_W, A_CONV), CONV_W ** -0.5),
        'gdn_A_log': jnp.log(uni((na, GDN_HEADS), 1.0, 16.0)),
        'gdn_dt_bias': jnp.log(jnp.expm1(gdn_dt)),
        'gdn_norm_w': 1.0 + nrm((na, GDN_DV), 0.02),
        'ml_b_i': nrm((na, ML_HEADS), 0.1),
        'ml_b_f': uni((na, ML_HEADS), 3.0, 6.0),
        'ml_norm_w': 1.0 + nrm((na, ML_DV), 0.02),
        'w_out': nrm((na, MIX_WIDTH, D_MODEL), MIX_WIDTH ** -0.5 * DN_BETA),
        's5_A_re': -0.5 * jnp.exp(nrm((ns, S5_GROUPS, S5_STATE), 0.02)),
        's5_A_im': s5_a_im + nrm((ns, S5_GROUPS, S5_STATE), 0.01),
        's5_log_dt': uni((ns, S5_GROUPS), math.log(1e-3), math.log(1e-1)),
        's5_B_re': nrm((ns, S5_GROUPS, S5_STATE, S5_GROUP), (2 * S5_GROUP) ** -0.5),
        's5_B_im': nrm((ns, S5_GROUPS, S5_STATE, S5_GROUP), (2 * S5_GROUP) ** -0.5),
        's5_C_re': nrm((ns, S5_GROUPS, S5_GROUP, S5_STATE), (2 * S5_STATE) ** -0.5),
        's5_C_im': nrm((ns, S5_GROUPS, S5_GROUP, S5_STATE), (2 * S5_STATE) ** -0.5),
        's5_D': nrm((ns, D_MODEL)),
        's5_w_glu_a': nrm((ns, D_MODEL, D_MODEL), D_MODEL ** -0.5 * DN_BETA),
        's5_w_glu_b': nrm((ns, D_MODEL, D_MODEL), D_MODEL ** -0.5),
        'router_w': nrm((D_MODEL, N_EXPERTS), D_MODEL ** -0.5),
        'router_b': nrm((N_EXPERTS,), 0.01),
        'moe_w_gate': nrm((DEPTH, N_EXPERTS, D_MODEL, D_FF_EXPERT), D_MODEL ** -0.5),
        'moe_w_up': nrm((DEPTH, N_EXPERTS, D_MODEL, D_FF_EXPERT), D_MODEL ** -0.5),
        'moe_w_down': nrm((DEPTH, N_EXPERTS, D_FF_EXPERT, D_MODEL), D_FF_EXPERT ** -0.5 * DN_BETA),
        'ln_mix_g': 1.0 + nrm((DEPTH, D_MODEL), 0.02),
        'ln_mix_b': nrm((DEPTH, D_MODEL), 0.02),
        'ln_ffn_g': 1.0 + nrm((DEPTH, D_MODEL), 0.02),
        'ln_ffn_b': nrm((DEPTH, D_MODEL), 0.02),
    }


def reference(x_prompt, x_sample, state_gdn_conv, state_gdn_S, state_mlstm_C, state_mlstm_n, state_mlstm_m,
              state_s5_re, state_s5_im, w_in, gdn_conv_w, gdn_A_log, gdn_dt_bias, gdn_norm_w, ml_b_i, ml_b_f,
              ml_norm_w, w_out, s5_A_re, s5_A_im, s5_log_dt, s5_B_re, s5_B_im, s5_C_re, s5_C_im, s5_D,
              s5_w_glu_a, s5_w_glu_b, router_w, router_b, moe_w_gate, moe_w_up, moe_w_down,
              ln_mix_g, ln_mix_b, ln_ffn_g, ln_ffn_b):
    params = dict(w_in=w_in, gdn_conv_w=gdn_conv_w, gdn_A_log=gdn_A_log, gdn_dt_bias=gdn_dt_bias,
                  gdn_norm_w=gdn_norm_w, ml_b_i=ml_b_i, ml_b_f=ml_b_f, ml_norm_w=ml_norm_w, w_out=w_out,
                  s5_A_re=s5_A_re, s5_A_im=s5_A_im, s5_log_dt=s5_log_dt, s5_B_re=s5_B_re, s5_B_im=s5_B_im,
                  s5_C_re=s5_C_re, s5_C_im=s5_C_im, s5_D=s5_D, s5_w_glu_a=s5_w_glu_a, s5_w_glu_b=s5_w_glu_b,
                  router_w=router_w, router_b=router_b, moe_w_gate=moe_w_gate, moe_w_up=moe_w_up,
                  moe_w_down=moe_w_down, ln_mix_g=ln_mix_g, ln_mix_b=ln_mix_b, ln_ffn_g=ln_ffn_g, ln_ffn_b=ln_ffn_b)
    bp = x_prompt.shape[0]
    dt = x_prompt.dtype
    z_conv = jnp.zeros((bp, N_AB_LAYERS, CONV_W - 1, A_CONV), dt)
    z_S = jnp.zeros((bp, N_AB_LAYERS, GDN_HEADS, GDN_DK, GDN_DV), dt)
    z_C = jnp.zeros((bp, N_AB_LAYERS, ML_HEADS, ML_DK, ML_DV), dt)
    z_n = jnp.zeros((bp, N_AB_LAYERS, ML_HEADS, ML_DK), dt)
    z_m = jnp.zeros((bp, N_AB_LAYERS, ML_HEADS), dt)
    z_re = jnp.zeros((bp, N_S5_LAYERS, S5_GROUPS, S5_STATE), dt)
    z_im = jnp.zeros((bp, N_S5_LAYERS, S5_GROUPS, S5_STATE), dt)
    y_prompt, p_conv, p_S, p_C, p_n, p_m, p_re, p_im = _trunk(
        x_prompt, z_conv, z_S, z_C, z_n, z_m, z_re, z_im, params)
    y_sample, s_conv, s_S, s_C, s_n, s_m, s_re, s_im = _trunk(
        x_sample, state_gdn_conv, state_gdn_S, state_mlstm_C, state_mlstm_n, state_mlstm_m,
        state_s5_re, state_s5_im, params)
    return (y_prompt, y_sample, p_conv, p_S, p_C, p_n, p_m, p_re, p_im,
            s_conv, s_S, s_C, s_n, s_m, s_re, s_im)
```

```python
import functools

import jax
import jax.numpy as jnp
from jax import lax
from jax.experimental import pallas as pl
from jax.experimental.pallas import tpu as pltpu

F32 = jnp.float32
BF16 = jnp.bfloat16
HIGHEST = lax.Precision.HIGHEST

D_MODEL = 1024
DEPTH = 2
N_HEADS = 4
HEAD_DIM = 128
CONV_W = 4
CHUNK = 64
A_CONV = 3 * N_HEADS * HEAD_DIM
S5_GROUP = 16
S5_GROUPS = D_MODEL // S5_GROUP
S5_STATE = 64
N_EXPERTS = 16
EXPERTS_PER_GROUP = 4
N_EXPERT_GROUPS = N_EXPERTS // EXPERTS_PER_GROUP
D_FF = 512
DN_ALPHA = (2 * DEPTH) ** 0.25
LN_EPS = 1e-5
RMS_EPS = 1e-6
NEG_BIG = -1e30

QA, KA, VA, ZA, QB, KB, VB, OB, GT = 0, 512, 1024, 1536, 2048, 2560, 3072, 3584, 4096
D_IN_PAD = 4224
G_DEC, G_BETA, G_IN, G_FG = 0, 4, 8, 12

LANES = 128
SUBLANES = 8
VMEM_LIMIT = 56 * 1024 * 1024


def _bf(x):
    return x.astype(BF16)


def _nn(a, b):
    return jnp.dot(_bf(a), _bf(b), preferred_element_type=F32)


def _nt(a, b):
    return lax.dot_general(_bf(a), _bf(b), (((1,), (1,)), ((), ())), preferred_element_type=F32)


def _tn(a, b):
    return lax.dot_general(_bf(a), _bf(b), (((0,), (0,)), ((), ())), preferred_element_type=F32)


def _sigmoid(x):
    return 1.0 / (1.0 + jnp.exp(-x))


def _softplus(x):
    return jnp.maximum(x, 0.0) + jnp.log(1.0 + jnp.exp(-jnp.abs(x)))


def _silu(x):
    return x * _sigmoid(x)


def _layer_norm(y, g, b):
    mu = jnp.mean(y, axis=-1, keepdims=True)
    yc = y - mu
    var = jnp.mean(yc * yc, axis=-1, keepdims=True)
    return yc * lax.rsqrt(var + LN_EPS) * g + b


def _rms(x, w):
    return x * lax.rsqrt(jnp.mean(x * x, axis=-1, keepdims=True) + RMS_EPS) * w


def _gate_transform(raw, gp):
    lane = lax.broadcasted_iota(jnp.int32, raw.shape, 1)
    dec = -jnp.exp(gp[0:1, :]) * _softplus(raw + gp[1:2, :])
    beta = _sigmoid(raw)
    ipre = raw + gp[2:3, :]
    logf = -_softplus(-(raw + gp[3:4, :]))
    return jnp.where(lane < G_BETA, dec,
                     jnp.where(lane < G_IN, beta,
                               jnp.where(lane < G_FG, ipre,
                                         jnp.where(lane < G_FG + N_HEADS, logf, 0.0))))


def _unit_lower_inverse_minus_eye(a):
    c = a.shape[0]
    q = -a
    r = q
    steps = max(1, (c - 1).bit_length()) - 1
    for _ in range(steps):
        q = _nn(q, q)
        r = r + q + _nn(r, q)
    return r


def _ab_prompt_kernel(x_ref, win_ref, wout_ref, convw_ref, gp_ref, ln_ref,
                      h_ref, hist_ref, s_out, c_out, n_out, m_out,
                      proj, qkv, gates, gcum, merged, s_s, c_s, n_s, m_s, *, tb, n_t):
    t = pl.program_id(1)

    @pl.when(t == 0)
    def _():
        proj[0:SUBLANES, :] = jnp.zeros((SUBLANES, D_IN_PAD), F32)
        s_s[...] = jnp.zeros_like(s_s)
        c_s[...] = jnp.zeros_like(c_s)
        n_s[...] = jnp.zeros_like(n_s)
        m_s[...] = jnp.zeros_like(m_s)

    x = x_ref[...]
    proj[SUBLANES:SUBLANES + tb, :] = jnp.dot(_bf(x), win_ref[...], preferred_element_type=F32)

    for blk in range(A_CONV // LANES):
        cs = slice(blk * LANES, (blk + 1) * LANES)
        acc = proj[SUBLANES:SUBLANES + tb, cs] * convw_ref[CONV_W - 1:CONV_W, cs]
        for j in range(1, CONV_W):
            acc = acc + proj[SUBLANES - j:SUBLANES - j + tb, cs] * convw_ref[CONV_W - 1 - j:CONV_W - j, cs]
        y = _silu(acc)
        if blk < 2 * N_HEADS:
            y = y * lax.rsqrt(jnp.sum(y * y, axis=-1, keepdims=True) + RMS_EPS)
            if blk < N_HEADS:
                y = y * HEAD_DIM ** -0.5
        qkv[:, cs] = y

    gt = _gate_transform(proj[SUBLANES:SUBLANES + tb, GT:GT + LANES], gp_ref[...])
    gates[...] = gt
    ri = lax.broadcasted_iota(jnp.int32, (tb, tb), 0)
    ci = lax.broadcasted_iota(jnp.int32, (tb, tb), 1)
    same_chunk = lax.shift_right_logical(ri, 6) == lax.shift_right_logical(ci, 6)
    ltri = jnp.where(same_chunk, jnp.where(ri >= ci, 1.0, 0.0), 0.0)
    gcum[...] = jnp.dot(ltri, gt, preferred_element_type=F32, precision=HIGHEST)

    ii = lax.broadcasted_iota(jnp.int32, (CHUNK, CHUNK), 0)
    jj = lax.broadcasted_iota(jnp.int32, (CHUNK, CHUNK), 1)
    incl = ii >= jj
    strict = ii > jj
    gdn_w = gp_ref[4:5, :]
    ml_w = gp_ref[5:6, :]

    def chunk_body(c, carry):
        r0 = pl.multiple_of(c * CHUNK, CHUNK)
        rows = pl.ds(r0, CHUNK)
        prow = pl.ds(pl.multiple_of(r0 + SUBLANES, SUBLANES), CHUNK)
        gt_c = gates[rows, :]
        cs_c = gcum[rows, :]
        gt_t = gt_c.T
        cs_t = cs_c.T
        for h in range(N_HEADS):
            hs = slice(h * HEAD_DIM, (h + 1) * HEAD_DIM)
            q = qkv[rows, QA + h * HEAD_DIM:QA + (h + 1) * HEAD_DIM]
            k = qkv[rows, KA + h * HEAD_DIM:KA + (h + 1) * HEAD_DIM]
            v = qkv[rows, VA + h * HEAD_DIM:VA + (h + 1) * HEAD_DIM]
            g_col = cs_c[:, G_DEC + h:G_DEC + h + 1]
            g_row = cs_t[G_DEC + h:G_DEC + h + 1, :]
            g_last = cs_c[CHUNK - 1:CHUNK, G_DEC + h:G_DEC + h + 1]
            beta = gt_c[:, G_BETA + h:G_BETA + h + 1]
            decay = jnp.where(incl, jnp.exp(jnp.where(incl, g_col - g_row, 0.0)), 0.0)
            kb = k * beta
            vb = v * beta
            a_low = jnp.where(strict, _nt(kb, k) * decay, 0.0)
            r = _unit_lower_inverse_minus_eye(a_low)
            e_g = jnp.exp(g_col)
            rhs = jnp.concatenate([vb, kb * e_g], axis=1)
            uw = rhs + _nn(r, rhs)
            u = uw[:, :HEAD_DIM]
            w = uw[:, HEAD_DIM:]
            attn = _nt(q, k) * decay
            s_old = s_s[h]
            v_new = u - _nn(w, s_old)
            o_a = _nn(q * e_g, s_old) + _nn(attn, v_new)
            s_s[h] = s_old * jnp.exp(g_last) + _tn(k * jnp.exp(g_last - g_col), v_new)
            z = proj[prow, ZA + h * HEAD_DIM:ZA + (h + 1) * HEAD_DIM]
            merged[rows, hs] = _rms(o_a, gdn_w) * _silu(z)

            qb = proj[prow, QB + h * HEAD_DIM:QB + (h + 1) * HEAD_DIM]
            kbm = proj[prow, KB + h * HEAD_DIM:KB + (h + 1) * HEAD_DIM] * HEAD_DIM ** -0.5
            vbm = proj[prow, VB + h * HEAD_DIM:VB + (h + 1) * HEAD_DIM]
            b_col = cs_c[:, G_FG + h:G_FG + h + 1]
            b_row = cs_t[G_FG + h:G_FG + h + 1, :]
            b_last = cs_c[CHUNK - 1:CHUNK, G_FG + h:G_FG + h + 1]
            i_col = gt_c[:, G_IN + h:G_IN + h + 1]
            i_row = gt_t[G_IN + h:G_IN + h + 1, :]
            dlog = jnp.where(incl, b_col - b_row + i_row, NEG_BIG)
            dmax = jnp.max(dlog, axis=-1, keepdims=True)
            qk = _nt(qb, kbm)
            m_old = m_s[h:h + 1, 0:1]
            c_old = c_s[h]
            n_old = n_s[h:h + 1, :]
            inter = b_col + m_old
            mt = jnp.maximum(inter, dmax)
            wts = jnp.exp(dlog - mt) * qk
            sc = jnp.exp(inter - mt)
            num = sc * _nn(qb, c_old) + _nn(wts, vbm)
            den = sc * jnp.sum(qb * n_old, axis=-1, keepdims=True) + jnp.sum(wts, axis=-1, keepdims=True)
            h_b = num / jnp.maximum(jnp.abs(den), jnp.exp(-mt))
            m_new = mt[CHUNK - 1:CHUNK, :]
            sd = jnp.exp(b_last + m_old - m_new)
            wk = jnp.exp(b_last - b_col + i_col - m_new)
            kw = kbm * wk
            c_s[h] = sd * c_old + _tn(kw, vbm)
            n_s[h:h + 1, :] = sd * n_old + jnp.sum(kw, axis=0, keepdims=True)
            m_s[h:h + 1, :] = jnp.broadcast_to(m_new, (1, LANES))
            o_gate = proj[prow, OB + h * HEAD_DIM:OB + (h + 1) * HEAD_DIM]
            merged[rows, N_HEADS * HEAD_DIM + h * HEAD_DIM:N_HEADS * HEAD_DIM + (h + 1) * HEAD_DIM] = (
                _rms(h_b, ml_w) * _sigmoid(o_gate))
        return carry

    lax.fori_loop(0, tb // CHUNK, chunk_body, 0)

    mix = jnp.dot(_bf(merged[...]), wout_ref[...], preferred_element_type=F32)
    h_ref[...] = _layer_norm(DN_ALPHA * x + mix, ln_ref[0:1, :], ln_ref[1:2, :])

    proj[0:SUBLANES, 0:A_CONV] = proj[tb:tb + SUBLANES, 0:A_CONV]

    @pl.when(t == n_t - 1)
    def _():
        hist_ref[...] = proj[tb:tb + SUBLANES, 0:A_CONV]
        s_out[...] = s_s[...]
        c_out[...] = c_s[...]
        n_out[...] = n_s[...]
        m_out[...] = m_s[...]


def _ab_prompt(x, win, wout, convw, gp, ln):
    bsz, length, _ = x.shape
    tb = min(256, length)
    assert length % tb == 0 and tb % CHUNK == 0 and length >= SUBLANES
    n_t = length // tb
    const = lambda shape: pl.BlockSpec(shape, lambda b, t: (0,) * len(shape))
    per_b = lambda shape: pl.BlockSpec((None,) + shape, lambda b, t: (b,) + (0,) * len(shape))
    return pl.pallas_call(
        functools.partial(_ab_prompt_kernel, tb=tb, n_t=n_t),
        grid=(bsz, n_t),
        in_specs=[
            pl.BlockSpec((None, tb, D_MODEL), lambda b, t: (b, t, 0)),
            const((D_MODEL, D_IN_PAD)), const((2 * N_HEADS * HEAD_DIM, D_MODEL)),
            const((CONV_W, A_CONV)), const((SUBLANES, LANES)), const((2, D_MODEL)),
        ],
        out_specs=[
            pl.BlockSpec((None, tb, D_MODEL), lambda b, t: (b, t, 0)),
            per_b((SUBLANES, A_CONV)), per_b((N_HEADS, HEAD_DIM, HEAD_DIM)), per_b((N_HEADS, HEAD_DIM, HEAD_DIM)),
            per_b((SUBLANES, LANES)), per_b((SUBLANES, LANES)),
        ],
        out_shape=[
            jax.ShapeDtypeStruct((bsz, length, D_MODEL), F32),
            jax.ShapeDtypeStruct((bsz, SUBLANES, A_CONV), F32),
            jax.ShapeDtypeStruct((bsz, N_HEADS, HEAD_DIM, HEAD_DIM), F32),
            jax.ShapeDtypeStruct((bsz, N_HEADS, HEAD_DIM, HEAD_DIM), F32),
            jax.ShapeDtypeStruct((bsz, SUBLANES, LANES), F32),
            jax.ShapeDtypeStruct((bsz, SUBLANES, LANES), F32),
        ],
        scratch_shapes=[
            pltpu.VMEM((tb + SUBLANES, D_IN_PAD), F32),
            pltpu.VMEM((tb, A_CONV), F32),
            pltpu.VMEM((tb, LANES), F32),
            pltpu.VMEM((tb, LANES), F32),
            pltpu.VMEM((tb, 2 * N_HEADS * HEAD_DIM), F32),
            pltpu.VMEM((N_HEADS, HEAD_DIM, HEAD_DIM), F32),
            pltpu.VMEM((N_HEADS, HEAD_DIM, HEAD_DIM), F32),
            pltpu.VMEM((SUBLANES, LANES), F32),
            pltpu.VMEM((SUBLANES, LANES), F32),
        ],
        compiler_params=pltpu.CompilerParams(
            dimension_semantics=("arbitrary", "arbitrary"), vmem_limit_bytes=VMEM_LIMIT),
        name="ab_prompt",
    )(x, win, wout, convw, gp, ln)


def _ab_decode_kernel(x_ref, win_ref, wout_ref, convw_ref, gp_ref, ln_ref, cbuf_ref, s_in, c_in, n_in, m_in,
                      y_ref, cbuf_out, s_out, c_out, n_out, m_out, proj, merged, *, bb, n_steps):
    i = pl.program_id(0)

    @pl.when(i == 0)
    def _():
        proj[...] = jnp.dot(_bf(x_ref[...]), win_ref[...], preferred_element_type=F32)

    rows = pl.ds(pl.multiple_of(i * bb, bb), bb)
    raw = proj[rows, 0:A_CONV]
    cbuf = cbuf_ref[...]
    conv = raw * convw_ref[CONV_W - 1:CONV_W, :]
    for j in range(CONV_W - 1):
        conv = conv + cbuf[:, j * A_CONV:(j + 1) * A_CONV] * convw_ref[j:j + 1, :]
    cbuf_out[:, 0:(CONV_W - 2) * A_CONV] = cbuf[:, A_CONV:(CONV_W - 1) * A_CONV]
    cbuf_out[:, (CONV_W - 2) * A_CONV:(CONV_W - 1) * A_CONV] = raw
    act = _silu(conv)
    gt = _gate_transform(proj[rows, GT:GT + LANES], gp_ref[...])
    gdn_w = gp_ref[4:5, :]
    ml_w = gp_ref[5:6, :]
    m_all = m_in[...]

    for h in range(N_HEADS):
        def head(off):
            return act[:, off + h * HEAD_DIM:off + (h + 1) * HEAD_DIM]
        q = head(QA)
        q = q * lax.rsqrt(jnp.sum(q * q, axis=-1, keepdims=True) + RMS_EPS) * HEAD_DIM ** -0.5
        k = head(KA)
        k = k * lax.rsqrt(jnp.sum(k * k, axis=-1, keepdims=True) + RMS_EPS)
        v = head(VA)
        q_t = q.T
        k_t = k.T
        qk = jnp.sum(q * k, axis=-1, keepdims=True)
        qb = proj[rows, QB + h * HEAD_DIM:QB + (h + 1) * HEAD_DIM]
        kbm = proj[rows, KB + h * HEAD_DIM:KB + (h + 1) * HEAD_DIM] * HEAD_DIM ** -0.5
        vbm = proj[rows, VB + h * HEAD_DIM:VB + (h + 1) * HEAD_DIM]
        qb_t = qb.T
        kb_t = kbm.T
        qkb = jnp.sum(qb * kbm, axis=-1, keepdims=True)
        n_old = n_in[:, h * HEAD_DIM:(h + 1) * HEAD_DIM]
        qn = jnp.sum(qb * n_old, axis=-1, keepdims=True)
        o_rows = []
        hb_rows = []
        n_rows = []
        m_rows = []
        for b in range(bb):
            s_old = s_in[b, h]
            k_c = k_t[:, b:b + 1]
            q_c = q_t[:, b:b + 1]
            e_g = jnp.exp(gt[b:b + 1, G_DEC + h:G_DEC + h + 1])
            beta = gt[b:b + 1, G_BETA + h:G_BETA + h + 1]
            k_s = jnp.sum(k_c * s_old, axis=0, keepdims=True)
            q_s = jnp.sum(q_c * s_old, axis=0, keepdims=True)
            v_new = v[b:b + 1, :] * beta - (beta * e_g) * k_s
            o_rows.append(e_g * q_s + qk[b:b + 1, :] * v_new)
            s_out[b, h] = s_old * e_g + k_c * v_new
            c_old = c_in[b, h]
            m_old = m_all[b:b + 1, h:h + 1]
            i_pre = gt[b:b + 1, G_IN + h:G_IN + h + 1]
            logf = gt[b:b + 1, G_FG + h:G_FG + h + 1]
            inter = logf + m_old
            mt = jnp.maximum(inter, i_pre)
            w_in = jnp.exp(i_pre - mt)
            sc = jnp.exp(inter - mt)
            wts = w_in * qkb[b:b + 1, :]
            q_cm = jnp.sum(qb_t[:, b:b + 1] * c_old, axis=0, keepdims=True)
            num = sc * q_cm + wts * vbm[b:b + 1, :]
            den = sc * qn[b:b + 1, :] + wts
            hb_rows.append(num / jnp.maximum(jnp.abs(den), jnp.exp(-mt)))
            c_out[b, h] = sc * c_old + (w_in * kb_t[:, b:b + 1]) * vbm[b:b + 1, :]
            n_rows.append(sc * n_old[b:b + 1, :] + w_in * kbm[b:b + 1, :])
            m_rows.append(mt)
        o_a = jnp.concatenate(o_rows, axis=0)
        h_b = jnp.concatenate(hb_rows, axis=0)
        n_out[:, h * HEAD_DIM:(h + 1) * HEAD_DIM] = jnp.concatenate(n_rows, axis=0)
        m_out[:, h:h + 1] = jnp.concatenate(m_rows, axis=0)
        z = proj[rows, ZA + h * HEAD_DIM:ZA + (h + 1) * HEAD_DIM]
        merged[rows, h * HEAD_DIM:(h + 1) * HEAD_DIM] = _rms(o_a, gdn_w) * _silu(z)
        o_gate = proj[rows, OB + h * HEAD_DIM:OB + (h + 1) * HEAD_DIM]
        merged[rows, N_HEADS * HEAD_DIM + h * HEAD_DIM:N_HEADS * HEAD_DIM + (h + 1) * HEAD_DIM] = (
            _rms(h_b, ml_w) * _sigmoid(o_gate))

    @pl.when(i == n_steps - 1)
    def _():
        mix = jnp.dot(_bf(merged[...]), wout_ref[...], preferred_element_type=F32)
        y_ref[...] = _layer_norm(DN_ALPHA * x_ref[...] + mix, ln_ref[0:1, :], ln_ref[1:2, :])


def _ab_decode(x, win, wout, convw, gp, ln, cbuf, s0, c0, n0, m0):
    nb = x.shape[0]
    bb = SUBLANES
    assert nb % bb == 0
    n_steps = nb // bb
    const = lambda shape: pl.BlockSpec(shape, lambda i: (0,) * len(shape))
    blk = lambda shape: pl.BlockSpec((bb,) + shape, lambda i: (i,) + (0,) * len(shape))
    hist = (CONV_W - 1) * A_CONV
    width = N_HEADS * HEAD_DIM
    return pl.pallas_call(
        functools.partial(_ab_decode_kernel, bb=bb, n_steps=n_steps),
        grid=(n_steps,),
        in_specs=[
            const((nb, D_MODEL)), const((D_MODEL, D_IN_PAD)), const((2 * width, D_MODEL)),
            const((CONV_W, A_CONV)), const((SUBLANES, LANES)), const((2, D_MODEL)),
            blk((hist,)), blk((N_HEADS, HEAD_DIM, HEAD_DIM)), blk((N_HEADS, HEAD_DIM, HEAD_DIM)),
            blk((width,)), blk((N_HEADS,)),
        ],
        out_specs=[
            const((nb, D_MODEL)), blk((hist,)), blk((N_HEADS, HEAD_DIM, HEAD_DIM)),
            blk((N_HEADS, HEAD_DIM, HEAD_DIM)), blk((width,)), blk((N_HEADS,)),
        ],
        out_shape=[
            jax.ShapeDtypeStruct((nb, D_MODEL), F32),
            jax.ShapeDtypeStruct((nb, hist), F32),
            jax.ShapeDtypeStruct((nb, N_HEADS, HEAD_DIM, HEAD_DIM), F32),
            jax.ShapeDtypeStruct((nb, N_HEADS, HEAD_DIM, HEAD_DIM), F32),
            jax.ShapeDtypeStruct((nb, width), F32),
            jax.ShapeDtypeStruct((nb, N_HEADS), F32),
        ],
        scratch_shapes=[pltpu.VMEM((nb, D_IN_PAD), F32), pltpu.VMEM((nb, 2 * width), F32)],
        compiler_params=pltpu.CompilerParams(dimension_semantics=("arbitrary",), vmem_limit_bytes=VMEM_LIMIT),
        name="ab_decode",
    )(x, win, wout, convw, gp, ln, cbuf, s0, c0, n0, m0)


def _second_largest_sum(a, b, c, d):
    hi1, lo1 = jnp.maximum(a, b), jnp.minimum(a, b)
    hi2, lo2 = jnp.maximum(c, d), jnp.minimum(c, d)
    return jnp.maximum(hi1, hi2) + jnp.maximum(jnp.minimum(hi1, hi2), jnp.maximum(lo1, lo2))


def _first_argmax(vals):
    best_v = vals[0]
    best_i = jnp.zeros(vals[0].shape, jnp.int32)
    for j in range(1, len(vals)):
        better = vals[j] > best_v
        best_v = jnp.where(better, vals[j], best_v)
        best_i = jnp.where(better, j, best_i)
    return best_i


def _router_kernel(x_ref, rw_ref, rb_ref, comb_ref):
    tm = x_ref.shape[0]
    logits = lax.dot_general(rw_ref[...], x_ref[...], (((1,), (1,)), ((), ())),
                             preferred_element_type=F32, precision=HIGHEST)
    ex = jnp.exp(logits - jnp.max(logits, axis=0, keepdims=True))
    probs = ex / jnp.sum(ex, axis=0, keepdims=True)
    sel = probs + rb_ref[...]
    p = [probs[j:j + 1, :] for j in range(N_EXPERTS)]
    s = [sel[j:j + 1, :] for j in range(N_EXPERTS)]
    scores = [_second_largest_sum(*s[EXPERTS_PER_GROUP * g:EXPERTS_PER_GROUP * (g + 1)])
              for g in range(N_EXPERT_GROUPS)]
    best = _first_argmax(scores)
    masked = [jnp.where(best == j // EXPERTS_PER_GROUP, s[j], -jnp.inf) for j in range(N_EXPERTS)]
    i1 = _first_argmax(masked)
    i2 = _first_argmax([jnp.where(i1 == j, -jnp.inf, masked[j]) for j in range(N_EXPERTS)])
    zero = jnp.zeros_like(p[0])
    p1 = functools.reduce(lambda a, b: a + b, [jnp.where(i1 == j, p[j], zero) for j in range(N_EXPERTS)])
    p2 = functools.reduce(lambda a, b: a + b, [jnp.where(i2 == j, p[j], zero) for j in range(N_EXPERTS)])
    tot = p1 + p2
    rows = [jnp.where(i1 == j, p1 / tot, zero) + jnp.where(i2 == j, p2 / tot, zero) for j in range(N_EXPERTS)]
    comb_t = jnp.concatenate(rows + [jnp.zeros((LANES - N_EXPERTS, tm), F32)], axis=0)
    comb_ref[...] = comb_t.T


def _router(x, rw_t, rb_col):
    t = x.shape[0]
    tm = min(256, t)
    assert t % tm == 0
    return pl.pallas_call(
        _router_kernel,
        grid=(t // tm,),
        in_specs=[pl.BlockSpec((tm, D_MODEL), lambda i: (i, 0)),
                  pl.BlockSpec((N_EXPERTS, D_MODEL), lambda i: (0, 0)),
                  pl.BlockSpec((N_EXPERTS, 1), lambda i: (0, 0))],
        out_specs=pl.BlockSpec((tm, LANES), lambda i: (i, 0)),
        out_shape=jax.ShapeDtypeStruct((t, LANES), F32),
        compiler_params=pltpu.CompilerParams(dimension_semantics=("arbitrary",), vmem_limit_bytes=VMEM_LIMIT),
        name="router",
    )(x, rw_t, rb_col)


def _moe_kernel(x_ref, comb_ref, wg_ref, wu_ref, wd_ref, ln_ref, o_ref, xb, acc):
    e = pl.program_id(1)

    @pl.when(e == 0)
    def _():
        xb[...] = _bf(x_ref[...])
        acc[...] = jnp.zeros_like(acc)

    g = jnp.dot(xb[...], wg_ref[...], preferred_element_type=F32)
    u = jnp.dot(xb[...], wu_ref[...], preferred_element_type=F32)
    comb = comb_ref[...]
    lane = lax.broadcasted_iota(jnp.int32, comb.shape, 1)
    cw = jnp.sum(jnp.where(lane == e, comb, 0.0), axis=-1, keepdims=True)
    hid = _silu(g) * u * cw
    acc[...] += jnp.dot(_bf(hid), wd_ref[...], preferred_element_type=F32)

    @pl.when(e == N_EXPERTS - 1)
    def _():
        o_ref[...] = _layer_norm(DN_ALPHA * x_ref[...] + acc[...], ln_ref[0:1, :], ln_ref[1:2, :])


def _moe(x, comb, wg, wu, wd, ln, layer):
    t = x.shape[0]
    tm = min(1024, t)
    assert t % tm == 0
    return pl.pallas_call(
        _moe_kernel,
        grid=(t // tm, N_EXPERTS),
        in_specs=[pl.BlockSpec((tm, D_MODEL), lambda i, e: (i, 0)),
                  pl.BlockSpec((tm, LANES), lambda i, e: (i, 0)),
                  pl.BlockSpec((None, None, D_MODEL, D_FF), lambda i, e: (layer, e, 0, 0)),
                  pl.BlockSpec((None, None, D_MODEL, D_FF), lambda i, e: (layer, e, 0, 0)),
                  pl.BlockSpec((None, None, D_FF, D_MODEL), lambda i, e: (layer, e, 0, 0)),
                  pl.BlockSpec((2, D_MODEL), lambda i, e: (0, 0))],
        out_specs=pl.BlockSpec((tm, D_MODEL), lambda i, e: (i, 0)),
        out_shape=jax.ShapeDtypeStruct((t, D_MODEL), F32),
        scratch_shapes=[pltpu.VMEM((tm, D_MODEL), BF16), pltpu.VMEM((tm, D_MODEL), F32)],
        compiler_params=pltpu.CompilerParams(
            dimension_semantics=("arbitrary", "arbitrary"), vmem_limit_bytes=VMEM_LIMIT),
        name="moe",
    )(x, comb, wg, wu, wd, ln)


def _s5_prep_kernel(are_ref, aim_ref, ldt_ref, bre_ref, bim_ref, abre_ref, abim_ref, bbre_ref, bbim_ref):
    a_r = are_ref[...]
    a_i = aim_ref[...]
    dt = jnp.exp(ldt_ref[...])
    mag = jnp.exp(dt * a_r)
    ab_re = mag * jnp.cos(dt * a_i)
    ab_im = mag * jnp.sin(dt * a_i)
    den = a_r * a_r + a_i * a_i
    nr = ab_re - 1.0
    z_re = (nr * a_r + ab_im * a_i) / den
    z_im = (ab_im * a_r - nr * a_i) / den
    abre_ref[...] = ab_re
    abim_ref[...] = ab_im
    bbre_ref[...] = z_re * bre_ref[...] - z_im * bim_ref[...]
    bbim_ref[...] = z_re * bim_ref[...] + z_im * bre_ref[...]


def _s5_prep(a_re, a_im, log_dt, b_re, b_im):
    n = S5_GROUPS * S5_STATE
    col = jax.ShapeDtypeStruct((n, 1), F32)
    mat = jax.ShapeDtypeStruct((n, S5_GROUP), F32)
    ldt = jnp.broadcast_to(log_dt[:, None], (S5_GROUPS, S5_STATE)).reshape(n, 1)
    return pl.pallas_call(_s5_prep_kernel, out_shape=[col, col, mat, mat], name="s5_prep")(
        a_re.reshape(n, 1), a_im.reshape(n, 1), ldt, b_re.reshape(n, S5_GROUP), b_im.reshape(n, S5_GROUP))


def _gelu_tanh(x):
    return 0.5 * x * (1.0 + jnp.tanh(0.7978845608028654 * (x + 0.044715 * (x * x * x))))


def _s5_kernel(x_ref, h0re_ref, h0im_ref, wre_ref, wim_ref, cre_ref, cim_ref, abre_ref, abim_ref, d_ref,
               gy_ref, hre_out, him_out, xtb, bu_re, bu_im, hre_s, him_s, *, nb, tt, n_t):
    t = pl.program_id(1)

    @pl.when(t == 0)
    def _():
        hre_s[...] = h0re_ref[...]
        him_s[...] = h0im_ref[...]

    for s in range(tt):
        xtb[s * nb:(s + 1) * nb, :] = x_ref[:, s, :]
    xv = xtb[...]
    xb = _bf(xv)
    bu_re[...] = jnp.dot(xb, wre_ref[...], preferred_element_type=F32)
    bu_im[...] = jnp.dot(xb, wim_ref[...], preferred_element_type=F32)

    a_re = jnp.broadcast_to(abre_ref[...], hre_s.shape)
    a_im = jnp.broadcast_to(abim_ref[...], hre_s.shape)

    def step(s, carry):
        h_re, h_im = carry
        rows = pl.ds(pl.multiple_of(s * nb, nb), nb)
        n_re = a_re * h_re - a_im * h_im + bu_re[rows, :]
        n_im = a_re * h_im + a_im * h_re + bu_im[rows, :]
        bu_re[rows, :] = n_re
        bu_im[rows, :] = n_im
        return n_re, n_im

    h_re, h_im = lax.fori_loop(0, tt, step, (hre_s[...], him_s[...]))
    hre_s[...] = h_re
    him_s[...] = h_im

    y = (jnp.dot(_bf(bu_re[...]), cre_ref[...], preferred_element_type=F32)
         - jnp.dot(_bf(bu_im[...]), cim_ref[...], preferred_element_type=F32)
         + d_ref[...] * xv)
    gy = _gelu_tanh(y)
    for s in range(tt):
        gy_ref[:, s, :] = gy[s * nb:(s + 1) * nb, :].astype(gy_ref.dtype)

    @pl.when(t == n_t - 1)
    def _():
        hre_out[...] = h_re
        him_out[...] = h_im


def _s5_scan(x, h0_re, h0_im, wre, wim, cre, cim, ab_re, ab_im, d_skip):
    nb, length, _ = x.shape
    tt = min(64, length)
    assert length % tt == 0 and nb % SUBLANES == 0
    n_t = length // tt
    n_k = D_MODEL // LANES
    sw = (LANES // S5_GROUP) * S5_STATE
    return pl.pallas_call(
        functools.partial(_s5_kernel, nb=nb, tt=tt, n_t=n_t),
        grid=(n_k, n_t),
        in_specs=[
            pl.BlockSpec((nb, tt, LANES), lambda k, t: (0, t, k)),
            pl.BlockSpec((nb, sw), lambda k, t: (0, k)), pl.BlockSpec((nb, sw), lambda k, t: (0, k)),
            pl.BlockSpec((None, LANES, sw), lambda k, t: (k, 0, 0)),
            pl.BlockSpec((None, LANES, sw), lambda k, t: (k, 0, 0)),
            pl.BlockSpec((None, sw, LANES), lambda k, t: (k, 0, 0)),
            pl.BlockSpec((None, sw, LANES), lambda k, t: (k, 0, 0)),
            pl.BlockSpec((1, sw), lambda k, t: (0, k)), pl.BlockSpec((1, sw), lambda k, t: (0, k)),
            pl.BlockSpec((1, LANES), lambda k, t: (0, k)),
        ],
        out_specs=[
            pl.BlockSpec((nb, tt, LANES), lambda k, t: (0, t, k)),
            pl.BlockSpec((nb, sw), lambda k, t: (0, k)), pl.BlockSpec((nb, sw), lambda k, t: (0, k)),
        ],
        out_shape=[
            jax.ShapeDtypeStruct((nb, length, D_MODEL), BF16),
            jax.ShapeDtypeStruct((nb, S5_GROUPS * S5_STATE), F32),
            jax.ShapeDtypeStruct((nb, S5_GROUPS * S5_STATE), F32),
        ],
        scratch_shapes=[
            pltpu.VMEM((nb * tt, LANES), F32),
            pltpu.VMEM((nb * tt, sw), F32), pltpu.VMEM((nb * tt, sw), F32),
            pltpu.VMEM((nb, sw), F32), pltpu.VMEM((nb, sw), F32),
        ],
        compiler_params=pltpu.CompilerParams(
            dimension_semantics=("arbitrary", "arbitrary"), vmem_limit_bytes=VMEM_LIMIT),
        name="s5_scan",
    )(x, h0_re, h0_im, wre, wim, cre, cim, ab_re, ab_im, d_skip)


def _glu_ln_kernel(gy_ref, h_ref, wa_ref, wb_ref, ln_ref, o_ref):
    gy = gy_ref[...]
    a = jnp.dot(gy, wa_ref[...], preferred_element_type=F32)
    b = jnp.dot(gy, wb_ref[...], preferred_element_type=F32)
    o_ref[...] = _layer_norm(DN_ALPHA * h_ref[...] + a * _sigmoid(b), ln_ref[0:1, :], ln_ref[1:2, :])


def _glu_ln(gy, h, wa, wb, ln):
    t = h.shape[0]
    tm = min(512, t)
    assert t % tm == 0
    return pl.pallas_call(
        _glu_ln_kernel,
        grid=(t // tm,),
        in_specs=[pl.BlockSpec((tm, D_MODEL), lambda i: (i, 0)), pl.BlockSpec((tm, D_MODEL), lambda i: (i, 0)),
                  pl.BlockSpec((D_MODEL, D_MODEL), lambda i: (0, 0)), pl.BlockSpec((D_MODEL, D_MODEL), lambda i: (0, 0)),
                  pl.BlockSpec((2, D_MODEL), lambda i: (0, 0))],
        out_specs=pl.BlockSpec((tm, D_MODEL), lambda i: (i, 0)),
        out_shape=jax.ShapeDtypeStruct((t, D_MODEL), F32),
        compiler_params=pltpu.CompilerParams(dimension_semantics=("arbitrary",), vmem_limit_bytes=VMEM_LIMIT),
        name="glu_ln",
    )(gy, h, wa, wb, ln)


def _block_diag_slices(m, rows_per_group, cols_per_group):
    gps = LANES // S5_GROUP
    m = m.reshape(S5_GROUPS // gps, gps, rows_per_group, cols_per_group)
    eye = jnp.eye(gps, dtype=m.dtype)
    out = m[:, :, :, None, :] * eye[None, :, None, :, None]
    return out.reshape(S5_GROUPS // gps, gps * rows_per_group, gps * cols_per_group)


def _prepare(p):
    w = p['w_in'][0]
    win = jnp.concatenate(
        [w[:, 0:1536], w[:, 1544:2056], w[:, 2056:3592], w[:, 3600:4112], w[:, 1536:1544], w[:, 3592:3600],
         jnp.zeros((D_MODEL, D_IN_PAD - 4112), w.dtype)], axis=1).astype(BF16)
    gp = jnp.zeros((SUBLANES, LANES), F32)
    gp = gp.at[0, G_DEC:G_DEC + N_HEADS].set(p['gdn_A_log'][0])
    gp = gp.at[1, G_DEC:G_DEC + N_HEADS].set(p['gdn_dt_bias'][0])
    gp = gp.at[2, G_IN:G_IN + N_HEADS].set(p['ml_b_i'][0])
    gp = gp.at[3, G_FG:G_FG + N_HEADS].set(p['ml_b_f'][0])
    gp = gp.at[4, :].set(p['gdn_norm_w'][0])
    gp = gp.at[5, :].set(p['ml_norm_w'][0])
    ab_re, ab_im, bb_re, bb_im = _s5_prep(p['s5_A_re'][0], p['s5_A_im'][0], p['s5_log_dt'][0],
                                          p['s5_B_re'][0], p['s5_B_im'][0])
    to_in = lambda bb: _block_diag_slices(
        bb.reshape(S5_GROUPS, S5_STATE, S5_GROUP).transpose(0, 2, 1), S5_GROUP, S5_STATE).astype(BF16)
    to_out = lambda c: _block_diag_slices(c.transpose(0, 2, 1), S5_STATE, S5_GROUP).astype(BF16)
    return dict(
        win=win, wout=p['w_out'][0].astype(BF16), convw=p['gdn_conv_w'][0], gp=gp,
        ln_mix=[jnp.stack([p['ln_mix_g'][l], p['ln_mix_b'][l]]) for l in range(DEPTH)],
        ln_ffn=[jnp.stack([p['ln_ffn_g'][l], p['ln_ffn_b'][l]]) for l in range(DEPTH)],
        rw_t=p['router_w'].T, rb_col=p['router_b'][:, None],
        wg=p['moe_w_gate'].astype(BF16), wu=p['moe_w_up'].astype(BF16), wd=p['moe_w_down'].astype(BF16),
        s5_wre=to_in(bb_re), s5_wim=to_in(bb_im),
        s5_cre=to_out(p['s5_C_re'][0]), s5_cim=to_out(p['s5_C_im'][0]),
        s5_abre=ab_re.reshape(1, -1), s5_abim=ab_im.reshape(1, -1), s5_d=p['s5_D'][0][None, :],
        glu_a=p['s5_w_glu_a'][0].astype(BF16), glu_b=p['s5_w_glu_b'][0].astype(BF16),
    )


def _ffn(h, w, layer):
    comb = _router(h, w['rw_t'], w['rb_col'])
    return _moe(h, comb, w['wg'], w['wu'], w['wd'], w['ln_ffn'][layer], layer)


def _s5_layer(h, nb, length, h0_re, h0_im, w):
    gy, re, im = _s5_scan(h.reshape(nb, length, D_MODEL), h0_re, h0_im, w['s5_wre'], w['s5_wim'],
                          w['s5_cre'], w['s5_cim'], w['s5_abre'], w['s5_abim'], w['s5_d'])
    h = _glu_ln(gy.reshape(nb * length, D_MODEL), h, w['glu_a'], w['glu_b'], w['ln_mix'][1])
    return h, re, im


def kernel(x_prompt, x_sample, state_gdn_conv, state_gdn_S, state_mlstm_C, state_mlstm_n, state_mlstm_m,
           state_s5_re, state_s5_im, w_in, gdn_conv_w, gdn_A_log, gdn_dt_bias, gdn_norm_w, ml_b_i, ml_b_f,
           ml_norm_w, w_out, s5_A_re, s5_A_im, s5_log_dt, s5_B_re, s5_B_im, s5_C_re, s5_C_im, s5_D,
           s5_w_glu_a, s5_w_glu_b, router_w, router_b, moe_w_gate, moe_w_up, moe_w_down,
           ln_mix_g, ln_mix_b, ln_ffn_g, ln_ffn_b):
    w = _prepare(dict(
        w_in=w_in, gdn_conv_w=gdn_conv_w, gdn_A_log=gdn_A_log, gdn_dt_bias=gdn_dt_bias, gdn_norm_w=gdn_norm_w,
        ml_b_i=ml_b_i, ml_b_f=ml_b_f, ml_norm_w=ml_norm_w, w_out=w_out, s5_A_re=s5_A_re, s5_A_im=s5_A_im,
        s5_log_dt=s5_log_dt, s5_B_re=s5_B_re, s5_B_im=s5_B_im, s5_C_re=s5_C_re, s5_C_im=s5_C_im, s5_D=s5_D,
        s5_w_glu_a=s5_w_glu_a, s5_w_glu_b=s5_w_glu_b, router_w=router_w, router_b=router_b,
        moe_w_gate=moe_w_gate, moe_w_up=moe_w_up, moe_w_down=moe_w_down,
        ln_mix_g=ln_mix_g, ln_mix_b=ln_mix_b, ln_ffn_g=ln_ffn_g, ln_ffn_b=ln_ffn_b))
    bp, lp, _ = x_prompt.shape
    bs, ls, _ = x_sample.shape
    assert ls == 1
    n_state = S5_GROUPS * S5_STATE

    h, p_hist, p_s, p_c, p_n, p_m = _ab_prompt(x_prompt, w['win'], w['wout'], w['convw'], w['gp'], w['ln_mix'][0])
    h = _ffn(h.reshape(bp * lp, D_MODEL), w, 0)
    zeros = jnp.zeros((bp, n_state), F32)
    h, p_re, p_im = _s5_layer(h, bp, lp, zeros, zeros, w)
    y_prompt = _ffn(h, w, 1).reshape(bp, lp, D_MODEL)

    hs, s_cbuf, s_s, s_c, s_n, s_m = _ab_decode(
        x_sample[:, 0], w['win'], w['wout'], w['convw'], w['gp'], w['ln_mix'][0],
        state_gdn_conv.reshape(bs, (CONV_W - 1) * A_CONV), state_gdn_S[:, 0], state_mlstm_C[:, 0],
        state_mlstm_n.reshape(bs, N_HEADS * HEAD_DIM), state_mlstm_m[:, 0])
    hs = _ffn(hs, w, 0)
    hs, s_re, s_im = _s5_layer(hs, bs, 1, state_s5_re.reshape(bs, n_state), state_s5_im.reshape(bs, n_state), w)
    y_sample = _ffn(hs, w, 1).reshape(bs, 1, D_MODEL)

    grp = lambda a, n: a.reshape(n, 1, S5_GROUPS, S5_STATE)
    return (
        y_prompt, y_sample,
        p_hist[:, None, SUBLANES - (CONV_W - 1):, :], p_s[:, None], p_c[:, None],
        p_n[:, None, :N_HEADS, :], p_m[:, None, :N_HEADS, 0], grp(p_re, bp), grp(p_im, bp),
        s_cbuf.reshape(bs, 1, CONV_W - 1, A_CONV), s_s[:, None], s_c[:, None],
        s_n.reshape(bs, 1, N_HEADS, HEAD_DIM), s_m[:, None], grp(s_re, bs), grp(s_im, bs),
    )
```

```python
import functools

import jax
import jax.numpy as jnp
from jax import lax
from jax.experimental import pallas as pl
from jax.experimental.pallas import tpu as pltpu

F32 = jnp.float32
BF16 = jnp.bfloat16
HIGHEST = lax.Precision.HIGHEST

D_MODEL = 1024
DEPTH = 2
N_HEADS = 4
HEAD_DIM = 128
CONV_W = 4
CHUNK = 64
A_CONV = 3 * N_HEADS * HEAD_DIM
S5_GROUP = 16
S5_GROUPS = D_MODEL // S5_GROUP
S5_STATE = 64
N_EXPERTS = 16
EXPERTS_PER_GROUP = 4
N_EXPERT_GROUPS = N_EXPERTS // EXPERTS_PER_GROUP
D_FF = 512
DN_ALPHA = (2 * DEPTH) ** 0.25
LN_EPS = 1e-5
RMS_EPS = 1e-6
NEG_BIG = -1e30

QA, KA, VA, ZA, QB, KB, VB, OB, GT = 0, 512, 1024, 1536, 2048, 2560, 3072, 3584, 4096
D_IN_PAD = 4224
G_DEC, G_BETA, G_IN, G_FG = 0, 4, 8, 12

LANES = 128
SUBLANES = 8
VMEM_LIMIT = 56 * 1024 * 1024


def _bf(x):
    return x.astype(BF16)


def _nn(a, b):
    return jnp.dot(_bf(a), _bf(b), preferred_element_type=F32)


def _nt(a, b):
    return lax.dot_general(_bf(a), _bf(b), (((1,), (1,)), ((), ())), preferred_element_type=F32)


def _tn(a, b):
    return lax.dot_general(_bf(a), _bf(b), (((0,), (0,)), ((), ())), preferred_element_type=F32)


def _sigmoid(x):
    return 1.0 / (1.0 + jnp.exp(-x))


def _softplus(x):
    return jnp.maximum(x, 0.0) + jnp.log(1.0 + jnp.exp(-jnp.abs(x)))


def _silu(x):
    return x * _sigmoid(x)


def _layer_norm(y, g, b):
    mu = jnp.mean(y, axis=-1, keepdims=True)
    yc = y - mu
    var = jnp.mean(yc * yc, axis=-1, keepdims=True)
    return yc * lax.rsqrt(var + LN_EPS) * g + b


def _rms(x, w):
    return x * lax.rsqrt(jnp.mean(x * x, axis=-1, keepdims=True) + RMS_EPS) * w


def _gate_transform(raw, gp):
    lane = lax.broadcasted_iota(jnp.int32, raw.shape, 1)
    dec = -jnp.exp(gp[0:1, :]) * _softplus(raw + gp[1:2, :])
    beta = _sigmoid(raw)
    ipre = raw + gp[2:3, :]
    logf = -_softplus(-(raw + gp[3:4, :]))
    return jnp.where(lane < G_BETA, dec,
                     jnp.where(lane < G_IN, beta,
                               jnp.where(lane < G_FG, ipre,
                                         jnp.where(lane < G_FG + N_HEADS, logf, 0.0))))


def _bnn(a, b):
    return lax.dot_general(_bf(a), _bf(b), (((2,), (1,)), ((0,), (0,))), preferred_element_type=F32)


def _bnt(a, b):
    return lax.dot_general(_bf(a), _bf(b), (((2,), (2,)), ((0,), (0,))), preferred_element_type=F32)


def _btn(a, b):
    return lax.dot_general(_bf(a), _bf(b), (((1,), (1,)), ((0,), (0,))), preferred_element_type=F32)


def _unit_lower_inverse_minus_eye(a):
    c = a.shape[-1]
    r = -a
    steps = max(1, (c - 1).bit_length()) - 1
    q = _bnn(r, r)
    for i in range(steps):
        rq = _bnn(r, q)
        qq = _bnn(q, q) if i + 1 < steps else None
        r = r + q + rq
        q = qq
    return r


def _ab_prompt_kernel(x_ref, win_ref, wout_ref, convw_ref, gp_ref, ln_ref,
                      h_ref, hist_ref, s_out, c_out, n_out, m_out,
                      proj, qkv, gates, gcum, merged, s_s, c_s, n_s, m_s,
                      u_s, w_s, attn_s, dlog_s, qk_s, dmax_s, *, tb, n_t):
    t = pl.program_id(1)
    nc = tb // CHUNK

    @pl.when(t == 0)
    def _():
        proj[0:SUBLANES, :] = jnp.zeros((SUBLANES, D_IN_PAD), F32)
        s_s[...] = jnp.zeros_like(s_s)
        c_s[...] = jnp.zeros_like(c_s)
        n_s[...] = jnp.zeros_like(n_s)
        m_s[...] = jnp.zeros_like(m_s)

    x = x_ref[...]
    proj[SUBLANES:SUBLANES + tb, :] = jnp.dot(_bf(x), win_ref[...], preferred_element_type=F32)

    for blk in range(A_CONV // LANES):
        cs = slice(blk * LANES, (blk + 1) * LANES)
        acc = proj[SUBLANES:SUBLANES + tb, cs] * convw_ref[CONV_W - 1:CONV_W, cs]
        for j in range(1, CONV_W):
            acc = acc + proj[SUBLANES - j:SUBLANES - j + tb, cs] * convw_ref[CONV_W - 1 - j:CONV_W - j, cs]
        y = _silu(acc)
        if blk < 2 * N_HEADS:
            y = y * lax.rsqrt(jnp.sum(y * y, axis=-1, keepdims=True) + RMS_EPS)
            if blk < N_HEADS:
                y = y * HEAD_DIM ** -0.5
        qkv[:, cs] = y

    gt = _gate_transform(proj[SUBLANES:SUBLANES + tb, GT:GT + LANES], gp_ref[...])
    gates[...] = gt
    ri = lax.broadcasted_iota(jnp.int32, (tb, tb), 0)
    ci = lax.broadcasted_iota(jnp.int32, (tb, tb), 1)
    same_chunk = lax.shift_right_logical(ri, 6) == lax.shift_right_logical(ci, 6)
    ltri = jnp.where(same_chunk, jnp.where(ri >= ci, 1.0, 0.0), 0.0)
    gcum[...] = jnp.dot(ltri, gt, preferred_element_type=F32, precision=HIGHEST)

    ii = lax.broadcasted_iota(jnp.int32, (CHUNK, CHUNK), 0)
    jj = lax.broadcasted_iota(jnp.int32, (CHUNK, CHUNK), 1)
    incl = ii >= jj
    strict = ii > jj
    gdn_w = gp_ref[4:5, :]
    ml_w = gp_ref[5:6, :]

    pairs = [(c, h) for c in range(nc) for h in range(N_HEADS)]

    def tile_heads(ref, row0, col0):
        return jnp.stack([ref[row0 + c * CHUNK:row0 + (c + 1) * CHUNK, col0 + h * HEAD_DIM:col0 + (h + 1) * HEAD_DIM]
                          for c, h in pairs])

    def tile_cols(ref, lane0):
        return jnp.stack([ref[c * CHUNK:(c + 1) * CHUNK, lane0 + h:lane0 + h + 1] for c, h in pairs])

    def tile_rows(transposed, lane0):
        return jnp.stack([transposed[c][lane0 + h:lane0 + h + 1, :] for c, h in pairs])

    cs_t = [gcum[c * CHUNK:(c + 1) * CHUNK, :].T for c in range(nc)]
    gt_t = [gates[c * CHUNK:(c + 1) * CHUNK, :].T for c in range(nc)]
    q3 = tile_heads(qkv, 0, QA)
    k3 = tile_heads(qkv, 0, KA)
    v3 = tile_heads(qkv, 0, VA)
    g_col3 = tile_cols(gcum, G_DEC)
    beta3 = tile_cols(gates, G_BETA)
    decay3 = jnp.where(incl, jnp.exp(jnp.where(incl, g_col3 - tile_rows(cs_t, G_DEC), 0.0)), 0.0)
    kb3 = k3 * beta3
    a_low3 = jnp.where(strict, _bnt(kb3, k3) * decay3, 0.0)
    attn_s[...] = _bnt(q3, k3) * decay3
    qk_s[...] = _bnt(tile_heads(proj, SUBLANES, QB), tile_heads(proj, SUBLANES, KB) * HEAD_DIM ** -0.5)
    r3 = _unit_lower_inverse_minus_eye(a_low3)
    rhs3 = jnp.concatenate([v3 * beta3, kb3 * jnp.exp(g_col3)], axis=2)
    uw3 = rhs3 + _bnn(r3, rhs3)
    u_s[...] = uw3[:, :, :HEAD_DIM]
    w_s[...] = uw3[:, :, HEAD_DIM:]
    dlog3 = jnp.where(incl, tile_cols(gcum, G_FG) - tile_rows(cs_t, G_FG) + tile_rows(gt_t, G_IN), NEG_BIG)
    dlog_s[...] = dlog3
    dmax_s[...] = jnp.max(dlog3, axis=-1, keepdims=True)

    def chunk_body(c, carry):
        r0 = pl.multiple_of(c * CHUNK, CHUNK)
        rows = pl.ds(r0, CHUNK)
        prow = pl.ds(pl.multiple_of(r0 + SUBLANES, SUBLANES), CHUNK)
        last = pl.ds(r0 + CHUNK - 1, 1)
        pc = pl.ds(pl.multiple_of(c * N_HEADS, N_HEADS), N_HEADS)

        def heads(ref, rws, col0):
            return jnp.stack([ref[rws, col0 + h * HEAD_DIM:col0 + (h + 1) * HEAD_DIM] for h in range(N_HEADS)])

        def cols(ref, rws, lane0):
            return jnp.stack([ref[rws, lane0 + h:lane0 + h + 1] for h in range(N_HEADS)])

        q = heads(qkv, rows, QA)
        k = heads(qkv, rows, KA)
        g_col = cols(gcum, rows, G_DEC)
        g_last = cols(gcum, last, G_DEC)
        qb = heads(proj, prow, QB)
        kbm = heads(proj, prow, KB) * HEAD_DIM ** -0.5
        vbm = heads(proj, prow, VB)
        b_col = cols(gcum, rows, G_FG)
        b_last = cols(gcum, last, G_FG)
        i_col = cols(gates, rows, G_IN)
        s_old = s_s[...]
        c_old = c_s[...]
        n_old = n_s[:, 0:1, :]
        m_old = m_s[:, 0:1, 0:1]
        inter = b_col + m_old
        mt = jnp.maximum(inter, dmax_s[pc])
        wts = jnp.exp(dlog_s[pc] - mt) * qk_s[pc]
        sc = jnp.exp(inter - mt)
        m_new = mt[:, CHUNK - 1:CHUNK, :]
        sd = jnp.exp(b_last + m_old - m_new)
        kw = kbm * jnp.exp(b_last - b_col + i_col - m_new)
        w_state = _bnn(w_s[pc], s_old)
        q_state = _bnn(q * jnp.exp(g_col), s_old)
        q_mem = _bnn(qb, c_old)
        w_val = _bnn(wts, vbm)
        kv = _btn(kw, vbm)
        v_new = u_s[pc] - w_state
        o_a = q_state + _bnn(attn_s[pc], v_new)
        s_s[...] = s_old * jnp.exp(g_last) + _btn(k * jnp.exp(g_last - g_col), v_new)
        num = sc * q_mem + w_val
        den = sc * jnp.sum(qb * n_old, axis=-1, keepdims=True) + jnp.sum(wts, axis=-1, keepdims=True)
        h_b = num / jnp.maximum(jnp.abs(den), jnp.exp(-mt))
        c_s[...] = sd * c_old + kv
        n_s[...] = jnp.broadcast_to(sd * n_old + jnp.sum(kw, axis=1, keepdims=True), n_s.shape)
        m_s[...] = jnp.broadcast_to(m_new, m_s.shape)
        o_n = _rms(o_a, gdn_w)
        h_n = _rms(h_b, ml_w)
        for h in range(N_HEADS):
            hs = slice(h * HEAD_DIM, (h + 1) * HEAD_DIM)
            hs2 = slice((N_HEADS + h) * HEAD_DIM, (N_HEADS + h + 1) * HEAD_DIM)
            merged[rows, hs] = o_n[h] * _silu(proj[prow, ZA + h * HEAD_DIM:ZA + (h + 1) * HEAD_DIM])
            merged[rows, hs2] = h_n[h] * _sigmoid(proj[prow, OB + h * HEAD_DIM:OB + (h + 1) * HEAD_DIM])
        return carry

    lax.fori_loop(0, nc, chunk_body, 0)

    mix = jnp.dot(_bf(merged[...]), wout_ref[...], preferred_element_type=F32)
    h_ref[...] = _layer_norm(DN_ALPHA * x + mix, ln_ref[0:1, :], ln_ref[1:2, :])

    proj[0:SUBLANES, 0:A_CONV] = proj[tb:tb + SUBLANES, 0:A_CONV]

    @pl.when(t == n_t - 1)
    def _():
        hist_ref[...] = proj[tb:tb + SUBLANES, 0:A_CONV]
        s_out[...] = s_s[...]
        c_out[...] = c_s[...]
        n_out[...] = n_s[...]
        m_out[...] = m_s[...]


def _ab_prompt(x, win, wout, convw, gp, ln):
    bsz, length, _ = x.shape
    tb = min(256, length)
    assert length % tb == 0 and tb % CHUNK == 0 and length >= SUBLANES
    n_t = length // tb
    n_pairs = (tb // CHUNK) * N_HEADS
    const = lambda shape: pl.BlockSpec(shape, lambda b, t: (0,) * len(shape))
    per_b =lambda shape: pl.BlockSpec((None,) + shape, lambda b, t: (b,) + (0,) * len(shape))
    return pl.pallas_call(
        functools.partial(_ab_prompt_kernel, tb=tb, n_t=n_t),
        grid=(bsz, n_t),
        in_specs=[
            pl.BlockSpec((None, tb, D_MODEL), lambda b, t: (b, t, 0)),
            const((D_MODEL, D_IN_PAD)), const((2 * N_HEADS * HEAD_DIM, D_MODEL)),
            const((CONV_W, A_CONV)), const((SUBLANES, LANES)), const((2, D_MODEL)),
        ],
        out_specs=[
            pl.BlockSpec((None, tb, D_MODEL), lambda b, t: (b, t, 0)),
            per_b((SUBLANES, A_CONV)), per_b((N_HEADS, HEAD_DIM, HEAD_DIM)), per_b((N_HEADS, HEAD_DIM, HEAD_DIM)),
            per_b((N_HEADS, SUBLANES, LANES)), per_b((N_HEADS, SUBLANES, LANES)),
        ],
        out_shape=[
            jax.ShapeDtypeStruct((bsz, length, D_MODEL), F32),
            jax.ShapeDtypeStruct((bsz, SUBLANES, A_CONV), F32),
            jax.ShapeDtypeStruct((bsz, N_HEADS, HEAD_DIM, HEAD_DIM), F32),
            jax.ShapeDtypeStruct((bsz, N_HEADS, HEAD_DIM, HEAD_DIM), F32),
            jax.ShapeDtypeStruct((bsz, N_HEADS, SUBLANES, LANES), F32),
            jax.ShapeDtypeStruct((bsz, N_HEADS, SUBLANES, LANES), F32),
        ],
        scratch_shapes=[
            pltpu.VMEM((tb + SUBLANES, D_IN_PAD), F32),
            pltpu.VMEM((tb, A_CONV), F32),
            pltpu.VMEM((tb, LANES), F32),
            pltpu.VMEM((tb, LANES), F32),
            pltpu.VMEM((tb, 2 * N_HEADS * HEAD_DIM), F32),
            pltpu.VMEM((N_HEADS, HEAD_DIM, HEAD_DIM), F32),
            pltpu.VMEM((N_HEADS, HEAD_DIM, HEAD_DIM), F32),
            pltpu.VMEM((N_HEADS, SUBLANES, LANES), F32),
            pltpu.VMEM((N_HEADS, SUBLANES, LANES), F32),
            pltpu.VMEM((n_pairs, CHUNK, HEAD_DIM), F32),
            pltpu.VMEM((n_pairs, CHUNK, HEAD_DIM), F32),
            pltpu.VMEM((n_pairs, CHUNK, CHUNK), F32),
            pltpu.VMEM((n_pairs, CHUNK, CHUNK), F32),
            pltpu.VMEM((n_pairs, CHUNK, CHUNK), F32),
            pltpu.VMEM((n_pairs, CHUNK, 1), F32),
        ],
        compiler_params=pltpu.CompilerParams(
            dimension_semantics=("arbitrary", "arbitrary"), vmem_limit_bytes=VMEM_LIMIT),
        name="ab_prompt",
    )(x, win, wout, convw, gp, ln)


def _ab_decode_kernel(x_ref, win_ref, wout_ref, convw_ref, gp_ref, ln_ref, cbuf_ref, s_in, c_in, n_in, m_in,
                      y_ref, cbuf_out, s_out, c_out, n_out, m_out, proj, merged, *, bb, n_steps):
    i = pl.program_id(0)

    @pl.when(i == 0)
    def _():
        proj[...] = jnp.dot(_bf(x_ref[...]), win_ref[...], preferred_element_type=F32)

    rows = pl.ds(pl.multiple_of(i * bb, bb), bb)
    raw = proj[rows, 0:A_CONV]
    cbuf = cbuf_ref[...]
    conv = raw * convw_ref[CONV_W - 1:CONV_W, :]
    for j in range(CONV_W - 1):
        conv = conv + cbuf[:, j * A_CONV:(j + 1) * A_CONV] * convw_ref[j:j + 1, :]
    cbuf_out[:, 0:(CONV_W - 2) * A_CONV] = cbuf[:, A_CONV:(CONV_W - 1) * A_CONV]
    cbuf_out[:, (CONV_W - 2) * A_CONV:(CONV_W - 1) * A_CONV] = raw
    act = _silu(conv)
    gt = _gate_transform(proj[rows, GT:GT + LANES], gp_ref[...])
    gdn_w = gp_ref[4:5, :]
    ml_w = gp_ref[5:6, :]
    m_all = m_in[...]

    for h in range(N_HEADS):
        def head(off):
            return act[:, off + h * HEAD_DIM:off + (h + 1) * HEAD_DIM]
        q = head(QA)
        q = q * lax.rsqrt(jnp.sum(q * q, axis=-1, keepdims=True) + RMS_EPS) * HEAD_DIM ** -0.5
        k = head(KA)
        k = k * lax.rsqrt(jnp.sum(k * k, axis=-1, keepdims=True) + RMS_EPS)
        v = head(VA)
        q_t = q.T
        k_t = k.T
        qk = jnp.sum(q * k, axis=-1, keepdims=True)
        qb = proj[rows, QB + h * HEAD_DIM:QB + (h + 1) * HEAD_DIM]
        kbm = proj[rows, KB + h * HEAD_DIM:KB + (h + 1) * HEAD_DIM] * HEAD_DIM ** -0.5
        vbm = proj[rows, VB + h * HEAD_DIM:VB + (h + 1) * HEAD_DIM]
        qb_t = qb.T
        kb_t = kbm.T
        qkb = jnp.sum(qb * kbm, axis=-1, keepdims=True)
        n_old = n_in[:, h * HEAD_DIM:(h + 1) * HEAD_DIM]
        qn = jnp.sum(qb * n_old, axis=-1, keepdims=True)
        o_rows = []
        hb_rows = []
        n_rows = []
        m_rows = []
        for b in range(bb):
            s_old = s_in[b, h]
            k_c = k_t[:, b:b + 1]
            q_c = q_t[:, b:b + 1]
            e_g = jnp.exp(gt[b:b + 1, G_DEC + h:G_DEC + h + 1])
            beta = gt[b:b + 1, G_BETA + h:G_BETA + h + 1]
            k_s = jnp.sum(k_c * s_old, axis=0, keepdims=True)
            q_s = jnp.sum(q_c * s_old, axis=0, keepdims=True)
            v_new = v[b:b + 1, :] * beta - (beta * e_g) * k_s
            o_rows.append(e_g * q_s + qk[b:b + 1, :] * v_new)
            s_out[b, h] = s_old * e_g + k_c * v_new
            c_old = c_in[b, h]
            m_old = m_all[b:b + 1, h:h + 1]
            i_pre = gt[b:b + 1, G_IN + h:G_IN + h + 1]
            logf = gt[b:b + 1, G_FG + h:G_FG + h + 1]
            inter = logf + m_old
            mt = jnp.maximum(inter, i_pre)
            w_in = jnp.exp(i_pre - mt)
            sc = jnp.exp(inter - mt)
            wts = w_in * qkb[b:b + 1, :]
            q_cm = jnp.sum(qb_t[:, b:b + 1] * c_old, axis=0, keepdims=True)
            num = sc * q_cm + wts * vbm[b:b + 1, :]
            den = sc * qn[b:b + 1, :] + wts
            hb_rows.append(num / jnp.maximum(jnp.abs(den), jnp.exp(-mt)))
            c_out[b, h] = sc * c_old + (w_in * kb_t[:, b:b + 1]) * vbm[b:b + 1, :]
            n_rows.append(sc * n_old[b:b + 1, :] + w_in * kbm[b:b + 1, :])
            m_rows.append(mt)
        o_a = jnp.concatenate(o_rows, axis=0)
        h_b = jnp.concatenate(hb_rows, axis=0)
        n_out[:, h * HEAD_DIM:(h + 1) * HEAD_DIM] = jnp.concatenate(n_rows, axis=0)
        m_out[:, h:h + 1] = jnp.concatenate(m_rows, axis=0)
        z = proj[rows, ZA + h * HEAD_DIM:ZA + (h + 1) * HEAD_DIM]
        merged[rows, h * HEAD_DIM:(h + 1) * HEAD_DIM] = _rms(o_a, gdn_w) * _silu(z)
        o_gate = proj[rows, OB + h * HEAD_DIM:OB + (h + 1) * HEAD_DIM]
        merged[rows, N_HEADS * HEAD_DIM + h * HEAD_DIM:N_HEADS * HEAD_DIM + (h + 1) * HEAD_DIM] = (
            _rms(h_b, ml_w) * _sigmoid(o_gate))

    @pl.when(i == n_steps - 1)
    def _():
        mix = jnp.dot(_bf(merged[...]), wout_ref[...], preferred_element_type=F32)
        y_ref[...] = _layer_norm(DN_ALPHA * x_ref[...] + mix, ln_ref[0:1, :], ln_ref[1:2, :])


def _ab_decode(x, win, wout, convw, gp, ln, cbuf, s0, c0, n0, m0):
    nb = x.shape[0]
    bb = SUBLANES
    assert nb % bb == 0
    n_steps = nb // bb
    const = lambda shape: pl.BlockSpec(shape, lambda i: (0,) * len(shape))
    blk = lambda shape: pl.BlockSpec((bb,) + shape, lambda i: (i,) + (0,) * len(shape))
    hist = (CONV_W - 1) * A_CONV
    width = N_HEADS * HEAD_DIM
    return pl.pallas_call(
        functools.partial(_ab_decode_kernel, bb=bb, n_steps=n_steps),
        grid=(n_steps,),
        in_specs=[
            const((nb, D_MODEL)), const((D_MODEL, D_IN_PAD)), const((2 * width, D_MODEL)),
            const((CONV_W, A_CONV)), const((SUBLANES, LANES)), const((2, D_MODEL)),
            blk((hist,)), blk((N_HEADS, HEAD_DIM, HEAD_DIM)), blk((N_HEADS, HEAD_DIM, HEAD_DIM)),
            blk((width,)), blk((N_HEADS,)),
        ],
        out_specs=[
            const((nb, D_MODEL)), blk((hist,)), blk((N_HEADS, HEAD_DIM, HEAD_DIM)),
            blk((N_HEADS, HEAD_DIM, HEAD_DIM)), blk((width,)), blk((N_HEADS,)),
        ],
        out_shape=[
            jax.ShapeDtypeStruct((nb, D_MODEL), F32),
            jax.ShapeDtypeStruct((nb, hist), F32),
            jax.ShapeDtypeStruct((nb, N_HEADS, HEAD_DIM, HEAD_DIM), F32),
            jax.ShapeDtypeStruct((nb, N_HEADS, HEAD_DIM, HEAD_DIM), F32),
            jax.ShapeDtypeStruct((nb, width), F32),
            jax.ShapeDtypeStruct((nb, N_HEADS), F32),
        ],
        scratch_shapes=[pltpu.VMEM((nb, D_IN_PAD), F32), pltpu.VMEM((nb, 2 * width), F32)],
        compiler_params=pltpu.CompilerParams(dimension_semantics=("arbitrary",), vmem_limit_bytes=VMEM_LIMIT),
        name="ab_decode",
    )(x, win, wout, convw, gp, ln, cbuf, s0, c0, n0, m0)


def _second_largest_sum(a, b, c, d):
    hi1, lo1 = jnp.maximum(a, b), jnp.minimum(a, b)
    hi2, lo2 = jnp.maximum(c, d), jnp.minimum(c, d)
    return jnp.maximum(hi1, hi2) + jnp.maximum(jnp.minimum(hi1, hi2), jnp.maximum(lo1, lo2))


def _first_argmax(vals):
    best_v = vals[0]
    best_i = jnp.zeros(vals[0].shape, jnp.int32)
    for j in range(1, len(vals)):
        better = vals[j] > best_v
        best_v = jnp.where(better, vals[j], best_v)
        best_i = jnp.where(better, j, best_i)
    return best_i


def _router_kernel(x_ref, rw_ref, rb_ref, comb_ref):
    tm = x_ref.shape[0]
    logits = lax.dot_general(rw_ref[...], x_ref[...], (((1,), (1,)), ((), ())),
                             preferred_element_type=F32, precision=HIGHEST)
    ex = jnp.exp(logits - jnp.max(logits, axis=0, keepdims=True))
    probs = ex / jnp.sum(ex, axis=0, keepdims=True)
    sel = probs + rb_ref[...]
    p = [probs[j:j + 1, :] for j in range(N_EXPERTS)]
    s = [sel[j:j + 1, :] for j in range(N_EXPERTS)]
    scores = [_second_largest_sum(*s[EXPERTS_PER_GROUP * g:EXPERTS_PER_GROUP * (g + 1)])
              for g in range(N_EXPERT_GROUPS)]
    best = _first_argmax(scores)
    masked = [jnp.where(best == j // EXPERTS_PER_GROUP, s[j], -jnp.inf) for j in range(N_EXPERTS)]
    i1 = _first_argmax(masked)
    i2 = _first_argmax([jnp.where(i1 == j, -jnp.inf, masked[j]) for j in range(N_EXPERTS)])
    zero = jnp.zeros_like(p[0])
    p1 = functools.reduce(lambda a, b: a + b, [jnp.where(i1 == j, p[j], zero) for j in range(N_EXPERTS)])
    p2 = functools.reduce(lambda a, b: a + b, [jnp.where(i2 == j, p[j], zero) for j in range(N_EXPERTS)])
    tot = p1 + p2
    rows = [jnp.where(i1 == j, p1 / tot, zero) + jnp.where(i2 == j, p2 / tot, zero) for j in range(N_EXPERTS)]
    comb_t = jnp.concatenate(rows + [jnp.zeros((LANES - N_EXPERTS, tm), F32)], axis=0)
    comb_ref[...] = comb_t.T


def _router(x, rw_t, rb_col):
    t = x.shape[0]
    tm = min(256, t)
    assert t % tm == 0
    return pl.pallas_call(
        _router_kernel,
        grid=(t // tm,),
        in_specs=[pl.BlockSpec((tm, D_MODEL), lambda i: (i, 0)),
                  pl.BlockSpec((N_EXPERTS, D_MODEL), lambda i: (0, 0)),
                  pl.BlockSpec((N_EXPERTS, 1), lambda i: (0, 0))],
        out_specs=pl.BlockSpec((tm, LANES), lambda i: (i, 0)),
        out_shape=jax.ShapeDtypeStruct((t, LANES), F32),
        compiler_params=pltpu.CompilerParams(dimension_semantics=("arbitrary",), vmem_limit_bytes=VMEM_LIMIT),
        name="router",
    )(x, rw_t, rb_col)


def _moe_kernel(x_ref, comb_ref, wg_ref, wu_ref, wd_ref, ln_ref, o_ref, xb, acc):
    e = pl.program_id(1)

    @pl.when(e == 0)
    def _():
        xb[...] = _bf(x_ref[...])
        acc[...] = jnp.zeros_like(acc)

    g = jnp.dot(xb[...], wg_ref[...], preferred_element_type=F32)
    u = jnp.dot(xb[...], wu_ref[...], preferred_element_type=F32)
    comb = comb_ref[...]
    lane = lax.broadcasted_iota(jnp.int32, comb.shape, 1)
    cw = jnp.sum(jnp.where(lane == e, comb, 0.0), axis=-1, keepdims=True)
    hid = _silu(g) * u * cw
    acc[...] += jnp.dot(_bf(hid), wd_ref[...], preferred_element_type=F32)

    @pl.when(e == N_EXPERTS - 1)
    def _():
        o_ref[...] = _layer_norm(DN_ALPHA * x_ref[...] + acc[...], ln_ref[0:1, :], ln_ref[1:2, :])


def _moe(x, comb, wg, wu, wd, ln, layer):
    t = x.shape[0]
    tm = min(1024, t)
    assert t % tm == 0
    return pl.pallas_call(
        _moe_kernel,
        grid=(t // tm, N_EXPERTS),
        in_specs=[pl.BlockSpec((tm, D_MODEL), lambda i, e: (i, 0)),
                  pl.BlockSpec((tm, LANES), lambda i, e: (i, 0)),
                  pl.BlockSpec((None, None, D_MODEL, D_FF), lambda i, e: (layer, e, 0, 0)),
                  pl.BlockSpec((None, None, D_MODEL, D_FF), lambda i, e: (layer, e, 0, 0)),
                  pl.BlockSpec((None, None, D_FF, D_MODEL), lambda i, e: (layer, e, 0, 0)),
                  pl.BlockSpec((2, D_MODEL), lambda i, e: (0, 0))],
        out_specs=pl.BlockSpec((tm, D_MODEL), lambda i, e: (i, 0)),
        out_shape=jax.ShapeDtypeStruct((t, D_MODEL), F32),
        scratch_shapes=[pltpu.VMEM((tm, D_MODEL), BF16), pltpu.VMEM((tm, D_MODEL), F32)],
        compiler_params=pltpu.CompilerParams(
            dimension_semantics=("arbitrary", "arbitrary"), vmem_limit_bytes=VMEM_LIMIT),
        name="moe",
    )(x, comb, wg, wu, wd, ln)


def _s5_prep_kernel(are_ref, aim_ref, ldt_ref, bre_ref, bim_ref, abre_ref, abim_ref, bbre_ref, bbim_ref):
    a_r = are_ref[...]
    a_i = aim_ref[...]
    dt = jnp.exp(ldt_ref[...])
    mag = jnp.exp(dt * a_r)
    ab_re = mag * jnp.cos(dt * a_i)
    ab_im = mag * jnp.sin(dt * a_i)
    den = a_r * a_r + a_i * a_i
    nr = ab_re - 1.0
    z_re = (nr * a_r + ab_im * a_i) / den
    z_im = (ab_im * a_r - nr * a_i) / den
    abre_ref[...] = ab_re
    abim_ref[...] = ab_im
    bbre_ref[...] = z_re * bre_ref[...] - z_im * bim_ref[...]
    bbim_ref[...] = z_re * bim_ref[...] + z_im * bre_ref[...]


def _s5_prep(a_re, a_im, log_dt, b_re, b_im):
    n = S5_GROUPS * S5_STATE
    col = jax.ShapeDtypeStruct((n, 1), F32)
    mat = jax.ShapeDtypeStruct((n, S5_GROUP), F32)
    ldt = jnp.broadcast_to(log_dt[:, None], (S5_GROUPS, S5_STATE)).reshape(n, 1)
    return pl.pallas_call(_s5_prep_kernel, out_shape=[col, col, mat, mat], name="s5_prep")(
        a_re.reshape(n, 1), a_im.reshape(n, 1), ldt, b_re.reshape(n, S5_GROUP), b_im.reshape(n, S5_GROUP))


def _gelu_tanh(x):
    return 0.5 * x * (1.0 + jnp.tanh(0.7978845608028654 * (x + 0.044715 * (x * x * x))))


def _s5_kernel(x_ref, h0re_ref, h0im_ref, wre_ref, wim_ref, cre_ref, cim_ref, abre_ref, abim_ref, d_ref,
               gy_ref, hre_out, him_out, xtb, bu_re, bu_im, hre_s, him_s, *, nb, tt, n_t):
    t = pl.program_id(1)

    @pl.when(t == 0)
    def _():
        hre_s[...] = h0re_ref[...]
        him_s[...] = h0im_ref[...]

    for s in range(tt):
        xtb[s * nb:(s + 1) * nb, :] = x_ref[:, s, :]
    xv = xtb[...]
    xb = _bf(xv)
    bu_re[...] = jnp.dot(xb, wre_ref[...], preferred_element_type=F32)
    bu_im[...] = jnp.dot(xb, wim_ref[...], preferred_element_type=F32)

    a_re = jnp.broadcast_to(abre_ref[...], hre_s.shape)
    a_im = jnp.broadcast_to(abim_ref[...], hre_s.shape)

    def step(s, carry):
        h_re, h_im = carry
        rows = pl.ds(pl.multiple_of(s * nb, nb), nb)
        n_re = a_re * h_re - a_im * h_im + bu_re[rows, :]
        n_im = a_re * h_im + a_im * h_re + bu_im[rows, :]
        bu_re[rows, :] = n_re
        bu_im[rows, :] = n_im
        return n_re, n_im

    h_re, h_im = lax.fori_loop(0, tt, step, (hre_s[...], him_s[...]))
    hre_s[...] = h_re
    him_s[...] = h_im

    y = (jnp.dot(_bf(bu_re[...]), cre_ref[...], preferred_element_type=F32)
         - jnp.dot(_bf(bu_im[...]), cim_ref[...], preferred_element_type=F32)
         + d_ref[...] * xv)
    gy = _gelu_tanh(y)
    for s in range(tt):
        gy_ref[:, s, :] = gy[s * nb:(s + 1) * nb, :].astype(gy_ref.dtype)

    @pl.when(t == n_t - 1)
    def _():
        hre_out[...] = h_re
        him_out[...] = h_im


def _s5_scan(x, h0_re, h0_im, wre, wim, cre, cim, ab_re, ab_im, d_skip):
    nb, length, _ = x.shape
    tt = min(64, length)
    assert length % tt == 0 and nb % SUBLANES == 0
    n_t = length // tt
    n_k = D_MODEL // LANES
    sw = (LANES // S5_GROUP) * S5_STATE
    return pl.pallas_call(
        functools.partial(_s5_kernel, nb=nb, tt=tt, n_t=n_t),
        grid=(n_k, n_t),
        in_specs=[
            pl.BlockSpec((nb, tt, LANES), lambda k, t: (0, t, k)),
            pl.BlockSpec((nb, sw), lambda k, t: (0, k)), pl.BlockSpec((nb, sw), lambda k, t: (0, k)),
            pl.BlockSpec((None, LANES, sw), lambda k, t: (k, 0, 0)),
            pl.BlockSpec((None, LANES, sw), lambda k, t: (k, 0, 0)),
            pl.BlockSpec((None, sw, LANES), lambda k, t: (k, 0, 0)),
            pl.BlockSpec((None, sw, LANES), lambda k, t: (k, 0, 0)),
            pl.BlockSpec((1, sw), lambda k, t: (0, k)), pl.BlockSpec((1, sw), lambda k, t: (0, k)),
            pl.BlockSpec((1, LANES), lambda k, t: (0, k)),
        ],
        out_specs=[
            pl.BlockSpec((nb, tt, LANES), lambda k, t: (0, t, k)),
            pl.BlockSpec((nb, sw), lambda k, t: (0, k)), pl.BlockSpec((nb, sw), lambda k, t: (0, k)),
        ],
        out_shape=[
            jax.ShapeDtypeStruct((nb, length, D_MODEL), BF16),
            jax.ShapeDtypeStruct((nb, S5_GROUPS * S5_STATE), F32),
            jax.ShapeDtypeStruct((nb, S5_GROUPS * S5_STATE), F32),
        ],
        scratch_shapes=[
            pltpu.VMEM((nb * tt, LANES), F32),
            pltpu.VMEM((nb * tt, sw), F32), pltpu.VMEM((nb * tt, sw), F32),
            pltpu.VMEM((nb, sw), F32), pltpu.VMEM((nb, sw), F32),
        ],
        compiler_params=pltpu.CompilerParams(
            dimension_semantics=("arbitrary", "arbitrary"), vmem_limit_bytes=VMEM_LIMIT),
        name="s5_scan",
    )(x, h0_re, h0_im, wre, wim, cre, cim, ab_re, ab_im, d_skip)


def _glu_ln_kernel(gy_ref, h_ref, wa_ref, wb_ref, ln_ref, o_ref):
    gy = gy_ref[...]
    a = jnp.dot(gy, wa_ref[...], preferred_element_type=F32)
    b = jnp.dot(gy, wb_ref[...], preferred_element_type=F32)
    o_ref[...] = _layer_norm(DN_ALPHA * h_ref[...] + a * _sigmoid(b), ln_ref[0:1, :], ln_ref[1:2, :])


def _glu_ln(gy, h, wa, wb, ln):
    t = h.shape[0]
    tm = min(512, t)
    assert t % tm == 0
    return pl.pallas_call(
        _glu_ln_kernel,
        grid=(t // tm,),
        in_specs=[pl.BlockSpec((tm, D_MODEL), lambda i: (i, 0)), pl.BlockSpec((tm, D_MODEL), lambda i: (i, 0)),
                  pl.BlockSpec((D_MODEL, D_MODEL), lambda i: (0, 0)), pl.BlockSpec((D_MODEL, D_MODEL), lambda i: (0, 0)),
                  pl.BlockSpec((2, D_MODEL), lambda i: (0, 0))],
        out_specs=pl.BlockSpec((tm, D_MODEL), lambda i: (i, 0)),
        out_shape=jax.ShapeDtypeStruct((t, D_MODEL), F32),
        compiler_params=pltpu.CompilerParams(dimension_semantics=("arbitrary",), vmem_limit_bytes=VMEM_LIMIT),
        name="glu_ln",
    )(gy, h, wa, wb, ln)


def _block_diag_slices(m, rows_per_group, cols_per_group):
    gps = LANES // S5_GROUP
    m = m.reshape(S5_GROUPS // gps, gps, rows_per_group, cols_per_group)
    eye = jnp.eye(gps, dtype=m.dtype)
    out = m[:, :, :, None, :] * eye[None, :, None, :, None]
    return out.reshape(S5_GROUPS // gps, gps * rows_per_group, gps * cols_per_group)


def _prepare(p):
    w = p['w_in'][0]
    win = jnp.concatenate(
        [w[:, 0:1536], w[:, 1544:2056], w[:, 2056:3592], w[:, 3600:4112], w[:, 1536:1544], w[:, 3592:3600],
         jnp.zeros((D_MODEL, D_IN_PAD - 4112), w.dtype)], axis=1).astype(BF16)
    gp = jnp.zeros((SUBLANES, LANES), F32)
    gp = gp.at[0, G_DEC:G_DEC + N_HEADS].set(p['gdn_A_log'][0])
    gp = gp.at[1, G_DEC:G_DEC + N_HEADS].set(p['gdn_dt_bias'][0])
    gp = gp.at[2, G_IN:G_IN + N_HEADS].set(p['ml_b_i'][0])
    gp = gp.at[3, G_FG:G_FG + N_HEADS].set(p['ml_b_f'][0])
    gp = gp.at[4, :].set(p['gdn_norm_w'][0])
    gp = gp.at[5, :].set(p['ml_norm_w'][0])
    ab_re, ab_im, bb_re, bb_im = _s5_prep(p['s5_A_re'][0], p['s5_A_im'][0], p['s5_log_dt'][0],
                                          p['s5_B_re'][0], p['s5_B_im'][0])
    to_in = lambda bb: _block_diag_slices(
        bb.reshape(S5_GROUPS, S5_STATE, S5_GROUP).transpose(0, 2, 1), S5_GROUP, S5_STATE).astype(BF16)
    to_out = lambda c: _block_diag_slices(c.transpose(0, 2, 1), S5_STATE, S5_GROUP).astype(BF16)
    return dict(
        win=win, wout=p['w_out'][0].astype(BF16), convw=p['gdn_conv_w'][0], gp=gp,
        ln_mix=[jnp.stack([p['ln_mix_g'][l], p['ln_mix_b'][l]]) for l in range(DEPTH)],
        ln_ffn=[jnp.stack([p['ln_ffn_g'][l], p['ln_ffn_b'][l]]) for l in range(DEPTH)],
        rw_t=p['router_w'].T, rb_col=p['router_b'][:, None],
        wg=p['moe_w_gate'].astype(BF16), wu=p['moe_w_up'].astype(BF16), wd=p['moe_w_down'].astype(BF16),
        s5_wre=to_in(bb_re), s5_wim=to_in(bb_im),
        s5_cre=to_out(p['s5_C_re'][0]), s5_cim=to_out(p['s5_C_im'][0]),
        s5_abre=ab_re.reshape(1, -1), s5_abim=ab_im.reshape(1, -1), s5_d=p['s5_D'][0][None, :],
        glu_a=p['s5_w_glu_a'][0].astype(BF16), glu_b=p['s5_w_glu_b'][0].astype(BF16),
    )


def _ffn(h, w, layer):
    comb = _router(h, w['rw_t'], w['rb_col'])
    return _moe(h, comb, w['wg'], w['wu'], w['wd'], w['ln_ffn'][layer], layer)


def _s5_layer(h, nb, length, h0_re, h0_im, w):
    gy, re, im = _s5_scan(h.reshape(nb, length, D_MODEL), h0_re, h0_im, w['s5_wre'], w['s5_wim'],
                          w['s5_cre'], w['s5_cim'], w['s5_abre'], w['s5_abim'], w['s5_d'])
    h = _glu_ln(gy.reshape(nb * length, D_MODEL), h, w['glu_a'], w['glu_b'], w['ln_mix'][1])
    return h, re, im


def kernel(x_prompt, x_sample, state_gdn_conv, state_gdn_S, state_mlstm_C, state_mlstm_n, state_mlstm_m,
           state_s5_re, state_s5_im, w_in, gdn_conv_w, gdn_A_log, gdn_dt_bias, gdn_norm_w, ml_b_i, ml_b_f,
           ml_norm_w, w_out, s5_A_re, s5_A_im, s5_log_dt, s5_B_re, s5_B_im, s5_C_re, s5_C_im, s5_D,
           s5_w_glu_a, s5_w_glu_b, router_w, router_b, moe_w_gate, moe_w_up, moe_w_down,
           ln_mix_g, ln_mix_b, ln_ffn_g, ln_ffn_b):
    w = _prepare(dict(
        w_in=w_in, gdn_conv_w=gdn_conv_w, gdn_A_log=gdn_A_log, gdn_dt_bias=gdn_dt_bias, gdn_norm_w=gdn_norm_w,
        ml_b_i=ml_b_i, ml_b_f=ml_b_f, ml_norm_w=ml_norm_w, w_out=w_out, s5_A_re=s5_A_re, s5_A_im=s5_A_im,
        s5_log_dt=s5_log_dt, s5_B_re=s5_B_re, s5_B_im=s5_B_im, s5_C_re=s5_C_re, s5_C_im=s5_C_im, s5_D=s5_D,
        s5_w_glu_a=s5_w_glu_a, s5_w_glu_b=s5_w_glu_b, router_w=router_w, router_b=router_b,
        moe_w_gate=moe_w_gate, moe_w_up=moe_w_up, moe_w_down=moe_w_down,
        ln_mix_g=ln_mix_g, ln_mix_b=ln_mix_b, ln_ffn_g=ln_ffn_g, ln_ffn_b=ln_ffn_b))
    bp, lp, _ = x_prompt.shape
    bs, ls, _ = x_sample.shape
    assert ls == 1
    n_state = S5_GROUPS * S5_STATE

    h, p_hist, p_s, p_c, p_n, p_m = _ab_prompt(x_prompt, w['win'], w['wout'], w['convw'], w['gp'], w['ln_mix'][0])
    h = _ffn(h.reshape(bp * lp, D_MODEL), w, 0)
    zeros = jnp.zeros((bp, n_state), F32)
    h, p_re, p_im = _s5_layer(h, bp, lp, zeros, zeros, w)
    y_prompt = _ffn(h, w, 1).reshape(bp, lp, D_MODEL)

    hs, s_cbuf, s_s, s_c, s_n, s_m = _ab_decode(
        x_sample[:, 0], w['win'], w['wout'], w['convw'], w['gp'], w['ln_mix'][0],
        state_gdn_conv.reshape(bs, (CONV_W - 1) * A_CONV), state_gdn_S[:, 0], state_mlstm_C[:, 0],
        state_mlstm_n.reshape(bs, N_HEADS * HEAD_DIM), state_mlstm_m[:, 0])
    hs = _ffn(hs, w, 0)
    hs, s_re, s_im = _s5_layer(hs, bs, 1, state_s5_re.reshape(bs, n_state), state_s5_im.reshape(bs, n_state), w)
    y_sample = _ffn(hs, w, 1).reshape(bs, 1, D_MODEL)

    grp = lambda a, n: a.reshape(n, 1, S5_GROUPS, S5_STATE)
    return (
        y_prompt, y_sample,
        p_hist[:, None, SUBLANES - (CONV_W - 1):, :], p_s[:, None], p_c[:, None],
        p_n[:, None, :, 0, :], p_m[:, None, :, 0, 0], grp(p_re, bp), grp(p_im, bp),
        s_cbuf.reshape(bs, 1, CONV_W - 1, A_CONV), s_s[:, None], s_c[:, None],
        s_n.reshape(bs, 1, N_HEADS, HEAD_DIM), s_m[:, None], grp(s_re, bs), grp(s_im, bs),
    )
```

```python
import functools

import jax
import jax.numpy as jnp
from jax import lax
from jax.experimental import pallas as pl
from jax.experimental.pallas import tpu as pltpu

F32 = jnp.float32
BF16 = jnp.bfloat16
HIGHEST = lax.Precision.HIGHEST

D_MODEL = 1024
DEPTH = 2
N_HEADS = 4
HEAD_DIM = 128
CONV_W = 4
CHUNK = 64
A_CONV = 3 * N_HEADS * HEAD_DIM
S5_GROUP = 16
S5_GROUPS = D_MODEL // S5_GROUP
S5_STATE = 64
N_EXPERTS = 16
EXPERTS_PER_GROUP = 4
N_EXPERT_GROUPS = N_EXPERTS // EXPERTS_PER_GROUP
D_FF = 512
DN_ALPHA = (2 * DEPTH) ** 0.25
LN_EPS = 1e-5
RMS_EPS = 1e-6
NEG_BIG = -1e30

QA, KA, VA, ZA, QB, KB, VB, OB, GT = 0, 512, 1024, 1536, 2048, 2560, 3072, 3584, 4096
D_IN_PAD = 4224
G_DEC, G_BETA, G_IN, G_FG = 0, 4, 8, 12

LANES = 128
SUBLANES = 8
VMEM_LIMIT = 56 * 1024 * 1024


def _bf(x):
    return x.astype(BF16)


def _nn(a, b):
    return jnp.dot(_bf(a), _bf(b), preferred_element_type=F32)


def _nt(a, b):
    return lax.dot_general(_bf(a), _bf(b), (((1,), (1,)), ((), ())), preferred_element_type=F32)


def _tn(a, b):
    return lax.dot_general(_bf(a), _bf(b), (((0,), (0,)), ((), ())), preferred_element_type=F32)


def _sigmoid(x):
    return 1.0 / (1.0 + jnp.exp(-x))


def _softplus(x):
    return jnp.maximum(x, 0.0) + jnp.log(1.0 + jnp.exp(-jnp.abs(x)))


def _silu(x):
    return x * _sigmoid(x)


def _layer_norm(y, g, b):
    mu = jnp.mean(y, axis=-1, keepdims=True)
    yc = y - mu
    var = jnp.mean(yc * yc, axis=-1, keepdims=True)
    return yc * lax.rsqrt(var + LN_EPS) * g + b


def _rms(x, w):
    return x * lax.rsqrt(jnp.mean(x * x, axis=-1, keepdims=True) + RMS_EPS) * w


def _gate_transform(raw, gp):
    lane = lax.broadcasted_iota(jnp.int32, raw.shape, 1)
    dec = -jnp.exp(gp[0:1, :]) * _softplus(raw + gp[1:2, :])
    beta = _sigmoid(raw)
    ipre = raw + gp[2:3, :]
    logf = -_softplus(-(raw + gp[3:4, :]))
    return jnp.where(lane < G_BETA, dec,
                     jnp.where(lane < G_IN, beta,
                               jnp.where(lane < G_FG, ipre,
                                         jnp.where(lane < G_FG + N_HEADS, logf, 0.0))))


def _bnn(a, b):
    return lax.dot_general(_bf(a), _bf(b), (((2,), (1,)), ((0,), (0,))), preferred_element_type=F32)


def _bnt(a, b):
    return lax.dot_general(_bf(a), _bf(b), (((2,), (2,)), ((0,), (0,))), preferred_element_type=F32)


def _btn(a, b):
    return lax.dot_general(_bf(a), _bf(b), (((1,), (1,)), ((0,), (0,))), preferred_element_type=F32)


def _unit_lower_inverse_minus_eye(a):
    c = a.shape[-1]
    r = -a
    steps = max(1, (c - 1).bit_length()) - 1
    q = _bnn(r, r)
    for i in range(steps):
        rq = _bnn(r, q)
        qq = _bnn(q, q) if i + 1 < steps else None
        r = r + q + rq
        q = qq
    return r


def _ab_prompt_kernel(x_ref, win_ref, wout_ref, convw_ref, gp_ref, ln_ref,
                      h_ref, hist_ref, s_out, c_out, n_out, m_out,
                      proj, qkv, gates, gcum, merged, s_s, c_s, n_s, m_s,
                      u_s, w_s, attn_s, dlog_s, qk_s, dmax_s, *, tb, n_t):
    t = pl.program_id(1)
    nc = tb // CHUNK

    @pl.when(t == 0)
    def _():
        proj[0:SUBLANES, :] = jnp.zeros((SUBLANES, D_IN_PAD), F32)
        s_s[...] = jnp.zeros_like(s_s)
        c_s[...] = jnp.zeros_like(c_s)
        n_s[...] = jnp.zeros_like(n_s)
        m_s[...] = jnp.zeros_like(m_s)

    x = x_ref[...]
    proj[SUBLANES:SUBLANES + tb, :] = jnp.dot(_bf(x), win_ref[...], preferred_element_type=F32)

    for blk in range(A_CONV // LANES):
        cs = slice(blk * LANES, (blk + 1) * LANES)
        acc = proj[SUBLANES:SUBLANES + tb, cs] * convw_ref[CONV_W - 1:CONV_W, cs]
        for j in range(1, CONV_W):
            acc = acc + proj[SUBLANES - j:SUBLANES - j + tb, cs] * convw_ref[CONV_W - 1 - j:CONV_W - j, cs]
        y = _silu(acc)
        if blk < 2 * N_HEADS:
            y = y * lax.rsqrt(jnp.sum(y * y, axis=-1, keepdims=True) + RMS_EPS)
            if blk < N_HEADS:
                y = y * HEAD_DIM ** -0.5
        qkv[:, cs] = y

    gt = _gate_transform(proj[SUBLANES:SUBLANES + tb, GT:GT + LANES], gp_ref[...])
    gates[...] = gt
    ri = lax.broadcasted_iota(jnp.int32, (tb, tb), 0)
    ci = lax.broadcasted_iota(jnp.int32, (tb, tb), 1)
    same_chunk = lax.shift_right_logical(ri, 6) == lax.shift_right_logical(ci, 6)
    ltri = jnp.where(same_chunk, jnp.where(ri >= ci, 1.0, 0.0), 0.0)
    gcum[...] = jnp.dot(ltri, gt, preferred_element_type=F32, precision=HIGHEST)

    ii = lax.broadcasted_iota(jnp.int32, (CHUNK, CHUNK), 0)
    jj = lax.broadcasted_iota(jnp.int32, (CHUNK, CHUNK), 1)
    incl = ii >= jj
    strict = ii > jj
    gdn_w = gp_ref[4:5, :]
    ml_w = gp_ref[5:6, :]

    pairs = [(c, h) for c in range(nc) for h in range(N_HEADS)]

    def tile_heads(ref, row0, col0):
        return jnp.stack([ref[row0 + c * CHUNK:row0 + (c + 1) * CHUNK, col0 + h * HEAD_DIM:col0 + (h + 1) * HEAD_DIM]
                          for c, h in pairs])

    def tile_cols(ref, lane0):
        return jnp.stack([ref[c * CHUNK:(c + 1) * CHUNK, lane0 + h:lane0 + h + 1] for c, h in pairs])

    def tile_rows(transposed, lane0):
        return jnp.stack([transposed[c][lane0 + h:lane0 + h + 1, :] for c, h in pairs])

    cs_t = [gcum[c * CHUNK:(c + 1) * CHUNK, :].T for c in range(nc)]
    gt_t = [gates[c * CHUNK:(c + 1) * CHUNK, :].T for c in range(nc)]
    q3 = tile_heads(qkv, 0, QA)
    k3 = tile_heads(qkv, 0, KA)
    v3 = tile_heads(qkv, 0, VA)
    g_col3 = tile_cols(gcum, G_DEC)
    beta3 = tile_cols(gates, G_BETA)
    decay3 = jnp.where(incl, jnp.exp(jnp.where(incl, g_col3 - tile_rows(cs_t, G_DEC), 0.0)), 0.0)
    kb3 = k3 * beta3
    a_low3 = jnp.where(strict, _bnt(kb3, k3) * decay3, 0.0)
    attn_s[...] = _bnt(q3, k3) * decay3
    qk_s[...] = _bnt(tile_heads(proj, SUBLANES, QB), tile_heads(proj, SUBLANES, KB) * HEAD_DIM ** -0.5)
    r3 = _unit_lower_inverse_minus_eye(a_low3)
    rhs3 = jnp.concatenate([v3 * beta3, kb3 * jnp.exp(g_col3)], axis=2)
    uw3 = rhs3 + _bnn(r3, rhs3)
    u_s[...] = uw3[:, :, :HEAD_DIM]
    w_s[...] = uw3[:, :, HEAD_DIM:]
    dlog3 = jnp.where(incl, tile_cols(gcum, G_FG) - tile_rows(cs_t, G_FG) + tile_rows(gt_t, G_IN), NEG_BIG)
    dlog_s[...] = dlog3
    dmax_s[...] = jnp.max(dlog3, axis=-1, keepdims=True)

    def chunk_body(c, carry):
        r0 = pl.multiple_of(c * CHUNK, CHUNK)
        rows = pl.ds(r0, CHUNK)
        prow = pl.ds(pl.multiple_of(r0 + SUBLANES, SUBLANES), CHUNK)
        last = pl.ds(r0 + CHUNK - 1, 1)
        pc = pl.ds(pl.multiple_of(c * N_HEADS, N_HEADS), N_HEADS)

        def heads(ref, rws, col0):
            return jnp.stack([ref[rws, col0 + h * HEAD_DIM:col0 + (h + 1) * HEAD_DIM] for h in range(N_HEADS)])

        def cols(ref, rws, lane0):
            return jnp.stack([ref[rws, lane0 + h:lane0 + h + 1] for h in range(N_HEADS)])

        q = heads(qkv, rows, QA)
        k = heads(qkv, rows, KA)
        g_col = cols(gcum, rows, G_DEC)
        g_last = cols(gcum, last, G_DEC)
        qb = heads(proj, prow, QB)
        kbm = heads(proj, prow, KB) * HEAD_DIM ** -0.5
        vbm = heads(proj, prow, VB)
        b_col = cols(gcum, rows, G_FG)
        b_last = cols(gcum, last, G_FG)
        i_col = cols(gates, rows, G_IN)
        s_old = s_s[...]
        c_old = c_s[...]
        n_old = n_s[:, 0:1, :]
        m_old = m_s[:, 0:1, 0:1]
        inter = b_col + m_old
        mt = jnp.maximum(inter, dmax_s[pc])
        wts = jnp.exp(dlog_s[pc] - mt) * qk_s[pc]
        sc = jnp.exp(inter - mt)
        m_new = mt[:, CHUNK - 1:CHUNK, :]
        sd = jnp.exp(b_last + m_old - m_new)
        kw = kbm * jnp.exp(b_last - b_col + i_col - m_new)
        w_state = _bnn(w_s[pc], s_old)
        q_state = _bnn(q * jnp.exp(g_col), s_old)
        q_mem = _bnn(qb, c_old)
        w_val = _bnn(wts, vbm)
        kv = _btn(kw, vbm)
        v_new = u_s[pc] - w_state
        o_a = q_state + _bnn(attn_s[pc], v_new)
        s_s[...] = s_old * jnp.exp(g_last) + _btn(k * jnp.exp(g_last - g_col), v_new)
        num = sc * q_mem + w_val
        den = sc * jnp.sum(qb * n_old, axis=-1, keepdims=True) + jnp.sum(wts, axis=-1, keepdims=True)
        h_b = num / jnp.maximum(jnp.abs(den), jnp.exp(-mt))
        c_s[...] = sd * c_old + kv
        n_s[...] = jnp.broadcast_to(sd * n_old + jnp.sum(kw, axis=1, keepdims=True), n_s.shape)
        m_s[...] = jnp.broadcast_to(m_new, m_s.shape)
        o_n = _rms(o_a, gdn_w)
        h_n = _rms(h_b, ml_w)
        for h in range(N_HEADS):
            hs = slice(h * HEAD_DIM, (h + 1) * HEAD_DIM)
            hs2 = slice((N_HEADS + h) * HEAD_DIM, (N_HEADS + h + 1) * HEAD_DIM)
            merged[rows, hs] = o_n[h] * _silu(proj[prow, ZA + h * HEAD_DIM:ZA + (h + 1) * HEAD_DIM])
            merged[rows, hs2] = h_n[h] * _sigmoid(proj[prow, OB + h * HEAD_DIM:OB + (h + 1) * HEAD_DIM])
        return carry

    lax.fori_loop(0, nc, chunk_body, 0)

    mix = jnp.dot(_bf(merged[...]), wout_ref[...], preferred_element_type=F32)
    h_ref[...] = _layer_norm(DN_ALPHA * x + mix, ln_ref[0:1, :], ln_ref[1:2, :])

    proj[0:SUBLANES, 0:A_CONV] = proj[tb:tb + SUBLANES, 0:A_CONV]

    @pl.when(t == n_t - 1)
    def _():
        hist_ref[...] = proj[tb:tb + SUBLANES, 0:A_CONV]
        s_out[...] = s_s[...]
        c_out[...] = c_s[...]
        n_out[...] = n_s[...]
        m_out[...] = m_s[...]


def _ab_prompt(x, win, wout, convw, gp, ln):
    bsz, length, _ = x.shape
    tb = min(256, length)
    assert length % tb == 0 and tb % CHUNK == 0 and length >= SUBLANES
    n_t = length // tb
    n_pairs = (tb // CHUNK) * N_HEADS
    const = lambda shape: pl.BlockSpec(shape, lambda b, t: (0,) * len(shape))
    per_b =lambda shape: pl.BlockSpec((None,) + shape, lambda b, t: (b,) + (0,) * len(shape))
    return pl.pallas_call(
        functools.partial(_ab_prompt_kernel, tb=tb, n_t=n_t),
        grid=(bsz, n_t),
        in_specs=[
            pl.BlockSpec((None, tb, D_MODEL), lambda b, t: (b, t, 0)),
            const((D_MODEL, D_IN_PAD)), const((2 * N_HEADS * HEAD_DIM, D_MODEL)),
            const((CONV_W, A_CONV)), const((SUBLANES, LANES)), const((2, D_MODEL)),
        ],
        out_specs=[
            pl.BlockSpec((None, tb, D_MODEL), lambda b, t: (b, t, 0)),
            per_b((SUBLANES, A_CONV)), per_b((N_HEADS, HEAD_DIM, HEAD_DIM)), per_b((N_HEADS, HEAD_DIM, HEAD_DIM)),
            per_b((N_HEADS, SUBLANES, LANES)), per_b((N_HEADS, SUBLANES, LANES)),
        ],
        out_shape=[
            jax.ShapeDtypeStruct((bsz, length, D_MODEL), F32),
            jax.ShapeDtypeStruct((bsz, SUBLANES, A_CONV), F32),
            jax.ShapeDtypeStruct((bsz, N_HEADS, HEAD_DIM, HEAD_DIM), F32),
            jax.ShapeDtypeStruct((bsz, N_HEADS, HEAD_DIM, HEAD_DIM), F32),
            jax.ShapeDtypeStruct((bsz, N_HEADS, SUBLANES, LANES), F32),
            jax.ShapeDtypeStruct((bsz, N_HEADS, SUBLANES, LANES), F32),
        ],
        scratch_shapes=[
            pltpu.VMEM((tb + SUBLANES, D_IN_PAD), F32),
            pltpu.VMEM((tb, A_CONV), F32),
            pltpu.VMEM((tb, LANES), F32),
            pltpu.VMEM((tb, LANES), F32),
            pltpu.VMEM((tb, 2 * N_HEADS * HEAD_DIM), F32),
            pltpu.VMEM((N_HEADS, HEAD_DIM, HEAD_DIM), F32),
            pltpu.VMEM((N_HEADS, HEAD_DIM, HEAD_DIM), F32),
            pltpu.VMEM((N_HEADS, SUBLANES, LANES), F32),
            pltpu.VMEM((N_HEADS, SUBLANES, LANES), F32),
            pltpu.VMEM((n_pairs, CHUNK, HEAD_DIM), F32),
            pltpu.VMEM((n_pairs, CHUNK, HEAD_DIM), F32),
            pltpu.VMEM((n_pairs, CHUNK, CHUNK), F32),
            pltpu.VMEM((n_pairs, CHUNK, CHUNK), F32),
            pltpu.VMEM((n_pairs, CHUNK, CHUNK), F32),
            pltpu.VMEM((n_pairs, CHUNK, 1), F32),
        ],
        compiler_params=pltpu.CompilerParams(
            dimension_semantics=("arbitrary", "arbitrary"), vmem_limit_bytes=VMEM_LIMIT),
        name="ab_prompt",
    )(x, win, wout, convw, gp, ln)


def _ab_decode_kernel(x_ref, win_ref, wout_ref, convw_ref, gp_ref, ln_ref, cbuf_ref, s_in, c_in, n_in, m_in,
                      y_ref, cbuf_out, s_out, c_out, n_out, m_out, proj, merged, *, bb, n_steps):
    i = pl.program_id(0)

    @pl.when(i == 0)
    def _():
        proj[...] = jnp.dot(_bf(x_ref[...]), win_ref[...], preferred_element_type=F32)

    rows = pl.ds(pl.multiple_of(i * bb, bb), bb)
    raw = proj[rows, 0:A_CONV]
    cbuf = cbuf_ref[...]
    conv = raw * convw_ref[CONV_W - 1:CONV_W, :]
    for j in range(CONV_W - 1):
        conv = conv + cbuf[:, j * A_CONV:(j + 1) * A_CONV] * convw_ref[j:j + 1, :]
    cbuf_out[:, 0:(CONV_W - 2) * A_CONV] = cbuf[:, A_CONV:(CONV_W - 1) * A_CONV]
    cbuf_out[:, (CONV_W - 2) * A_CONV:(CONV_W - 1) * A_CONV] = raw
    act = _silu(conv)
    gt = _gate_transform(proj[rows, GT:GT + LANES], gp_ref[...])
    gdn_w = gp_ref[4:5, :]
    ml_w = gp_ref[5:6, :]
    m_all = m_in[...]

    for h in range(N_HEADS):
        def head(off):
            return act[:, off + h * HEAD_DIM:off + (h + 1) * HEAD_DIM]
        q = head(QA)
        q = q * lax.rsqrt(jnp.sum(q * q, axis=-1, keepdims=True) + RMS_EPS) * HEAD_DIM ** -0.5
        k = head(KA)
        k = k * lax.rsqrt(jnp.sum(k * k, axis=-1, keepdims=True) + RMS_EPS)
        v = head(VA)
        q_t = q.T
        k_t = k.T
        qk = jnp.sum(q * k, axis=-1, keepdims=True)
        qb = proj[rows, QB + h * HEAD_DIM:QB + (h + 1) * HEAD_DIM]
        kbm = proj[rows, KB + h * HEAD_DIM:KB + (h + 1) * HEAD_DIM] * HEAD_DIM ** -0.5
        vbm = proj[rows, VB + h * HEAD_DIM:VB + (h + 1) * HEAD_DIM]
        qb_t = qb.T
        kb_t = kbm.T
        qkb = jnp.sum(qb * kbm, axis=-1, keepdims=True)
        n_old = n_in[:, h * HEAD_DIM:(h + 1) * HEAD_DIM]
        qn = jnp.sum(qb * n_old, axis=-1, keepdims=True)
        o_rows = []
        hb_rows = []
        n_rows = []
        m_rows = []
        for b in range(bb):
            s_old = s_in[b, h]
            k_c = k_t[:, b:b + 1]
            q_c = q_t[:, b:b + 1]
            e_g = jnp.exp(gt[b:b + 1, G_DEC + h:G_DEC + h + 1])
            beta = gt[b:b + 1, G_BETA + h:G_BETA + h + 1]
            k_s = jnp.sum(k_c * s_old, axis=0, keepdims=True)
            q_s = jnp.sum(q_c * s_old, axis=0, keepdims=True)
            v_new = v[b:b + 1, :] * beta - (beta * e_g) * k_s
            o_rows.append(e_g * q_s + qk[b:b + 1, :] * v_new)
            s_out[b, h] = s_old * e_g + k_c * v_new
            c_old = c_in[b, h]
            m_old = m_all[b:b + 1, h:h + 1]
            i_pre = gt[b:b + 1, G_IN + h:G_IN + h + 1]
            logf = gt[b:b + 1, G_FG + h:G_FG + h + 1]
            inter = logf + m_old
            mt = jnp.maximum(inter, i_pre)
            w_in = jnp.exp(i_pre - mt)
            sc = jnp.exp(inter - mt)
            wts = w_in * qkb[b:b + 1, :]
            q_cm = jnp.sum(qb_t[:, b:b + 1] * c_old, axis=0, keepdims=True)
            num = sc * q_cm + wts * vbm[b:b + 1, :]
            den = sc * qn[b:b + 1, :] + wts
            hb_rows.append(num / jnp.maximum(jnp.abs(den), jnp.exp(-mt)))
            c_out[b, h] = sc * c_old + (w_in * kb_t[:, b:b + 1]) * vbm[b:b + 1, :]
            n_rows.append(sc * n_old[b:b + 1, :] + w_in * kbm[b:b + 1, :])
            m_rows.append(mt)
        o_a = jnp.concatenate(o_rows, axis=0)
        h_b = jnp.concatenate(hb_rows, axis=0)
        n_out[:, h * HEAD_DIM:(h + 1) * HEAD_DIM] = jnp.concatenate(n_rows, axis=0)
        m_out[:, h:h + 1] = jnp.concatenate(m_rows, axis=0)
        z = proj[rows, ZA + h * HEAD_DIM:ZA + (h + 1) * HEAD_DIM]
        merged[rows, h * HEAD_DIM:(h + 1) * HEAD_DIM] = _rms(o_a, gdn_w) * _silu(z)
        o_gate = proj[rows, OB + h * HEAD_DIM:OB + (h + 1) * HEAD_DIM]
        merged[rows, N_HEADS * HEAD_DIM + h * HEAD_DIM:N_HEADS * HEAD_DIM + (h + 1) * HEAD_DIM] = (
            _rms(h_b, ml_w) * _sigmoid(o_gate))

    @pl.when(i == n_steps - 1)
    def _():
        mix = jnp.dot(_bf(merged[...]), wout_ref[...], preferred_element_type=F32)
        y_ref[...] = _layer_norm(DN_ALPHA * x_ref[...] + mix, ln_ref[0:1, :], ln_ref[1:2, :])


def _ab_decode(x, win, wout, convw, gp, ln, cbuf, s0, c0, n0, m0):
    nb = x.shape[0]
    bb = SUBLANES
    assert nb % bb == 0
    n_steps = nb // bb
    const = lambda shape: pl.BlockSpec(shape, lambda i: (0,) * len(shape))
    blk = lambda shape: pl.BlockSpec((bb,) + shape, lambda i: (i,) + (0,) * len(shape))
    hist = (CONV_W - 1) * A_CONV
    width = N_HEADS * HEAD_DIM
    return pl.pallas_call(
        functools.partial(_ab_decode_kernel, bb=bb, n_steps=n_steps),
        grid=(n_steps,),
        in_specs=[
            const((nb, D_MODEL)), const((D_MODEL, D_IN_PAD)), const((2 * width, D_MODEL)),
            const((CONV_W, A_CONV)), const((SUBLANES, LANES)), const((2, D_MODEL)),
            blk((hist,)), blk((N_HEADS, HEAD_DIM, HEAD_DIM)), blk((N_HEADS, HEAD_DIM, HEAD_DIM)),
            blk((width,)), blk((N_HEADS,)),
        ],
        out_specs=[
            const((nb, D_MODEL)), blk((hist,)), blk((N_HEADS, HEAD_DIM, HEAD_DIM)),
            blk((N_HEADS, HEAD_DIM, HEAD_DIM)), blk((width,)), blk((N_HEADS,)),
        ],
        out_shape=[
            jax.ShapeDtypeStruct((nb, D_MODEL), F32),
            jax.ShapeDtypeStruct((nb, hist), F32),
            jax.ShapeDtypeStruct((nb, N_HEADS, HEAD_DIM, HEAD_DIM), F32),
            jax.ShapeDtypeStruct((nb, N_HEADS, HEAD_DIM, HEAD_DIM), F32),
            jax.ShapeDtypeStruct((nb, width), F32),
            jax.ShapeDtypeStruct((nb, N_HEADS), F32),
        ],
        scratch_shapes=[pltpu.VMEM((nb, D_IN_PAD), F32), pltpu.VMEM((nb, 2 * width), F32)],
        compiler_params=pltpu.CompilerParams(dimension_semantics=("arbitrary",), vmem_limit_bytes=VMEM_LIMIT),
        name="ab_decode",
    )(x, win, wout, convw, gp, ln, cbuf, s0, c0, n0, m0)


def _second_largest_sum(a, b, c, d):
    hi1, lo1 = jnp.maximum(a, b), jnp.minimum(a, b)
    hi2, lo2 = jnp.maximum(c, d), jnp.minimum(c, d)
    return jnp.maximum(hi1, hi2) + jnp.maximum(jnp.minimum(hi1, hi2), jnp.maximum(lo1, lo2))


def _first_argmax(vals):
    best_v = vals[0]
    best_i = jnp.zeros(vals[0].shape, jnp.int32)
    for j in range(1, len(vals)):
        better = vals[j] > best_v
        best_v = jnp.where(better, vals[j], best_v)
        best_i = jnp.where(better, j, best_i)
    return best_i


def _router_kernel(x_ref, rw_ref, rb_ref, meta_ref, cnt_ref, carry, *, n_steps):
    i = pl.program_id(0)
    tm = x_ref.shape[0]

    @pl.when(i == 0)
    def _():
        carry[...] = jnp.zeros_like(carry)

    logits = lax.dot_general(rw_ref[...], x_ref[...], (((1,), (1,)), ((), ())),
                             preferred_element_type=F32, precision=HIGHEST)
    ex = jnp.exp(logits - jnp.max(logits, axis=0, keepdims=True))
    probs = ex / jnp.sum(ex, axis=0, keepdims=True)
    sel = probs + rb_ref[...]
    p = [probs[j:j + 1, :] for j in range(N_EXPERTS)]
    s = [sel[j:j + 1, :] for j in range(N_EXPERTS)]
    scores = [_second_largest_sum(*s[EXPERTS_PER_GROUP * g:EXPERTS_PER_GROUP * (g + 1)])
              for g in range(N_EXPERT_GROUPS)]
    best = _first_argmax(scores)
    masked = [jnp.where(best == j // EXPERTS_PER_GROUP, s[j], -jnp.inf) for j in range(N_EXPERTS)]
    i1 = _first_argmax(masked)
    i2 = _first_argmax([jnp.where(i1 == j, -jnp.inf, masked[j]) for j in range(N_EXPERTS)])
    zero = jnp.zeros_like(p[0])
    p1 = functools.reduce(lambda a, b: a + b, [jnp.where(i1 == j, p[j], zero) for j in range(N_EXPERTS)])
    p2 = functools.reduce(lambda a, b: a + b, [jnp.where(i2 == j, p[j], zero) for j in range(N_EXPERTS)])
    tot = p1 + p2
    rows = [jnp.where(i1 == j, p1 / tot, zero) + jnp.where(i2 == j, p2 / tot, zero) for j in range(N_EXPERTS)]
    in_group = [best == g for g in range(N_EXPERT_GROUPS)]
    local = [functools.reduce(lambda a, b: a + b,
                              [jnp.where(in_group[g], rows[EXPERTS_PER_GROUP * g + e], zero)
                               for g in range(N_EXPERT_GROUPS)])
             for e in range(EXPERTS_PER_GROUP)]
    onehot = jnp.concatenate([jnp.where(m, 1.0, 0.0) for m in in_group]
                             + [jnp.zeros((SUBLANES - N_EXPERT_GROUPS, tm), F32)], axis=0)
    ri = lax.broadcasted_iota(jnp.int32, (tm, tm), 0)
    ci = lax.broadcasted_iota(jnp.int32, (tm, tm), 1)
    incl = jnp.dot(_bf(onehot), _bf(jnp.where(ri <= ci, 1.0, 0.0)), preferred_element_type=F32)
    prev = carry[...]
    rank = jnp.sum(onehot * (incl - 1.0 + prev[:, 0:1]), axis=0, keepdims=True)
    carry[...] = prev + incl[:, tm - 1:tm]
    meta_ref[...] = jnp.concatenate([best.astype(F32), rank] + local + [zero, zero], axis=0)

    @pl.when(i == n_steps - 1)
    def _():
        cnt_ref[...] = carry[...]


def _router(x, rw_t, rb_col):
    t = x.shape[0]
    tm = min(256, t)
    assert t % tm == 0
    n_steps = t // tm
    return pl.pallas_call(
        functools.partial(_router_kernel, n_steps=n_steps),
        grid=(n_steps,),
        in_specs=[pl.BlockSpec((tm, D_MODEL), lambda i: (i, 0)),
                  pl.BlockSpec((N_EXPERTS, D_MODEL), lambda i: (0, 0)),
                  pl.BlockSpec((N_EXPERTS, 1), lambda i: (0, 0))],
        out_specs=[pl.BlockSpec((SUBLANES, tm), lambda i: (0, i)),
                   pl.BlockSpec((SUBLANES, LANES), lambda i: (0, 0))],
        out_shape=[jax.ShapeDtypeStruct((SUBLANES, t), F32), jax.ShapeDtypeStruct((SUBLANES, LANES), F32)],
        scratch_shapes=[pltpu.VMEM((SUBLANES, LANES), F32)],
        compiler_params=pltpu.CompilerParams(dimension_semantics=("arbitrary",), vmem_limit_bytes=VMEM_LIMIT),
        name="router",
    )(x, rw_t, rb_col)


def _moe_kernel(tg_ref, nv_ref, tok_cur, tok_nxt, cw_ref, x_hbm, wg_ref, wu_ref, wd_ref, ln_ref,
                y_hbm, xg, stage, gsem, ssem, *, tm, n_tiles):
    i = pl.program_id(0)
    slot = lax.rem(i, 2)

    def row_gather(tok_ref, dst_slot):
        def body(r, c):
            pltpu.make_async_copy(x_hbm.at[pl.ds(tok_ref[0, r], 1), :], xg.at[dst_slot, pl.ds(r, 1), :],
                                  gsem.at[dst_slot]).start()
            return c
        lax.fori_loop(0, tm, body, 0, unroll=8)

    def scatter_wait(count, src_slot):
        for p in [1 << b for b in range(tm.bit_length())]:
            @pl.when(lax.bitwise_and(count, p) != 0)
            def _():
                pltpu.make_async_copy(stage.at[src_slot, pl.ds(0, p), :], y_hbm.at[pl.ds(0, p), :],
                                      ssem.at[src_slot]).wait()

    @pl.when(i == 0)
    def _():
        row_gather(tok_cur, 0)

    @pl.when(i + 1 < n_tiles)
    def _():
        row_gather(tok_nxt, 1 - slot)

    pltpu.make_async_copy(x_hbm.at[pl.ds(0, tm), :], xg.at[slot], gsem.at[slot]).wait()

    @pl.when(i >= 2)
    def _():
        scatter_wait(nv_ref[jnp.maximum(i - 2, 0)], slot)

    n_valid = nv_ref[i]

    @pl.when(n_valid > 0)
    def _():
        x = xg[slot]
        xb = _bf(x)
        cw = cw_ref[...].T
        acc = jnp.zeros((tm, D_MODEL), F32)
        for e in range(EXPERTS_PER_GROUP):
            g = jnp.dot(xb, wg_ref[e], preferred_element_type=F32)
            u = jnp.dot(xb, wu_ref[e], preferred_element_type=F32)
            hid = _silu(g) * u * cw[:, e:e + 1]
            acc = acc + jnp.dot(_bf(hid), wd_ref[e], preferred_element_type=F32)
        stage[slot] = _layer_norm(DN_ALPHA * x + acc, ln_ref[0:1, :], ln_ref[1:2, :])

        def body(r, c):
            pltpu.make_async_copy(stage.at[slot, pl.ds(r, 1), :], y_hbm.at[pl.ds(tok_cur[0, r], 1), :],
                                  ssem.at[slot]).start()
            return c
        lax.fori_loop(0, n_valid, body, 0)

    @pl.when(i == n_tiles - 1)
    def _():
        scatter_wait(nv_ref[jnp.maximum(i - 1, 0)], 1 - slot)
        scatter_wait(n_valid, slot)


def _moe(x, meta, counts, wg, wu, wd, ln, layer):
    t = x.shape[0]
    tm = min(256, t)
    assert t % tm == 0 and tm & (tm - 1) == 0
    n_tiles = t // tm + N_EXPERT_GROUPS
    n_slots = n_tiles * tm
    grp = meta[0].astype(jnp.int32)
    rank = meta[1].astype(jnp.int32)
    cnt = counts[:N_EXPERT_GROUPS, 0].astype(jnp.int32)
    padded = ((cnt + tm - 1) // tm) * tm
    g_end = jnp.cumsum(padded)
    g_off = g_end - padded
    pos = jnp.take(g_off, grp) + rank
    tok_of_slot = jnp.zeros((n_slots,), jnp.int32).at[pos].set(jnp.arange(t, dtype=jnp.int32))
    tile_start = jnp.arange(n_tiles, dtype=jnp.int32) * tm
    tile_grp = jnp.minimum(jnp.sum(tile_start[:, None] >= g_end[None, :], axis=1), N_EXPERT_GROUPS - 1).astype(jnp.int32)
    tile_valid = jnp.clip(jnp.take(g_off + cnt, tile_grp) - tile_start, 0, tm).astype(jnp.int32)
    cw_sorted = jnp.take(meta, tok_of_slot, axis=1)
    cw_sorted = jnp.roll(cw_sorted, -2, axis=0)
    tok3 = tok_of_slot.reshape(n_tiles, 1, tm)

    grid_spec = pltpu.PrefetchScalarGridSpec(
        num_scalar_prefetch=2,
        grid=(n_tiles,),
        in_specs=[
            pl.BlockSpec((None, 1, tm), lambda i, tg, nv: (i, 0, 0), memory_space=pltpu.SMEM),
            pl.BlockSpec((None, 1, tm), lambda i, tg, nv: (jnp.minimum(i + 1, n_tiles - 1), 0, 0),
                         memory_space=pltpu.SMEM),
            pl.BlockSpec((SUBLANES, tm), lambda i, tg, nv: (0, i)),
            pl.BlockSpec(memory_space=pl.ANY),
            pl.BlockSpec((None, EXPERTS_PER_GROUP, D_MODEL, D_FF), lambda i, tg, nv: (layer, tg[i], 0, 0)),
            pl.BlockSpec((None, EXPERTS_PER_GROUP, D_MODEL, D_FF), lambda i, tg, nv: (layer, tg[i], 0, 0)),
            pl.BlockSpec((None, EXPERTS_PER_GROUP, D_FF, D_MODEL), lambda i, tg, nv: (layer, tg[i], 0, 0)),
            pl.BlockSpec((2, D_MODEL), lambda i, tg, nv: (0, 0)),
        ],
        out_specs=pl.BlockSpec(memory_space=pl.ANY),
        scratch_shapes=[pltpu.VMEM((2, tm, D_MODEL), F32), pltpu.VMEM((2, tm, D_MODEL), F32),
                        pltpu.SemaphoreType.DMA((2,)), pltpu.SemaphoreType.DMA((2,))],
    )
    return pl.pallas_call(
        functools.partial(_moe_kernel, tm=tm, n_tiles=n_tiles),
        grid_spec=grid_spec,
        out_shape=jax.ShapeDtypeStruct((t, D_MODEL), F32),
        compiler_params=pltpu.CompilerParams(dimension_semantics=("arbitrary",), vmem_limit_bytes=VMEM_LIMIT),
        name="moe",
    )(tile_grp, tile_valid, tok3, tok3, cw_sorted, x, wg, wu, wd, ln)


def _s5_prep_kernel(are_ref, aim_ref, ldt_ref, bre_ref, bim_ref, abre_ref, abim_ref, bbre_ref, bbim_ref):
    a_r = are_ref[...]
    a_i = aim_ref[...]
    dt = jnp.exp(ldt_ref[...])
    mag = jnp.exp(dt * a_r)
    ab_re = mag * jnp.cos(dt * a_i)
    ab_im = mag * jnp.sin(dt * a_i)
    den = a_r * a_r + a_i * a_i
    nr = ab_re - 1.0
    z_re = (nr * a_r + ab_im * a_i) / den
    z_im = (ab_im * a_r - nr * a_i) / den
    abre_ref[...] = ab_re
    abim_ref[...] = ab_im
    bbre_ref[...] = z_re * bre_ref[...] - z_im * bim_ref[...]
    bbim_ref[...] = z_re * bim_ref[...] + z_im * bre_ref[...]


def _s5_prep(a_re, a_im, log_dt, b_re, b_im):
    n = S5_GROUPS * S5_STATE
    col = jax.ShapeDtypeStruct((n, 1), F32)
    mat = jax.ShapeDtypeStruct((n, S5_GROUP), F32)
    ldt = jnp.broadcast_to(log_dt[:, None], (S5_GROUPS, S5_STATE)).reshape(n, 1)
    return pl.pallas_call(_s5_prep_kernel, out_shape=[col, col, mat, mat], name="s5_prep")(
        a_re.reshape(n, 1), a_im.reshape(n, 1), ldt, b_re.reshape(n, S5_GROUP), b_im.reshape(n, S5_GROUP))


def _gelu_tanh(x):
    return 0.5 * x * (1.0 + jnp.tanh(0.7978845608028654 * (x + 0.044715 * (x * x * x))))


def _s5_kernel(x_ref, h0re_ref, h0im_ref, wre_ref, wim_ref, cre_ref, cim_ref, abre_ref, abim_ref, d_ref,
               gy_ref, hre_out, him_out, xtb, bu_re, bu_im, hre_s, him_s, *, nb, tt, n_t):
    t = pl.program_id(1)

    @pl.when(t == 0)
    def _():
        hre_s[...] = h0re_ref[...]
        him_s[...] = h0im_ref[...]

    for s in range(tt):
        xtb[s * nb:(s + 1) * nb, :] = x_ref[:, s, :]
    xv = xtb[...]
    xb = _bf(xv)
    bu_re[...] = jnp.dot(xb, wre_ref[...], preferred_element_type=F32)
    bu_im[...] = jnp.dot(xb, wim_ref[...], preferred_element_type=F32)

    a_re = jnp.broadcast_to(abre_ref[...], hre_s.shape)
    a_im = jnp.broadcast_to(abim_ref[...], hre_s.shape)

    def step(s, carry):
        h_re, h_im = carry
        rows = pl.ds(pl.multiple_of(s * nb, nb), nb)
        n_re = a_re * h_re - a_im * h_im + bu_re[rows, :]
        n_im = a_re * h_im + a_im * h_re + bu_im[rows, :]
        bu_re[rows, :] = n_re
        bu_im[rows, :] = n_im
        return n_re, n_im

    h_re, h_im = lax.fori_loop(0, tt, step, (hre_s[...], him_s[...]))
    hre_s[...] = h_re
    him_s[...] = h_im

    y = (jnp.dot(_bf(bu_re[...]), cre_ref[...], preferred_element_type=F32)
         - jnp.dot(_bf(bu_im[...]), cim_ref[...], preferred_element_type=F32)
         + d_ref[...] * xv)
    gy = _gelu_tanh(y)
    for s in range(tt):
        gy_ref[:, s, :] = gy[s * nb:(s + 1) * nb, :].astype(gy_ref.dtype)

    @pl.when(t == n_t - 1)
    def _():
        hre_out[...] = h_re
        him_out[...] = h_im


def _s5_scan(x, h0_re, h0_im, wre, wim, cre, cim, ab_re, ab_im, d_skip):
    nb, length, _ = x.shape
    tt = min(64, length)
    assert length % tt == 0 and nb % SUBLANES == 0
    n_t = length // tt
    n_k = D_MODEL // LANES
    sw = (LANES // S5_GROUP) * S5_STATE
    return pl.pallas_call(
        functools.partial(_s5_kernel, nb=nb, tt=tt, n_t=n_t),
        grid=(n_k, n_t),
        in_specs=[
            pl.BlockSpec((nb, tt, LANES), lambda k, t: (0, t, k)),
            pl.BlockSpec((nb, sw), lambda k, t: (0, k)), pl.BlockSpec((nb, sw), lambda k, t: (0, k)),
            pl.BlockSpec((None, LANES, sw), lambda k, t: (k, 0, 0)),
            pl.BlockSpec((None, LANES, sw), lambda k, t: (k, 0, 0)),
            pl.BlockSpec((None, sw, LANES), lambda k, t: (k, 0, 0)),
            pl.BlockSpec((None, sw, LANES), lambda k, t: (k, 0, 0)),
            pl.BlockSpec((1, sw), lambda k, t: (0, k)), pl.BlockSpec((1, sw), lambda k, t: (0, k)),
            pl.BlockSpec((1, LANES), lambda k, t: (0, k)),
        ],
        out_specs=[
            pl.BlockSpec((nb, tt, LANES), lambda k, t: (0, t, k)),
            pl.BlockSpec((nb, sw), lambda k, t: (0, k)), pl.BlockSpec((nb, sw), lambda k, t: (0, k)),
        ],
        out_shape=[
            jax.ShapeDtypeStruct((nb, length, D_MODEL), BF16),
            jax.ShapeDtypeStruct((nb, S5_GROUPS * S5_STATE), F32),
            jax.ShapeDtypeStruct((nb, S5_GROUPS * S5_STATE), F32),
        ],
        scratch_shapes=[
            pltpu.VMEM((nb * tt, LANES), F32),
            pltpu.VMEM((nb * tt, sw), F32), pltpu.VMEM((nb * tt, sw), F32),
            pltpu.VMEM((nb, sw), F32), pltpu.VMEM((nb, sw), F32),
        ],
        compiler_params=pltpu.CompilerParams(
            dimension_semantics=("arbitrary", "arbitrary"), vmem_limit_bytes=VMEM_LIMIT),
        name="s5_scan",
    )(x, h0_re, h0_im, wre, wim, cre, cim, ab_re, ab_im, d_skip)


def _glu_ln_kernel(gy_ref, h_ref, wa_ref, wb_ref, ln_ref, o_ref):
    gy = gy_ref[...]
    a = jnp.dot(gy, wa_ref[...], preferred_element_type=F32)
    b = jnp.dot(gy, wb_ref[...], preferred_element_type=F32)
    o_ref[...] = _layer_norm(DN_ALPHA * h_ref[...] + a * _sigmoid(b), ln_ref[0:1, :], ln_ref[1:2, :])


def _glu_ln(gy, h, wa, wb, ln):
    t = h.shape[0]
    tm = min(512, t)
    assert t % tm == 0
    return pl.pallas_call(
        _glu_ln_kernel,
        grid=(t // tm,),
        in_specs=[pl.BlockSpec((tm, D_MODEL), lambda i: (i, 0)), pl.BlockSpec((tm, D_MODEL), lambda i: (i, 0)),
                  pl.BlockSpec((D_MODEL, D_MODEL), lambda i: (0, 0)), pl.BlockSpec((D_MODEL, D_MODEL), lambda i: (0, 0)),
                  pl.BlockSpec((2, D_MODEL), lambda i: (0, 0))],
        out_specs=pl.BlockSpec((tm, D_MODEL), lambda i: (i, 0)),
        out_shape=jax.ShapeDtypeStruct((t, D_MODEL), F32),
        compiler_params=pltpu.CompilerParams(dimension_semantics=("arbitrary",), vmem_limit_bytes=VMEM_LIMIT),
        name="glu_ln",
    )(gy, h, wa, wb, ln)


def _block_diag_slices(m, rows_per_group, cols_per_group):
    gps = LANES // S5_GROUP
    m = m.reshape(S5_GROUPS // gps, gps, rows_per_group, cols_per_group)
    eye = jnp.eye(gps, dtype=m.dtype)
    out = m[:, :, :, None, :] * eye[None, :, None, :, None]
    return out.reshape(S5_GROUPS // gps, gps * rows_per_group, gps * cols_per_group)


def _prepare(p):
    w = p['w_in'][0]
    win = jnp.concatenate(
        [w[:, 0:1536], w[:, 1544:2056], w[:, 2056:3592], w[:, 3600:4112], w[:, 1536:1544], w[:, 3592:3600],
         jnp.zeros((D_MODEL, D_IN_PAD - 4112), w.dtype)], axis=1).astype(BF16)
    gp = jnp.zeros((SUBLANES, LANES), F32)
    gp = gp.at[0, G_DEC:G_DEC + N_HEADS].set(p['gdn_A_log'][0])
    gp = gp.at[1, G_DEC:G_DEC + N_HEADS].set(p['gdn_dt_bias'][0])
    gp = gp.at[2, G_IN:G_IN + N_HEADS].set(p['ml_b_i'][0])
    gp = gp.at[3, G_FG:G_FG + N_HEADS].set(p['ml_b_f'][0])
    gp = gp.at[4, :].set(p['gdn_norm_w'][0])
    gp = gp.at[5, :].set(p['ml_norm_w'][0])
    ab_re, ab_im, bb_re, bb_im = _s5_prep(p['s5_A_re'][0], p['s5_A_im'][0], p['s5_log_dt'][0],
                                          p['s5_B_re'][0], p['s5_B_im'][0])
    to_in = lambda bb: _block_diag_slices(
        bb.reshape(S5_GROUPS, S5_STATE, S5_GROUP).transpose(0, 2, 1), S5_GROUP, S5_STATE).astype(BF16)
    to_out = lambda c: _block_diag_slices(c.transpose(0, 2, 1), S5_STATE, S5_GROUP).astype(BF16)
    return dict(
        win=win, wout=p['w_out'][0].astype(BF16), convw=p['gdn_conv_w'][0], gp=gp,
        ln_mix=[jnp.stack([p['ln_mix_g'][l], p['ln_mix_b'][l]]) for l in range(DEPTH)],
        ln_ffn=[jnp.stack([p['ln_ffn_g'][l], p['ln_ffn_b'][l]]) for l in range(DEPTH)],
        rw_t=p['router_w'].T, rb_col=p['router_b'][:, None],
        wg=p['moe_w_gate'].astype(BF16), wu=p['moe_w_up'].astype(BF16), wd=p['moe_w_down'].astype(BF16),
        s5_wre=to_in(bb_re), s5_wim=to_in(bb_im),
        s5_cre=to_out(p['s5_C_re'][0]), s5_cim=to_out(p['s5_C_im'][0]),
        s5_abre=ab_re.reshape(1, -1), s5_abim=ab_im.reshape(1, -1), s5_d=p['s5_D'][0][None, :],
        glu_a=p['s5_w_glu_a'][0].astype(BF16), glu_b=p['s5_w_glu_b'][0].astype(BF16),
    )


def _ffn(h, w, layer):
    meta, counts = _router(h, w['rw_t'], w['rb_col'])
    return _moe(h, meta, counts, w['wg'], w['wu'], w['wd'], w['ln_ffn'][layer], layer)


def _s5_layer(h, nb, length, h0_re, h0_im, w):
    gy, re, im = _s5_scan(h.reshape(nb, length, D_MODEL), h0_re, h0_im, w['s5_wre'], w['s5_wim'],
                          w['s5_cre'], w['s5_cim'], w['s5_abre'], w['s5_abim'], w['s5_d'])
    h = _glu_ln(gy.reshape(nb * length, D_MODEL), h, w['glu_a'], w['glu_b'], w['ln_mix'][1])
    return h, re, im


def kernel(x_prompt, x_sample, state_gdn_conv, state_gdn_S, state_mlstm_C, state_mlstm_n, state_mlstm_m,
           state_s5_re, state_s5_im, w_in, gdn_conv_w, gdn_A_log, gdn_dt_bias, gdn_norm_w, ml_b_i, ml_b_f,
           ml_norm_w, w_out, s5_A_re, s5_A_im, s5_log_dt, s5_B_re, s5_B_im, s5_C_re, s5_C_im, s5_D,
           s5_w_glu_a, s5_w_glu_b, router_w, router_b, moe_w_gate, moe_w_up, moe_w_down,
           ln_mix_g, ln_mix_b, ln_ffn_g, ln_ffn_b):
    w = _prepare(dict(
        w_in=w_in, gdn_conv_w=gdn_conv_w, gdn_A_log=gdn_A_log, gdn_dt_bias=gdn_dt_bias, gdn_norm_w=gdn_norm_w,
        ml_b_i=ml_b_i, ml_b_f=ml_b_f, ml_norm_w=ml_norm_w, w_out=w_out, s5_A_re=s5_A_re, s5_A_im=s5_A_im,
        s5_log_dt=s5_log_dt, s5_B_re=s5_B_re, s5_B_im=s5_B_im, s5_C_re=s5_C_re, s5_C_im=s5_C_im, s5_D=s5_D,
        s5_w_glu_a=s5_w_glu_a, s5_w_glu_b=s5_w_glu_b, router_w=router_w, router_b=router_b,
        moe_w_gate=moe_w_gate, moe_w_up=moe_w_up, moe_w_down=moe_w_down,
        ln_mix_g=ln_mix_g, ln_mix_b=ln_mix_b, ln_ffn_g=ln_ffn_g, ln_ffn_b=ln_ffn_b))
    bp, lp, _ = x_prompt.shape
    bs, ls, _ = x_sample.shape
    assert ls == 1
    n_state = S5_GROUPS * S5_STATE

    h, p_hist, p_s, p_c, p_n, p_m = _ab_prompt(x_prompt, w['win'], w['wout'], w['convw'], w['gp'], w['ln_mix'][0])
    h = _ffn(h.reshape(bp * lp, D_MODEL), w, 0)
    zeros = jnp.zeros((bp, n_state), F32)
    h, p_re, p_im = _s5_layer(h, bp, lp, zeros, zeros, w)
    y_prompt = _ffn(h, w, 1).reshape(bp, lp, D_MODEL)

    hs, s_cbuf, s_s, s_c, s_n, s_m = _ab_decode(
        x_sample[:, 0], w['win'], w['wout'], w['convw'], w['gp'], w['ln_mix'][0],
        state_gdn_conv.reshape(bs, (CONV_W - 1) * A_CONV), state_gdn_S[:, 0], state_mlstm_C[:, 0],
        state_mlstm_n.reshape(bs, N_HEADS * HEAD_DIM), state_mlstm_m[:, 0])
    hs = _ffn(hs, w, 0)
    hs, s_re, s_im = _s5_layer(hs, bs, 1, state_s5_re.reshape(bs, n_state), state_s5_im.reshape(bs, n_state), w)
    y_sample = _ffn(hs, w, 1).reshape(bs, 1, D_MODEL)

    grp = lambda a, n: a.reshape(n, 1, S5_GROUPS, S5_STATE)
    return (
        y_prompt, y_sample,
        p_hist[:, None, SUBLANES - (CONV_W - 1):, :], p_s[:, None], p_c[:, None],
        p_n[:, None, :, 0, :], p_m[:, None, :, 0, 0], grp(p_re, bp), grp(p_im, bp),
        s_cbuf.reshape(bs, 1, CONV_W - 1, A_CONV), s_s[:, None], s_c[:, None],
        s_n.reshape(bs, 1, N_HEADS, HEAD_DIM), s_m[:, None], grp(s_re, bs), grp(s_im, bs),
    )
```

```python
import functools

import jax
import jax.numpy as jnp
from jax import lax
from jax.experimental import pallas as pl
from jax.experimental.pallas import tpu as pltpu

F32 = jnp.float32
BF16 = jnp.bfloat16
HIGHEST = lax.Precision.HIGHEST

D_MODEL = 1024
DEPTH = 2
N_HEADS = 4
HEAD_DIM = 128
CONV_W = 4
CHUNK = 64
A_CONV = 3 * N_HEADS * HEAD_DIM
S5_GROUP = 16
S5_GROUPS = D_MODEL // S5_GROUP
S5_STATE = 64
N_EXPERTS = 16
EXPERTS_PER_GROUP = 4
N_EXPERT_GROUPS = N_EXPERTS // EXPERTS_PER_GROUP
D_FF = 512
DN_ALPHA = (2 * DEPTH) ** 0.25
LN_EPS = 1e-5
RMS_EPS = 1e-6
NEG_BIG = -1e30

QA, KA, VA, ZA, QB, KB, VB, OB, GT = 0, 512, 1024, 1536, 2048, 2560, 3072, 3584, 4096
D_IN_PAD = 4224
G_DEC, G_BETA, G_IN, G_FG = 0, 4, 8, 12

LANES = 128
SUBLANES = 8
VMEM_LIMIT = 56 * 1024 * 1024


def _bf(x):
    return x.astype(BF16)


def _nn(a, b):
    return jnp.dot(_bf(a), _bf(b), preferred_element_type=F32)


def _nt(a, b):
    return lax.dot_general(_bf(a), _bf(b), (((1,), (1,)), ((), ())), preferred_element_type=F32)


def _tn(a, b):
    return lax.dot_general(_bf(a), _bf(b), (((0,), (0,)), ((), ())), preferred_element_type=F32)


def _sigmoid(x):
    return 1.0 / (1.0 + jnp.exp(-x))


def _softplus(x):
    return jnp.maximum(x, 0.0) + jnp.log(1.0 + jnp.exp(-jnp.abs(x)))


def _silu(x):
    return x * _sigmoid(x)


def _layer_norm(y, g, b):
    mu = jnp.mean(y, axis=-1, keepdims=True)
    yc = y - mu
    var = jnp.mean(yc * yc, axis=-1, keepdims=True)
    return yc * lax.rsqrt(var + LN_EPS) * g + b


def _rms(x, w):
    return x * lax.rsqrt(jnp.mean(x * x, axis=-1, keepdims=True) + RMS_EPS) * w


def _gate_transform(raw, gp):
    lane = lax.broadcasted_iota(jnp.int32, raw.shape, 1)
    dec = -jnp.exp(gp[0:1, :]) * _softplus(raw + gp[1:2, :])
    beta = _sigmoid(raw)
    ipre = raw + gp[2:3, :]
    logf = -_softplus(-(raw + gp[3:4, :]))
    return jnp.where(lane < G_BETA, dec,
                     jnp.where(lane < G_IN, beta,
                               jnp.where(lane < G_FG, ipre,
                                         jnp.where(lane < G_FG + N_HEADS, logf, 0.0))))


def _bnn(a, b):
    return lax.dot_general(_bf(a), _bf(b), (((2,), (1,)), ((0,), (0,))), preferred_element_type=F32)


def _bnt(a, b):
    return lax.dot_general(_bf(a), _bf(b), (((2,), (2,)), ((0,), (0,))), preferred_element_type=F32)


def _btn(a, b):
    return lax.dot_general(_bf(a), _bf(b), (((1,), (1,)), ((0,), (0,))), preferred_element_type=F32)


def _unit_lower_inverse_minus_eye(a):
    c = a.shape[-1]
    r = -a
    steps = max(1, (c - 1).bit_length()) - 1
    q = _bnn(r, r)
    for i in range(steps):
        rq = _bnn(r, q)
        qq = _bnn(q, q) if i + 1 < steps else None
        r = r + q + rq
        q = qq
    return r


def _ab_prompt_kernel(x_ref, win_ref, wout_ref, convw_ref, gp_ref, ln_ref,
                      h_ref, hist_ref, s_out, c_out, n_out, m_out,
                      proj, qkv, gates, gcum, merged, s_s, c_s, n_s, m_s,
                      u_s, w_s, attn_s, dlog_s, qk_s, dmax_s, *, tb, n_t):
    t = pl.program_id(1)
    nc = tb // CHUNK

    @pl.when(t == 0)
    def _():
        proj[0:SUBLANES, :] = jnp.zeros((SUBLANES, D_IN_PAD), F32)
        s_s[...] = jnp.zeros_like(s_s)
        c_s[...] = jnp.zeros_like(c_s)
        n_s[...] = jnp.zeros_like(n_s)
        m_s[...] = jnp.zeros_like(m_s)

    x = x_ref[...]
    proj[SUBLANES:SUBLANES + tb, :] = jnp.dot(_bf(x), win_ref[...], preferred_element_type=F32)

    for blk in range(A_CONV // LANES):
        cs = slice(blk * LANES, (blk + 1) * LANES)
        acc = proj[SUBLANES:SUBLANES + tb, cs] * convw_ref[CONV_W - 1:CONV_W, cs]
        for j in range(1, CONV_W):
            acc = acc + proj[SUBLANES - j:SUBLANES - j + tb, cs] * convw_ref[CONV_W - 1 - j:CONV_W - j, cs]
        y = _silu(acc)
        if blk < 2 * N_HEADS:
            y = y * lax.rsqrt(jnp.sum(y * y, axis=-1, keepdims=True) + RMS_EPS)
            if blk < N_HEADS:
                y = y * HEAD_DIM ** -0.5
        qkv[:, cs] = y

    gt = _gate_transform(proj[SUBLANES:SUBLANES + tb, GT:GT + LANES], gp_ref[...])
    gates[...] = gt
    ri = lax.broadcasted_iota(jnp.int32, (tb, tb), 0)
    ci = lax.broadcasted_iota(jnp.int32, (tb, tb), 1)
    same_chunk = lax.shift_right_logical(ri, 6) == lax.shift_right_logical(ci, 6)
    ltri = jnp.where(same_chunk, jnp.where(ri >= ci, 1.0, 0.0), 0.0)
    gcum[...] = jnp.dot(ltri, gt, preferred_element_type=F32, precision=HIGHEST)

    ii = lax.broadcasted_iota(jnp.int32, (CHUNK, CHUNK), 0)
    jj = lax.broadcasted_iota(jnp.int32, (CHUNK, CHUNK), 1)
    incl = ii >= jj
    strict = ii > jj
    gdn_w = gp_ref[4:5, :]
    ml_w = gp_ref[5:6, :]

    pairs = [(c, h) for c in range(nc) for h in range(N_HEADS)]

    def tile_heads(ref, row0, col0):
        return jnp.stack([ref[row0 + c * CHUNK:row0 + (c + 1) * CHUNK, col0 + h * HEAD_DIM:col0 + (h + 1) * HEAD_DIM]
                          for c, h in pairs])

    def tile_cols(ref, lane0):
        return jnp.stack([ref[c * CHUNK:(c + 1) * CHUNK, lane0 + h:lane0 + h + 1] for c, h in pairs])

    def tile_rows(transposed, lane0):
        return jnp.stack([transposed[c][lane0 + h:lane0 + h + 1, :] for c, h in pairs])

    cs_t = [gcum[c * CHUNK:(c + 1) * CHUNK, :].T for c in range(nc)]
    gt_t = [gates[c * CHUNK:(c + 1) * CHUNK, :].T for c in range(nc)]
    q3 = tile_heads(qkv, 0, QA)
    k3 = tile_heads(qkv, 0, KA)
    v3 = tile_heads(qkv, 0, VA)
    g_col3 = tile_cols(gcum, G_DEC)
    beta3 = tile_cols(gates, G_BETA)
    decay3 = jnp.where(incl, jnp.exp(jnp.where(incl, g_col3 - tile_rows(cs_t, G_DEC), 0.0)), 0.0)
    kb3 = k3 * beta3
    a_low3 = jnp.where(strict, _bnt(kb3, k3) * decay3, 0.0)
    attn_s[...] = _bnt(q3, k3) * decay3
    qk_s[...] = _bnt(tile_heads(proj, SUBLANES, QB), tile_heads(proj, SUBLANES, KB) * HEAD_DIM ** -0.5)
    r3 = _unit_lower_inverse_minus_eye(a_low3)
    rhs3 = jnp.concatenate([v3 * beta3, kb3 * jnp.exp(g_col3)], axis=2)
    uw3 = rhs3 + _bnn(r3, rhs3)
    u_s[...] = uw3[:, :, :HEAD_DIM]
    w_s[...] = uw3[:, :, HEAD_DIM:]
    dlog3 = jnp.where(incl, tile_cols(gcum, G_FG) - tile_rows(cs_t, G_FG) + tile_rows(gt_t, G_IN), NEG_BIG)
    dlog_s[...] = dlog3
    dmax_s[...] = jnp.max(dlog3, axis=-1, keepdims=True)

    def chunk_body(c, carry):
        r0 = pl.multiple_of(c * CHUNK, CHUNK)
        rows = pl.ds(r0, CHUNK)
        prow = pl.ds(pl.multiple_of(r0 + SUBLANES, SUBLANES), CHUNK)
        last = pl.ds(r0 + CHUNK - 1, 1)
        pc = pl.ds(pl.multiple_of(c * N_HEADS, N_HEADS), N_HEADS)

        def heads(ref, rws, col0):
            return jnp.stack([ref[rws, col0 + h * HEAD_DIM:col0 + (h + 1) * HEAD_DIM] for h in range(N_HEADS)])

        def cols(ref, rws, lane0):
            return jnp.stack([ref[rws, lane0 + h:lane0 + h + 1] for h in range(N_HEADS)])

        q = heads(qkv, rows, QA)
        k = heads(qkv, rows, KA)
        g_col = cols(gcum, rows, G_DEC)
        g_last = cols(gcum, last, G_DEC)
        qb = heads(proj, prow, QB)
        kbm = heads(proj, prow, KB) * HEAD_DIM ** -0.5
        vbm = heads(proj, prow, VB)
        b_col = cols(gcum, rows, G_FG)
        b_last = cols(gcum, last, G_FG)
        i_col = cols(gates, rows, G_IN)
        s_old = s_s[...]
        c_old = c_s[...]
        n_old = n_s[:, 0:1, :]
        m_old = m_s[:, 0:1, 0:1]
        inter = b_col + m_old
        mt = jnp.maximum(inter, dmax_s[pc])
        wts = jnp.exp(dlog_s[pc] - mt) * qk_s[pc]
        sc = jnp.exp(inter - mt)
        m_new = mt[:, CHUNK - 1:CHUNK, :]
        sd = jnp.exp(b_last + m_old - m_new)
        kw = kbm * jnp.exp(b_last - b_col + i_col - m_new)
        w_state = _bnn(w_s[pc], s_old)
        q_state = _bnn(q * jnp.exp(g_col), s_old)
        q_mem = _bnn(qb, c_old)
        w_val = _bnn(wts, vbm)
        kv = _btn(kw, vbm)
        v_new = u_s[pc] - w_state
        o_a = q_state + _bnn(attn_s[pc], v_new)
        s_s[...] = s_old * jnp.exp(g_last) + _btn(k * jnp.exp(g_last - g_col), v_new)
        num = sc * q_mem + w_val
        den = sc * jnp.sum(qb * n_old, axis=-1, keepdims=True) + jnp.sum(wts, axis=-1, keepdims=True)
        h_b = num / jnp.maximum(jnp.abs(den), jnp.exp(-mt))
        c_s[...] = sd * c_old + kv
        n_s[...] = jnp.broadcast_to(sd * n_old + jnp.sum(kw, axis=1, keepdims=True), n_s.shape)
        m_s[...] = jnp.broadcast_to(m_new, m_s.shape)
        o_n = _rms(o_a, gdn_w)
        h_n = _rms(h_b, ml_w)
        for h in range(N_HEADS):
            hs = slice(h * HEAD_DIM, (h + 1) * HEAD_DIM)
            hs2 = slice((N_HEADS + h) * HEAD_DIM, (N_HEADS + h + 1) * HEAD_DIM)
            merged[rows, hs] = o_n[h] * _silu(proj[prow, ZA + h * HEAD_DIM:ZA + (h + 1) * HEAD_DIM])
            merged[rows, hs2] = h_n[h] * _sigmoid(proj[prow, OB + h * HEAD_DIM:OB + (h + 1) * HEAD_DIM])
        return carry

    lax.fori_loop(0, nc, chunk_body, 0)

    mix = jnp.dot(_bf(merged[...]), wout_ref[...], preferred_element_type=F32)
    h_ref[...] = _layer_norm(DN_ALPHA * x + mix, ln_ref[0:1, :], ln_ref[1:2, :])

    proj[0:SUBLANES, 0:A_CONV] = proj[tb:tb + SUBLANES, 0:A_CONV]

    @pl.when(t == n_t - 1)
    def _():
        hist_ref[...] = proj[tb:tb + SUBLANES, 0:A_CONV]
        s_out[...] = s_s[...]
        c_out[...] = c_s[...]
        n_out[...] = n_s[...]
        m_out[...] = m_s[...]


def _ab_prompt(x, win, wout, convw, gp, ln):
    bsz, length, _ = x.shape
    tb = min(256, length)
    assert length % tb == 0 and tb % CHUNK == 0 and length >= SUBLANES
    n_t = length // tb
    n_pairs = (tb // CHUNK) * N_HEADS
    const = lambda shape: pl.BlockSpec(shape, lambda b, t: (0,) * len(shape))
    per_b =lambda shape: pl.BlockSpec((None,) + shape, lambda b, t: (b,) + (0,) * len(shape))
    return pl.pallas_call(
        functools.partial(_ab_prompt_kernel, tb=tb, n_t=n_t),
        grid=(bsz, n_t),
        in_specs=[
            pl.BlockSpec((None, tb, D_MODEL), lambda b, t: (b, t, 0)),
            const((D_MODEL, D_IN_PAD)), const((2 * N_HEADS * HEAD_DIM, D_MODEL)),
            const((CONV_W, A_CONV)), const((SUBLANES, LANES)), const((2, D_MODEL)),
        ],
        out_specs=[
            pl.BlockSpec((None, tb, D_MODEL), lambda b, t: (b, t, 0)),
            per_b((SUBLANES, A_CONV)), per_b((N_HEADS, HEAD_DIM, HEAD_DIM)), per_b((N_HEADS, HEAD_DIM, HEAD_DIM)),
            per_b((N_HEADS, SUBLANES, LANES)), per_b((N_HEADS, SUBLANES, LANES)),
        ],
        out_shape=[
            jax.ShapeDtypeStruct((bsz, length, D_MODEL), F32),
            jax.ShapeDtypeStruct((bsz, SUBLANES, A_CONV), F32),
            jax.ShapeDtypeStruct((bsz, N_HEADS, HEAD_DIM, HEAD_DIM), F32),
            jax.ShapeDtypeStruct((bsz, N_HEADS, HEAD_DIM, HEAD_DIM), F32),
            jax.ShapeDtypeStruct((bsz, N_HEADS, SUBLANES, LANES), F32),
            jax.ShapeDtypeStruct((bsz, N_HEADS, SUBLANES, LANES), F32),
        ],
        scratch_shapes=[
            pltpu.VMEM((tb + SUBLANES, D_IN_PAD), F32),
            pltpu.VMEM((tb, A_CONV), F32),
            pltpu.VMEM((tb, LANES), F32),
            pltpu.VMEM((tb, LANES), F32),
            pltpu.VMEM((tb, 2 * N_HEADS * HEAD_DIM), F32),
            pltpu.VMEM((N_HEADS, HEAD_DIM, HEAD_DIM), F32),
            pltpu.VMEM((N_HEADS, HEAD_DIM, HEAD_DIM), F32),
            pltpu.VMEM((N_HEADS, SUBLANES, LANES), F32),
            pltpu.VMEM((N_HEADS, SUBLANES, LANES), F32),
            pltpu.VMEM((n_pairs, CHUNK, HEAD_DIM), F32),
            pltpu.VMEM((n_pairs, CHUNK, HEAD_DIM), F32),
            pltpu.VMEM((n_pairs, CHUNK, CHUNK), F32),
            pltpu.VMEM((n_pairs, CHUNK, CHUNK), F32),
            pltpu.VMEM((n_pairs, CHUNK, CHUNK), F32),
            pltpu.VMEM((n_pairs, CHUNK, 1), F32),
        ],
        compiler_params=pltpu.CompilerParams(
            dimension_semantics=("arbitrary", "arbitrary"), vmem_limit_bytes=VMEM_LIMIT),
        name="ab_prompt",
    )(x, win, wout, convw, gp, ln)


def _ab_decode_kernel(x_ref, win_ref, wout_ref, convw_ref, gp_ref, ln_ref, cbuf_ref, s_in, c_in, n_in, m_in,
                      y_ref, cbuf_out, s_out, c_out, n_out, m_out, proj, merged, *, bb, n_steps):
    i = pl.program_id(0)

    @pl.when(i == 0)
    def _():
        proj[...] = jnp.dot(_bf(x_ref[...]), win_ref[...], preferred_element_type=F32)

    rows = pl.ds(pl.multiple_of(i * bb, bb), bb)
    raw = proj[rows, 0:A_CONV]
    cbuf = cbuf_ref[...]
    conv = raw * convw_ref[CONV_W - 1:CONV_W, :]
    for j in range(CONV_W - 1):
        conv = conv + cbuf[:, j * A_CONV:(j + 1) * A_CONV] * convw_ref[j:j + 1, :]
    cbuf_out[:, 0:(CONV_W - 2) * A_CONV] = cbuf[:, A_CONV:(CONV_W - 1) * A_CONV]
    cbuf_out[:, (CONV_W - 2) * A_CONV:(CONV_W - 1) * A_CONV] = raw
    act = _silu(conv)
    gt = _gate_transform(proj[rows, GT:GT + LANES], gp_ref[...])
    gdn_w = gp_ref[4:5, :]
    ml_w = gp_ref[5:6, :]
    m_all = m_in[...]

    for h in range(N_HEADS):
        def head(off):
            return act[:, off + h * HEAD_DIM:off + (h + 1) * HEAD_DIM]
        q = head(QA)
        q = q * lax.rsqrt(jnp.sum(q * q, axis=-1, keepdims=True) + RMS_EPS) * HEAD_DIM ** -0.5
        k = head(KA)
        k = k * lax.rsqrt(jnp.sum(k * k, axis=-1, keepdims=True) + RMS_EPS)
        v = head(VA)
        q_t = q.T
        k_t = k.T
        qk = jnp.sum(q * k, axis=-1, keepdims=True)
        qb = proj[rows, QB + h * HEAD_DIM:QB + (h + 1) * HEAD_DIM]
        kbm = proj[rows, KB + h * HEAD_DIM:KB + (h + 1) * HEAD_DIM] * HEAD_DIM ** -0.5
        vbm = proj[rows, VB + h * HEAD_DIM:VB + (h + 1) * HEAD_DIM]
        qb_t = qb.T
        kb_t = kbm.T
        qkb = jnp.sum(qb * kbm, axis=-1, keepdims=True)
        n_old = n_in[:, h * HEAD_DIM:(h + 1) * HEAD_DIM]
        qn = jnp.sum(qb * n_old, axis=-1, keepdims=True)
        o_rows = []
        hb_rows = []
        n_rows = []
        m_rows = []
        for b in range(bb):
            s_old = s_in[b, h]
            k_c = k_t[:, b:b + 1]
            q_c = q_t[:, b:b + 1]
            e_g = jnp.exp(gt[b:b + 1, G_DEC + h:G_DEC + h + 1])
            beta = gt[b:b + 1, G_BETA + h:G_BETA + h + 1]
            k_s = jnp.sum(k_c * s_old, axis=0, keepdims=True)
            q_s = jnp.sum(q_c * s_old, axis=0, keepdims=True)
            v_new = v[b:b + 1, :] * beta - (beta * e_g) * k_s
            o_rows.append(e_g * q_s + qk[b:b + 1, :] * v_new)
            s_out[b, h] = s_old * e_g + k_c * v_new
            c_old = c_in[b, h]
            m_old = m_all[b:b + 1, h:h + 1]
            i_pre = gt[b:b + 1, G_IN + h:G_IN + h + 1]
            logf = gt[b:b + 1, G_FG + h:G_FG + h + 1]
            inter = logf + m_old
            mt = jnp.maximum(inter, i_pre)
            w_in = jnp.exp(i_pre - mt)
            sc = jnp.exp(inter - mt)
            wts = w_in * qkb[b:b + 1, :]
            q_cm = jnp.sum(qb_t[:, b:b + 1] * c_old, axis=0, keepdims=True)
            num = sc * q_cm + wts * vbm[b:b + 1, :]
            den = sc * qn[b:b + 1, :] + wts
            hb_rows.append(num / jnp.maximum(jnp.abs(den), jnp.exp(-mt)))
            c_out[b, h] = sc * c_old + (w_in * kb_t[:, b:b + 1]) * vbm[b:b + 1, :]
            n_rows.append(sc * n_old[b:b + 1, :] + w_in * kbm[b:b + 1, :])
            m_rows.append(mt)
        o_a = jnp.concatenate(o_rows, axis=0)
        h_b = jnp.concatenate(hb_rows, axis=0)
        n_out[:, h * HEAD_DIM:(h + 1) * HEAD_DIM] = jnp.concatenate(n_rows, axis=0)
        m_out[:, h:h + 1] = jnp.concatenate(m_rows, axis=0)
        z = proj[rows, ZA + h * HEAD_DIM:ZA + (h + 1) * HEAD_DIM]
        merged[rows, h * HEAD_DIM:(h + 1) * HEAD_DIM] = _rms(o_a, gdn_w) * _silu(z)
        o_gate = proj[rows, OB + h * HEAD_DIM:OB + (h + 1) * HEAD_DIM]
        merged[rows, N_HEADS * HEAD_DIM + h * HEAD_DIM:N_HEADS * HEAD_DIM + (h + 1) * HEAD_DIM] = (
            _rms(h_b, ml_w) * _sigmoid(o_gate))

    @pl.when(i == n_steps - 1)
    def _():
        mix = jnp.dot(_bf(merged[...]), wout_ref[...], preferred_element_type=F32)
        y_ref[...] = _layer_norm(DN_ALPHA * x_ref[...] + mix, ln_ref[0:1, :], ln_ref[1:2, :])


def _ab_decode(x, win, wout, convw, gp, ln, cbuf, s0, c0, n0, m0):
    nb = x.shape[0]
    bb = SUBLANES
    assert nb % bb == 0
    n_steps = nb // bb
    const = lambda shape: pl.BlockSpec(shape, lambda i: (0,) * len(shape))
    blk = lambda shape: pl.BlockSpec((bb,) + shape, lambda i: (i,) + (0,) * len(shape))
    hist = (CONV_W - 1) * A_CONV
    width = N_HEADS * HEAD_DIM
    return pl.pallas_call(
        functools.partial(_ab_decode_kernel, bb=bb, n_steps=n_steps),
        grid=(n_steps,),
        in_specs=[
            const((nb, D_MODEL)), const((D_MODEL, D_IN_PAD)), const((2 * width, D_MODEL)),
            const((CONV_W, A_CONV)), const((SUBLANES, LANES)), const((2, D_MODEL)),
            blk((hist,)), blk((N_HEADS, HEAD_DIM, HEAD_DIM)), blk((N_HEADS, HEAD_DIM, HEAD_DIM)),
            blk((width,)), blk((N_HEADS,)),
        ],
        out_specs=[
            const((nb, D_MODEL)), blk((hist,)), blk((N_HEADS, HEAD_DIM, HEAD_DIM)),
            blk((N_HEADS, HEAD_DIM, HEAD_DIM)), blk((width,)), blk((N_HEADS,)),
        ],
        out_shape=[
            jax.ShapeDtypeStruct((nb, D_MODEL), F32),
            jax.ShapeDtypeStruct((nb, hist), F32),
            jax.ShapeDtypeStruct((nb, N_HEADS, HEAD_DIM, HEAD_DIM), F32),
            jax.ShapeDtypeStruct((nb, N_HEADS, HEAD_DIM, HEAD_DIM), F32),
            jax.ShapeDtypeStruct((nb, width), F32),
            jax.ShapeDtypeStruct((nb, N_HEADS), F32),
        ],
        scratch_shapes=[pltpu.VMEM((nb, D_IN_PAD), F32), pltpu.VMEM((nb, 2 * width), F32)],
        compiler_params=pltpu.CompilerParams(dimension_semantics=("arbitrary",), vmem_limit_bytes=VMEM_LIMIT),
        name="ab_decode",
    )(x, win, wout, convw, gp, ln, cbuf, s0, c0, n0, m0)


def _second_largest_sum(a, b, c, d):
    hi1, lo1 = jnp.maximum(a, b), jnp.minimum(a, b)
    hi2, lo2 = jnp.maximum(c, d), jnp.minimum(c, d)
    return jnp.maximum(hi1, hi2) + jnp.maximum(jnp.minimum(hi1, hi2), jnp.maximum(lo1, lo2))


def _first_argmax(vals):
    best_v = vals[0]
    best_i = jnp.zeros(vals[0].shape, jnp.int32)
    for j in range(1, len(vals)):
        better = vals[j] > best_v
        best_v = jnp.where(better, vals[j], best_v)
        best_i = jnp.where(better, j, best_i)
    return best_i


def _router_kernel(x_ref, rw_ref, rb_ref, meta_ref, cnt_ref, carry, *, n_steps):
    i = pl.program_id(0)
    tm = x_ref.shape[0]

    @pl.when(i == 0)
    def _():
        carry[...] = jnp.zeros_like(carry)

    logits = lax.dot_general(rw_ref[...], x_ref[...], (((1,), (1,)), ((), ())),
                             preferred_element_type=F32, precision=HIGHEST)
    ex = jnp.exp(logits - jnp.max(logits, axis=0, keepdims=True))
    probs = ex / jnp.sum(ex, axis=0, keepdims=True)
    sel = probs + rb_ref[...]
    p = [probs[j:j + 1, :] for j in range(N_EXPERTS)]
    s = [sel[j:j + 1, :] for j in range(N_EXPERTS)]
    scores = [_second_largest_sum(*s[EXPERTS_PER_GROUP * g:EXPERTS_PER_GROUP * (g + 1)])
              for g in range(N_EXPERT_GROUPS)]
    best = _first_argmax(scores)
    masked = [jnp.where(best == j // EXPERTS_PER_GROUP, s[j], -jnp.inf) for j in range(N_EXPERTS)]
    i1 = _first_argmax(masked)
    i2 = _first_argmax([jnp.where(i1 == j, -jnp.inf, masked[j]) for j in range(N_EXPERTS)])
    zero = jnp.zeros_like(p[0])
    p1 = functools.reduce(lambda a, b: a + b, [jnp.where(i1 == j, p[j], zero) for j in range(N_EXPERTS)])
    p2 = functools.reduce(lambda a, b: a + b, [jnp.where(i2 == j, p[j], zero) for j in range(N_EXPERTS)])
    tot = p1 + p2
    rows = [jnp.where(i1 == j, p1 / tot, zero) + jnp.where(i2 == j, p2 / tot, zero) for j in range(N_EXPERTS)]
    in_group = [best == g for g in range(N_EXPERT_GROUPS)]
    local = [functools.reduce(lambda a, b: a + b,
                              [jnp.where(in_group[g], rows[EXPERTS_PER_GROUP * g + e], zero)
                               for g in range(N_EXPERT_GROUPS)])
             for e in range(EXPERTS_PER_GROUP)]
    onehot = jnp.concatenate([jnp.where(m, 1.0, 0.0) for m in in_group]
                             + [jnp.zeros((SUBLANES - N_EXPERT_GROUPS, tm), F32)], axis=0)
    ri = lax.broadcasted_iota(jnp.int32, (tm, tm), 0)
    ci = lax.broadcasted_iota(jnp.int32, (tm, tm), 1)
    incl = jnp.dot(_bf(onehot), _bf(jnp.where(ri <= ci, 1.0, 0.0)), preferred_element_type=F32)
    prev = carry[...]
    rank = jnp.sum(onehot * (incl - 1.0 + prev[:, 0:1]), axis=0, keepdims=True)
    carry[...] = prev + incl[:, tm - 1:tm]
    meta_ref[...] = jnp.concatenate([best.astype(F32), rank] + local + [zero, zero], axis=0)

    @pl.when(i == n_steps - 1)
    def _():
        cnt_ref[...] = carry[...]


def _router(x, rw_t, rb_col):
    t = x.shape[0]
    tm = min(256, t)
    assert t % tm == 0
    n_steps = t // tm
    return pl.pallas_call(
        functools.partial(_router_kernel, n_steps=n_steps),
        grid=(n_steps,),
        in_specs=[pl.BlockSpec((tm, D_MODEL), lambda i: (i, 0)),
                  pl.BlockSpec((N_EXPERTS, D_MODEL), lambda i: (0, 0)),
                  pl.BlockSpec((N_EXPERTS, 1), lambda i: (0, 0))],
        out_specs=[pl.BlockSpec((SUBLANES, tm), lambda i: (0, i)),
                   pl.BlockSpec((SUBLANES, LANES), lambda i: (0, 0))],
        out_shape=[jax.ShapeDtypeStruct((SUBLANES, t), F32), jax.ShapeDtypeStruct((SUBLANES, LANES), F32)],
        scratch_shapes=[pltpu.VMEM((SUBLANES, LANES), F32)],
        compiler_params=pltpu.CompilerParams(dimension_semantics=("arbitrary",), vmem_limit_bytes=VMEM_LIMIT),
        name="router",
    )(x, rw_t, rb_col)


def _moe_kernel(tg_ref, nv_ref, tok_prv, tok_cur, tok_nxt, cw_ref, x_hbm, wg_ref, wu_ref, wd_ref, ln_ref,
                y_hbm, xg, stage, gsem, ssem, *, tm, n_tiles):
    i = pl.program_id(0)
    slot = lax.rem(i, 2)
    other = 1 - slot
    n_valid = nv_ref[i]
    n_prev = jnp.where(i >= 1, nv_ref[jnp.maximum(i - 1, 0)], 0)
    has_next = i + 1 < n_tiles
    part = tm // EXPERTS_PER_GROUP

    def gather_row(tok_ref, r, dst_slot):
        return pltpu.make_async_copy(x_hbm.at[pl.ds(tok_ref[0, r], 1), :], xg.at[dst_slot, pl.ds(r, 1), :],
                                     gsem.at[dst_slot])

    def scatter_row(tok_ref, r, src_slot):
        return pltpu.make_async_copy(stage.at[src_slot, pl.ds(r, 1), :], y_hbm.at[pl.ds(tok_ref[0, r], 1), :],
                                     ssem.at[src_slot])

    def start_neighbour_rows(lo, hi):
        for r in range(lo, hi):
            @pl.when(has_next)
            def _():
                gather_row(tok_nxt, r, other).start()

            @pl.when(r < n_prev)
            def _():
                scatter_row(tok_prv, r, other).start()

    def scatter_wait(count, src_slot):
        for p in [1 << b for b in range(tm.bit_length())]:
            @pl.when(lax.bitwise_and(count, p) != 0)
            def _():
                pltpu.make_async_copy(stage.at[src_slot, pl.ds(0, p), :], y_hbm.at[pl.ds(0, p), :],
                                      ssem.at[src_slot]).wait()

    @pl.when(i == 0)
    def _():
        def body(r, c):
            gather_row(tok_cur, r, 0).start()
            return c
        lax.fori_loop(0, tm, body, 0, unroll=8)

    pltpu.make_async_copy(x_hbm.at[pl.ds(0, tm), :], xg.at[slot], gsem.at[slot]).wait()

    @pl.when(i >= 2)
    def _():
        scatter_wait(nv_ref[jnp.maximum(i - 2, 0)], slot)

    @pl.when(n_valid > 0)
    def _():
        x = xg[slot]
        xb = _bf(x)
        cw = cw_ref[...].T
        acc = jnp.zeros((tm, D_MODEL), F32)
        for e in range(EXPERTS_PER_GROUP):
            g = jnp.dot(xb, wg_ref[e], preferred_element_type=F32)
            u = jnp.dot(xb, wu_ref[e], preferred_element_type=F32)
            hid = _silu(g) * u * cw[:, e:e + 1]
            acc = acc + jnp.dot(_bf(hid), wd_ref[e], preferred_element_type=F32)
            start_neighbour_rows(e * part, (e + 1) * part)
        stage[slot] = _layer_norm(DN_ALPHA * x + acc, ln_ref[0:1, :], ln_ref[1:2, :])

    @pl.when(n_valid <= 0)
    def _():
        start_neighbour_rows(0, tm)

    @pl.when(i == n_tiles - 1)
    def _():
        def body(r, c):
            scatter_row(tok_cur, r, slot).start()
            return c
        lax.fori_loop(0, n_valid, body, 0)
        scatter_wait(n_prev, other)
        scatter_wait(n_valid, slot)


def _moe(x, meta, counts, wg, wu, wd, ln, layer):
    t = x.shape[0]
    tm = min(256, t)
    assert t % tm == 0 and tm & (tm - 1) == 0
    n_tiles = t // tm + N_EXPERT_GROUPS
    n_slots = n_tiles * tm
    grp = meta[0].astype(jnp.int32)
    rank = meta[1].astype(jnp.int32)
    cnt = counts[:N_EXPERT_GROUPS, 0].astype(jnp.int32)
    padded = ((cnt + tm - 1) // tm) * tm
    g_end = jnp.cumsum(padded)
    g_off = g_end - padded
    pos = jnp.take(g_off, grp) + rank
    tok_of_slot = jnp.zeros((n_slots,), jnp.int32).at[pos].set(jnp.arange(t, dtype=jnp.int32))
    tile_start = jnp.arange(n_tiles, dtype=jnp.int32) * tm
    tile_grp = jnp.minimum(jnp.sum(tile_start[:, None] >= g_end[None, :], axis=1), N_EXPERT_GROUPS - 1).astype(jnp.int32)
    tile_valid = jnp.clip(jnp.take(g_off + cnt, tile_grp) - tile_start, 0, tm).astype(jnp.int32)
    cw_sorted = jnp.take(meta, tok_of_slot, axis=1)
    cw_sorted = jnp.roll(cw_sorted, -2, axis=0)
    tok3 = tok_of_slot.reshape(n_tiles, 1, tm)

    grid_spec = pltpu.PrefetchScalarGridSpec(
        num_scalar_prefetch=2,
        grid=(n_tiles,),
        in_specs=[
            pl.BlockSpec((None, 1, tm), lambda i, tg, nv: (jnp.maximum(i - 1, 0), 0, 0), memory_space=pltpu.SMEM),
            pl.BlockSpec((None, 1, tm), lambda i, tg, nv: (i, 0, 0), memory_space=pltpu.SMEM),
            pl.BlockSpec((None, 1, tm), lambda i, tg, nv: (jnp.minimum(i + 1, n_tiles - 1), 0, 0),
                         memory_space=pltpu.SMEM),
            pl.BlockSpec((SUBLANES, tm), lambda i, tg, nv: (0, i)),
            pl.BlockSpec(memory_space=pl.ANY),
            pl.BlockSpec((None, EXPERTS_PER_GROUP, D_MODEL, D_FF), lambda i, tg, nv: (layer, tg[i], 0, 0)),
            pl.BlockSpec((None, EXPERTS_PER_GROUP, D_MODEL, D_FF), lambda i, tg, nv: (layer, tg[i], 0, 0)),
            pl.BlockSpec((None, EXPERTS_PER_GROUP, D_FF, D_MODEL), lambda i, tg, nv: (layer, tg[i], 0, 0)),
            pl.BlockSpec((2, D_MODEL), lambda i, tg, nv: (0, 0)),
        ],
        out_specs=pl.BlockSpec(memory_space=pl.ANY),
        scratch_shapes=[pltpu.VMEM((2, tm, D_MODEL), F32), pltpu.VMEM((2, tm, D_MODEL), F32),
                        pltpu.SemaphoreType.DMA((2,)), pltpu.SemaphoreType.DMA((2,))],
    )
    return pl.pallas_call(
        functools.partial(_moe_kernel, tm=tm, n_tiles=n_tiles),
        grid_spec=grid_spec,
        out_shape=jax.ShapeDtypeStruct((t, D_MODEL), F32),
        compiler_params=pltpu.CompilerParams(dimension_semantics=("arbitrary",), vmem_limit_bytes=VMEM_LIMIT),
        name="moe",
    )(tile_grp, tile_valid, tok3, tok3, tok3, cw_sorted, x, wg, wu, wd, ln)


def _s5_prep_kernel(are_ref, aim_ref, ldt_ref, bre_ref, bim_ref, abre_ref, abim_ref, bbre_ref, bbim_ref):
    a_r = are_ref[...]
    a_i = aim_ref[...]
    dt = jnp.exp(ldt_ref[...])
    mag = jnp.exp(dt * a_r)
    ab_re = mag * jnp.cos(dt * a_i)
    ab_im = mag * jnp.sin(dt * a_i)
    den = a_r * a_r + a_i * a_i
    nr = ab_re - 1.0
    z_re = (nr * a_r + ab_im * a_i) / den
    z_im = (ab_im * a_r - nr * a_i) / den
    abre_ref[...] = ab_re
    abim_ref[...] = ab_im
    bbre_ref[...] = z_re * bre_ref[...] - z_im * bim_ref[...]
    bbim_ref[...] = z_re * bim_ref[...] + z_im * bre_ref[...]


def _s5_prep(a_re, a_im, log_dt, b_re, b_im):
    n = S5_GROUPS * S5_STATE
    col = jax.ShapeDtypeStruct((n, 1), F32)
    mat = jax.ShapeDtypeStruct((n, S5_GROUP), F32)
    ldt = jnp.broadcast_to(log_dt[:, None], (S5_GROUPS, S5_STATE)).reshape(n, 1)
    return pl.pallas_call(_s5_prep_kernel, out_shape=[col, col, mat, mat], name="s5_prep")(
        a_re.reshape(n, 1), a_im.reshape(n, 1), ldt, b_re.reshape(n, S5_GROUP), b_im.reshape(n, S5_GROUP))


def _gelu_tanh(x):
    return 0.5 * x * (1.0 + jnp.tanh(0.7978845608028654 * (x + 0.044715 * (x * x * x))))


def _s5_kernel(x_ref, h0re_ref, h0im_ref, wre_ref, wim_ref, cre_ref, cim_ref, abre_ref, abim_ref, d_ref,
               gy_ref, hre_out, him_out, xtb, bu_re, bu_im, hre_s, him_s, *, nb, tt, n_t):
    t = pl.program_id(1)

    @pl.when(t == 0)
    def _():
        hre_s[...] = h0re_ref[...]
        him_s[...] = h0im_ref[...]

    for s in range(tt):
        xtb[s * nb:(s + 1) * nb, :] = x_ref[:, s, :]
    xv = xtb[...]
    xb = _bf(xv)
    bu_re[...] = jnp.dot(xb, wre_ref[...], preferred_element_type=F32)
    bu_im[...] = jnp.dot(xb, wim_ref[...], preferred_element_type=F32)

    a_re = jnp.broadcast_to(abre_ref[...], hre_s.shape)
    a_im = jnp.broadcast_to(abim_ref[...], hre_s.shape)

    def step(s, carry):
        h_re, h_im = carry
        rows = pl.ds(pl.multiple_of(s * nb, nb), nb)
        n_re = a_re * h_re - a_im * h_im + bu_re[rows, :]
        n_im = a_re * h_im + a_im * h_re + bu_im[rows, :]
        bu_re[rows, :] = n_re
        bu_im[rows, :] = n_im
        return n_re, n_im

    h_re, h_im = lax.fori_loop(0, tt, step, (hre_s[...], him_s[...]))
    hre_s[...] = h_re
    him_s[...] = h_im

    y = (jnp.dot(_bf(bu_re[...]), cre_ref[...], preferred_element_type=F32)
         - jnp.dot(_bf(bu_im[...]), cim_ref[...], preferred_element_type=F32)
         + d_ref[...] * xv)
    gy = _gelu_tanh(y)
    for s in range(tt):
        gy_ref[:, s, :] = gy[s * nb:(s + 1) * nb, :].astype(gy_ref.dtype)

    @pl.when(t == n_t - 1)
    def _():
        hre_out[...] = h_re
        him_out[...] = h_im


def _s5_scan(x, h0_re, h0_im, wre, wim, cre, cim, ab_re, ab_im, d_skip):
    nb, length, _ = x.shape
    tt = min(64, length)
    assert length % tt == 0 and nb % SUBLANES == 0
    n_t = length // tt
    n_k = D_MODEL // LANES
    sw = (LANES // S5_GROUP) * S5_STATE
    return pl.pallas_call(
        functools.partial(_s5_kernel, nb=nb, tt=tt, n_t=n_t),
        grid=(n_k, n_t),
        in_specs=[
            pl.BlockSpec((nb, tt, LANES), lambda k, t: (0, t, k)),
            pl.BlockSpec((nb, sw), lambda k, t: (0, k)), pl.BlockSpec((nb, sw), lambda k, t: (0, k)),
            pl.BlockSpec((None, LANES, sw), lambda k, t: (k, 0, 0)),
            pl.BlockSpec((None, LANES, sw), lambda k, t: (k, 0, 0)),
            pl.BlockSpec((None, sw, LANES), lambda k, t: (k, 0, 0)),
            pl.BlockSpec((None, sw, LANES), lambda k, t: (k, 0, 0)),
            pl.BlockSpec((1, sw), lambda k, t: (0, k)), pl.BlockSpec((1, sw), lambda k, t: (0, k)),
            pl.BlockSpec((1, LANES), lambda k, t: (0, k)),
        ],
        out_specs=[
            pl.BlockSpec((nb, tt, LANES), lambda k, t: (0, t, k)),
            pl.BlockSpec((nb, sw), lambda k, t: (0, k)), pl.BlockSpec((nb, sw), lambda k, t: (0, k)),
        ],
        out_shape=[
            jax.ShapeDtypeStruct((nb, length, D_MODEL), BF16),
            jax.ShapeDtypeStruct((nb, S5_GROUPS * S5_STATE), F32),
            jax.ShapeDtypeStruct((nb, S5_GROUPS * S5_STATE), F32),
        ],
        scratch_shapes=[
            pltpu.VMEM((nb * tt, LANES), F32),
            pltpu.VMEM((nb * tt, sw), F32), pltpu.VMEM((nb * tt, sw), F32),
            pltpu.VMEM((nb, sw), F32), pltpu.VMEM((nb, sw), F32),
        ],
        compiler_params=pltpu.CompilerParams(
            dimension_semantics=("arbitrary", "arbitrary"), vmem_limit_bytes=VMEM_LIMIT),
        name="s5_scan",
    )(x, h0_re, h0_im, wre, wim, cre, cim, ab_re, ab_im, d_skip)


def _glu_ln_kernel(gy_ref, h_ref, wa_ref, wb_ref, ln_ref, o_ref):
    gy = gy_ref[...]
    a = jnp.dot(gy, wa_ref[...], preferred_element_type=F32)
    b = jnp.dot(gy, wb_ref[...], preferred_element_type=F32)
    o_ref[...] = _layer_norm(DN_ALPHA * h_ref[...] + a * _sigmoid(b), ln_ref[0:1, :], ln_ref[1:2, :])


def _glu_ln(gy, h, wa, wb, ln):
    t = h.shape[0]
    tm = min(512, t)
    assert t % tm == 0
    return pl.pallas_call(
        _glu_ln_kernel,
        grid=(t // tm,),
        in_specs=[pl.BlockSpec((tm, D_MODEL), lambda i: (i, 0)), pl.BlockSpec((tm, D_MODEL), lambda i: (i, 0)),
                  pl.BlockSpec((D_MODEL, D_MODEL), lambda i: (0, 0)), pl.BlockSpec((D_MODEL, D_MODEL), lambda i: (0, 0)),
                  pl.BlockSpec((2, D_MODEL), lambda i: (0, 0))],
        out_specs=pl.BlockSpec((tm, D_MODEL), lambda i: (i, 0)),
        out_shape=jax.ShapeDtypeStruct((t, D_MODEL), F32),
        compiler_params=pltpu.CompilerParams(dimension_semantics=("arbitrary",), vmem_limit_bytes=VMEM_LIMIT),
        name="glu_ln",
    )(gy, h, wa, wb, ln)


def _block_diag_slices(m, rows_per_group, cols_per_group):
    gps = LANES // S5_GROUP
    m = m.reshape(S5_GROUPS // gps, gps, rows_per_group, cols_per_group)
    eye = jnp.eye(gps, dtype=m.dtype)
    out = m[:, :, :, None, :] * eye[None, :, None, :, None]
    return out.reshape(S5_GROUPS // gps, gps * rows_per_group, gps * cols_per_group)


def _prepare(p):
    w = p['w_in'][0]
    win = jnp.concatenate(
        [w[:, 0:1536], w[:, 1544:2056], w[:, 2056:3592], w[:, 3600:4112], w[:, 1536:1544], w[:, 3592:3600],
         jnp.zeros((D_MODEL, D_IN_PAD - 4112), w.dtype)], axis=1).astype(BF16)
    gp = jnp.zeros((SUBLANES, LANES), F32)
    gp = gp.at[0, G_DEC:G_DEC + N_HEADS].set(p['gdn_A_log'][0])
    gp = gp.at[1, G_DEC:G_DEC + N_HEADS].set(p['gdn_dt_bias'][0])
    gp = gp.at[2, G_IN:G_IN + N_HEADS].set(p['ml_b_i'][0])
    gp = gp.at[3, G_FG:G_FG + N_HEADS].set(p['ml_b_f'][0])
    gp = gp.at[4, :].set(p['gdn_norm_w'][0])
    gp = gp.at[5, :].set(p['ml_norm_w'][0])
    ab_re, ab_im, bb_re, bb_im = _s5_prep(p['s5_A_re'][0], p['s5_A_im'][0], p['s5_log_dt'][0],
                                          p['s5_B_re'][0], p['s5_B_im'][0])
    to_in = lambda bb: _block_diag_slices(
        bb.reshape(S5_GROUPS, S5_STATE, S5_GROUP).transpose(0, 2, 1), S5_GROUP, S5_STATE).astype(BF16)
    to_out = lambda c: _block_diag_slices(c.transpose(0, 2, 1), S5_STATE, S5_GROUP).astype(BF16)
    return dict(
        win=win, wout=p['w_out'][0].astype(BF16), convw=p['gdn_conv_w'][0], gp=gp,
        ln_mix=[jnp.stack([p['ln_mix_g'][l], p['ln_mix_b'][l]]) for l in range(DEPTH)],
        ln_ffn=[jnp.stack([p['ln_ffn_g'][l], p['ln_ffn_b'][l]]) for l in range(DEPTH)],
        rw_t=p['router_w'].T, rb_col=p['router_b'][:, None],
        wg=p['moe_w_gate'].astype(BF16), wu=p['moe_w_up'].astype(BF16), wd=p['moe_w_down'].astype(BF16),
        s5_wre=to_in(bb_re), s5_wim=to_in(bb_im),
        s5_cre=to_out(p['s5_C_re'][0]), s5_cim=to_out(p['s5_C_im'][0]),
        s5_abre=ab_re.reshape(1, -1), s5_abim=ab_im.reshape(1, -1), s5_d=p['s5_D'][0][None, :],
        glu_a=p['s5_w_glu_a'][0].astype(BF16), glu_b=p['s5_w_glu_b'][0].astype(BF16),
    )


def _ffn(h, w, layer):
    meta, counts = _router(h, w['rw_t'], w['rb_col'])
    return _moe(h, meta, counts, w['wg'], w['wu'], w['wd'], w['ln_ffn'][layer], layer)


def _s5_layer(h, nb, length, h0_re, h0_im, w):
    gy, re, im = _s5_scan(h.reshape(nb, length, D_MODEL), h0_re, h0_im, w['s5_wre'], w['s5_wim'],
                          w['s5_cre'], w['s5_cim'], w['s5_abre'], w['s5_abim'], w['s5_d'])
    h = _glu_ln(gy.reshape(nb * length, D_MODEL), h, w['glu_a'], w['glu_b'], w['ln_mix'][1])
    return h, re, im


def kernel(x_prompt, x_sample, state_gdn_conv, state_gdn_S, state_mlstm_C, state_mlstm_n, state_mlstm_m,
           state_s5_re, state_s5_im, w_in, gdn_conv_w, gdn_A_log, gdn_dt_bias, gdn_norm_w, ml_b_i, ml_b_f,
           ml_norm_w, w_out, s5_A_re, s5_A_im, s5_log_dt, s5_B_re, s5_B_im, s5_C_re, s5_C_im, s5_D,
           s5_w_glu_a, s5_w_glu_b, router_w, router_b, moe_w_gate, moe_w_up, moe_w_down,
           ln_mix_g, ln_mix_b, ln_ffn_g, ln_ffn_b):
    w = _prepare(dict(
        w_in=w_in, gdn_conv_w=gdn_conv_w, gdn_A_log=gdn_A_log, gdn_dt_bias=gdn_dt_bias, gdn_norm_w=gdn_norm_w,
        ml_b_i=ml_b_i, ml_b_f=ml_b_f, ml_norm_w=ml_norm_w, w_out=w_out, s5_A_re=s5_A_re, s5_A_im=s5_A_im,
        s5_log_dt=s5_log_dt, s5_B_re=s5_B_re, s5_B_im=s5_B_im, s5_C_re=s5_C_re, s5_C_im=s5_C_im, s5_D=s5_D,
        s5_w_glu_a=s5_w_glu_a, s5_w_glu_b=s5_w_glu_b, router_w=router_w, router_b=router_b,
        moe_w_gate=moe_w_gate, moe_w_up=moe_w_up, moe_w_down=moe_w_down,
        ln_mix_g=ln_mix_g, ln_mix_b=ln_mix_b, ln_ffn_g=ln_ffn_g, ln_ffn_b=ln_ffn_b))
    bp, lp, _ = x_prompt.shape
    bs, ls, _ = x_sample.shape
    assert ls == 1
    n_state = S5_GROUPS * S5_STATE

    h, p_hist, p_s, p_c, p_n, p_m = _ab_prompt(x_prompt, w['win'], w['wout'], w['convw'], w['gp'], w['ln_mix'][0])
    h = _ffn(h.reshape(bp * lp, D_MODEL), w, 0)
    zeros = jnp.zeros((bp, n_state), F32)
    h, p_re, p_im = _s5_layer(h, bp, lp, zeros, zeros, w)
    y_prompt = _ffn(h, w, 1).reshape(bp, lp, D_MODEL)

    hs, s_cbuf, s_s, s_c, s_n, s_m = _ab_decode(
        x_sample[:, 0], w['win'], w['wout'], w['convw'], w['gp'], w['ln_mix'][0],
        state_gdn_conv.reshape(bs, (CONV_W - 1) * A_CONV), state_gdn_S[:, 0], state_mlstm_C[:, 0],
        state_mlstm_n.reshape(bs, N_HEADS * HEAD_DIM), state_mlstm_m[:, 0])
    hs = _ffn(hs, w, 0)
    hs, s_re, s_im = _s5_layer(hs, bs, 1, state_s5_re.reshape(bs, n_state), state_s5_im.reshape(bs, n_state), w)
    y_sample = _ffn(hs, w, 1).reshape(bs, 1, D_MODEL)

    grp = lambda a, n: a.reshape(n, 1, S5_GROUPS, S5_STATE)
    return (
        y_prompt, y_sample,
        p_hist[:, None, SUBLANES - (CONV_W - 1):, :], p_s[:, None], p_c[:, None],
        p_n[:, None, :, 0, :], p_m[:, None, :, 0, 0], grp(p_re, bp), grp(p_im, bp),
        s_cbuf.reshape(bs, 1, CONV_W - 1, A_CONV), s_s[:, None], s_c[:, None],
        s_n.reshape(bs, 1, N_HEADS, HEAD_DIM), s_m[:, None], grp(s_re, bs), grp(s_im, bs),
    )
```

```python
import functools

import jax
import jax.numpy as jnp
from jax import lax
from jax.experimental import pallas as pl
from jax.experimental.pallas import tpu as pltpu

F32 = jnp.float32
BF16 = jnp.bfloat16
HIGHEST = lax.Precision.HIGHEST

D_MODEL = 1024
DEPTH = 2
N_HEADS = 4
HEAD_DIM = 128
CONV_W = 4
CHUNK = 64
A_CONV = 3 * N_HEADS * HEAD_DIM
S5_GROUP = 16
S5_GROUPS = D_MODEL // S5_GROUP
S5_STATE = 64
N_EXPERTS = 16
EXPERTS_PER_GROUP = 4
N_EXPERT_GROUPS = N_EXPERTS // EXPERTS_PER_GROUP
D_FF = 512
DN_ALPHA = (2 * DEPTH) ** 0.25
LN_EPS = 1e-5
RMS_EPS = 1e-6
NEG_BIG = -1e30

QA, KA, VA, ZA, QB, KB, VB, OB, GT = 0, 512, 1024, 1536, 2048, 2560, 3072, 3584, 4096
D_IN_PAD = 4224
G_DEC, G_BETA, G_IN, G_FG = 0, 4, 8, 12

S5_TT = 64
S5_UNROLL = 64

LANES = 128
SUBLANES = 8
VMEM_LIMIT = 56 * 1024 * 1024


def _bf(x):
    return x.astype(BF16)


def _nn(a, b):
    return jnp.dot(_bf(a), _bf(b), preferred_element_type=F32)


def _nt(a, b):
    return lax.dot_general(_bf(a), _bf(b), (((1,), (1,)), ((), ())), preferred_element_type=F32)


def _tn(a, b):
    return lax.dot_general(_bf(a), _bf(b), (((0,), (0,)), ((), ())), preferred_element_type=F32)


def _sigmoid(x):
    return 1.0 / (1.0 + jnp.exp(-x))


def _softplus(x):
    return jnp.maximum(x, 0.0) + jnp.log(1.0 + jnp.exp(-jnp.abs(x)))


def _silu(x):
    return x * _sigmoid(x)


def _layer_norm(y, g, b):
    mu = jnp.mean(y, axis=-1, keepdims=True)
    yc = y - mu
    var = jnp.mean(yc * yc, axis=-1, keepdims=True)
    return yc * lax.rsqrt(var + LN_EPS) * g + b


def _rms(x, w):
    return x * lax.rsqrt(jnp.mean(x * x, axis=-1, keepdims=True) + RMS_EPS) * w


def _gate_transform(raw, gp):
    lane = lax.broadcasted_iota(jnp.int32, raw.shape, 1)
    dec = -jnp.exp(gp[0:1, :]) * _softplus(raw + gp[1:2, :])
    beta = _sigmoid(raw)
    ipre = raw + gp[2:3, :]
    logf = -_softplus(-(raw + gp[3:4, :]))
    return jnp.where(lane < G_BETA, dec,
                     jnp.where(lane < G_IN, beta,
                               jnp.where(lane < G_FG, ipre,
                                         jnp.where(lane < G_FG + N_HEADS, logf, 0.0))))


def _bnn(a, b):
    return lax.dot_general(_bf(a), _bf(b), (((2,), (1,)), ((0,), (0,))), preferred_element_type=F32)


def _bnt(a, b):
    return lax.dot_general(_bf(a), _bf(b), (((2,), (2,)), ((0,), (0,))), preferred_element_type=F32)


def _btn(a, b):
    return lax.dot_general(_bf(a), _bf(b), (((1,), (1,)), ((0,), (0,))), preferred_element_type=F32)


def _unit_lower_inverse_minus_eye(a):
    c = a.shape[-1]
    r = -a
    steps = max(1, (c - 1).bit_length()) - 1
    q = _bnn(r, r)
    for i in range(steps):
        rq = _bnn(r, q)
        qq = _bnn(q, q) if i + 1 < steps else None
        r = r + q + rq
        q = qq
    return r


def _ab_prompt_kernel(x_ref, win_ref, wout_ref, convw_ref, gp_ref, ln_ref,
                      h_ref, hist_ref, s_out, c_out, n_out, m_out,
                      proj, qkv, gates, gcum, merged, s_s, c_s, n_s, m_s,
                      u_s, w_s, attn_s, dlog_s, qk_s, dmax_s, *, tb, n_t):
    t = pl.program_id(1)
    nc = tb // CHUNK

    @pl.when(t == 0)
    def _():
        proj[0:SUBLANES, :] = jnp.zeros((SUBLANES, D_IN_PAD), F32)
        s_s[...] = jnp.zeros_like(s_s)
        c_s[...] = jnp.zeros_like(c_s)
        n_s[...] = jnp.zeros_like(n_s)
        m_s[...] = jnp.zeros_like(m_s)

    x = x_ref[...]
    proj[SUBLANES:SUBLANES + tb, :] = jnp.dot(_bf(x), win_ref[...], preferred_element_type=F32)

    for blk in range(A_CONV // LANES):
        cs = slice(blk * LANES, (blk + 1) * LANES)
        acc = proj[SUBLANES:SUBLANES + tb, cs] * convw_ref[CONV_W - 1:CONV_W, cs]
        for j in range(1, CONV_W):
            acc = acc + proj[SUBLANES - j:SUBLANES - j + tb, cs] * convw_ref[CONV_W - 1 - j:CONV_W - j, cs]
        y = _silu(acc)
        if blk < 2 * N_HEADS:
            y = y * lax.rsqrt(jnp.sum(y * y, axis=-1, keepdims=True) + RMS_EPS)
            if blk < N_HEADS:
                y = y * HEAD_DIM ** -0.5
        qkv[:, cs] = y

    gt = _gate_transform(proj[SUBLANES:SUBLANES + tb, GT:GT + LANES], gp_ref[...])
    gates[...] = gt
    ri = lax.broadcasted_iota(jnp.int32, (tb, tb), 0)
    ci = lax.broadcasted_iota(jnp.int32, (tb, tb), 1)
    same_chunk = lax.shift_right_logical(ri, 6) == lax.shift_right_logical(ci, 6)
    ltri = jnp.where(same_chunk, jnp.where(ri >= ci, 1.0, 0.0), 0.0)
    gcum[...] = jnp.dot(ltri, gt, preferred_element_type=F32, precision=HIGHEST)

    ii = lax.broadcasted_iota(jnp.int32, (CHUNK, CHUNK), 0)
    jj = lax.broadcasted_iota(jnp.int32, (CHUNK, CHUNK), 1)
    incl = ii >= jj
    strict = ii > jj
    gdn_w = gp_ref[4:5, :]
    ml_w = gp_ref[5:6, :]

    pairs = [(c, h) for c in range(nc) for h in range(N_HEADS)]

    def tile_heads(ref, row0, col0):
        return jnp.stack([ref[row0 + c * CHUNK:row0 + (c + 1) * CHUNK, col0 + h * HEAD_DIM:col0 + (h + 1) * HEAD_DIM]
                          for c, h in pairs])

    def tile_cols(ref, lane0):
        return jnp.stack([ref[c * CHUNK:(c + 1) * CHUNK, lane0 + h:lane0 + h + 1] for c, h in pairs])

    def tile_rows(transposed, lane0):
        return jnp.stack([transposed[c][lane0 + h:lane0 + h + 1, :] for c, h in pairs])

    cs_t = [gcum[c * CHUNK:(c + 1) * CHUNK, :].T for c in range(nc)]
    gt_t = [gates[c * CHUNK:(c + 1) * CHUNK, :].T for c in range(nc)]
    q3 = tile_heads(qkv, 0, QA)
    k3 = tile_heads(qkv, 0, KA)
    v3 = tile_heads(qkv, 0, VA)
    g_col3 = tile_cols(gcum, G_DEC)
    beta3 = tile_cols(gates, G_BETA)
    decay3 = jnp.where(incl, jnp.exp(jnp.where(incl, g_col3 - tile_rows(cs_t, G_DEC), 0.0)), 0.0)
    kb3 = k3 * beta3
    a_low3 = jnp.where(strict, _bnt(kb3, k3) * decay3, 0.0)
    attn_s[...] = _bnt(q3, k3) * decay3
    qk_s[...] = _bnt(tile_heads(proj, SUBLANES, QB), tile_heads(proj, SUBLANES, KB) * HEAD_DIM ** -0.5)
    r3 = _unit_lower_inverse_minus_eye(a_low3)
    rhs3 = jnp.concatenate([v3 * beta3, kb3 * jnp.exp(g_col3)], axis=2)
    uw3 = rhs3 + _bnn(r3, rhs3)
    u_s[...] = uw3[:, :, :HEAD_DIM]
    w_s[...] = uw3[:, :, HEAD_DIM:]
    dlog3 = jnp.where(incl, tile_cols(gcum, G_FG) - tile_rows(cs_t, G_FG) + tile_rows(gt_t, G_IN), NEG_BIG)
    dlog_s[...] = dlog3
    dmax_s[...] = jnp.max(dlog3, axis=-1, keepdims=True)

    def chunk_body(c, carry):
        r0 = pl.multiple_of(c * CHUNK, CHUNK)
        rows = pl.ds(r0, CHUNK)
        prow = pl.ds(pl.multiple_of(r0 + SUBLANES, SUBLANES), CHUNK)
        last = pl.ds(r0 + CHUNK - 1, 1)
        pc = pl.ds(pl.multiple_of(c * N_HEADS, N_HEADS), N_HEADS)

        def heads(ref, rws, col0):
            return jnp.stack([ref[rws, col0 + h * HEAD_DIM:col0 + (h + 1) * HEAD_DIM] for h in range(N_HEADS)])

        def cols(ref, rws, lane0):
            return jnp.stack([ref[rws, lane0 + h:lane0 + h + 1] for h in range(N_HEADS)])

        q = heads(qkv, rows, QA)
        k = heads(qkv, rows, KA)
        g_col = cols(gcum, rows, G_DEC)
        g_last = cols(gcum, last, G_DEC)
        qb = heads(proj, prow, QB)
        kbm = heads(proj, prow, KB) * HEAD_DIM ** -0.5
        vbm = heads(proj, prow, VB)
        b_col = cols(gcum, rows, G_FG)
        b_last = cols(gcum, last, G_FG)
        i_col = cols(gates, rows, G_IN)
        s_old = s_s[...]
        c_old = c_s[...]
        n_old = n_s[:, 0:1, :]
        m_old = m_s[:, 0:1, 0:1]
        inter = b_col + m_old
        mt = jnp.maximum(inter, dmax_s[pc])
        wts = jnp.exp(dlog_s[pc] - mt) * qk_s[pc]
        sc = jnp.exp(inter - mt)
        m_new = mt[:, CHUNK - 1:CHUNK, :]
        sd = jnp.exp(b_last + m_old - m_new)
        kw = kbm * jnp.exp(b_last - b_col + i_col - m_new)
        w_state = _bnn(w_s[pc], s_old)
        q_state = _bnn(q * jnp.exp(g_col), s_old)
        q_mem = _bnn(qb, c_old)
        w_val = _bnn(wts, vbm)
        kv = _btn(kw, vbm)
        v_new = u_s[pc] - w_state
        o_a = q_state + _bnn(attn_s[pc], v_new)
        s_s[...] = s_old * jnp.exp(g_last) + _btn(k * jnp.exp(g_last - g_col), v_new)
        num = sc * q_mem + w_val
        den = sc * jnp.sum(qb * n_old, axis=-1, keepdims=True) + jnp.sum(wts, axis=-1, keepdims=True)
        h_b = num / jnp.maximum(jnp.abs(den), jnp.exp(-mt))
        c_s[...] = sd * c_old + kv
        n_s[...] = jnp.broadcast_to(sd * n_old + jnp.sum(kw, axis=1, keepdims=True), n_s.shape)
        m_s[...] = jnp.broadcast_to(m_new, m_s.shape)
        o_n = _rms(o_a, gdn_w)
        h_n = _rms(h_b, ml_w)
        for h in range(N_HEADS):
            hs = slice(h * HEAD_DIM, (h + 1) * HEAD_DIM)
            hs2 = slice((N_HEADS + h) * HEAD_DIM, (N_HEADS + h + 1) * HEAD_DIM)
            merged[rows, hs] = o_n[h] * _silu(proj[prow, ZA + h * HEAD_DIM:ZA + (h + 1) * HEAD_DIM])
            merged[rows, hs2] = h_n[h] * _sigmoid(proj[prow, OB + h * HEAD_DIM:OB + (h + 1) * HEAD_DIM])
        return carry

    lax.fori_loop(0, nc, chunk_body, 0)

    mix = jnp.dot(_bf(merged[...]), wout_ref[...], preferred_element_type=F32)
    h_ref[...] = _layer_norm(DN_ALPHA * x + mix, ln_ref[0:1, :], ln_ref[1:2, :])

    proj[0:SUBLANES, 0:A_CONV] = proj[tb:tb + SUBLANES, 0:A_CONV]

    @pl.when(t == n_t - 1)
    def _():
        hist_ref[...] = proj[tb:tb + SUBLANES, 0:A_CONV]
        s_out[...] = s_s[...]
        c_out[...] = c_s[...]
        n_out[...] = n_s[...]
        m_out[...] = m_s[...]


def _ab_prompt(x, win, wout, convw, gp, ln):
    bsz, length, _ = x.shape
    tb = min(256, length)
    assert length % tb == 0 and tb % CHUNK == 0 and length >= SUBLANES
    n_t = length // tb
    n_pairs = (tb // CHUNK) * N_HEADS
    const = lambda shape: pl.BlockSpec(shape, lambda b, t: (0,) * len(shape))
    per_b =lambda shape: pl.BlockSpec((None,) + shape, lambda b, t: (b,) + (0,) * len(shape))
    return pl.pallas_call(
        functools.partial(_ab_prompt_kernel, tb=tb, n_t=n_t),
        grid=(bsz, n_t),
        in_specs=[
            pl.BlockSpec((None, tb, D_MODEL), lambda b, t: (b, t, 0)),
            const((D_MODEL, D_IN_PAD)), const((2 * N_HEADS * HEAD_DIM, D_MODEL)),
            const((CONV_W, A_CONV)), const((SUBLANES, LANES)), const((2, D_MODEL)),
        ],
        out_specs=[
            pl.BlockSpec((None, tb, D_MODEL), lambda b, t: (b, t, 0)),
            per_b((SUBLANES, A_CONV)), per_b((N_HEADS, HEAD_DIM, HEAD_DIM)), per_b((N_HEADS, HEAD_DIM, HEAD_DIM)),
            per_b((N_HEADS, SUBLANES, LANES)), per_b((N_HEADS, SUBLANES, LANES)),
        ],
        out_shape=[
            jax.ShapeDtypeStruct((bsz, length, D_MODEL), F32),
            jax.ShapeDtypeStruct((bsz, SUBLANES, A_CONV), F32),
            jax.ShapeDtypeStruct((bsz, N_HEADS, HEAD_DIM, HEAD_DIM), F32),
            jax.ShapeDtypeStruct((bsz, N_HEADS, HEAD_DIM, HEAD_DIM), F32),
            jax.ShapeDtypeStruct((bsz, N_HEADS, SUBLANES, LANES), F32),
            jax.ShapeDtypeStruct((bsz, N_HEADS, SUBLANES, LANES), F32),
        ],
        scratch_shapes=[
            pltpu.VMEM((tb + SUBLANES, D_IN_PAD), F32),
            pltpu.VMEM((tb, A_CONV), F32),
            pltpu.VMEM((tb, LANES), F32),
            pltpu.VMEM((tb, LANES), F32),
            pltpu.VMEM((tb, 2 * N_HEADS * HEAD_DIM), F32),
            pltpu.VMEM((N_HEADS, HEAD_DIM, HEAD_DIM), F32),
            pltpu.VMEM((N_HEADS, HEAD_DIM, HEAD_DIM), F32),
            pltpu.VMEM((N_HEADS, SUBLANES, LANES), F32),
            pltpu.VMEM((N_HEADS, SUBLANES, LANES), F32),
            pltpu.VMEM((n_pairs, CHUNK, HEAD_DIM), F32),
            pltpu.VMEM((n_pairs, CHUNK, HEAD_DIM), F32),
            pltpu.VMEM((n_pairs, CHUNK, CHUNK), F32),
            pltpu.VMEM((n_pairs, CHUNK, CHUNK), F32),
            pltpu.VMEM((n_pairs, CHUNK, CHUNK), F32),
            pltpu.VMEM((n_pairs, CHUNK, 1), F32),
        ],
        compiler_params=pltpu.CompilerParams(
            dimension_semantics=("arbitrary", "arbitrary"), vmem_limit_bytes=VMEM_LIMIT),
        name="ab_prompt",
    )(x, win, wout, convw, gp, ln)


def _ab_decode_kernel(x_ref, win_ref, wout_ref, convw_ref, gp_ref, ln_ref, cbuf_ref, s_in, c_in, n_in, m_in,
                      y_ref, cbuf_out, s_out, c_out, n_out, m_out, proj, merged, *, bb, n_steps):
    i = pl.program_id(0)

    @pl.when(i == 0)
    def _():
        proj[...] = jnp.dot(_bf(x_ref[...]), win_ref[...], preferred_element_type=F32)

    rows = pl.ds(pl.multiple_of(i * bb, bb), bb)
    raw = proj[rows, 0:A_CONV]
    cbuf = cbuf_ref[...]
    conv = raw * convw_ref[CONV_W - 1:CONV_W, :]
    for j in range(CONV_W - 1):
        conv = conv + cbuf[:, j * A_CONV:(j + 1) * A_CONV] * convw_ref[j:j + 1, :]
    cbuf_out[:, 0:(CONV_W - 2) * A_CONV] = cbuf[:, A_CONV:(CONV_W - 1) * A_CONV]
    cbuf_out[:, (CONV_W - 2) * A_CONV:(CONV_W - 1) * A_CONV] = raw
    act = _silu(conv)
    gt = _gate_transform(proj[rows, GT:GT + LANES], gp_ref[...])
    gdn_w = gp_ref[4:5, :]
    ml_w = gp_ref[5:6, :]
    m_all = m_in[...]

    for h in range(N_HEADS):
        def head(off):
            return act[:, off + h * HEAD_DIM:off + (h + 1) * HEAD_DIM]
        q = head(QA)
        q = q * lax.rsqrt(jnp.sum(q * q, axis=-1, keepdims=True) + RMS_EPS) * HEAD_DIM ** -0.5
        k = head(KA)
        k = k * lax.rsqrt(jnp.sum(k * k, axis=-1, keepdims=True) + RMS_EPS)
        v = head(VA)
        q_t = q.T
        k_t = k.T
        qk = jnp.sum(q * k, axis=-1, keepdims=True)
        qb = proj[rows, QB + h * HEAD_DIM:QB + (h + 1) * HEAD_DIM]
        kbm = proj[rows, KB + h * HEAD_DIM:KB + (h + 1) * HEAD_DIM] * HEAD_DIM ** -0.5
        vbm = proj[rows, VB + h * HEAD_DIM:VB + (h + 1) * HEAD_DIM]
        qb_t = qb.T
        kb_t = kbm.T
        qkb = jnp.sum(qb * kbm, axis=-1, keepdims=True)
        n_old = n_in[:, h * HEAD_DIM:(h + 1) * HEAD_DIM]
        qn = jnp.sum(qb * n_old, axis=-1, keepdims=True)
        o_rows = []
        hb_rows = []
        n_rows = []
        m_rows = []
        for b in range(bb):
            s_old = s_in[b, h]
            k_c = k_t[:, b:b + 1]
            q_c = q_t[:, b:b + 1]
            e_g = jnp.exp(gt[b:b + 1, G_DEC + h:G_DEC + h + 1])
            beta = gt[b:b + 1, G_BETA + h:G_BETA + h + 1]
            k_s = jnp.sum(k_c * s_old, axis=0, keepdims=True)
            q_s = jnp.sum(q_c * s_old, axis=0, keepdims=True)
            v_new = v[b:b + 1, :] * beta - (beta * e_g) * k_s
            o_rows.append(e_g * q_s + qk[b:b + 1, :] * v_new)
            s_out[b, h] = s_old * e_g + k_c * v_new
            c_old = c_in[b, h]
            m_old = m_all[b:b + 1, h:h + 1]
            i_pre = gt[b:b + 1, G_IN + h:G_IN + h + 1]
            logf = gt[b:b + 1, G_FG + h:G_FG + h + 1]
            inter = logf + m_old
            mt = jnp.maximum(inter, i_pre)
            w_in = jnp.exp(i_pre - mt)
            sc = jnp.exp(inter - mt)
            wts = w_in * qkb[b:b + 1, :]
            q_cm = jnp.sum(qb_t[:, b:b + 1] * c_old, axis=0, keepdims=True)
            num = sc * q_cm + wts * vbm[b:b + 1, :]
            den = sc * qn[b:b + 1, :] + wts
            hb_rows.append(num / jnp.maximum(jnp.abs(den), jnp.exp(-mt)))
            c_out[b, h] = sc * c_old + (w_in * kb_t[:, b:b + 1]) * vbm[b:b + 1, :]
            n_rows.append(sc * n_old[b:b + 1, :] + w_in * kbm[b:b + 1, :])
            m_rows.append(mt)
        o_a = jnp.concatenate(o_rows, axis=0)
        h_b = jnp.concatenate(hb_rows, axis=0)
        n_out[:, h * HEAD_DIM:(h + 1) * HEAD_DIM] = jnp.concatenate(n_rows, axis=0)
        m_out[:, h:h + 1] = jnp.concatenate(m_rows, axis=0)
        z = proj[rows, ZA + h * HEAD_DIM:ZA + (h + 1) * HEAD_DIM]
        merged[rows, h * HEAD_DIM:(h + 1) * HEAD_DIM] = _rms(o_a, gdn_w) * _silu(z)
        o_gate = proj[rows, OB + h * HEAD_DIM:OB + (h + 1) * HEAD_DIM]
        merged[rows, N_HEADS * HEAD_DIM + h * HEAD_DIM:N_HEADS * HEAD_DIM + (h + 1) * HEAD_DIM] = (
            _rms(h_b, ml_w) * _sigmoid(o_gate))

    @pl.when(i == n_steps - 1)
    def _():
        mix = jnp.dot(_bf(merged[...]), wout_ref[...], preferred_element_type=F32)
        y_ref[...] = _layer_norm(DN_ALPHA * x_ref[...] + mix, ln_ref[0:1, :], ln_ref[1:2, :])


def _ab_decode(x, win, wout, convw, gp, ln, cbuf, s0, c0, n0, m0):
    nb = x.shape[0]
    bb = SUBLANES
    assert nb % bb == 0
    n_steps = nb // bb
    const = lambda shape: pl.BlockSpec(shape, lambda i: (0,) * len(shape))
    blk = lambda shape: pl.BlockSpec((bb,) + shape, lambda i: (i,) + (0,) * len(shape))
    hist = (CONV_W - 1) * A_CONV
    width = N_HEADS * HEAD_DIM
    return pl.pallas_call(
        functools.partial(_ab_decode_kernel, bb=bb, n_steps=n_steps),
        grid=(n_steps,),
        in_specs=[
            const((nb, D_MODEL)), const((D_MODEL, D_IN_PAD)), const((2 * width, D_MODEL)),
            const((CONV_W, A_CONV)), const((SUBLANES, LANES)), const((2, D_MODEL)),
            blk((hist,)), blk((N_HEADS, HEAD_DIM, HEAD_DIM)), blk((N_HEADS, HEAD_DIM, HEAD_DIM)),
            blk((width,)), blk((N_HEADS,)),
        ],
        out_specs=[
            const((nb, D_MODEL)), blk((hist,)), blk((N_HEADS, HEAD_DIM, HEAD_DIM)),
            blk((N_HEADS, HEAD_DIM, HEAD_DIM)), blk((width,)), blk((N_HEADS,)),
        ],
        out_shape=[
            jax.ShapeDtypeStruct((nb, D_MODEL), F32),
            jax.ShapeDtypeStruct((nb, hist), F32),
            jax.ShapeDtypeStruct((nb, N_HEADS, HEAD_DIM, HEAD_DIM), F32),
            jax.ShapeDtypeStruct((nb, N_HEADS, HEAD_DIM, HEAD_DIM), F32),
            jax.ShapeDtypeStruct((nb, width), F32),
            jax.ShapeDtypeStruct((nb, N_HEADS), F32),
        ],
        scratch_shapes=[pltpu.VMEM((nb, D_IN_PAD), F32), pltpu.VMEM((nb, 2 * width), F32)],
        compiler_params=pltpu.CompilerParams(dimension_semantics=("arbitrary",), vmem_limit_bytes=VMEM_LIMIT),
        name="ab_decode",
    )(x, win, wout, convw, gp, ln, cbuf, s0, c0, n0, m0)


def _second_largest_sum(a, b, c, d):
    hi1, lo1 = jnp.maximum(a, b), jnp.minimum(a, b)
    hi2, lo2 = jnp.maximum(c, d), jnp.minimum(c, d)
    return jnp.maximum(hi1, hi2) + jnp.maximum(jnp.minimum(hi1, hi2), jnp.maximum(lo1, lo2))


def _first_argmax(vals):
    best_v = vals[0]
    best_i = jnp.zeros(vals[0].shape, jnp.int32)
    for j in range(1, len(vals)):
        better = vals[j] > best_v
        best_v = jnp.where(better, vals[j], best_v)
        best_i = jnp.where(better, j, best_i)
    return best_i


def _router_kernel(x_ref, rw_ref, rb_ref, meta_ref, cnt_ref, carry, *, n_steps):
    i = pl.program_id(0)
    tm = x_ref.shape[0]

    @pl.when(i == 0)
    def _():
        carry[...] = jnp.zeros_like(carry)

    logits = lax.dot_general(rw_ref[...], x_ref[...], (((1,), (1,)), ((), ())),
                             preferred_element_type=F32, precision=HIGHEST)
    ex = jnp.exp(logits - jnp.max(logits, axis=0, keepdims=True))
    probs = ex / jnp.sum(ex, axis=0, keepdims=True)
    sel = probs + rb_ref[...]
    p = [probs[j:j + 1, :] for j in range(N_EXPERTS)]
    s = [sel[j:j + 1, :] for j in range(N_EXPERTS)]
    scores = [_second_largest_sum(*s[EXPERTS_PER_GROUP * g:EXPERTS_PER_GROUP * (g + 1)])
              for g in range(N_EXPERT_GROUPS)]
    best = _first_argmax(scores)
    masked = [jnp.where(best == j // EXPERTS_PER_GROUP, s[j], -jnp.inf) for j in range(N_EXPERTS)]
    i1 = _first_argmax(masked)
    i2 = _first_argmax([jnp.where(i1 == j, -jnp.inf, masked[j]) for j in range(N_EXPERTS)])
    zero = jnp.zeros_like(p[0])
    p1 = functools.reduce(lambda a, b: a + b, [jnp.where(i1 == j, p[j], zero) for j in range(N_EXPERTS)])
    p2 = functools.reduce(lambda a, b: a + b, [jnp.where(i2 == j, p[j], zero) for j in range(N_EXPERTS)])
    tot = p1 + p2
    rows = [jnp.where(i1 == j, p1 / tot, zero) + jnp.where(i2 == j, p2 / tot, zero) for j in range(N_EXPERTS)]
    in_group = [best == g for g in range(N_EXPERT_GROUPS)]
    local = [functools.reduce(lambda a, b: a + b,
                              [jnp.where(in_group[g], rows[EXPERTS_PER_GROUP * g + e], zero)
                               for g in range(N_EXPERT_GROUPS)])
             for e in range(EXPERTS_PER_GROUP)]
    onehot = jnp.concatenate([jnp.where(m, 1.0, 0.0) for m in in_group]
                             + [jnp.zeros((SUBLANES - N_EXPERT_GROUPS, tm), F32)], axis=0)
    ri = lax.broadcasted_iota(jnp.int32, (tm, tm), 0)
    ci = lax.broadcasted_iota(jnp.int32, (tm, tm), 1)
    incl = jnp.dot(_bf(onehot), _bf(jnp.where(ri <= ci, 1.0, 0.0)), preferred_element_type=F32)
    prev = carry[...]
    rank = jnp.sum(onehot * (incl - 1.0 + prev[:, 0:1]), axis=0, keepdims=True)
    carry[...] = prev + incl[:, tm - 1:tm]
    row_id = (i * tm + lax.broadcasted_iota(jnp.int32, (1, tm), 1)).astype(F32)
    meta_ref[...] = jnp.concatenate([row_id] + local + [best.astype(F32), rank, zero], axis=0).T

    @pl.when(i == n_steps - 1)
    def _():
        cnt_ref[...] = carry[...]


def _router(x, rw_t, rb_col):
    t = x.shape[0]
    tm = min(256, t)
    assert t % tm == 0
    n_steps = t // tm
    return pl.pallas_call(
        functools.partial(_router_kernel, n_steps=n_steps),
        grid=(n_steps,),
        in_specs=[pl.BlockSpec((tm, D_MODEL), lambda i: (i, 0)),
                  pl.BlockSpec((N_EXPERTS, D_MODEL), lambda i: (0, 0)),
                  pl.BlockSpec((N_EXPERTS, 1), lambda i: (0, 0))],
        out_specs=[pl.BlockSpec((tm, SUBLANES), lambda i: (i, 0)),
                   pl.BlockSpec((SUBLANES, LANES), lambda i: (0, 0))],
        out_shape=[jax.ShapeDtypeStruct((t, SUBLANES), F32), jax.ShapeDtypeStruct((SUBLANES, LANES), F32)],
        scratch_shapes=[pltpu.VMEM((SUBLANES, LANES), F32)],
        compiler_params=pltpu.CompilerParams(dimension_semantics=("arbitrary",), vmem_limit_bytes=VMEM_LIMIT),
        name="router",
    )(x, rw_t, rb_col)


def _moe_kernel(tg_ref, nv_ref, dst_prv, src_cur, dst_cur, src_nxt, cw_ref, x_hbm, wg_ref, wu_ref, wd_ref, ln_ref,
                y_hbm, xg, stage, gsem, ssem, *, tm, n_tiles):
    i = pl.program_id(0)
    slot = lax.rem(i, 2)
    other = 1 - slot
    n_valid = nv_ref[i]
    n_prev = jnp.where(i >= 1, nv_ref[jnp.maximum(i - 1, 0)], 0)
    has_next = i + 1 < n_tiles
    part = tm // EXPERTS_PER_GROUP

    def gather_row(tok_ref, r, dst_slot):
        return pltpu.make_async_copy(x_hbm.at[pl.ds(tok_ref[0, r], 1), :], xg.at[dst_slot, pl.ds(r, 1), :],
                                     gsem.at[dst_slot])

    def scatter_row(tok_ref, r, src_slot):
        return pltpu.make_async_copy(stage.at[src_slot, pl.ds(r, 1), :], y_hbm.at[pl.ds(tok_ref[0, r], 1), :],
                                     ssem.at[src_slot])

    def start_neighbour_rows(lo, hi):
        for r in range(lo, hi):
            @pl.when(has_next)
            def _():
                gather_row(src_nxt, r, other).start()

            @pl.when(r < n_prev)
            def _():
                scatter_row(dst_prv, r, other).start()

    def scatter_wait(count, src_slot):
        for p in [1 << b for b in range(tm.bit_length())]:
            @pl.when(lax.bitwise_and(count, p) != 0)
            def _():
                pltpu.make_async_copy(stage.at[src_slot, pl.ds(0, p), :], y_hbm.at[pl.ds(0, p), :],
                                      ssem.at[src_slot]).wait()

    @pl.when(i == 0)
    def _():
        def body(r, c):
            gather_row(src_cur, r, 0).start()
            return c
        lax.fori_loop(0, tm, body, 0, unroll=8)

    pltpu.make_async_copy(x_hbm.at[pl.ds(0, tm), :], xg.at[slot], gsem.at[slot]).wait()

    @pl.when(i >= 2)
    def _():
        scatter_wait(nv_ref[jnp.maximum(i - 2, 0)], slot)

    @pl.when(n_valid > 0)
    def _():
        x = xg[slot]
        xb = _bf(x)
        cw = cw_ref[...]
        acc = jnp.zeros((tm, D_MODEL), F32)
        for e in range(EXPERTS_PER_GROUP):
            g = jnp.dot(xb, wg_ref[e], preferred_element_type=F32)
            u = jnp.dot(xb, wu_ref[e], preferred_element_type=F32)
            hid = _silu(g) * u * cw[:, 1 + e:2 + e]
            acc = acc + jnp.dot(_bf(hid), wd_ref[e], preferred_element_type=F32)
            start_neighbour_rows(e * part, (e + 1) * part)
        stage[slot] = _layer_norm(DN_ALPHA * x + acc, ln_ref[0:1, :], ln_ref[1:2, :])

    @pl.when(n_valid <= 0)
    def _():
        start_neighbour_rows(0, tm)

    @pl.when(i == n_tiles - 1)
    def _():
        def body(r, c):
            scatter_row(dst_cur, r, slot).start()
            return c
        lax.fori_loop(0, n_valid, body, 0)
        scatter_wait(n_prev, other)
        scatter_wait(n_valid, slot)


def _moe(x, meta, counts, wg, wu, wd, ln, layer, out_row):
    t = x.shape[0]
    tm = min(256, t)
    assert t % tm == 0 and tm & (tm - 1) == 0
    n_tiles = t // tm + N_EXPERT_GROUPS
    n_slots = n_tiles * tm
    grp = meta[:, 1 + EXPERTS_PER_GROUP].astype(jnp.int32)
    rank = meta[:, 2 + EXPERTS_PER_GROUP].astype(jnp.int32)
    cnt = counts[:N_EXPERT_GROUPS, 0].astype(jnp.int32)
    padded = ((cnt + tm - 1) // tm) * tm
    g_end = jnp.cumsum(padded)
    g_off = g_end - padded
    pos = jnp.take(g_off, grp) + rank
    slot_meta = jnp.zeros((n_slots, SUBLANES), F32).at[pos].set(meta)
    tok_of_slot = slot_meta[:, 0].astype(jnp.int32)
    tile_start = jnp.arange(n_tiles, dtype=jnp.int32) * tm
    tile_grp = jnp.minimum(jnp.sum(tile_start[:, None] >= g_end[None, :], axis=1), N_EXPERT_GROUPS - 1).astype(jnp.int32)
    tile_valid = jnp.clip(jnp.take(g_off + cnt, tile_grp) - tile_start, 0, tm).astype(jnp.int32)
    src3 = tok_of_slot.reshape(n_tiles, 1, tm)
    dst3 = out_row(tok_of_slot).reshape(n_tiles, 1, tm)

    grid_spec = pltpu.PrefetchScalarGridSpec(
        num_scalar_prefetch=2,
        grid=(n_tiles,),
        in_specs=[
            pl.BlockSpec((None, 1, tm), lambda i, tg, nv: (jnp.maximum(i - 1, 0), 0, 0), memory_space=pltpu.SMEM),
            pl.BlockSpec((None, 1, tm), lambda i, tg, nv: (i, 0, 0), memory_space=pltpu.SMEM),
            pl.BlockSpec((None, 1, tm), lambda i, tg, nv: (i, 0, 0), memory_space=pltpu.SMEM),
            pl.BlockSpec((None, 1, tm), lambda i, tg, nv: (jnp.minimum(i + 1, n_tiles - 1), 0, 0),
                         memory_space=pltpu.SMEM),
            pl.BlockSpec((tm, SUBLANES), lambda i, tg, nv: (i, 0)),
            pl.BlockSpec(memory_space=pl.ANY),
            pl.BlockSpec((None, EXPERTS_PER_GROUP, D_MODEL, D_FF), lambda i, tg, nv: (layer, tg[i], 0, 0)),
            pl.BlockSpec((None, EXPERTS_PER_GROUP, D_MODEL, D_FF), lambda i, tg, nv: (layer, tg[i], 0, 0)),
            pl.BlockSpec((None, EXPERTS_PER_GROUP, D_FF, D_MODEL), lambda i, tg, nv: (layer, tg[i], 0, 0)),
            pl.BlockSpec((2, D_MODEL), lambda i, tg, nv: (0, 0)),
        ],
        out_specs=pl.BlockSpec(memory_space=pl.ANY),
        scratch_shapes=[pltpu.VMEM((2, tm, D_MODEL), F32), pltpu.VMEM((2, tm, D_MODEL), F32),
                        pltpu.SemaphoreType.DMA((2,)), pltpu.SemaphoreType.DMA((2,))],
    )
    return pl.pallas_call(
        functools.partial(_moe_kernel, tm=tm, n_tiles=n_tiles),
        grid_spec=grid_spec,
        out_shape=jax.ShapeDtypeStruct((t, D_MODEL), F32),
        compiler_params=pltpu.CompilerParams(dimension_semantics=("arbitrary",), vmem_limit_bytes=VMEM_LIMIT),
        name="moe",
    )(tile_grp, tile_valid, dst3, src3, dst3, src3, slot_meta, x, wg, wu, wd, ln)


def _s5_prep_kernel(are_ref, aim_ref, ldt_ref, bre_ref, bim_ref, abre_ref, abim_ref, bbre_ref, bbim_ref):
    a_r = are_ref[...]
    a_i = aim_ref[...]
    dt = jnp.exp(ldt_ref[...])
    mag = jnp.exp(dt * a_r)
    ab_re = mag * jnp.cos(dt * a_i)
    ab_im = mag * jnp.sin(dt * a_i)
    den = a_r * a_r + a_i * a_i
    nr = ab_re - 1.0
    z_re = (nr * a_r + ab_im * a_i) / den
    z_im = (ab_im * a_r - nr * a_i) / den
    abre_ref[...] = ab_re
    abim_ref[...] = ab_im
    bbre_ref[...] = z_re * bre_ref[...] - z_im * bim_ref[...]
    bbim_ref[...] = z_re * bim_ref[...] + z_im * bre_ref[...]


def _s5_prep(a_re, a_im, log_dt, b_re, b_im):
    n = S5_GROUPS * S5_STATE
    col = jax.ShapeDtypeStruct((n, 1), F32)
    mat = jax.ShapeDtypeStruct((n, S5_GROUP), F32)
    ldt = jnp.broadcast_to(log_dt[:, None], (S5_GROUPS, S5_STATE)).reshape(n, 1)
    return pl.pallas_call(_s5_prep_kernel, out_shape=[col, col, mat, mat], name="s5_prep")(
        a_re.reshape(n, 1), a_im.reshape(n, 1), ldt, b_re.reshape(n, S5_GROUP), b_im.reshape(n, S5_GROUP))


def _gelu_tanh(x):
    return 0.5 * x * (1.0 + jnp.tanh(0.7978845608028654 * (x + 0.044715 * (x * x * x))))


def _s5_kernel(x_ref, h0re_ref, h0im_ref, wre_ref, wim_ref, cre_ref, cim_ref, abre_ref, abim_ref, d_ref,
               gy_ref, hre_out, him_out, bu_re, bu_im, hre_s, him_s, *, nb, tt, n_t):
    t = pl.program_id(0)
    sw = wre_ref.shape[2]

    @pl.when(t == 0)
    def _():
        hre_s[...] = h0re_ref[...]
        him_s[...] = h0im_ref[...]

    for k in range(D_MODEL // LANES):
        ls = slice(k * LANES, (k + 1) * LANES)
        ss = slice(k * sw, (k + 1) * sw)
        buf = k % 2
        xv = x_ref[:, ls]
        xb = _bf(xv)
        bu_re[buf] = jnp.dot(xb, wre_ref[k], preferred_element_type=F32)
        bu_im[buf] = jnp.dot(xb, wim_ref[k], preferred_element_type=F32)
        a_re = jnp.broadcast_to(abre_ref[:, ss], (nb, sw))
        a_im = jnp.broadcast_to(abim_ref[:, ss], (nb, sw))

        def step(s, carry, buf=buf, a_re=a_re, a_im=a_im):
            h_re, h_im = carry
            rows = pl.ds(pl.multiple_of(s * nb, nb), nb)
            n_re = a_re * h_re - a_im * h_im + bu_re[buf, rows, :]
            n_im = a_re * h_im + a_im * h_re + bu_im[buf, rows, :]
            bu_re[buf, rows, :] = n_re
            bu_im[buf, rows, :] = n_im
            return n_re, n_im

        h_re, h_im = lax.fori_loop(0, tt, step, (hre_s[:, ss], him_s[:, ss]), unroll=min(tt, S5_UNROLL))
        hre_s[:, ss] = h_re
        him_s[:, ss] = h_im
        y = (jnp.dot(_bf(bu_re[buf]), cre_ref[k], preferred_element_type=F32)
             - jnp.dot(_bf(bu_im[buf]), cim_ref[k], preferred_element_type=F32)
             + d_ref[:, ls] * xv)
        gy_ref[:, ls] = _gelu_tanh(y).astype(gy_ref.dtype)

    @pl.when(t == n_t - 1)
    def _():
        hre_out[...] = hre_s[...]
        him_out[...] = him_s[...]


def _s5_scan(x, nb, h0_re, h0_im, wre, wim, cre, cim, ab_re, ab_im, d_skip):
    length = x.shape[0] // nb
    tt = min(S5_TT, length)
    assert length % tt == 0 and nb % SUBLANES == 0
    n_t = length // tt
    n_k = D_MODEL // LANES
    sw = (LANES // S5_GROUP) * S5_STATE
    n_state = S5_GROUPS * S5_STATE
    const = lambda shape: pl.BlockSpec(shape, lambda t: (0,) * len(shape))
    return pl.pallas_call(
        functools.partial(_s5_kernel, nb=nb, tt=tt, n_t=n_t),
        grid=(n_t,),
        in_specs=[
            pl.BlockSpec((tt * nb, D_MODEL), lambda t: (t, 0)),
            const((nb, n_state)), const((nb, n_state)),
            const((n_k, LANES, sw)), const((n_k, LANES, sw)), const((n_k, sw, LANES)), const((n_k, sw, LANES)),
            const((1, n_state)), const((1, n_state)), const((1, D_MODEL)),
        ],
        out_specs=[pl.BlockSpec((tt * nb, D_MODEL), lambda t: (t, 0)), const((nb, n_state)), const((nb, n_state))],
        out_shape=[
            jax.ShapeDtypeStruct((length * nb, D_MODEL), BF16),
            jax.ShapeDtypeStruct((nb, n_state), F32),
            jax.ShapeDtypeStruct((nb, n_state), F32),
        ],
        scratch_shapes=[
            pltpu.VMEM((2, nb * tt, sw), F32), pltpu.VMEM((2, nb * tt, sw), F32),
            pltpu.VMEM((nb, n_state), F32), pltpu.VMEM((nb, n_state), F32),
        ],
        compiler_params=pltpu.CompilerParams(dimension_semantics=("arbitrary",), vmem_limit_bytes=VMEM_LIMIT),
        name="s5_scan",
    )(x, h0_re, h0_im, wre, wim, cre, cim, ab_re, ab_im, d_skip)


def _glu_ln_kernel(gy_ref, h_ref, wa_ref, wb_ref, ln_ref, o_ref):
    gy = gy_ref[...]
    a = jnp.dot(gy, wa_ref[...], preferred_element_type=F32)
    b = jnp.dot(gy, wb_ref[...], preferred_element_type=F32)
    o_ref[...] = _layer_norm(DN_ALPHA * h_ref[...] + a * _sigmoid(b), ln_ref[0:1, :], ln_ref[1:2, :])


def _glu_ln(gy, h, wa, wb, ln):
    t = h.shape[0]
    tm = min(512, t)
    assert t % tm == 0
    return pl.pallas_call(
        _glu_ln_kernel,
        grid=(t // tm,),
        in_specs=[pl.BlockSpec((tm, D_MODEL), lambda i: (i, 0)), pl.BlockSpec((tm, D_MODEL), lambda i: (i, 0)),
                  pl.BlockSpec((D_MODEL, D_MODEL), lambda i: (0, 0)), pl.BlockSpec((D_MODEL, D_MODEL), lambda i: (0, 0)),
                  pl.BlockSpec((2, D_MODEL), lambda i: (0, 0))],
        out_specs=pl.BlockSpec((tm, D_MODEL), lambda i: (i, 0)),
        out_shape=jax.ShapeDtypeStruct((t, D_MODEL), F32),
        compiler_params=pltpu.CompilerParams(dimension_semantics=("arbitrary",), vmem_limit_bytes=VMEM_LIMIT),
        name="glu_ln",
    )(gy, h, wa, wb, ln)


def _block_diag_slices(m, rows_per_group, cols_per_group):
    gps = LANES // S5_GROUP
    m = m.reshape(S5_GROUPS // gps, gps, rows_per_group, cols_per_group)
    eye = jnp.eye(gps, dtype=m.dtype)
    out = m[:, :, :, None, :] * eye[None, :, None, :, None]
    return out.reshape(S5_GROUPS // gps, gps * rows_per_group, gps * cols_per_group)


def _prepare(p):
    w = p['w_in'][0]
    win = jnp.concatenate(
        [w[:, 0:1536], w[:, 1544:2056], w[:, 2056:3592], w[:, 3600:4112], w[:, 1536:1544], w[:, 3592:3600],
         jnp.zeros((D_MODEL, D_IN_PAD - 4112), w.dtype)], axis=1).astype(BF16)
    at_lane = lambda v, lane0: jnp.pad(v, (lane0, LANES - lane0 - v.shape[0]))
    gp = jnp.stack([at_lane(p['gdn_A_log'][0], G_DEC), at_lane(p['gdn_dt_bias'][0], G_DEC),
                    at_lane(p['ml_b_i'][0], G_IN), at_lane(p['ml_b_f'][0], G_FG),
                    p['gdn_norm_w'][0], p['ml_norm_w'][0], jnp.zeros((LANES,), F32), jnp.zeros((LANES,), F32)])
    ab_re, ab_im, bb_re, bb_im = _s5_prep(p['s5_A_re'][0], p['s5_A_im'][0], p['s5_log_dt'][0],
                                          p['s5_B_re'][0], p['s5_B_im'][0])
    to_in = lambda bb: _block_diag_slices(
        bb.reshape(S5_GROUPS, S5_STATE, S5_GROUP).transpose(0, 2, 1), S5_GROUP, S5_STATE).astype(BF16)
    to_out = lambda c: _block_diag_slices(c.transpose(0, 2, 1), S5_STATE, S5_GROUP).astype(BF16)
    return dict(
        win=win, wout=p['w_out'][0].astype(BF16), convw=p['gdn_conv_w'][0], gp=gp,
        ln_mix=[jnp.stack([p['ln_mix_g'][l], p['ln_mix_b'][l]]) for l in range(DEPTH)],
        ln_ffn=[jnp.stack([p['ln_ffn_g'][l], p['ln_ffn_b'][l]]) for l in range(DEPTH)],
        rw_t=p['router_w'].T, rb_col=p['router_b'][:, None],
        wg=p['moe_w_gate'].astype(BF16), wu=p['moe_w_up'].astype(BF16), wd=p['moe_w_down'].astype(BF16),
        s5_wre=to_in(bb_re), s5_wim=to_in(bb_im),
        s5_cre=to_out(p['s5_C_re'][0]), s5_cim=to_out(p['s5_C_im'][0]),
        s5_abre=ab_re.reshape(1, -1), s5_abim=ab_im.reshape(1, -1), s5_d=p['s5_D'][0][None, :],
        glu_a=p['s5_w_glu_a'][0].astype(BF16), glu_b=p['s5_w_glu_b'][0].astype(BF16),
    )


def _ffn(h, w, layer, out_row):
    meta, counts = _router(h, w['rw_t'], w['rb_col'])
    return _moe(h, meta, counts, w['wg'], w['wu'], w['wd'], w['ln_ffn'][layer], layer, out_row)


def _s5_layer(h, nb, h0_re, h0_im, w):
    gy, re, im = _s5_scan(h, nb, h0_re, h0_im, w['s5_wre'], w['s5_wim'],
                          w['s5_cre'], w['s5_cim'], w['s5_abre'], w['s5_abim'], w['s5_d'])
    h = _glu_ln(gy, h, w['glu_a'], w['glu_b'], w['ln_mix'][1])
    return h, re, im


def kernel(x_prompt, x_sample, state_gdn_conv, state_gdn_S, state_mlstm_C, state_mlstm_n, state_mlstm_m,
           state_s5_re, state_s5_im, w_in, gdn_conv_w, gdn_A_log, gdn_dt_bias, gdn_norm_w, ml_b_i, ml_b_f,
           ml_norm_w, w_out, s5_A_re, s5_A_im, s5_log_dt, s5_B_re, s5_B_im, s5_C_re, s5_C_im, s5_D,
           s5_w_glu_a, s5_w_glu_b, router_w, router_b, moe_w_gate, moe_w_up, moe_w_down,
           ln_mix_g, ln_mix_b, ln_ffn_g, ln_ffn_b):
    w = _prepare(dict(
        w_in=w_in, gdn_conv_w=gdn_conv_w, gdn_A_log=gdn_A_log, gdn_dt_bias=gdn_dt_bias, gdn_norm_w=gdn_norm_w,
        ml_b_i=ml_b_i, ml_b_f=ml_b_f, ml_norm_w=ml_norm_w, w_out=w_out, s5_A_re=s5_A_re, s5_A_im=s5_A_im,
        s5_log_dt=s5_log_dt, s5_B_re=s5_B_re, s5_B_im=s5_B_im, s5_C_re=s5_C_re, s5_C_im=s5_C_im, s5_D=s5_D,
        s5_w_glu_a=s5_w_glu_a, s5_w_glu_b=s5_w_glu_b, router_w=router_w, router_b=router_b,
        moe_w_gate=moe_w_gate, moe_w_up=moe_w_up, moe_w_down=moe_w_down,
        ln_mix_g=ln_mix_g, ln_mix_b=ln_mix_b, ln_ffn_g=ln_ffn_g, ln_ffn_b=ln_ffn_b))
    bp, lp, _ = x_prompt.shape
    bs, ls, _ = x_sample.shape
    assert ls == 1
    n_state = S5_GROUPS * S5_STATE

    h, p_hist, p_s, p_c, p_n, p_m = _ab_prompt(x_prompt, w['win'], w['wout'], w['convw'], w['gp'], w['ln_mix'][0])
    h = _ffn(h.reshape(bp * lp, D_MODEL), w, 0, lambda r: (r % lp) * bp + r // lp)
    zeros = jnp.zeros((bp, n_state), F32)
    h, p_re, p_im = _s5_layer(h, bp, zeros, zeros, w)
    y_prompt = _ffn(h, w, 1, lambda r: (r % bp) * lp + r // bp).reshape(bp, lp, D_MODEL)

    hs, s_cbuf, s_s, s_c, s_n, s_m = _ab_decode(
        x_sample[:, 0], w['win'], w['wout'], w['convw'], w['gp'], w['ln_mix'][0],
        state_gdn_conv.reshape(bs, (CONV_W - 1) * A_CONV), state_gdn_S[:, 0], state_mlstm_C[:, 0],
        state_mlstm_n.reshape(bs, N_HEADS * HEAD_DIM), state_mlstm_m[:, 0])
    hs = _ffn(hs, w, 0, lambda r: r)
    hs, s_re, s_im = _s5_layer(hs, bs, state_s5_re.reshape(bs, n_state), state_s5_im.reshape(bs, n_state), w)
    y_sample = _ffn(hs, w, 1, lambda r: r).reshape(bs, 1, D_MODEL)

    grp = lambda a, n: a.reshape(n, 1, S5_GROUPS, S5_STATE)
    return (
        y_prompt, y_sample,
        p_hist[:, None, SUBLANES - (CONV_W - 1):, :], p_s[:, None], p_c[:, None],
        p_n[:, None, :, 0, :], p_m[:, None, :, 0, 0], grp(p_re, bp), grp(p_im, bp),
        s_cbuf.reshape(bs, 1, CONV_W - 1, A_CONV), s_s[:, None], s_c[:, None],
        s_n.reshape(bs, 1, N_HEADS, HEAD_DIM), s_m[:, None], grp(s_re, bs), grp(s_im, bs),
    )
```

```python
import functools

import jax
import jax.numpy as jnp
from jax import lax
from jax.experimental import pallas as pl
from jax.experimental.pallas import tpu as pltpu

F32 = jnp.float32
BF16 = jnp.bfloat16
HIGHEST = lax.Precision.HIGHEST

D_MODEL = 1024
DEPTH = 2
N_HEADS = 4
HEAD_DIM = 128
CONV_W = 4
CHUNK = 64
A_CONV = 3 * N_HEADS * HEAD_DIM
S5_GROUP = 16
S5_GROUPS = D_MODEL // S5_GROUP
S5_STATE = 64
N_EXPERTS = 16
EXPERTS_PER_GROUP = 4
N_EXPERT_GROUPS = N_EXPERTS // EXPERTS_PER_GROUP
D_FF = 512
DN_ALPHA = (2 * DEPTH) ** 0.25
LN_EPS = 1e-5
RMS_EPS = 1e-6
NEG_BIG = -1e30

QA, KA, VA, ZA, QB, KB, VB, OB, GT = 0, 512, 1024, 1536, 2048, 2560, 3072, 3584, 4096
D_IN_PAD = 4224
G_DEC, G_BETA, G_IN, G_FG = 0, 4, 8, 12

S5_TT = 64
S5_UNROLL = 64

LANES = 128
SUBLANES = 8
VMEM_LIMIT = 56 * 1024 * 1024


def _bf(x):
    return x.astype(BF16)


def _nn(a, b):
    return jnp.dot(_bf(a), _bf(b), preferred_element_type=F32)


def _nt(a, b):
    return lax.dot_general(_bf(a), _bf(b), (((1,), (1,)), ((), ())), preferred_element_type=F32)


def _tn(a, b):
    return lax.dot_general(_bf(a), _bf(b), (((0,), (0,)), ((), ())), preferred_element_type=F32)


def _sigmoid(x):
    return 1.0 / (1.0 + jnp.exp(-x))


def _softplus(x):
    return jnp.maximum(x, 0.0) + jnp.log(1.0 + jnp.exp(-jnp.abs(x)))


def _silu(x):
    return x * _sigmoid(x)


def _layer_norm(y, g, b):
    mu = jnp.mean(y, axis=-1, keepdims=True)
    yc = y - mu
    var = jnp.mean(yc * yc, axis=-1, keepdims=True)
    return yc * lax.rsqrt(var + LN_EPS) * g + b


def _rms(x, w):
    return x * lax.rsqrt(jnp.mean(x * x, axis=-1, keepdims=True) + RMS_EPS) * w


def _gate_transform(raw, gp):
    lane = lax.broadcasted_iota(jnp.int32, raw.shape, 1)
    dec = -jnp.exp(gp[0:1, :]) * _softplus(raw + gp[1:2, :])
    beta = _sigmoid(raw)
    ipre = raw + gp[2:3, :]
    logf = -_softplus(-(raw + gp[3:4, :]))
    return jnp.where(lane < G_BETA, dec,
                     jnp.where(lane < G_IN, beta,
                               jnp.where(lane < G_FG, ipre,
                                         jnp.where(lane < G_FG + N_HEADS, logf, 0.0))))


def _bnn(a, b):
    return lax.dot_general(_bf(a), _bf(b), (((2,), (1,)), ((0,), (0,))), preferred_element_type=F32)


def _bnt(a, b):
    return lax.dot_general(_bf(a), _bf(b), (((2,), (2,)), ((0,), (0,))), preferred_element_type=F32)


def _btn(a, b):
    return lax.dot_general(_bf(a), _bf(b), (((1,), (1,)), ((0,), (0,))), preferred_element_type=F32)


def _unit_lower_inverse_minus_eye(a):
    c = a.shape[-1]
    r = -a
    steps = max(1, (c - 1).bit_length()) - 1
    q = _bnn(r, r)
    for i in range(steps):
        rq = _bnn(r, q)
        qq = _bnn(q, q) if i + 1 < steps else None
        r = r + q + rq
        q = qq
    return r


def _ab_prompt_kernel(x_ref, win_ref, wout_ref, convw_ref, gp_ref, ln_ref,
                      h_ref, hist_ref, s_out, c_out, n_out, m_out,
                      proj, qkv, gates, gcum, merged, s_s, c_s, n_s, m_s,
                      u_s, w_s, attn_s, dlog_s, qk_s, dmax_s, *, tb, n_t):
    t = pl.program_id(1)
    nc = tb // CHUNK

    @pl.when(t == 0)
    def _():
        proj[0:SUBLANES, :] = jnp.zeros((SUBLANES, D_IN_PAD), F32)
        s_s[...] = jnp.zeros_like(s_s)
        c_s[...] = jnp.zeros_like(c_s)
        n_s[...] = jnp.zeros_like(n_s)
        m_s[...] = jnp.zeros_like(m_s)

    x = x_ref[...]
    xb = _bf(x)
    proj[SUBLANES:SUBLANES + tb, 0:A_CONV] = jnp.dot(xb, win_ref[:, 0:A_CONV], preferred_element_type=F32)
    proj[SUBLANES:SUBLANES + tb, A_CONV:] = jnp.dot(xb, win_ref[:, A_CONV:], preferred_element_type=F32)

    for blk in range(A_CONV // LANES):
        cs = slice(blk * LANES, (blk + 1) * LANES)
        acc = proj[SUBLANES:SUBLANES + tb, cs] * convw_ref[CONV_W - 1:CONV_W, cs]
        for j in range(1, CONV_W):
            acc = acc + proj[SUBLANES - j:SUBLANES - j + tb, cs] * convw_ref[CONV_W - 1 - j:CONV_W - j, cs]
        y = _silu(acc)
        if blk < 2 * N_HEADS:
            y = y * lax.rsqrt(jnp.sum(y * y, axis=-1, keepdims=True) + RMS_EPS)
            if blk < N_HEADS:
                y = y * HEAD_DIM ** -0.5
        qkv[:, cs] = y

    gt = _gate_transform(proj[SUBLANES:SUBLANES + tb, GT:GT + LANES], gp_ref[...])
    gates[...] = gt
    ri = lax.broadcasted_iota(jnp.int32, (tb, tb), 0)
    ci = lax.broadcasted_iota(jnp.int32, (tb, tb), 1)
    same_chunk = lax.shift_right_logical(ri, 6) == lax.shift_right_logical(ci, 6)
    ltri = jnp.where(same_chunk, jnp.where(ri >= ci, 1.0, 0.0), 0.0)
    gcum[...] = jnp.dot(ltri, gt, preferred_element_type=F32, precision=HIGHEST)

    ii = lax.broadcasted_iota(jnp.int32, (CHUNK, CHUNK), 0)
    jj = lax.broadcasted_iota(jnp.int32, (CHUNK, CHUNK), 1)
    incl = ii >= jj
    strict = ii > jj
    gdn_w = gp_ref[4:5, :]
    ml_w = gp_ref[5:6, :]

    pairs = [(c, h) for c in range(nc) for h in range(N_HEADS)]

    def tile_heads(ref, row0, col0):
        return jnp.stack([ref[row0 + c * CHUNK:row0 + (c + 1) * CHUNK, col0 + h * HEAD_DIM:col0 + (h + 1) * HEAD_DIM]
                          for c, h in pairs])

    def tile_cols(ref, lane0):
        return jnp.stack([ref[c * CHUNK:(c + 1) * CHUNK, lane0 + h:lane0 + h + 1] for c, h in pairs])

    def tile_rows(transposed, lane0):
        return jnp.stack([transposed[c][lane0 + h:lane0 + h + 1, :] for c, h in pairs])

    cs_t = [gcum[c * CHUNK:(c + 1) * CHUNK, :].T for c in range(nc)]
    gt_t = [gates[c * CHUNK:(c + 1) * CHUNK, :].T for c in range(nc)]
    q3 = tile_heads(qkv, 0, QA)
    k3 = tile_heads(qkv, 0, KA)
    v3 = tile_heads(qkv, 0, VA)
    g_col3 = tile_cols(gcum, G_DEC)
    beta3 = tile_cols(gates, G_BETA)
    decay3 = jnp.where(incl, jnp.exp(jnp.where(incl, g_col3 - tile_rows(cs_t, G_DEC), 0.0)), 0.0)
    kb3 = k3 * beta3
    a_low3 = jnp.where(strict, _bnt(kb3, k3) * decay3, 0.0)
    attn_s[...] = _bnt(q3, k3) * decay3
    qk_s[...] = _bnt(tile_heads(proj, SUBLANES, QB), tile_heads(proj, SUBLANES, KB) * HEAD_DIM ** -0.5)
    r3 = _unit_lower_inverse_minus_eye(a_low3)
    rhs3 = jnp.concatenate([v3 * beta3, kb3 * jnp.exp(g_col3)], axis=2)
    uw3 = rhs3 + _bnn(r3, rhs3)
    u_s[...] = uw3[:, :, :HEAD_DIM]
    w_s[...] = uw3[:, :, HEAD_DIM:]
    dlog3 = jnp.where(incl, tile_cols(gcum, G_FG) - tile_rows(cs_t, G_FG) + tile_rows(gt_t, G_IN), NEG_BIG)
    dlog_s[...] = dlog3
    dmax_s[...] = jnp.max(dlog3, axis=-1, keepdims=True)

    def chunk_body(c, carry):
        r0 = pl.multiple_of(c * CHUNK, CHUNK)
        rows = pl.ds(r0, CHUNK)
        prow = pl.ds(pl.multiple_of(r0 + SUBLANES, SUBLANES), CHUNK)
        last = pl.ds(r0 + CHUNK - 1, 1)
        pc = pl.ds(pl.multiple_of(c * N_HEADS, N_HEADS), N_HEADS)

        def heads(ref, rws, col0):
            return jnp.stack([ref[rws, col0 + h * HEAD_DIM:col0 + (h + 1) * HEAD_DIM] for h in range(N_HEADS)])

        def cols(ref, rws, lane0):
            return jnp.stack([ref[rws, lane0 + h:lane0 + h + 1] for h in range(N_HEADS)])

        q = heads(qkv, rows, QA)
        k = heads(qkv, rows, KA)
        g_col = cols(gcum, rows, G_DEC)
        g_last = cols(gcum, last, G_DEC)
        qb = heads(proj, prow, QB)
        kbm = heads(proj, prow, KB) * HEAD_DIM ** -0.5
        vbm = heads(proj, prow, VB)
        b_col = cols(gcum, rows, G_FG)
        b_last = cols(gcum, last, G_FG)
        i_col = cols(gates, rows, G_IN)
        s_old = s_s[...]
        c_old = c_s[...]
        n_old = n_s[:, 0:1, :]
        m_old = m_s[:, 0:1, 0:1]
        inter = b_col + m_old
        mt = jnp.maximum(inter, dmax_s[pc])
        wts = jnp.exp(dlog_s[pc] - mt) * qk_s[pc]
        sc = jnp.exp(inter - mt)
        m_new = mt[:, CHUNK - 1:CHUNK, :]
        sd = jnp.exp(b_last + m_old - m_new)
        kw = kbm * jnp.exp(b_last - b_col + i_col - m_new)
        w_state = _bnn(w_s[pc], s_old)
        q_state = _bnn(q * jnp.exp(g_col), s_old)
        q_mem = _bnn(qb, c_old)
        w_val = _bnn(wts, vbm)
        kv = _btn(kw, vbm)
        v_new = u_s[pc] - w_state
        o_a = q_state + _bnn(attn_s[pc], v_new)
        s_s[...] = s_old * jnp.exp(g_last) + _btn(k * jnp.exp(g_last - g_col), v_new)
        num = sc * q_mem + w_val
        den = sc * jnp.sum(qb * n_old, axis=-1, keepdims=True) + jnp.sum(wts, axis=-1, keepdims=True)
        h_b = num / jnp.maximum(jnp.abs(den), jnp.exp(-mt))
        c_s[...] = sd * c_old + kv
        n_s[...] = jnp.broadcast_to(sd * n_old + jnp.sum(kw, axis=1, keepdims=True), n_s.shape)
        m_s[...] = jnp.broadcast_to(m_new, m_s.shape)
        o_n = _rms(o_a, gdn_w)
        h_n = _rms(h_b, ml_w)
        for h in range(N_HEADS):
            hs = slice(h * HEAD_DIM, (h + 1) * HEAD_DIM)
            hs2 = slice((N_HEADS + h) * HEAD_DIM, (N_HEADS + h + 1) * HEAD_DIM)
            merged[rows, hs] = o_n[h] * _silu(proj[prow, ZA + h * HEAD_DIM:ZA + (h + 1) * HEAD_DIM])
            merged[rows, hs2] = h_n[h] * _sigmoid(proj[prow, OB + h * HEAD_DIM:OB + (h + 1) * HEAD_DIM])
        return carry

    lax.fori_loop(0, nc, chunk_body, 0)

    mix = jnp.dot(_bf(merged[...]), wout_ref[...], preferred_element_type=F32)
    h_ref[...] = _layer_norm(DN_ALPHA * x + mix, ln_ref[0:1, :], ln_ref[1:2, :])

    proj[0:SUBLANES, 0:A_CONV] = proj[tb:tb + SUBLANES, 0:A_CONV]

    @pl.when(t == n_t - 1)
    def _():
        hist_ref[...] = proj[tb:tb + SUBLANES, 0:A_CONV]
        s_out[...] = s_s[...]
        c_out[...] = c_s[...]
        n_out[...] = n_s[...]
        m_out[...] = m_s[...]


def _ab_prompt(x, win, wout, convw, gp, ln):
    bsz, length, _ = x.shape
    tb = min(256, length)
    assert length % tb == 0 and tb % CHUNK == 0 and length >= SUBLANES
    n_t = length // tb
    n_pairs = (tb // CHUNK) * N_HEADS
    const = lambda shape: pl.BlockSpec(shape, lambda b, t: (0,) * len(shape))
    per_b =lambda shape: pl.BlockSpec((None,) + shape, lambda b, t: (b,) + (0,) * len(shape))
    return pl.pallas_call(
        functools.partial(_ab_prompt_kernel, tb=tb, n_t=n_t),
        grid=(bsz, n_t),
        in_specs=[
            pl.BlockSpec((None, tb, D_MODEL), lambda b, t: (b, t, 0)),
            const((D_MODEL, D_IN_PAD)), const((2 * N_HEADS * HEAD_DIM, D_MODEL)),
            const((CONV_W, A_CONV)), const((SUBLANES, LANES)), const((2, D_MODEL)),
        ],
        out_specs=[
            pl.BlockSpec((None, tb, D_MODEL), lambda b, t: (b, t, 0)),
            per_b((SUBLANES, A_CONV)), per_b((N_HEADS, HEAD_DIM, HEAD_DIM)), per_b((N_HEADS, HEAD_DIM, HEAD_DIM)),
            per_b((N_HEADS, SUBLANES, LANES)), per_b((N_HEADS, SUBLANES, LANES)),
        ],
        out_shape=[
            jax.ShapeDtypeStruct((bsz, length, D_MODEL), F32),
            jax.ShapeDtypeStruct((bsz, SUBLANES, A_CONV), F32),
            jax.ShapeDtypeStruct((bsz, N_HEADS, HEAD_DIM, HEAD_DIM), F32),
            jax.ShapeDtypeStruct((bsz, N_HEADS, HEAD_DIM, HEAD_DIM), F32),
            jax.ShapeDtypeStruct((bsz, N_HEADS, SUBLANES, LANES), F32),
            jax.ShapeDtypeStruct((bsz, N_HEADS, SUBLANES, LANES), F32),
        ],
        scratch_shapes=[
            pltpu.VMEM((tb + SUBLANES, D_IN_PAD), F32),
            pltpu.VMEM((tb, A_CONV), F32),
            pltpu.VMEM((tb, LANES), F32),
            pltpu.VMEM((tb, LANES), F32),
            pltpu.VMEM((tb, 2 * N_HEADS * HEAD_DIM), F32),
            pltpu.VMEM((N_HEADS, HEAD_DIM, HEAD_DIM), F32),
            pltpu.VMEM((N_HEADS, HEAD_DIM, HEAD_DIM), F32),
            pltpu.VMEM((N_HEADS, SUBLANES, LANES), F32),
            pltpu.VMEM((N_HEADS, SUBLANES, LANES), F32),
            pltpu.VMEM((n_pairs, CHUNK, HEAD_DIM), F32),
            pltpu.VMEM((n_pairs, CHUNK, HEAD_DIM), F32),
            pltpu.VMEM((n_pairs, CHUNK, CHUNK), F32),
            pltpu.VMEM((n_pairs, CHUNK, CHUNK), F32),
            pltpu.VMEM((n_pairs, CHUNK, CHUNK), F32),
            pltpu.VMEM((n_pairs, CHUNK, 1), F32),
        ],
        compiler_params=pltpu.CompilerParams(
            dimension_semantics=("arbitrary", "arbitrary"), vmem_limit_bytes=VMEM_LIMIT),
        name="ab_prompt",
    )(x, win, wout, convw, gp, ln)


def _ab_decode_kernel(x_ref, win_ref, wout_ref, convw_ref, gp_ref, ln_ref, cbuf_ref, s_in, c_in, n_in, m_in,
                      y_ref, cbuf_out, s_out, c_out, n_out, m_out, proj, merged, *, bb, n_steps):
    i = pl.program_id(0)

    @pl.when(i == 0)
    def _():
        proj[...] = jnp.dot(_bf(x_ref[...]), win_ref[...], preferred_element_type=F32)

    rows = pl.ds(pl.multiple_of(i * bb, bb), bb)
    raw = proj[rows, 0:A_CONV]
    cbuf = cbuf_ref[...]
    conv = raw * convw_ref[CONV_W - 1:CONV_W, :]
    for j in range(CONV_W - 1):
        conv = conv + cbuf[:, j * A_CONV:(j + 1) * A_CONV] * convw_ref[j:j + 1, :]
    cbuf_out[:, 0:(CONV_W - 2) * A_CONV] = cbuf[:, A_CONV:(CONV_W - 1) * A_CONV]
    cbuf_out[:, (CONV_W - 2) * A_CONV:(CONV_W - 1) * A_CONV] = raw
    act = _silu(conv)
    gt = _gate_transform(proj[rows, GT:GT + LANES], gp_ref[...])
    gdn_w = gp_ref[4:5, :]
    ml_w = gp_ref[5:6, :]
    m_all = m_in[...]

    for h in range(N_HEADS):
        def head(off):
            return act[:, off + h * HEAD_DIM:off + (h + 1) * HEAD_DIM]
        q = head(QA)
        q = q * lax.rsqrt(jnp.sum(q * q, axis=-1, keepdims=True) + RMS_EPS) * HEAD_DIM ** -0.5
        k = head(KA)
        k = k * lax.rsqrt(jnp.sum(k * k, axis=-1, keepdims=True) + RMS_EPS)
        v = head(VA)
        q_t = q.T
        k_t = k.T
        qk = jnp.sum(q * k, axis=-1, keepdims=True)
        qb = proj[rows, QB + h * HEAD_DIM:QB + (h + 1) * HEAD_DIM]
        kbm = proj[rows, KB + h * HEAD_DIM:KB + (h + 1) * HEAD_DIM] * HEAD_DIM ** -0.5
        vbm = proj[rows, VB + h * HEAD_DIM:VB + (h + 1) * HEAD_DIM]
        qb_t = qb.T
        kb_t = kbm.T
        qkb = jnp.sum(qb * kbm, axis=-1, keepdims=True)
        n_old = n_in[:, h * HEAD_DIM:(h + 1) * HEAD_DIM]
        qn = jnp.sum(qb * n_old, axis=-1, keepdims=True)
        o_rows = []
        hb_rows = []
        n_rows = []
        m_rows = []
        for b in range(bb):
            s_old = s_in[b, h]
            k_c = k_t[:, b:b + 1]
            q_c = q_t[:, b:b + 1]
            e_g = jnp.exp(gt[b:b + 1, G_DEC + h:G_DEC + h + 1])
            beta = gt[b:b + 1, G_BETA + h:G_BETA + h + 1]
            k_s = jnp.sum(k_c * s_old, axis=0, keepdims=True)
            q_s = jnp.sum(q_c * s_old, axis=0, keepdims=True)
            v_new = v[b:b + 1, :] * beta - (beta * e_g) * k_s
            o_rows.append(e_g * q_s + qk[b:b + 1, :] * v_new)
            s_out[b, h] = s_old * e_g + k_c * v_new
            c_old = c_in[b, h]
            m_old = m_all[b:b + 1, h:h + 1]
            i_pre = gt[b:b + 1, G_IN + h:G_IN + h + 1]
            logf = gt[b:b + 1, G_FG + h:G_FG + h + 1]
            inter = logf + m_old
            mt = jnp.maximum(inter, i_pre)
            w_in = jnp.exp(i_pre - mt)
            sc = jnp.exp(inter - mt)
            wts = w_in * qkb[b:b + 1, :]
            q_cm = jnp.sum(qb_t[:, b:b + 1] * c_old, axis=0, keepdims=True)
            num = sc * q_cm + wts * vbm[b:b + 1, :]
            den = sc * qn[b:b + 1, :] + wts
            hb_rows.append(num / jnp.maximum(jnp.abs(den), jnp.exp(-mt)))
            c_out[b, h] = sc * c_old + (w_in * kb_t[:, b:b + 1]) * vbm[b:b + 1, :]
            n_rows.append(sc * n_old[b:b + 1, :] + w_in * kbm[b:b + 1, :])
            m_rows.append(mt)
        o_a = jnp.concatenate(o_rows, axis=0)
        h_b = jnp.concatenate(hb_rows, axis=0)
        n_out[:, h * HEAD_DIM:(h + 1) * HEAD_DIM] = jnp.concatenate(n_rows, axis=0)
        m_out[:, h:h + 1] = jnp.concatenate(m_rows, axis=0)
        z = proj[rows, ZA + h * HEAD_DIM:ZA + (h + 1) * HEAD_DIM]
        merged[rows, h * HEAD_DIM:(h + 1) * HEAD_DIM] = _rms(o_a, gdn_w) * _silu(z)
        o_gate = proj[rows, OB + h * HEAD_DIM:OB + (h + 1) * HEAD_DIM]
        merged[rows, N_HEADS * HEAD_DIM + h * HEAD_DIM:N_HEADS * HEAD_DIM + (h + 1) * HEAD_DIM] = (
            _rms(h_b, ml_w) * _sigmoid(o_gate))

    @pl.when(i == n_steps - 1)
    def _():
        mix = jnp.dot(_bf(merged[...]), wout_ref[...], preferred_element_type=F32)
        y_ref[...] = _layer_norm(DN_ALPHA * x_ref[...] + mix, ln_ref[0:1, :], ln_ref[1:2, :])


def _ab_decode(x, win, wout, convw, gp, ln, cbuf, s0, c0, n0, m0):
    nb = x.shape[0]
    bb = SUBLANES
    assert nb % bb == 0
    n_steps = nb // bb
    const = lambda shape: pl.BlockSpec(shape, lambda i: (0,) * len(shape))
    blk = lambda shape: pl.BlockSpec((bb,) + shape, lambda i: (i,) + (0,) * len(shape))
    hist = (CONV_W - 1) * A_CONV
    width = N_HEADS * HEAD_DIM
    return pl.pallas_call(
        functools.partial(_ab_decode_kernel, bb=bb, n_steps=n_steps),
        grid=(n_steps,),
        in_specs=[
            const((nb, D_MODEL)), const((D_MODEL, D_IN_PAD)), const((2 * width, D_MODEL)),
            const((CONV_W, A_CONV)), const((SUBLANES, LANES)), const((2, D_MODEL)),
            blk((hist,)), blk((N_HEADS, HEAD_DIM, HEAD_DIM)), blk((N_HEADS, HEAD_DIM, HEAD_DIM)),
            blk((width,)), blk((N_HEADS,)),
        ],
        out_specs=[
            const((nb, D_MODEL)), blk((hist,)), blk((N_HEADS, HEAD_DIM, HEAD_DIM)),
            blk((N_HEADS, HEAD_DIM, HEAD_DIM)), blk((width,)), blk((N_HEADS,)),
        ],
        out_shape=[
            jax.ShapeDtypeStruct((nb, D_MODEL), F32),
            jax.ShapeDtypeStruct((nb, hist), F32),
            jax.ShapeDtypeStruct((nb, N_HEADS, HEAD_DIM, HEAD_DIM), F32),
            jax.ShapeDtypeStruct((nb, N_HEADS, HEAD_DIM, HEAD_DIM), F32),
            jax.ShapeDtypeStruct((nb, width), F32),
            jax.ShapeDtypeStruct((nb, N_HEADS), F32),
        ],
        scratch_shapes=[pltpu.VMEM((nb, D_IN_PAD), F32), pltpu.VMEM((nb, 2 * width), F32)],
        compiler_params=pltpu.CompilerParams(dimension_semantics=("arbitrary",), vmem_limit_bytes=VMEM_LIMIT),
        name="ab_decode",
    )(x, win, wout, convw, gp, ln, cbuf, s0, c0, n0, m0)


def _second_largest_sum(a, b, c, d):
    hi1, lo1 = jnp.maximum(a, b), jnp.minimum(a, b)
    hi2, lo2 = jnp.maximum(c, d), jnp.minimum(c, d)
    return jnp.maximum(hi1, hi2) + jnp.maximum(jnp.minimum(hi1, hi2), jnp.maximum(lo1, lo2))


def _first_argmax(vals):
    best_v = vals[0]
    best_i = jnp.zeros(vals[0].shape, jnp.int32)
    for j in range(1, len(vals)):
        better = vals[j] > best_v
        best_v = jnp.where(better, vals[j], best_v)
        best_i = jnp.where(better, j, best_i)
    return best_i


def _router_kernel(x_ref, rw_ref, rb_ref, meta_ref, cnt_ref, carry, *, n_steps):
    i = pl.program_id(0)
    tm = x_ref.shape[0]

    @pl.when(i == 0)
    def _():
        carry[...] = jnp.zeros_like(carry)

    x = x_ref[...]
    rw = rw_ref[...]
    x_hi = _bf(x)
    x_lo = _bf(x - x_hi.astype(F32))
    w_hi = _bf(rw)
    w_lo = _bf(rw - w_hi.astype(F32))
    nt = lambda a, b: lax.dot_general(a, b, (((1,), (1,)), ((), ())), preferred_element_type=F32)
    logits = nt(w_hi, x_hi) + (nt(w_hi, x_lo) + nt(w_lo, x_hi))
    ex = jnp.exp(logits - jnp.max(logits, axis=0, keepdims=True))
    probs = ex / jnp.sum(ex, axis=0, keepdims=True)
    sel = probs + rb_ref[...]
    p = [probs[j:j + 1, :] for j in range(N_EXPERTS)]
    s = [sel[j:j + 1, :] for j in range(N_EXPERTS)]
    scores = [_second_largest_sum(*s[EXPERTS_PER_GROUP * g:EXPERTS_PER_GROUP * (g + 1)])
              for g in range(N_EXPERT_GROUPS)]
    best = _first_argmax(scores)
    masked = [jnp.where(best == j // EXPERTS_PER_GROUP, s[j], -jnp.inf) for j in range(N_EXPERTS)]
    i1 = _first_argmax(masked)
    i2 = _first_argmax([jnp.where(i1 == j, -jnp.inf, masked[j]) for j in range(N_EXPERTS)])
    zero = jnp.zeros_like(p[0])
    p1 = functools.reduce(lambda a, b: a + b, [jnp.where(i1 == j, p[j], zero) for j in range(N_EXPERTS)])
    p2 = functools.reduce(lambda a, b: a + b, [jnp.where(i2 == j, p[j], zero) for j in range(N_EXPERTS)])
    tot = p1 + p2
    rows = [jnp.where(i1 == j, p1 / tot, zero) + jnp.where(i2 == j, p2 / tot, zero) for j in range(N_EXPERTS)]
    in_group = [best == g for g in range(N_EXPERT_GROUPS)]
    local = [functools.reduce(lambda a, b: a + b,
                              [jnp.where(in_group[g], rows[EXPERTS_PER_GROUP * g + e], zero)
                               for g in range(N_EXPERT_GROUPS)])
             for e in range(EXPERTS_PER_GROUP)]
    onehot = jnp.concatenate([jnp.where(m, 1.0, 0.0) for m in in_group]
                             + [jnp.zeros((SUBLANES - N_EXPERT_GROUPS, tm), F32)], axis=0)
    ri = lax.broadcasted_iota(jnp.int32, (tm, tm), 0)
    ci = lax.broadcasted_iota(jnp.int32, (tm, tm), 1)
    incl = jnp.dot(_bf(onehot), _bf(jnp.where(ri <= ci, 1.0, 0.0)), preferred_element_type=F32)
    prev = carry[...]
    rank = jnp.sum(onehot * (incl - 1.0 + prev[:, 0:1]), axis=0, keepdims=True)
    carry[...] = prev + incl[:, tm - 1:tm]
    row_id = (i * tm + lax.broadcasted_iota(jnp.int32, (1, tm), 1)).astype(F32)
    meta_ref[...] = jnp.concatenate([row_id] + local + [best.astype(F32), rank, zero], axis=0).T

    @pl.when(i == n_steps - 1)
    def _():
        cnt_ref[...] = carry[...]


def _router(x, rw_t, rb_col):
    t = x.shape[0]
    tm = min(256, t)
    assert t % tm == 0
    n_steps = t // tm
    return pl.pallas_call(
        functools.partial(_router_kernel, n_steps=n_steps),
        grid=(n_steps,),
        in_specs=[pl.BlockSpec((tm, D_MODEL), lambda i: (i, 0)),
                  pl.BlockSpec((N_EXPERTS, D_MODEL), lambda i: (0, 0)),
                  pl.BlockSpec((N_EXPERTS, 1), lambda i: (0, 0))],
        out_specs=[pl.BlockSpec((tm, SUBLANES), lambda i: (i, 0)),
                   pl.BlockSpec((SUBLANES, LANES), lambda i: (0, 0))],
        out_shape=[jax.ShapeDtypeStruct((t, SUBLANES), F32), jax.ShapeDtypeStruct((SUBLANES, LANES), F32)],
        scratch_shapes=[pltpu.VMEM((SUBLANES, LANES), F32)],
        compiler_params=pltpu.CompilerParams(dimension_semantics=("arbitrary",), vmem_limit_bytes=VMEM_LIMIT),
        name="router",
    )(x, rw_t, rb_col)


def _moe_kernel(tg_ref, nv_ref, src_cur, dst_cur, src_nxt, cw_ref, x_hbm, wg_ref, wu_ref, wd_ref, ln_ref,
                y_hbm, xg, stage, gsem, ssem, *, tm, n_tiles):
    i = pl.program_id(0)
    slot = lax.rem(i, 2)
    other = 1 - slot
    n_valid = nv_ref[i]
    n_prev = jnp.where(i >= 1, nv_ref[jnp.maximum(i - 1, 0)], 0)
    n_back2 = jnp.where(i >= 2, nv_ref[jnp.maximum(i - 2, 0)], 0)
    has_next = i + 1 < n_tiles

    def gather_row(tok_ref, r, dst_slot):
        return pltpu.make_async_copy(x_hbm.at[pl.ds(tok_ref[0, r], 1), :], xg.at[dst_slot, pl.ds(r, 1), :],
                                     gsem.at[dst_slot])

    def scatter_row(tok_ref, r, src_slot):
        return pltpu.make_async_copy(stage.at[src_slot, pl.ds(r, 1), :], y_hbm.at[pl.ds(tok_ref[0, r], 1), :],
                                     ssem.at[src_slot])

    def scatter_wait(count, src_slot):
        for p in [1 << b for b in range(tm.bit_length())]:
            @pl.when(lax.bitwise_and(count, p) != 0)
            def _():
                pltpu.make_async_copy(stage.at[src_slot, pl.ds(0, p), :], y_hbm.at[pl.ds(0, p), :],
                                      ssem.at[src_slot]).wait()

    @pl.when(i == 0)
    def _():
        def body(r, c):
            gather_row(src_cur, r, 0).start()
            return c
        lax.fori_loop(0, tm, body, 0, unroll=8)

    pltpu.make_async_copy(x_hbm.at[pl.ds(0, tm), :], xg.at[slot], gsem.at[slot]).wait()

    for r in range(tm):
        @pl.when(has_next)
        def _():
            gather_row(src_nxt, r, other).start(priority=r % 2)

    @pl.when(n_valid > 0)
    def _():
        x = xg[slot]
        xb = _bf(x)
        cw = cw_ref[...]
        acc = jnp.zeros((tm, D_MODEL), F32)
        for e in range(EXPERTS_PER_GROUP):
            g = jnp.dot(xb, wg_ref[e], preferred_element_type=F32)
            u = jnp.dot(xb, wu_ref[e], preferred_element_type=F32)
            hid = _silu(g) * u * cw[:, 1 + e:2 + e]
            acc = acc + jnp.dot(_bf(hid), wd_ref[e], preferred_element_type=F32)
        y = _layer_norm(DN_ALPHA * x + acc, ln_ref[0:1, :], ln_ref[1:2, :])

        @pl.when(i >= 2)
        def _():
            scatter_wait(n_back2, slot)
        stage[slot] = y

    @pl.when(jnp.logical_and(n_valid <= 0, i >= 2))
    def _():
        scatter_wait(n_back2, slot)

    for r in range(tm):
        @pl.when(r < n_valid)
        def _():
            scatter_row(dst_cur, r, slot).start(priority=r % 2)

    @pl.when(i == n_tiles - 1)
    def _():
        scatter_wait(n_prev, other)
        scatter_wait(n_valid, slot)


def _moe(x, meta, counts, wg, wu, wd, ln, layer, out_row):
    t = x.shape[0]
    tm = min(256, t)
    assert t % tm == 0 and tm & (tm - 1) == 0
    n_tiles = t // tm + N_EXPERT_GROUPS
    n_slots = n_tiles * tm
    grp = meta[:, 1 + EXPERTS_PER_GROUP].astype(jnp.int32)
    rank = meta[:, 2 + EXPERTS_PER_GROUP].astype(jnp.int32)
    cnt = counts[:N_EXPERT_GROUPS, 0].astype(jnp.int32)
    padded = ((cnt + tm - 1) // tm) * tm
    g_end = jnp.cumsum(padded)
    g_off = g_end - padded
    pos = jnp.take(g_off, grp) + rank
    slot_meta = jnp.zeros((n_slots, SUBLANES), F32).at[pos].set(meta)
    tok_of_slot = slot_meta[:, 0].astype(jnp.int32)
    tile_start = jnp.arange(n_tiles, dtype=jnp.int32) * tm
    tile_grp = jnp.minimum(jnp.sum(tile_start[:, None] >= g_end[None, :], axis=1), N_EXPERT_GROUPS - 1).astype(jnp.int32)
    tile_valid = jnp.clip(jnp.take(g_off + cnt, tile_grp) - tile_start, 0, tm).astype(jnp.int32)
    src3 = tok_of_slot.reshape(n_tiles, 1, tm)
    dst3 = out_row(tok_of_slot).reshape(n_tiles, 1, tm)

    grid_spec = pltpu.PrefetchScalarGridSpec(
        num_scalar_prefetch=2,
        grid=(n_tiles,),
        in_specs=[
            pl.BlockSpec((None, 1, tm), lambda i, tg, nv: (i, 0, 0), memory_space=pltpu.SMEM),
            pl.BlockSpec((None, 1, tm), lambda i, tg, nv: (i, 0, 0), memory_space=pltpu.SMEM),
            pl.BlockSpec((None, 1, tm), lambda i, tg, nv: (jnp.minimum(i + 1, n_tiles - 1), 0, 0),
                         memory_space=pltpu.SMEM),
            pl.BlockSpec((tm, SUBLANES), lambda i, tg, nv: (i, 0)),
            pl.BlockSpec(memory_space=pl.ANY),
            pl.BlockSpec((None, EXPERTS_PER_GROUP, D_MODEL, D_FF), lambda i, tg, nv: (layer, tg[i], 0, 0)),
            pl.BlockSpec((None, EXPERTS_PER_GROUP, D_MODEL, D_FF), lambda i, tg, nv: (layer, tg[i], 0, 0)),
            pl.BlockSpec((None, EXPERTS_PER_GROUP, D_FF, D_MODEL), lambda i, tg, nv: (layer, tg[i], 0, 0)),
            pl.BlockSpec((2, D_MODEL), lambda i, tg, nv: (0, 0)),
        ],
        out_specs=pl.BlockSpec(memory_space=pl.ANY),
        scratch_shapes=[pltpu.VMEM((2, tm, D_MODEL), F32), pltpu.VMEM((2, tm, D_MODEL), F32),
                        pltpu.SemaphoreType.DMA((2,)), pltpu.SemaphoreType.DMA((2,))],
    )
    return pl.pallas_call(
        functools.partial(_moe_kernel, tm=tm, n_tiles=n_tiles),
        grid_spec=grid_spec,
        out_shape=jax.ShapeDtypeStruct((t, D_MODEL), F32),
        compiler_params=pltpu.CompilerParams(dimension_semantics=("arbitrary",), vmem_limit_bytes=VMEM_LIMIT),
        name="moe",
    )(tile_grp, tile_valid, src3, dst3, src3, slot_meta, x, wg, wu, wd, ln)


def _s5_prep_kernel(are_ref, aim_ref, ldt_ref, bre_ref, bim_ref, abre_ref, abim_ref, bbre_ref, bbim_ref):
    a_r = are_ref[...]
    a_i = aim_ref[...]
    dt = jnp.exp(ldt_ref[...])
    mag = jnp.exp(dt * a_r)
    ab_re = mag * jnp.cos(dt * a_i)
    ab_im = mag * jnp.sin(dt * a_i)
    den = a_r * a_r + a_i * a_i
    nr = ab_re - 1.0
    z_re = (nr * a_r + ab_im * a_i) / den
    z_im = (ab_im * a_r - nr * a_i) / den
    abre_ref[...] = ab_re
    abim_ref[...] = ab_im
    bbre_ref[...] = z_re * bre_ref[...] - z_im * bim_ref[...]
    bbim_ref[...] = z_re * bim_ref[...] + z_im * bre_ref[...]


def _s5_prep(a_re, a_im, log_dt, b_re, b_im):
    n = S5_GROUPS * S5_STATE
    col = jax.ShapeDtypeStruct((n, 1), F32)
    mat = jax.ShapeDtypeStruct((n, S5_GROUP), F32)
    ldt = jnp.broadcast_to(log_dt[:, None], (S5_GROUPS, S5_STATE)).reshape(n, 1)
    return pl.pallas_call(_s5_prep_kernel, out_shape=[col, col, mat, mat], name="s5_prep")(
        a_re.reshape(n, 1), a_im.reshape(n, 1), ldt, b_re.reshape(n, S5_GROUP), b_im.reshape(n, S5_GROUP))


def _gelu_tanh(x):
    return 0.5 * x * (1.0 + jnp.tanh(0.7978845608028654 * (x + 0.044715 * (x * x * x))))


def _s5_kernel(x_ref, h0re_ref, h0im_ref, wre_ref, wim_ref, cre_ref, cim_ref, abre_ref, abim_ref, d_ref,
               gy_ref, hre_out, him_out, bu_re, bu_im, hre_s, him_s, *, nb, tt, n_t):
    t = pl.program_id(0)
    sw = wre_ref.shape[2]

    @pl.when(t == 0)
    def _():
        hre_s[...] = h0re_ref[...]
        him_s[...] = h0im_ref[...]

    for k in range(D_MODEL // LANES):
        ls = slice(k * LANES, (k + 1) * LANES)
        ss = slice(k * sw, (k + 1) * sw)
        buf = k % 2
        xv = x_ref[:, ls]
        xb = _bf(xv)
        bu_re[buf] = jnp.dot(xb, wre_ref[k], preferred_element_type=F32)
        bu_im[buf] = jnp.dot(xb, wim_ref[k], preferred_element_type=F32)
        a_re = jnp.broadcast_to(abre_ref[:, ss], (nb, sw))
        a_im = jnp.broadcast_to(abim_ref[:, ss], (nb, sw))

        def step(s, carry, buf=buf, a_re=a_re, a_im=a_im):
            h_re, h_im = carry
            rows = pl.ds(pl.multiple_of(s * nb, nb), nb)
            n_re = a_re * h_re - a_im * h_im + bu_re[buf, rows, :]
            n_im = a_re * h_im + a_im * h_re + bu_im[buf, rows, :]
            bu_re[buf, rows, :] = n_re
            bu_im[buf, rows, :] = n_im
            return n_re, n_im

        h_re, h_im = lax.fori_loop(0, tt, step, (hre_s[:, ss], him_s[:, ss]), unroll=min(tt, S5_UNROLL))
        hre_s[:, ss] = h_re
        him_s[:, ss] = h_im
        y = (jnp.dot(_bf(bu_re[buf]), cre_ref[k], preferred_element_type=F32)
             - jnp.dot(_bf(bu_im[buf]), cim_ref[k], preferred_element_type=F32)
             + d_ref[:, ls] * xv)
        gy_ref[:, ls] = _gelu_tanh(y).astype(gy_ref.dtype)

    @pl.when(t == n_t - 1)
    def _():
        hre_out[...] = hre_s[...]
        him_out[...] = him_s[...]


def _s5_scan(x, nb, h0_re, h0_im, wre, wim, cre, cim, ab_re, ab_im, d_skip):
    length = x.shape[0] // nb
    tt = min(S5_TT, length)
    assert length % tt == 0 and nb % SUBLANES == 0
    n_t = length // tt
    n_k = D_MODEL // LANES
    sw = (LANES // S5_GROUP) * S5_STATE
    n_state = S5_GROUPS * S5_STATE
    const = lambda shape: pl.BlockSpec(shape, lambda t: (0,) * len(shape))
    return pl.pallas_call(
        functools.partial(_s5_kernel, nb=nb, tt=tt, n_t=n_t),
        grid=(n_t,),
        in_specs=[
            pl.BlockSpec((tt * nb, D_MODEL), lambda t: (t, 0)),
            const((nb, n_state)), const((nb, n_state)),
            const((n_k, LANES, sw)), const((n_k, LANES, sw)), const((n_k, sw, LANES)), const((n_k, sw, LANES)),
            const((1, n_state)), const((1, n_state)), const((1, D_MODEL)),
        ],
        out_specs=[pl.BlockSpec((tt * nb, D_MODEL), lambda t: (t, 0)), const((nb, n_state)), const((nb, n_state))],
        out_shape=[
            jax.ShapeDtypeStruct((length * nb, D_MODEL), BF16),
            jax.ShapeDtypeStruct((nb, n_state), F32),
            jax.ShapeDtypeStruct((nb, n_state), F32),
        ],
        scratch_shapes=[
            pltpu.VMEM((2, nb * tt, sw), F32), pltpu.VMEM((2, nb * tt, sw), F32),
            pltpu.VMEM((nb, n_state), F32), pltpu.VMEM((nb, n_state), F32),
        ],
        compiler_params=pltpu.CompilerParams(dimension_semantics=("arbitrary",), vmem_limit_bytes=VMEM_LIMIT),
        name="s5_scan",
    )(x, h0_re, h0_im, wre, wim, cre, cim, ab_re, ab_im, d_skip)


def _glu_ln_kernel(gy_ref, h_ref, wa_ref, wb_ref, ln_ref, o_ref):
    gy = gy_ref[...]
    a = jnp.dot(gy, wa_ref[...], preferred_element_type=F32)
    b = jnp.dot(gy, wb_ref[...], preferred_element_type=F32)
    o_ref[...] = _layer_norm(DN_ALPHA * h_ref[...] + a * _sigmoid(b), ln_ref[0:1, :], ln_ref[1:2, :])


def _glu_ln(gy, h, wa, wb, ln):
    t = h.shape[0]
    tm = min(512, t)
    assert t % tm == 0
    return pl.pallas_call(
        _glu_ln_kernel,
        grid=(t // tm,),
        in_specs=[pl.BlockSpec((tm, D_MODEL), lambda i: (i, 0)), pl.BlockSpec((tm, D_MODEL), lambda i: (i, 0)),
                  pl.BlockSpec((D_MODEL, D_MODEL), lambda i: (0, 0)), pl.BlockSpec((D_MODEL, D_MODEL), lambda i: (0, 0)),
                  pl.BlockSpec((2, D_MODEL), lambda i: (0, 0))],
        out_specs=pl.BlockSpec((tm, D_MODEL), lambda i: (i, 0)),
        out_shape=jax.ShapeDtypeStruct((t, D_MODEL), F32),
        compiler_params=pltpu.CompilerParams(dimension_semantics=("arbitrary",), vmem_limit_bytes=VMEM_LIMIT),
        name="glu_ln",
    )(gy, h, wa, wb, ln)


def _block_diag_slices(m, rows_per_group, cols_per_group):
    gps = LANES // S5_GROUP
    m = m.reshape(S5_GROUPS // gps, gps, rows_per_group, cols_per_group)
    eye = jnp.eye(gps, dtype=m.dtype)
    out = m[:, :, :, None, :] * eye[None, :, None, :, None]
    return out.reshape(S5_GROUPS // gps, gps * rows_per_group, gps * cols_per_group)


def _prepare(p):
    w = p['w_in'][0]
    win = jnp.concatenate(
        [w[:, 0:1536], w[:, 1544:2056], w[:, 2056:3592], w[:, 3600:4112], w[:, 1536:1544], w[:, 3592:3600],
         jnp.zeros((D_MODEL, D_IN_PAD - 4112), w.dtype)], axis=1).astype(BF16)
    at_lane = lambda v, lane0: jnp.pad(v, (lane0, LANES - lane0 - v.shape[0]))
    gp = jnp.stack([at_lane(p['gdn_A_log'][0], G_DEC), at_lane(p['gdn_dt_bias'][0], G_DEC),
                    at_lane(p['ml_b_i'][0], G_IN), at_lane(p['ml_b_f'][0], G_FG),
                    p['gdn_norm_w'][0], p['ml_norm_w'][0], jnp.zeros((LANES,), F32), jnp.zeros((LANES,), F32)])
    ab_re, ab_im, bb_re, bb_im = _s5_prep(p['s5_A_re'][0], p['s5_A_im'][0], p['s5_log_dt'][0],
                                          p['s5_B_re'][0], p['s5_B_im'][0])
    to_in = lambda bb: _block_diag_slices(
        bb.reshape(S5_GROUPS, S5_STATE, S5_GROUP).transpose(0, 2, 1), S5_GROUP, S5_STATE).astype(BF16)
    to_out = lambda c: _block_diag_slices(c.transpose(0, 2, 1), S5_STATE, S5_GROUP).astype(BF16)
    return dict(
        win=win, wout=p['w_out'][0].astype(BF16), convw=p['gdn_conv_w'][0], gp=gp,
        ln_mix=[jnp.stack([p['ln_mix_g'][l], p['ln_mix_b'][l]]) for l in range(DEPTH)],
        ln_ffn=[jnp.stack([p['ln_ffn_g'][l], p['ln_ffn_b'][l]]) for l in range(DEPTH)],
        rw_t=p['router_w'].T, rb_col=p['router_b'][:, None],
        wg=p['moe_w_gate'].astype(BF16), wu=p['moe_w_up'].astype(BF16), wd=p['moe_w_down'].astype(BF16),
        s5_wre=to_in(bb_re), s5_wim=to_in(bb_im),
        s5_cre=to_out(p['s5_C_re'][0]), s5_cim=to_out(p['s5_C_im'][0]),
        s5_abre=ab_re.reshape(1, -1), s5_abim=ab_im.reshape(1, -1), s5_d=p['s5_D'][0][None, :],
        glu_a=p['s5_w_glu_a'][0].astype(BF16), glu_b=p['s5_w_glu_b'][0].astype(BF16),
    )


def _ffn(h, w, layer, out_row):
    meta, counts = _router(h, w['rw_t'], w['rb_col'])
    return _moe(h, meta, counts, w['wg'], w['wu'], w['wd'], w['ln_ffn'][layer], layer, out_row)


def _s5_layer(h, nb, h0_re, h0_im, w):
    gy, re, im = _s5_scan(h, nb, h0_re, h0_im, w['s5_wre'], w['s5_wim'],
                          w['s5_cre'], w['s5_cim'], w['s5_abre'], w['s5_abim'], w['s5_d'])
    h = _glu_ln(gy, h, w['glu_a'], w['glu_b'], w['ln_mix'][1])
    return h, re, im


def kernel(x_prompt, x_sample, state_gdn_conv, state_gdn_S, state_mlstm_C, state_mlstm_n, state_mlstm_m,
           state_s5_re, state_s5_im, w_in, gdn_conv_w, gdn_A_log, gdn_dt_bias, gdn_norm_w, ml_b_i, ml_b_f,
           ml_norm_w, w_out, s5_A_re, s5_A_im, s5_log_dt, s5_B_re, s5_B_im, s5_C_re, s5_C_im, s5_D,
           s5_w_glu_a, s5_w_glu_b, router_w, router_b, moe_w_gate, moe_w_up, moe_w_down,
           ln_mix_g, ln_mix_b, ln_ffn_g, ln_ffn_b):
    w = _prepare(dict(
        w_in=w_in, gdn_conv_w=gdn_conv_w, gdn_A_log=gdn_A_log, gdn_dt_bias=gdn_dt_bias, gdn_norm_w=gdn_norm_w,
        ml_b_i=ml_b_i, ml_b_f=ml_b_f, ml_norm_w=ml_norm_w, w_out=w_out, s5_A_re=s5_A_re, s5_A_im=s5_A_im,
        s5_log_dt=s5_log_dt, s5_B_re=s5_B_re, s5_B_im=s5_B_im, s5_C_re=s5_C_re, s5_C_im=s5_C_im, s5_D=s5_D,
        s5_w_glu_a=s5_w_glu_a, s5_w_glu_b=s5_w_glu_b, router_w=router_w, router_b=router_b,
        moe_w_gate=moe_w_gate, moe_w_up=moe_w_up, moe_w_down=moe_w_down,
        ln_mix_g=ln_mix_g, ln_mix_b=ln_mix_b, ln_ffn_g=ln_ffn_g, ln_ffn_b=ln_ffn_b))
    bp, lp, _ = x_prompt.shape
    bs, ls, _ = x_sample.shape
    assert ls == 1
    n_state = S5_GROUPS * S5_STATE

    h, p_hist, p_s, p_c, p_n, p_m = _ab_prompt(x_prompt, w['win'], w['wout'], w['convw'], w['gp'], w['ln_mix'][0])
    h = _ffn(h.reshape(bp * lp, D_MODEL), w, 0, lambda r: (r % lp) * bp + r // lp)
    zeros = jnp.zeros((bp, n_state), F32)
    h, p_re, p_im = _s5_layer(h, bp, zeros, zeros, w)
    y_prompt = _ffn(h, w, 1, lambda r: (r % bp) * lp + r // bp).reshape(bp, lp, D_MODEL)

    hs, s_cbuf, s_s, s_c, s_n, s_m = _ab_decode(
        x_sample[:, 0], w['win'], w['wout'], w['convw'], w['gp'], w['ln_mix'][0],
        state_gdn_conv.reshape(bs, (CONV_W - 1) * A_CONV), state_gdn_S[:, 0], state_mlstm_C[:, 0],
        state_mlstm_n.reshape(bs, N_HEADS * HEAD_DIM), state_mlstm_m[:, 0])
    hs = _ffn(hs, w, 0, lambda r: r)
    hs, s_re, s_im = _s5_layer(hs, bs, state_s5_re.reshape(bs, n_state), state_s5_im.reshape(bs, n_state), w)
    y_sample = _ffn(hs, w, 1, lambda r: r).reshape(bs, 1, D_MODEL)

    grp = lambda a, n: a.reshape(n, 1, S5_GROUPS, S5_STATE)
    return (
        y_prompt, y_sample,
        p_hist[:, None, SUBLANES - (CONV_W - 1):, :], p_s[:, None], p_c[:, None],
        p_n[:, None, :, 0, :], p_m[:, None, :, 0, 0], grp(p_re, bp), grp(p_im, bp),
        s_cbuf.reshape(bs, 1, CONV_W - 1, A_CONV), s_s[:, None], s_c[:, None],
        s_n.reshape(bs, 1, N_HEADS, HEAD_DIM), s_m[:, None], grp(s_re, bs), grp(s_im, bs),
    )
```

```python
import functools

import jax
import jax.numpy as jnp
from jax import lax
from jax.experimental import pallas as pl
from jax.experimental.pallas import tpu as pltpu

F32 = jnp.float32
BF16 = jnp.bfloat16
HIGHEST = lax.Precision.HIGHEST

D_MODEL = 1024
DEPTH = 2
N_HEADS = 4
HEAD_DIM = 128
CONV_W = 4
CHUNK = 64
A_CONV = 3 * N_HEADS * HEAD_DIM
S5_GROUP = 16
S5_GROUPS = D_MODEL // S5_GROUP
S5_STATE = 64
N_EXPERTS = 16
EXPERTS_PER_GROUP = 4
N_EXPERT_GROUPS = N_EXPERTS // EXPERTS_PER_GROUP
D_FF = 512
DN_ALPHA = (2 * DEPTH) ** 0.25
LN_EPS = 1e-5
RMS_EPS = 1e-6
NEG_BIG = -1e30

QA, KA, VA, ZA, QB, KB, VB, OB, GT = 0, 512, 1024, 1536, 2048, 2560, 3072, 3584, 4096
D_IN_PAD = 4224
G_DEC, G_BETA, G_IN, G_FG = 0, 4, 8, 12

S5_TT = 64
S5_UNROLL = 64

LANES = 128
SUBLANES = 8
VMEM_LIMIT = 56 * 1024 * 1024


def _bf(x):
    return x.astype(BF16)


def _nn(a, b):
    return jnp.dot(_bf(a), _bf(b), preferred_element_type=F32)


def _nt(a, b):
    return lax.dot_general(_bf(a), _bf(b), (((1,), (1,)), ((), ())), preferred_element_type=F32)


def _tn(a, b):
    return lax.dot_general(_bf(a), _bf(b), (((0,), (0,)), ((), ())), preferred_element_type=F32)


def _sigmoid(x):
    return 1.0 / (1.0 + jnp.exp(-x))


def _softplus(x):
    return jnp.maximum(x, 0.0) + jnp.log(1.0 + jnp.exp(-jnp.abs(x)))


def _silu(x):
    return x * _sigmoid(x)


def _layer_norm(y, g, b):
    mu = jnp.mean(y, axis=-1, keepdims=True)
    yc = y - mu
    var = jnp.mean(yc * yc, axis=-1, keepdims=True)
    return yc * lax.rsqrt(var + LN_EPS) * g + b


def _rms(x, w):
    return x * lax.rsqrt(jnp.mean(x * x, axis=-1, keepdims=True) + RMS_EPS) * w


def _gate_transform(raw, gp):
    lane = lax.broadcasted_iota(jnp.int32, raw.shape, 1)
    dec = -jnp.exp(gp[0:1, :]) * _softplus(raw + gp[1:2, :])
    beta = _sigmoid(raw)
    ipre = raw + gp[2:3, :]
    logf = -_softplus(-(raw + gp[3:4, :]))
    return jnp.where(lane < G_BETA, dec,
                     jnp.where(lane < G_IN, beta,
                               jnp.where(lane < G_FG, ipre,
                                         jnp.where(lane < G_FG + N_HEADS, logf, 0.0))))


def _bnn(a, b):
    return lax.dot_general(_bf(a), _bf(b), (((2,), (1,)), ((0,), (0,))), preferred_element_type=F32)


def _bnt(a, b):
    return lax.dot_general(_bf(a), _bf(b), (((2,), (2,)), ((0,), (0,))), preferred_element_type=F32)


def _btn(a, b):
    return lax.dot_general(_bf(a), _bf(b), (((1,), (1,)), ((0,), (0,))), preferred_element_type=F32)


def _unit_lower_inverse_minus_eye(a):
    c = a.shape[-1]
    r = -a
    steps = max(1, (c - 1).bit_length()) - 1
    q = _bnn(r, r)
    for i in range(steps):
        rq = _bnn(r, q)
        qq = _bnn(q, q) if i + 1 < steps else None
        r = r + q + rq
        q = qq
    return r


def _ab_prompt_kernel(x_ref, win_ref, wout_ref, convw_ref, gp_ref, ln_ref,
                      h_ref, hist_ref, s_out, c_out, n_out, m_out,
                      proj, qkv, gates, gcum, merged, s_s, c_s, n_s, m_s,
                      u_s, w_s, attn_s, dlog_s, qk_s, dmax_s, *, tb, n_t):
    t = pl.program_id(1)
    nc = tb // CHUNK

    @pl.when(t == 0)
    def _():
        proj[0:SUBLANES, :] = jnp.zeros((SUBLANES, D_IN_PAD), F32)
        s_s[...] = jnp.zeros_like(s_s)
        c_s[...] = jnp.zeros_like(c_s)
        n_s[...] = jnp.zeros_like(n_s)
        m_s[...] = jnp.zeros_like(m_s)

    x = x_ref[...]
    xb = _bf(x)
    proj[SUBLANES:SUBLANES + tb, 0:A_CONV] = jnp.dot(xb, win_ref[:, 0:A_CONV], preferred_element_type=F32)
    proj[SUBLANES:SUBLANES + tb, A_CONV:] = jnp.dot(xb, win_ref[:, A_CONV:], preferred_element_type=F32)

    for blk in range(A_CONV // LANES):
        cs = slice(blk * LANES, (blk + 1) * LANES)
        acc = proj[SUBLANES:SUBLANES + tb, cs] * convw_ref[CONV_W - 1:CONV_W, cs]
        for j in range(1, CONV_W):
            acc = acc + proj[SUBLANES - j:SUBLANES - j + tb, cs] * convw_ref[CONV_W - 1 - j:CONV_W - j, cs]
        y = _silu(acc)
        if blk < 2 * N_HEADS:
            y = y * lax.rsqrt(jnp.sum(y * y, axis=-1, keepdims=True) + RMS_EPS)
            if blk < N_HEADS:
                y = y * HEAD_DIM ** -0.5
        qkv[:, cs] = y

    gt = _gate_transform(proj[SUBLANES:SUBLANES + tb, GT:GT + LANES], gp_ref[...])
    gates[...] = gt
    ri = lax.broadcasted_iota(jnp.int32, (tb, tb), 0)
    ci = lax.broadcasted_iota(jnp.int32, (tb, tb), 1)
    same_chunk = lax.shift_right_logical(ri, 6) == lax.shift_right_logical(ci, 6)
    ltri = jnp.where(same_chunk, jnp.where(ri >= ci, 1.0, 0.0), 0.0)
    gcum[...] = jnp.dot(ltri, gt, preferred_element_type=F32, precision=HIGHEST)

    ii = lax.broadcasted_iota(jnp.int32, (CHUNK, CHUNK), 0)
    jj = lax.broadcasted_iota(jnp.int32, (CHUNK, CHUNK), 1)
    incl = ii >= jj
    strict = ii > jj
    gdn_w = gp_ref[4:5, :]
    ml_w = gp_ref[5:6, :]

    pairs = [(c, h) for c in range(nc) for h in range(N_HEADS)]

    def tile_heads(ref, row0, col0):
        return jnp.stack([ref[row0 + c * CHUNK:row0 + (c + 1) * CHUNK, col0 + h * HEAD_DIM:col0 + (h + 1) * HEAD_DIM]
                          for c, h in pairs])

    def tile_cols(ref, lane0):
        return jnp.stack([ref[c * CHUNK:(c + 1) * CHUNK, lane0 + h:lane0 + h + 1] for c, h in pairs])

    def tile_rows(transposed, lane0):
        return jnp.stack([transposed[c][lane0 + h:lane0 + h + 1, :] for c, h in pairs])

    cs_t = [gcum[c * CHUNK:(c + 1) * CHUNK, :].T for c in range(nc)]
    gt_t = [gates[c * CHUNK:(c + 1) * CHUNK, :].T for c in range(nc)]
    q3 = tile_heads(qkv, 0, QA)
    k3 = tile_heads(qkv, 0, KA)
    v3 = tile_heads(qkv, 0, VA)
    g_col3 = tile_cols(gcum, G_DEC)
    beta3 = tile_cols(gates, G_BETA)
    decay3 = jnp.where(incl, jnp.exp(jnp.where(incl, g_col3 - tile_rows(cs_t, G_DEC), 0.0)), 0.0)
    kb3 = k3 * beta3
    a_low3 = jnp.where(strict, _bnt(kb3, k3) * decay3, 0.0)
    attn_s[...] = _bnt(q3, k3) * decay3
    qk_s[...] = _bnt(tile_heads(proj, SUBLANES, QB), tile_heads(proj, SUBLANES, KB) * HEAD_DIM ** -0.5)
    r3 = _unit_lower_inverse_minus_eye(a_low3)
    rhs3 = jnp.concatenate([v3 * beta3, kb3 * jnp.exp(g_col3)], axis=2)
    uw3 = rhs3 + _bnn(r3, rhs3)
    u_s[...] = uw3[:, :, :HEAD_DIM]
    w_s[...] = uw3[:, :, HEAD_DIM:]
    dlog3 = jnp.where(incl, tile_cols(gcum, G_FG) - tile_rows(cs_t, G_FG) + tile_rows(gt_t, G_IN), NEG_BIG)
    dlog_s[...] = dlog3
    dmax_s[...] = jnp.max(dlog3, axis=-1, keepdims=True)

    def chunk_body(c, carry):
        r0 = pl.multiple_of(c * CHUNK, CHUNK)
        rows = pl.ds(r0, CHUNK)
        prow = pl.ds(pl.multiple_of(r0 + SUBLANES, SUBLANES), CHUNK)
        last = pl.ds(r0 + CHUNK - 1, 1)
        pc = pl.ds(pl.multiple_of(c * N_HEADS, N_HEADS), N_HEADS)

        def heads(ref, rws, col0):
            return jnp.stack([ref[rws, col0 + h * HEAD_DIM:col0 + (h + 1) * HEAD_DIM] for h in range(N_HEADS)])

        def cols(ref, rws, lane0):
            return jnp.stack([ref[rws, lane0 + h:lane0 + h + 1] for h in range(N_HEADS)])

        q = heads(qkv, rows, QA)
        k = heads(qkv, rows, KA)
        g_col = cols(gcum, rows, G_DEC)
        g_last = cols(gcum, last, G_DEC)
        qb = heads(proj, prow, QB)
        kbm = heads(proj, prow, KB) * HEAD_DIM ** -0.5
        vbm = heads(proj, prow, VB)
        b_col = cols(gcum, rows, G_FG)
        b_last = cols(gcum, last, G_FG)
        i_col = cols(gates, rows, G_IN)
        s_old = s_s[...]
        c_old = c_s[...]
        n_old = n_s[:, 0:1, :]
        m_old = m_s[:, 0:1, 0:1]
        inter = b_col + m_old
        mt = jnp.maximum(inter, dmax_s[pc])
        wts = jnp.exp(dlog_s[pc] - mt) * qk_s[pc]
        sc = jnp.exp(inter - mt)
        m_new = mt[:, CHUNK - 1:CHUNK, :]
        sd = jnp.exp(b_last + m_old - m_new)
        kw = kbm * jnp.exp(b_last - b_col + i_col - m_new)
        w_state = _bnn(w_s[pc], s_old)
        q_state = _bnn(q * jnp.exp(g_col), s_old)
        q_mem = _bnn(qb, c_old)
        w_val = _bnn(wts, vbm)
        kv = _btn(kw, vbm)
        v_new = u_s[pc] - w_state
        o_a = q_state + _bnn(attn_s[pc], v_new)
        s_s[...] = s_old * jnp.exp(g_last) + _btn(k * jnp.exp(g_last - g_col), v_new)
        num = sc * q_mem + w_val
        den = sc * jnp.sum(qb * n_old, axis=-1, keepdims=True) + jnp.sum(wts, axis=-1, keepdims=True)
        h_b = num / jnp.maximum(jnp.abs(den), jnp.exp(-mt))
        c_s[...] = sd * c_old + kv
        n_s[...] = jnp.broadcast_to(sd * n_old + jnp.sum(kw, axis=1, keepdims=True), n_s.shape)
        m_s[...] = jnp.broadcast_to(m_new, m_s.shape)
        o_n = _rms(o_a, gdn_w)
        h_n = _rms(h_b, ml_w)
        for h in range(N_HEADS):
            hs = slice(h * HEAD_DIM, (h + 1) * HEAD_DIM)
            hs2 = slice((N_HEADS + h) * HEAD_DIM, (N_HEADS + h + 1) * HEAD_DIM)
            merged[rows, hs] = o_n[h] * _silu(proj[prow, ZA + h * HEAD_DIM:ZA + (h + 1) * HEAD_DIM])
            merged[rows, hs2] = h_n[h] * _sigmoid(proj[prow, OB + h * HEAD_DIM:OB + (h + 1) * HEAD_DIM])
        return carry

    lax.fori_loop(0, nc, chunk_body, 0)

    mix = jnp.dot(_bf(merged[...]), wout_ref[...], preferred_element_type=F32)
    h_ref[...] = _layer_norm(DN_ALPHA * x + mix, ln_ref[0:1, :], ln_ref[1:2, :])

    proj[0:SUBLANES, 0:A_CONV] = proj[tb:tb + SUBLANES, 0:A_CONV]

    @pl.when(t == n_t - 1)
    def _():
        hist_ref[...] = proj[tb:tb + SUBLANES, 0:A_CONV]
        s_out[...] = s_s[...]
        c_out[...] = c_s[...]
        n_out[...] = n_s[...]
        m_out[...] = m_s[...]


def _ab_prompt(x, win, wout, convw, gp, ln):
    bsz, length, _ = x.shape
    tb = min(256, length)
    assert length % tb == 0 and tb % CHUNK == 0 and length >= SUBLANES
    n_t = length // tb
    n_pairs = (tb // CHUNK) * N_HEADS
    const = lambda shape: pl.BlockSpec(shape, lambda b, t: (0,) * len(shape))
    per_b =lambda shape: pl.BlockSpec((None,) + shape, lambda b, t: (b,) + (0,) * len(shape))
    return pl.pallas_call(
        functools.partial(_ab_prompt_kernel, tb=tb, n_t=n_t),
        grid=(bsz, n_t),
        in_specs=[
            pl.BlockSpec((None, tb, D_MODEL), lambda b, t: (b, t, 0)),
            const((D_MODEL, D_IN_PAD)), const((2 * N_HEADS * HEAD_DIM, D_MODEL)),
            const((CONV_W, A_CONV)), const((SUBLANES, LANES)), const((2, D_MODEL)),
        ],
        out_specs=[
            pl.BlockSpec((None, tb, D_MODEL), lambda b, t: (b, t, 0)),
            per_b((SUBLANES, A_CONV)), per_b((N_HEADS, HEAD_DIM, HEAD_DIM)), per_b((N_HEADS, HEAD_DIM, HEAD_DIM)),
            per_b((N_HEADS, SUBLANES, LANES)), per_b((N_HEADS, SUBLANES, LANES)),
        ],
        out_shape=[
            jax.ShapeDtypeStruct((bsz, length, D_MODEL), F32),
            jax.ShapeDtypeStruct((bsz, SUBLANES, A_CONV), F32),
            jax.ShapeDtypeStruct((bsz, N_HEADS, HEAD_DIM, HEAD_DIM), F32),
            jax.ShapeDtypeStruct((bsz, N_HEADS, HEAD_DIM, HEAD_DIM), F32),
            jax.ShapeDtypeStruct((bsz, N_HEADS, SUBLANES, LANES), F32),
            jax.ShapeDtypeStruct((bsz, N_HEADS, SUBLANES, LANES), F32),
        ],
        scratch_shapes=[
            pltpu.VMEM((tb + SUBLANES, D_IN_PAD), F32),
            pltpu.VMEM((tb, A_CONV), F32),
            pltpu.VMEM((tb, LANES), F32),
            pltpu.VMEM((tb, LANES), F32),
            pltpu.VMEM((tb, 2 * N_HEADS * HEAD_DIM), F32),
            pltpu.VMEM((N_HEADS, HEAD_DIM, HEAD_DIM), F32),
            pltpu.VMEM((N_HEADS, HEAD_DIM, HEAD_DIM), F32),
            pltpu.VMEM((N_HEADS, SUBLANES, LANES), F32),
            pltpu.VMEM((N_HEADS, SUBLANES, LANES), F32),
            pltpu.VMEM((n_pairs, CHUNK, HEAD_DIM), F32),
            pltpu.VMEM((n_pairs, CHUNK, HEAD_DIM), F32),
            pltpu.VMEM((n_pairs, CHUNK, CHUNK), F32),
            pltpu.VMEM((n_pairs, CHUNK, CHUNK), F32),
            pltpu.VMEM((n_pairs, CHUNK, CHUNK), F32),
            pltpu.VMEM((n_pairs, CHUNK, 1), F32),
        ],
        compiler_params=pltpu.CompilerParams(
            dimension_semantics=("arbitrary", "arbitrary"), vmem_limit_bytes=VMEM_LIMIT),
        name="ab_prompt",
    )(x, win, wout, convw, gp, ln)


def _ab_decode_kernel(x_ref, win_ref, wout_ref, convw_ref, gp_ref, ln_ref, cbuf_ref, s_in, c_in, n_in, m_in,
                      y_ref, cbuf_out, s_out, c_out, n_out, m_out, proj, merged, *, bb, n_steps):
    i = pl.program_id(0)

    @pl.when(i == 0)
    def _():
        proj[...] = jnp.dot(_bf(x_ref[...]), win_ref[...], preferred_element_type=F32)

    rows = pl.ds(pl.multiple_of(i * bb, bb), bb)
    raw = proj[rows, 0:A_CONV]
    cbuf = cbuf_ref[...]
    conv = raw * convw_ref[CONV_W - 1:CONV_W, :]
    for j in range(CONV_W - 1):
        conv = conv + cbuf[:, j * A_CONV:(j + 1) * A_CONV] * convw_ref[j:j + 1, :]
    cbuf_out[:, 0:(CONV_W - 2) * A_CONV] = cbuf[:, A_CONV:(CONV_W - 1) * A_CONV]
    cbuf_out[:, (CONV_W - 2) * A_CONV:(CONV_W - 1) * A_CONV] = raw
    act = _silu(conv)
    gt = _gate_transform(proj[rows, GT:GT + LANES], gp_ref[...])
    gdn_w = gp_ref[4:5, :]
    ml_w = gp_ref[5:6, :]
    m_all = m_in[...]

    for h in range(N_HEADS):
        def head(off):
            return act[:, off + h * HEAD_DIM:off + (h + 1) * HEAD_DIM]
        q = head(QA)
        q = q * lax.rsqrt(jnp.sum(q * q, axis=-1, keepdims=True) + RMS_EPS) * HEAD_DIM ** -0.5
        k = head(KA)
        k = k * lax.rsqrt(jnp.sum(k * k, axis=-1, keepdims=True) + RMS_EPS)
        v = head(VA)
        q_t = q.T
        k_t = k.T
        qk = jnp.sum(q * k, axis=-1, keepdims=True)
        qb = proj[rows, QB + h * HEAD_DIM:QB + (h + 1) * HEAD_DIM]
        kbm = proj[rows, KB + h * HEAD_DIM:KB + (h + 1) * HEAD_DIM] * HEAD_DIM ** -0.5
        vbm = proj[rows, VB + h * HEAD_DIM:VB + (h + 1) * HEAD_DIM]
        qb_t = qb.T
        kb_t = kbm.T
        qkb = jnp.sum(qb * kbm, axis=-1, keepdims=True)
        n_old = n_in[:, h * HEAD_DIM:(h + 1) * HEAD_DIM]
        qn = jnp.sum(qb * n_old, axis=-1, keepdims=True)
        o_rows = []
        hb_rows = []
        n_rows = []
        m_rows = []
        for b in range(bb):
            s_old = s_in[b, h]
            k_c = k_t[:, b:b + 1]
            q_c = q_t[:, b:b + 1]
            e_g = jnp.exp(gt[b:b + 1, G_DEC + h:G_DEC + h + 1])
            beta = gt[b:b + 1, G_BETA + h:G_BETA + h + 1]
            k_s = jnp.sum(k_c * s_old, axis=0, keepdims=True)
            q_s = jnp.sum(q_c * s_old, axis=0, keepdims=True)
            v_new = v[b:b + 1, :] * beta - (beta * e_g) * k_s
            o_rows.append(e_g * q_s + qk[b:b + 1, :] * v_new)
            s_out[b, h] = s_old * e_g + k_c * v_new
            c_old = c_in[b, h]
            m_old = m_all[b:b + 1, h:h + 1]
            i_pre = gt[b:b + 1, G_IN + h:G_IN + h + 1]
            logf = gt[b:b + 1, G_FG + h:G_FG + h + 1]
            inter = logf + m_old
            mt = jnp.maximum(inter, i_pre)
            w_in = jnp.exp(i_pre - mt)
            sc = jnp.exp(inter - mt)
            wts = w_in * qkb[b:b + 1, :]
            q_cm = jnp.sum(qb_t[:, b:b + 1] * c_old, axis=0, keepdims=True)
            num = sc * q_cm + wts * vbm[b:b + 1, :]
            den = sc * qn[b:b + 1, :] + wts
            hb_rows.append(num / jnp.maximum(jnp.abs(den), jnp.exp(-mt)))
            c_out[b, h] = sc * c_old + (w_in * kb_t[:, b:b + 1]) * vbm[b:b + 1, :]
            n_rows.append(sc * n_old[b:b + 1, :] + w_in * kbm[b:b + 1, :])
            m_rows.append(mt)
        o_a = jnp.concatenate(o_rows, axis=0)
        h_b = jnp.concatenate(hb_rows, axis=0)
        n_out[:, h * HEAD_DIM:(h + 1) * HEAD_DIM] = jnp.concatenate(n_rows, axis=0)
        m_out[:, h:h + 1] = jnp.concatenate(m_rows, axis=0)
        z = proj[rows, ZA + h * HEAD_DIM:ZA + (h + 1) * HEAD_DIM]
        merged[rows, h * HEAD_DIM:(h + 1) * HEAD_DIM] = _rms(o_a, gdn_w) * _silu(z)
        o_gate = proj[rows, OB + h * HEAD_DIM:OB + (h + 1) * HEAD_DIM]
        merged[rows, N_HEADS * HEAD_DIM + h * HEAD_DIM:N_HEADS * HEAD_DIM + (h + 1) * HEAD_DIM] = (
            _rms(h_b, ml_w) * _sigmoid(o_gate))

    @pl.when(i == n_steps - 1)
    def _():
        mix = jnp.dot(_bf(merged[...]), wout_ref[...], preferred_element_type=F32)
        y_ref[...] = _layer_norm(DN_ALPHA * x_ref[...] + mix, ln_ref[0:1, :], ln_ref[1:2, :])


def _ab_decode(x, win, wout, convw, gp, ln, cbuf, s0, c0, n0, m0):
    nb = x.shape[0]
    bb = SUBLANES
    assert nb % bb == 0
    n_steps = nb // bb
    const = lambda shape: pl.BlockSpec(shape, lambda i: (0,) * len(shape))
    blk = lambda shape: pl.BlockSpec((bb,) + shape, lambda i: (i,) + (0,) * len(shape))
    hist = (CONV_W - 1) * A_CONV
    width = N_HEADS * HEAD_DIM
    return pl.pallas_call(
        functools.partial(_ab_decode_kernel, bb=bb, n_steps=n_steps),
        grid=(n_steps,),
        in_specs=[
            const((nb, D_MODEL)), const((D_MODEL, D_IN_PAD)), const((2 * width, D_MODEL)),
            const((CONV_W, A_CONV)), const((SUBLANES, LANES)), const((2, D_MODEL)),
            blk((hist,)), blk((N_HEADS, HEAD_DIM, HEAD_DIM)), blk((N_HEADS, HEAD_DIM, HEAD_DIM)),
            blk((width,)), blk((N_HEADS,)),
        ],
        out_specs=[
            const((nb, D_MODEL)), blk((hist,)), blk((N_HEADS, HEAD_DIM, HEAD_DIM)),
            blk((N_HEADS, HEAD_DIM, HEAD_DIM)), blk((width,)), blk((N_HEADS,)),
        ],
        out_shape=[
            jax.ShapeDtypeStruct((nb, D_MODEL), F32),
            jax.ShapeDtypeStruct((nb, hist), F32),
            jax.ShapeDtypeStruct((nb, N_HEADS, HEAD_DIM, HEAD_DIM), F32),
            jax.ShapeDtypeStruct((nb, N_HEADS, HEAD_DIM, HEAD_DIM), F32),
            jax.ShapeDtypeStruct((nb, width), F32),
            jax.ShapeDtypeStruct((nb, N_HEADS), F32),
        ],
        scratch_shapes=[pltpu.VMEM((nb, D_IN_PAD), F32), pltpu.VMEM((nb, 2 * width), F32)],
        compiler_params=pltpu.CompilerParams(dimension_semantics=("arbitrary",), vmem_limit_bytes=VMEM_LIMIT),
        name="ab_decode",
    )(x, win, wout, convw, gp, ln, cbuf, s0, c0, n0, m0)


def _second_largest_sum(a, b, c, d):
    hi1, lo1 = jnp.maximum(a, b), jnp.minimum(a, b)
    hi2, lo2 = jnp.maximum(c, d), jnp.minimum(c, d)
    return jnp.maximum(hi1, hi2) + jnp.maximum(jnp.minimum(hi1, hi2), jnp.maximum(lo1, lo2))


def _first_argmax(vals):
    best_v = vals[0]
    best_i = jnp.zeros(vals[0].shape, jnp.int32)
    for j in range(1, len(vals)):
        better = vals[j] > best_v
        best_v = jnp.where(better, vals[j], best_v)
        best_i = jnp.where(better, j, best_i)
    return best_i


def _router_kernel(x_ref, rw_ref, rb_ref, meta_ref, cnt_ref, carry, *, n_steps):
    i = pl.program_id(0)
    tm = x_ref.shape[0]

    @pl.when(i == 0)
    def _():
        carry[...] = jnp.zeros_like(carry)

    x = x_ref[...]
    rw = rw_ref[...]
    x_hi = _bf(x)
    x_lo = _bf(x - x_hi.astype(F32))
    w_hi = _bf(rw)
    w_lo = _bf(rw - w_hi.astype(F32))
    nt = lambda a, b: lax.dot_general(a, b, (((1,), (1,)), ((), ())), preferred_element_type=F32)
    logits = nt(w_hi, x_hi) + (nt(w_hi, x_lo) + nt(w_lo, x_hi))
    ex = jnp.exp(logits - jnp.max(logits, axis=0, keepdims=True))
    probs = ex / jnp.sum(ex, axis=0, keepdims=True)
    sel = probs + rb_ref[...]
    p = [probs[j:j + 1, :] for j in range(N_EXPERTS)]
    s = [sel[j:j + 1, :] for j in range(N_EXPERTS)]
    scores = [_second_largest_sum(*s[EXPERTS_PER_GROUP * g:EXPERTS_PER_GROUP * (g + 1)])
              for g in range(N_EXPERT_GROUPS)]
    best = _first_argmax(scores)
    masked = [jnp.where(best == j // EXPERTS_PER_GROUP, s[j], -jnp.inf) for j in range(N_EXPERTS)]
    i1 = _first_argmax(masked)
    i2 = _first_argmax([jnp.where(i1 == j, -jnp.inf, masked[j]) for j in range(N_EXPERTS)])
    zero = jnp.zeros_like(p[0])
    p1 = functools.reduce(lambda a, b: a + b, [jnp.where(i1 == j, p[j], zero) for j in range(N_EXPERTS)])
    p2 = functools.reduce(lambda a, b: a + b, [jnp.where(i2 == j, p[j], zero) for j in range(N_EXPERTS)])
    tot = p1 + p2
    rows = [jnp.where(i1 == j, p1 / tot, zero) + jnp.where(i2 == j, p2 / tot, zero) for j in range(N_EXPERTS)]
    in_group = [best == g for g in range(N_EXPERT_GROUPS)]
    local = [functools.reduce(lambda a, b: a + b,
                              [jnp.where(in_group[g], rows[EXPERTS_PER_GROUP * g + e], zero)
                               for g in range(N_EXPERT_GROUPS)])
             for e in range(EXPERTS_PER_GROUP)]
    onehot = jnp.concatenate([jnp.where(m, 1.0, 0.0) for m in in_group]
                             + [jnp.zeros((SUBLANES - N_EXPERT_GROUPS, tm), F32)], axis=0)
    ri = lax.broadcasted_iota(jnp.int32, (tm, tm), 0)
    ci = lax.broadcasted_iota(jnp.int32, (tm, tm), 1)
    incl = jnp.dot(_bf(onehot), _bf(jnp.where(ri <= ci, 1.0, 0.0)), preferred_element_type=F32)
    prev = carry[...]
    rank = jnp.sum(onehot * (incl - 1.0 + prev[:, 0:1]), axis=0, keepdims=True)
    carry[...] = prev + incl[:, tm - 1:tm]
    row_id = (i * tm + lax.broadcasted_iota(jnp.int32, (1, tm), 1)).astype(F32)
    meta_ref[...] = jnp.concatenate([row_id] + local + [best.astype(F32), rank, zero], axis=0).T

    @pl.when(i == n_steps - 1)
    def _():
        cnt_ref[...] = carry[...]


def _router(x, rw_t, rb_col):
    t = x.shape[0]
    tm = min(256, t)
    assert t % tm == 0
    n_steps = t // tm
    return pl.pallas_call(
        functools.partial(_router_kernel, n_steps=n_steps),
        grid=(n_steps,),
        in_specs=[pl.BlockSpec((tm, D_MODEL), lambda i: (i, 0)),
                  pl.BlockSpec((N_EXPERTS, D_MODEL), lambda i: (0, 0)),
                  pl.BlockSpec((N_EXPERTS, 1), lambda i: (0, 0))],
        out_specs=[pl.BlockSpec((tm, SUBLANES), lambda i: (i, 0)),
                   pl.BlockSpec((SUBLANES, LANES), lambda i: (0, 0))],
        out_shape=[jax.ShapeDtypeStruct((t, SUBLANES), F32), jax.ShapeDtypeStruct((SUBLANES, LANES), F32)],
        scratch_shapes=[pltpu.VMEM((SUBLANES, LANES), F32)],
        compiler_params=pltpu.CompilerParams(dimension_semantics=("arbitrary",), vmem_limit_bytes=VMEM_LIMIT),
        name="router",
    )(x, rw_t, rb_col)


def _moe_kernel(tg_ref, nv_ref, src_cur, dst_cur, src_nxt, cw_ref, x_hbm, wg_ref, wu_ref, wd_ref, ln_ref,
                y_hbm, xg, stage, gsem, ssem, *, tm, n_tiles):
    i = pl.program_id(0)
    slot = lax.rem(i, 2)
    other = 1 - slot
    n_valid = nv_ref[i]
    n_prev = jnp.where(i >= 1, nv_ref[jnp.maximum(i - 1, 0)], 0)
    n_back2 = jnp.where(i >= 2, nv_ref[jnp.maximum(i - 2, 0)], 0)
    has_next = i + 1 < n_tiles

    def gather_row(tok_ref, r, dst_slot):
        return pltpu.make_async_copy(x_hbm.at[pl.ds(tok_ref[0, r], 1), :], xg.at[dst_slot, pl.ds(r, 1), :],
                                     gsem.at[dst_slot])

    def scatter_row(tok_ref, r, src_slot):
        return pltpu.make_async_copy(stage.at[src_slot, pl.ds(r, 1), :], y_hbm.at[pl.ds(tok_ref[0, r], 1), :],
                                     ssem.at[src_slot])

    def scatter_wait(count, src_slot):
        for p in [1 << b for b in range(tm.bit_length())]:
            @pl.when(lax.bitwise_and(count, p) != 0)
            def _():
                pltpu.make_async_copy(stage.at[src_slot, pl.ds(0, p), :], y_hbm.at[pl.ds(0, p), :],
                                      ssem.at[src_slot]).wait()

    @pl.when(i == 0)
    def _():
        def body(r, c):
            gather_row(src_cur, r, 0).start()
            return c
        lax.fori_loop(0, tm, body, 0, unroll=8)

    pltpu.make_async_copy(x_hbm.at[pl.ds(0, tm), :], xg.at[slot], gsem.at[slot]).wait()

    for parity in range(2):
        @pl.when(jnp.logical_and(has_next, slot == parity))
        def _():
            for r in range(tm):
                gather_row(src_nxt, r, 1 - parity).start(priority=r % 2)

    @pl.when(n_valid > 0)
    def _():
        x = xg[slot]
        xb = _bf(x)
        cw = cw_ref[...]
        acc = jnp.zeros((tm, D_MODEL), F32)
        for e in range(EXPERTS_PER_GROUP):
            g = jnp.dot(xb, wg_ref[e], preferred_element_type=F32)
            u = jnp.dot(xb, wu_ref[e], preferred_element_type=F32)
            hid = _silu(g) * u * cw[:, 1 + e:2 + e]
            acc = acc + jnp.dot(_bf(hid), wd_ref[e], preferred_element_type=F32)
        y = _layer_norm(DN_ALPHA * x + acc, ln_ref[0:1, :], ln_ref[1:2, :])

        @pl.when(i >= 2)
        def _():
            scatter_wait(n_back2, slot)
        stage[slot] = y

    @pl.when(jnp.logical_and(n_valid <= 0, i >= 2))
    def _():
        scatter_wait(n_back2, slot)

    for parity in range(2):
        @pl.when(jnp.logical_and(n_valid == tm, slot == parity))
        def _():
            for r in range(tm):
                scatter_row(dst_cur, r, parity).start(priority=r % 2)

    @pl.when(n_valid < tm)
    def _():
        for r in range(tm):
            @pl.when(r < n_valid)
            def _():
                scatter_row(dst_cur, r, slot).start(priority=r % 2)

    @pl.when(i == n_tiles - 1)
    def _():
        scatter_wait(n_prev, other)
        scatter_wait(n_valid, slot)


def _moe(x, meta, counts, wg, wu, wd, ln, layer, out_row):
    t = x.shape[0]
    tm = min(512, t)
    assert t % tm == 0 and tm & (tm - 1) == 0
    n_tiles = t // tm + N_EXPERT_GROUPS
    n_slots = n_tiles * tm
    grp = meta[:, 1 + EXPERTS_PER_GROUP].astype(jnp.int32)
    rank = meta[:, 2 + EXPERTS_PER_GROUP].astype(jnp.int32)
    cnt = counts[:N_EXPERT_GROUPS, 0].astype(jnp.int32)
    padded = ((cnt + tm - 1) // tm) * tm
    g_end = jnp.cumsum(padded)
    g_off = g_end - padded
    pos = jnp.take(g_off, grp) + rank
    slot_meta = jnp.zeros((n_slots, SUBLANES), F32).at[pos].set(meta)
    tok_of_slot = slot_meta[:, 0].astype(jnp.int32)
    tile_start = jnp.arange(n_tiles, dtype=jnp.int32) * tm
    tile_grp = jnp.minimum(jnp.sum(tile_start[:, None] >= g_end[None, :], axis=1), N_EXPERT_GROUPS - 1).astype(jnp.int32)
    tile_valid = jnp.clip(jnp.take(g_off + cnt, tile_grp) - tile_start, 0, tm).astype(jnp.int32)
    src3 = tok_of_slot.reshape(n_tiles, 1, tm)
    dst3 = out_row(tok_of_slot).reshape(n_tiles, 1, tm)

    grid_spec = pltpu.PrefetchScalarGridSpec(
        num_scalar_prefetch=2,
        grid=(n_tiles,),
        in_specs=[
            pl.BlockSpec((None, 1, tm), lambda i, tg, nv: (i, 0, 0), memory_space=pltpu.SMEM),
            pl.BlockSpec((None, 1, tm), lambda i, tg, nv: (i, 0, 0), memory_space=pltpu.SMEM),
            pl.BlockSpec((None, 1, tm), lambda i, tg, nv: (jnp.minimum(i + 1, n_tiles - 1), 0, 0),
                         memory_space=pltpu.SMEM),
            pl.BlockSpec((tm, SUBLANES), lambda i, tg, nv: (i, 0)),
            pl.BlockSpec(memory_space=pl.ANY),
            pl.BlockSpec((None, EXPERTS_PER_GROUP, D_MODEL, D_FF), lambda i, tg, nv: (layer, tg[i], 0, 0)),
            pl.BlockSpec((None, EXPERTS_PER_GROUP, D_MODEL, D_FF), lambda i, tg, nv: (layer, tg[i], 0, 0)),
            pl.BlockSpec((None, EXPERTS_PER_GROUP, D_FF, D_MODEL), lambda i, tg, nv: (layer, tg[i], 0, 0)),
            pl.BlockSpec((2, D_MODEL), lambda i, tg, nv: (0, 0)),
        ],
        out_specs=pl.BlockSpec(memory_space=pl.ANY),
        scratch_shapes=[pltpu.VMEM((2, tm, D_MODEL), F32), pltpu.VMEM((2, tm, D_MODEL), F32),
                        pltpu.SemaphoreType.DMA((2,)), pltpu.SemaphoreType.DMA((2,))],
    )
    return pl.pallas_call(
        functools.partial(_moe_kernel, tm=tm, n_tiles=n_tiles),
        grid_spec=grid_spec,
        out_shape=jax.ShapeDtypeStruct((t, D_MODEL), F32),
        compiler_params=pltpu.CompilerParams(dimension_semantics=("arbitrary",), vmem_limit_bytes=VMEM_LIMIT),
        name="moe",
    )(tile_grp, tile_valid, src3, dst3, src3, slot_meta, x, wg, wu, wd, ln)


def _s5_prep_kernel(are_ref, aim_ref, ldt_ref, bre_ref, bim_ref, abre_ref, abim_ref, bbre_ref, bbim_ref):
    a_r = are_ref[...]
    a_i = aim_ref[...]
    dt = jnp.exp(ldt_ref[...])
    mag = jnp.exp(dt * a_r)
    ab_re = mag * jnp.cos(dt * a_i)
    ab_im = mag * jnp.sin(dt * a_i)
    den = a_r * a_r + a_i * a_i
    nr = ab_re - 1.0
    z_re = (nr * a_r + ab_im * a_i) / den
    z_im = (ab_im * a_r - nr * a_i) / den
    abre_ref[...] = ab_re
    abim_ref[...] = ab_im
    bbre_ref[...] = z_re * bre_ref[...] - z_im * bim_ref[...]
    bbim_ref[...] = z_re * bim_ref[...] + z_im * bre_ref[...]


def _s5_prep(a_re, a_im, log_dt, b_re, b_im):
    n = S5_GROUPS * S5_STATE
    col = jax.ShapeDtypeStruct((n, 1), F32)
    mat = jax.ShapeDtypeStruct((n, S5_GROUP), F32)
    ldt = jnp.broadcast_to(log_dt[:, None], (S5_GROUPS, S5_STATE)).reshape(n, 1)
    return pl.pallas_call(_s5_prep_kernel, out_shape=[col, col, mat, mat], name="s5_prep")(
        a_re.reshape(n, 1), a_im.reshape(n, 1), ldt, b_re.reshape(n, S5_GROUP), b_im.reshape(n, S5_GROUP))


def _gelu_tanh(x):
    return 0.5 * x * (1.0 + jnp.tanh(0.7978845608028654 * (x + 0.044715 * (x * x * x))))


def _s5_kernel(x_ref, h0re_ref, h0im_ref, wre_ref, wim_ref, cre_ref, cim_ref, abre_ref, abim_ref, d_ref,
               gy_ref, hre_out, him_out, bu_re, bu_im, hre_s, him_s, *, nb, tt, n_t):
    t = pl.program_id(0)
    sw = wre_ref.shape[2]

    @pl.when(t == 0)
    def _():
        hre_s[...] = h0re_ref[...]
        him_s[...] = h0im_ref[...]

    for k in range(D_MODEL // LANES):
        ls = slice(k * LANES, (k + 1) * LANES)
        ss = slice(k * sw, (k + 1) * sw)
        buf = k % 2
        xv = x_ref[:, ls]
        xb = _bf(xv)
        bu_re[buf] = jnp.dot(xb, wre_ref[k], preferred_element_type=F32)
        bu_im[buf] = jnp.dot(xb, wim_ref[k], preferred_element_type=F32)
        a_re = jnp.broadcast_to(abre_ref[:, ss], (nb, sw))
        a_im = jnp.broadcast_to(abim_ref[:, ss], (nb, sw))

        def step(s, carry, buf=buf, a_re=a_re, a_im=a_im):
            h_re, h_im = carry
            rows = pl.ds(pl.multiple_of(s * nb, nb), nb)
            n_re = a_re * h_re - a_im * h_im + bu_re[buf, rows, :]
            n_im = a_re * h_im + a_im * h_re + bu_im[buf, rows, :]
            bu_re[buf, rows, :] = n_re
            bu_im[buf, rows, :] = n_im
            return n_re, n_im

        h_re, h_im = lax.fori_loop(0, tt, step, (hre_s[:, ss], him_s[:, ss]), unroll=min(tt, S5_UNROLL))
        hre_s[:, ss] = h_re
        him_s[:, ss] = h_im
        y = (jnp.dot(_bf(bu_re[buf]), cre_ref[k], preferred_element_type=F32)
             - jnp.dot(_bf(bu_im[buf]), cim_ref[k], preferred_element_type=F32)
             + d_ref[:, ls] * xv)
        gy_ref[:, ls] = _gelu_tanh(y).astype(gy_ref.dtype)

    @pl.when(t == n_t - 1)
    def _():
        hre_out[...] = hre_s[...]
        him_out[...] = him_s[...]


def _s5_scan(x, nb, h0_re, h0_im, wre, wim, cre, cim, ab_re, ab_im, d_skip):
    length = x.shape[0] // nb
    tt = min(S5_TT, length)
    assert length % tt == 0 and nb % SUBLANES == 0
    n_t = length // tt
    n_k = D_MODEL // LANES
    sw = (LANES // S5_GROUP) * S5_STATE
    n_state = S5_GROUPS * S5_STATE
    const = lambda shape: pl.BlockSpec(shape, lambda t: (0,) * len(shape))
    return pl.pallas_call(
        functools.partial(_s5_kernel, nb=nb, tt=tt, n_t=n_t),
        grid=(n_t,),
        in_specs=[
            pl.BlockSpec((tt * nb, D_MODEL), lambda t: (t, 0)),
            const((nb, n_state)), const((nb, n_state)),
            const((n_k, LANES, sw)), const((n_k, LANES, sw)), const((n_k, sw, LANES)), const((n_k, sw, LANES)),
            const((1, n_state)), const((1, n_state)), const((1, D_MODEL)),
        ],
        out_specs=[pl.BlockSpec((tt * nb, D_MODEL), lambda t: (t, 0)), const((nb, n_state)), const((nb, n_state))],
        out_shape=[
            jax.ShapeDtypeStruct((length * nb, D_MODEL), BF16),
            jax.ShapeDtypeStruct((nb, n_state), F32),
            jax.ShapeDtypeStruct((nb, n_state), F32),
        ],
        scratch_shapes=[
            pltpu.VMEM((2, nb * tt, sw), F32), pltpu.VMEM((2, nb * tt, sw), F32),
            pltpu.VMEM((nb, n_state), F32), pltpu.VMEM((nb, n_state), F32),
        ],
        compiler_params=pltpu.CompilerParams(dimension_semantics=("arbitrary",), vmem_limit_bytes=VMEM_LIMIT),
        name="s5_scan",
    )(x, h0_re, h0_im, wre, wim, cre, cim, ab_re, ab_im, d_skip)


def _glu_ln_kernel(gy_ref, h_ref, wa_ref, wb_ref, ln_ref, o_ref):
    gy = gy_ref[...]
    a = jnp.dot(gy, wa_ref[...], preferred_element_type=F32)
    b = jnp.dot(gy, wb_ref[...], preferred_element_type=F32)
    o_ref[...] = _layer_norm(DN_ALPHA * h_ref[...] + a * _sigmoid(b), ln_ref[0:1, :], ln_ref[1:2, :])


def _glu_ln(gy, h, wa, wb, ln):
    t = h.shape[0]
    tm = min(512, t)
    assert t % tm == 0
    return pl.pallas_call(
        _glu_ln_kernel,
        grid=(t // tm,),
        in_specs=[pl.BlockSpec((tm, D_MODEL), lambda i: (i, 0)), pl.BlockSpec((tm, D_MODEL), lambda i: (i, 0)),
                  pl.BlockSpec((D_MODEL, D_MODEL), lambda i: (0, 0)), pl.BlockSpec((D_MODEL, D_MODEL), lambda i: (0, 0)),
                  pl.BlockSpec((2, D_MODEL), lambda i: (0, 0))],
        out_specs=pl.BlockSpec((tm, D_MODEL), lambda i: (i, 0)),
        out_shape=jax.ShapeDtypeStruct((t, D_MODEL), F32),
        compiler_params=pltpu.CompilerParams(dimension_semantics=("arbitrary",), vmem_limit_bytes=VMEM_LIMIT),
        name="glu_ln",
    )(gy, h, wa, wb, ln)


def _block_diag_slices(m, rows_per_group, cols_per_group):
    gps = LANES // S5_GROUP
    m = m.reshape(S5_GROUPS // gps, gps, rows_per_group, cols_per_group)
    eye = jnp.eye(gps, dtype=m.dtype)
    out = m[:, :, :, None, :] * eye[None, :, None, :, None]
    return out.reshape(S5_GROUPS // gps, gps * rows_per_group, gps * cols_per_group)


def _prepare(p):
    w = p['w_in'][0]
    win = jnp.concatenate(
        [w[:, 0:1536], w[:, 1544:2056], w[:, 2056:3592], w[:, 3600:4112], w[:, 1536:1544], w[:, 3592:3600],
         jnp.zeros((D_MODEL, D_IN_PAD - 4112), w.dtype)], axis=1).astype(BF16)
    at_lane = lambda v, lane0: jnp.pad(v, (lane0, LANES - lane0 - v.shape[0]))
    gp = jnp.stack([at_lane(p['gdn_A_log'][0], G_DEC), at_lane(p['gdn_dt_bias'][0], G_DEC),
                    at_lane(p['ml_b_i'][0], G_IN), at_lane(p['ml_b_f'][0], G_FG),
                    p['gdn_norm_w'][0], p['ml_norm_w'][0], jnp.zeros((LANES,), F32), jnp.zeros((LANES,), F32)])
    ab_re, ab_im, bb_re, bb_im = _s5_prep(p['s5_A_re'][0], p['s5_A_im'][0], p['s5_log_dt'][0],
                                          p['s5_B_re'][0], p['s5_B_im'][0])
    to_in = lambda bb: _block_diag_slices(
        bb.reshape(S5_GROUPS, S5_STATE, S5_GROUP).transpose(0, 2, 1), S5_GROUP, S5_STATE).astype(BF16)
    to_out = lambda c: _block_diag_slices(c.transpose(0, 2, 1), S5_STATE, S5_GROUP).astype(BF16)
    return dict(
        win=win, wout=p['w_out'][0].astype(BF16), convw=p['gdn_conv_w'][0], gp=gp,
        ln_mix=[jnp.stack([p['ln_mix_g'][l], p['ln_mix_b'][l]]) for l in range(DEPTH)],
        ln_ffn=[jnp.stack([p['ln_ffn_g'][l], p['ln_ffn_b'][l]]) for l in range(DEPTH)],
        rw_t=p['router_w'].T, rb_col=p['router_b'][:, None],
        wg=p['moe_w_gate'].astype(BF16), wu=p['moe_w_up'].astype(BF16), wd=p['moe_w_down'].astype(BF16),
        s5_wre=to_in(bb_re), s5_wim=to_in(bb_im),
        s5_cre=to_out(p['s5_C_re'][0]), s5_cim=to_out(p['s5_C_im'][0]),
        s5_abre=ab_re.reshape(1, -1), s5_abim=ab_im.reshape(1, -1), s5_d=p['s5_D'][0][None, :],
        glu_a=p['s5_w_glu_a'][0].astype(BF16), glu_b=p['s5_w_glu_b'][0].astype(BF16),
    )


def _ffn(h, w, layer, out_row):
    meta, counts = _router(h, w['rw_t'], w['rb_col'])
    return _moe(h, meta, counts, w['wg'], w['wu'], w['wd'], w['ln_ffn'][layer], layer, out_row)


def _s5_layer(h, nb, h0_re, h0_im, w):
    gy, re, im = _s5_scan(h, nb, h0_re, h0_im, w['s5_wre'], w['s5_wim'],
                          w['s5_cre'], w['s5_cim'], w['s5_abre'], w['s5_abim'], w['s5_d'])
    h = _glu_ln(gy, h, w['glu_a'], w['glu_b'], w['ln_mix'][1])
    return h, re, im


def kernel(x_prompt, x_sample, state_gdn_conv, state_gdn_S, state_mlstm_C, state_mlstm_n, state_mlstm_m,
           state_s5_re, state_s5_im, w_in, gdn_conv_w, gdn_A_log, gdn_dt_bias, gdn_norm_w, ml_b_i, ml_b_f,
           ml_norm_w, w_out, s5_A_re, s5_A_im, s5_log_dt, s5_B_re, s5_B_im, s5_C_re, s5_C_im, s5_D,
           s5_w_glu_a, s5_w_glu_b, router_w, router_b, moe_w_gate, moe_w_up, moe_w_down,
           ln_mix_g, ln_mix_b, ln_ffn_g, ln_ffn_b):
    w = _prepare(dict(
        w_in=w_in, gdn_conv_w=gdn_conv_w, gdn_A_log=gdn_A_log, gdn_dt_bias=gdn_dt_bias, gdn_norm_w=gdn_norm_w,
        ml_b_i=ml_b_i, ml_b_f=ml_b_f, ml_norm_w=ml_norm_w, w_out=w_out, s5_A_re=s5_A_re, s5_A_im=s5_A_im,
        s5_log_dt=s5_log_dt, s5_B_re=s5_B_re, s5_B_im=s5_B_im, s5_C_re=s5_C_re, s5_C_im=s5_C_im, s5_D=s5_D,
        s5_w_glu_a=s5_w_glu_a, s5_w_glu_b=s5_w_glu_b, router_w=router_w, router_b=router_b,
        moe_w_gate=moe_w_gate, moe_w_up=moe_w_up, moe_w_down=moe_w_down,
        ln_mix_g=ln_mix_g, ln_mix_b=ln_mix_b, ln_ffn_g=ln_ffn_g, ln_ffn_b=ln_ffn_b))
    bp, lp, _ = x_prompt.shape
    bs, ls, _ = x_sample.shape
    assert ls == 1
    n_state = S5_GROUPS * S5_STATE

    h, p_hist, p_s, p_c, p_n, p_m = _ab_prompt(x_prompt, w['win'], w['wout'], w['convw'], w['gp'], w['ln_mix'][0])
    h = _ffn(h.reshape(bp * lp, D_MODEL), w, 0, lambda r: (r % lp) * bp + r // lp)
    zeros = jnp.zeros((bp, n_state), F32)
    h, p_re, p_im = _s5_layer(h, bp, zeros, zeros, w)
    y_prompt = _ffn(h, w, 1, lambda r: (r % bp) * lp + r // bp).reshape(bp, lp, D_MODEL)

    hs, s_cbuf, s_s, s_c, s_n, s_m = _ab_decode(
        x_sample[:, 0], w['win'], w['wout'], w['convw'], w['gp'], w['ln_mix'][0],
        state_gdn_conv.reshape(bs, (CONV_W - 1) * A_CONV), state_gdn_S[:, 0], state_mlstm_C[:, 0],
        state_mlstm_n.reshape(bs, N_HEADS * HEAD_DIM), state_mlstm_m[:, 0])
    hs = _ffn(hs, w, 0, lambda r: r)
    hs, s_re, s_im = _s5_layer(hs, bs, state_s5_re.reshape(bs, n_state), state_s5_im.reshape(bs, n_state), w)
    y_sample = _ffn(hs, w, 1, lambda r: r).reshape(bs, 1, D_MODEL)

    grp = lambda a, n: a.reshape(n, 1, S5_GROUPS, S5_STATE)
    return (
        y_prompt, y_sample,
        p_hist[:, None, SUBLANES - (CONV_W - 1):, :], p_s[:, None], p_c[:, None],
        p_n[:, None, :, 0, :], p_m[:, None, :, 0, 0], grp(p_re, bp), grp(p_im, bp),
        s_cbuf.reshape(bs, 1, CONV_W - 1, A_CONV), s_s[:, None], s_c[:, None],
        s_n.reshape(bs, 1, N_HEADS, HEAD_DIM), s_m[:, None], grp(s_re, bs), grp(s_im, bs),
    )
```

```python
import functools

import jax
import jax.numpy as jnp
from jax import lax
from jax.experimental import pallas as pl
from jax.experimental.pallas import tpu as pltpu

F32 = jnp.float32
BF16 = jnp.bfloat16
HIGHEST = lax.Precision.HIGHEST

D_MODEL = 1024
DEPTH = 2
N_HEADS = 4
HEAD_DIM = 128
CONV_W = 4
CHUNK = 64
A_CONV = 3 * N_HEADS * HEAD_DIM
S5_GROUP = 16
S5_GROUPS = D_MODEL // S5_GROUP
S5_STATE = 64
N_EXPERTS = 16
EXPERTS_PER_GROUP = 4
N_EXPERT_GROUPS = N_EXPERTS // EXPERTS_PER_GROUP
D_FF = 512
DN_ALPHA = (2 * DEPTH) ** 0.25
LN_EPS = 1e-5
RMS_EPS = 1e-6
NEG_BIG = -1e30

QA, KA, VA, ZA, QB, KB, VB, OB, GT = 0, 512, 1024, 1536, 2048, 2560, 3072, 3584, 4096
D_IN_PAD = 4224
G_DEC, G_BETA, G_IN, G_FG = 0, 4, 8, 12

S5_TT = 64
S5_UNROLL = 64

LANES = 128
SUBLANES = 8
VMEM_LIMIT = 56 * 1024 * 1024


def _bf(x):
    return x.astype(BF16)


def _nn(a, b):
    return jnp.dot(_bf(a), _bf(b), preferred_element_type=F32)


def _nt(a, b):
    return lax.dot_general(_bf(a), _bf(b), (((1,), (1,)), ((), ())), preferred_element_type=F32)


def _tn(a, b):
    return lax.dot_general(_bf(a), _bf(b), (((0,), (0,)), ((), ())), preferred_element_type=F32)


def _sigmoid(x):
    return 1.0 / (1.0 + jnp.exp(-x))


def _softplus(x):
    return jnp.maximum(x, 0.0) + jnp.log(1.0 + jnp.exp(-jnp.abs(x)))


def _silu(x):
    return x * _sigmoid(x)


def _layer_norm(y, g, b):
    mu = jnp.mean(y, axis=-1, keepdims=True)
    yc = y - mu
    var = jnp.mean(yc * yc, axis=-1, keepdims=True)
    return yc * lax.rsqrt(var + LN_EPS) * g + b


def _rms(x, w):
    return x * lax.rsqrt(jnp.mean(x * x, axis=-1, keepdims=True) + RMS_EPS) * w


def _gate_transform(raw, gp):
    lane = lax.broadcasted_iota(jnp.int32, raw.shape, 1)
    dec = -jnp.exp(gp[0:1, :]) * _softplus(raw + gp[1:2, :])
    beta = _sigmoid(raw)
    ipre = raw + gp[2:3, :]
    logf = -_softplus(-(raw + gp[3:4, :]))
    return jnp.where(lane < G_BETA, dec,
                     jnp.where(lane < G_IN, beta,
                               jnp.where(lane < G_FG, ipre,
                                         jnp.where(lane < G_FG + N_HEADS, logf, 0.0))))


def _bnn(a, b):
    return lax.dot_general(_bf(a), _bf(b), (((2,), (1,)), ((0,), (0,))), preferred_element_type=F32)


def _bnt(a, b):
    return lax.dot_general(_bf(a), _bf(b), (((2,), (2,)), ((0,), (0,))), preferred_element_type=F32)


def _btn(a, b):
    return lax.dot_general(_bf(a), _bf(b), (((1,), (1,)), ((0,), (0,))), preferred_element_type=F32)


def _unit_lower_inverse_minus_eye(a, pack):
    p, c, _ = a.shape
    nb = p // pack
    w = pack * c
    r = -jnp.stack([jnp.concatenate([a[b * pack + j] for j in range(pack)], axis=1) for b in range(nb)])
    assert c & (c - 1) == 0
    block_of = lambda dim: lax.shift_right_logical(lax.broadcasted_iota(jnp.int32, (w, w), dim), c.bit_length() - 1)
    on_diag = block_of(0) == block_of(1)

    def block_diag(x):
        return jnp.where(on_diag, jnp.concatenate([x] * pack, axis=1), 0.0)

    steps = max(1, (c - 1).bit_length()) - 1
    q = _bnn(r, block_diag(r))
    for i in range(steps):
        bd = block_diag(q)
        if i + 1 < steps:
            both = _bnn(jnp.concatenate([r, q], axis=1), bd)
            rq, qq = both[:, :c, :], both[:, c:, :]
        else:
            rq, qq = _bnn(r, bd), None
        r = r + q + rq
        q = qq
    return jnp.stack([r[b][:, j * c:(j + 1) * c] for b in range(nb) for j in range(pack)])


def _ab_prompt_kernel(x_ref, win_ref, wout_ref, convw_ref, gp_ref, ln_ref,
                      h_ref, hist_ref, s_out, c_out, n_out, m_out,
                      proj, qkv, gates, gcum, merged, s_s, c_s, n_s, m_s,
                      u_s, w_s, attn_s, dlog_s, qk_s, dmax_s, kt_s, kbt_s, rows_s, *, tb, n_t):
    t = pl.program_id(1)
    nc = tb // CHUNK

    @pl.when(t == 0)
    def _():
        proj[0:SUBLANES, :] = jnp.zeros((SUBLANES, D_IN_PAD), F32)
        s_s[...] = jnp.zeros_like(s_s)
        c_s[...] = jnp.zeros_like(c_s)
        n_s[...] = jnp.zeros_like(n_s)
        m_s[...] = jnp.zeros_like(m_s)

    x = x_ref[...]
    xb = _bf(x)
    proj[SUBLANES:SUBLANES + tb, 0:A_CONV] = jnp.dot(xb, win_ref[:, 0:A_CONV], preferred_element_type=F32)
    proj[SUBLANES:SUBLANES + tb, A_CONV:] = jnp.dot(xb, win_ref[:, A_CONV:], preferred_element_type=F32)

    for blk in range(A_CONV // LANES):
        cs = slice(blk * LANES, (blk + 1) * LANES)
        acc = proj[SUBLANES:SUBLANES + tb, cs] * convw_ref[CONV_W - 1:CONV_W, cs]
        for j in range(1, CONV_W):
            acc = acc + proj[SUBLANES - j:SUBLANES - j + tb, cs] * convw_ref[CONV_W - 1 - j:CONV_W - j, cs]
        y = _silu(acc)
        if blk < 2 * N_HEADS:
            y = y * lax.rsqrt(jnp.sum(y * y, axis=-1, keepdims=True) + RMS_EPS)
            if blk < N_HEADS:
                y = y * HEAD_DIM ** -0.5
        qkv[:, cs] = y

    gt = _gate_transform(proj[SUBLANES:SUBLANES + tb, GT:GT + LANES], gp_ref[...])
    gates[...] = gt
    ri = lax.broadcasted_iota(jnp.int32, (tb, tb), 0)
    ci = lax.broadcasted_iota(jnp.int32, (tb, tb), 1)
    same_chunk = lax.shift_right_logical(ri, 6) == lax.shift_right_logical(ci, 6)
    ltri = jnp.where(same_chunk, jnp.where(ri >= ci, 1.0, 0.0), 0.0)
    gcum[...] = jnp.dot(ltri, gt, preferred_element_type=F32, precision=HIGHEST)

    ii = lax.broadcasted_iota(jnp.int32, (CHUNK, CHUNK), 0)
    jj = lax.broadcasted_iota(jnp.int32, (CHUNK, CHUNK), 1)
    incl = ii >= jj
    strict = ii > jj
    gdn_w = gp_ref[4:5, :]
    ml_w = gp_ref[5:6, :]

    pairs = [(c, h) for c in range(nc) for h in range(N_HEADS)]

    def tile_heads(ref, row0, col0):
        return jnp.stack([ref[row0 + c * CHUNK:row0 + (c + 1) * CHUNK, col0 + h * HEAD_DIM:col0 + (h + 1) * HEAD_DIM]
                          for c, h in pairs])

    def tile_cols(ref, lane0):
        return jnp.stack([ref[c * CHUNK:(c + 1) * CHUNK, lane0 + h:lane0 + h + 1] for c, h in pairs])

    def tile_rows(transposed, lane0):
        return jnp.stack([transposed[c][lane0 + h:lane0 + h + 1, :] for c, h in pairs])

    cs_t = [gcum[c * CHUNK:(c + 1) * CHUNK, :].T for c in range(nc)]
    gt_t = [gates[c * CHUNK:(c + 1) * CHUNK, :].T for c in range(nc)]
    q3 = tile_heads(qkv, 0, QA)
    k3 = tile_heads(qkv, 0, KA)
    v3 = tile_heads(qkv, 0, VA)
    g_col3 = tile_cols(gcum, G_DEC)
    beta3 = tile_cols(gates, G_BETA)
    decay3 = jnp.where(incl, jnp.exp(jnp.where(incl, g_col3 - tile_rows(cs_t, G_DEC), 0.0)), 0.0)
    kb3 = k3 * beta3
    a_low3 = jnp.where(strict, _bnt(kb3, k3) * decay3, 0.0)
    attn_s[...] = _bnt(q3, k3) * decay3
    kbm3 = tile_heads(proj, SUBLANES, KB) * HEAD_DIM ** -0.5
    qk_s[...] = _bnt(tile_heads(proj, SUBLANES, QB), kbm3)
    kt_s[...] = jnp.stack([k3[p].T for p in range(len(pairs))])
    kbt_s[...] = jnp.stack([kbm3[p].T for p in range(len(pairs))])
    zero_row = jnp.zeros((1, CHUNK), F32)
    rows_s[...] = jnp.stack([jnp.concatenate(
        [cs_t[c][G_DEC + h:G_DEC + h + 1, :], cs_t[c][G_FG + h:G_FG + h + 1, :], gt_t[c][G_IN + h:G_IN + h + 1, :]]
        + [zero_row] * (SUBLANES - 3), axis=0) for c, h in pairs])
    r3 = _unit_lower_inverse_minus_eye(a_low3, N_HEADS)
    rhs3 = jnp.concatenate([v3 * beta3, kb3 * jnp.exp(g_col3)], axis=2)
    uw3 = rhs3 + _bnn(r3, rhs3)
    u_s[...] = uw3[:, :, :HEAD_DIM]
    w_s[...] = uw3[:, :, HEAD_DIM:]
    dlog3 = jnp.where(incl, tile_cols(gcum, G_FG) - tile_rows(cs_t, G_FG) + tile_rows(gt_t, G_IN), NEG_BIG)
    dlog_s[...] = dlog3
    dmax_s[...] = jnp.max(dlog3, axis=-1, keepdims=True)

    def chunk_body(c, carry):
        r0 = pl.multiple_of(c * CHUNK, CHUNK)
        rows = pl.ds(r0, CHUNK)
        prow = pl.ds(pl.multiple_of(r0 + SUBLANES, SUBLANES), CHUNK)
        last = pl.ds(r0 + CHUNK - 1, 1)
        pc = pl.ds(pl.multiple_of(c * N_HEADS, N_HEADS), N_HEADS)

        def heads(ref, rws, col0):
            return jnp.stack([ref[rws, col0 + h * HEAD_DIM:col0 + (h + 1) * HEAD_DIM] for h in range(N_HEADS)])

        def cols(ref, rws, lane0):
            return jnp.stack([ref[rws, lane0 + h:lane0 + h + 1] for h in range(N_HEADS)])

        q = heads(qkv, rows, QA)
        g_col = cols(gcum, rows, G_DEC)
        g_last = cols(gcum, last, G_DEC)
        qb = heads(proj, prow, QB)
        kbm = heads(proj, prow, KB) * HEAD_DIM ** -0.5
        vbm = heads(proj, prow, VB)
        b_col = cols(gcum, rows, G_FG)
        b_last = cols(gcum, last, G_FG)
        row_forms = rows_s[pc]
        g_row = row_forms[:, 0:1, :]
        b_row = row_forms[:, 1:2, :]
        i_row = row_forms[:, 2:3, :]
        s_old = s_s[...]
        c_old = c_s[...]
        n_old = n_s[...]
        m_old = m_s[:, 0:1, 0:1]
        inter = b_col + m_old
        mt = jnp.maximum(inter, dmax_s[pc])
        wts = jnp.exp(dlog_s[pc] - mt) * qk_s[pc]
        sc = jnp.exp(inter - mt)
        m_new = mt[:, CHUNK - 1:CHUNK, :]
        sd = jnp.exp(b_last + m_old - m_new)
        wk_row = jnp.exp(b_last - b_row + i_row - m_new)
        w_state = _bnn(w_s[pc], s_old)
        q_state = _bnn(q * jnp.exp(g_col), s_old)
        q_mem = _bnn(qb, c_old)
        w_val = _bnn(wts, vbm)
        kv = _bnn(kbt_s[pc] * wk_row, vbm)
        k_sum = _bnn(jnp.broadcast_to(wk_row, (N_HEADS, SUBLANES, CHUNK)), kbm)
        v_new = u_s[pc] - w_state
        o_a = q_state + _bnn(attn_s[pc], v_new)
        s_s[...] = s_old * jnp.exp(g_last) + _bnn(kt_s[pc] * jnp.exp(g_last - g_row), v_new)
        num = sc * q_mem + w_val
        den = sc * jnp.sum(qb * n_old[:, 0:1, :], axis=-1, keepdims=True) + jnp.sum(wts, axis=-1, keepdims=True)
        h_b = num / jnp.maximum(jnp.abs(den), jnp.exp(-mt))
        c_s[...] = sd * c_old + kv
        n_s[...] = sd * n_old + k_sum
        m_s[...] = jnp.broadcast_to(m_new, m_s.shape)
        o_n = _rms(o_a, gdn_w)
        h_n = _rms(h_b, ml_w)
        for h in range(N_HEADS):
            hs = slice(h * HEAD_DIM, (h + 1) * HEAD_DIM)
            hs2 = slice((N_HEADS + h) * HEAD_DIM, (N_HEADS + h + 1) * HEAD_DIM)
            merged[rows, hs] = o_n[h] * _silu(proj[prow, ZA + h * HEAD_DIM:ZA + (h + 1) * HEAD_DIM])
            merged[rows, hs2] = h_n[h] * _sigmoid(proj[prow, OB + h * HEAD_DIM:OB + (h + 1) * HEAD_DIM])
        return carry

    lax.fori_loop(0, nc, chunk_body, 0)

    mix = jnp.dot(_bf(merged[...]), wout_ref[...], preferred_element_type=F32)
    h_ref[...] = _layer_norm(DN_ALPHA * x + mix, ln_ref[0:1, :], ln_ref[1:2, :])

    proj[0:SUBLANES, 0:A_CONV] = proj[tb:tb + SUBLANES, 0:A_CONV]

    @pl.when(t == n_t - 1)
    def _():
        hist_ref[...] = proj[tb:tb + SUBLANES, 0:A_CONV]
        s_out[...] = s_s[...]
        c_out[...] = c_s[...]
        n_out[...] = n_s[...]
        m_out[...] = m_s[...]


def _ab_prompt(x, win, wout, convw, gp, ln):
    bsz, length, _ = x.shape
    tb = min(512, length)
    assert length % tb == 0 and tb % CHUNK == 0 and length >= SUBLANES
    n_t = length // tb
    n_pairs = (tb // CHUNK) * N_HEADS
    const = lambda shape: pl.BlockSpec(shape, lambda b, t: (0,) * len(shape))
    per_b =lambda shape: pl.BlockSpec((None,) + shape, lambda b, t: (b,) + (0,) * len(shape))
    return pl.pallas_call(
        functools.partial(_ab_prompt_kernel, tb=tb, n_t=n_t),
        grid=(bsz, n_t),
        in_specs=[
            pl.BlockSpec((None, tb, D_MODEL), lambda b, t: (b, t, 0)),
            const((D_MODEL, D_IN_PAD)), const((2 * N_HEADS * HEAD_DIM, D_MODEL)),
            const((CONV_W, A_CONV)), const((SUBLANES, LANES)), const((2, D_MODEL)),
        ],
        out_specs=[
            pl.BlockSpec((None, tb, D_MODEL), lambda b, t: (b, t, 0)),
            per_b((SUBLANES, A_CONV)), per_b((N_HEADS, HEAD_DIM, HEAD_DIM)), per_b((N_HEADS, HEAD_DIM, HEAD_DIM)),
            per_b((N_HEADS, SUBLANES, LANES)), per_b((N_HEADS, SUBLANES, LANES)),
        ],
        out_shape=[
            jax.ShapeDtypeStruct((bsz, length, D_MODEL), F32),
            jax.ShapeDtypeStruct((bsz, SUBLANES, A_CONV), F32),
            jax.ShapeDtypeStruct((bsz, N_HEADS, HEAD_DIM, HEAD_DIM), F32),
            jax.ShapeDtypeStruct((bsz, N_HEADS, HEAD_DIM, HEAD_DIM), F32),
            jax.ShapeDtypeStruct((bsz, N_HEADS, SUBLANES, LANES), F32),
            jax.ShapeDtypeStruct((bsz, N_HEADS, SUBLANES, LANES), F32),
        ],
        scratch_shapes=[
            pltpu.VMEM((tb + SUBLANES, D_IN_PAD), F32),
            pltpu.VMEM((tb, A_CONV), F32),
            pltpu.VMEM((tb, LANES), F32),
            pltpu.VMEM((tb, LANES), F32),
            pltpu.VMEM((tb, 2 * N_HEADS * HEAD_DIM), F32),
            pltpu.VMEM((N_HEADS, HEAD_DIM, HEAD_DIM), F32),
            pltpu.VMEM((N_HEADS, HEAD_DIM, HEAD_DIM), F32),
            pltpu.VMEM((N_HEADS, SUBLANES, LANES), F32),
            pltpu.VMEM((N_HEADS, SUBLANES, LANES), F32),
            pltpu.VMEM((n_pairs, CHUNK, HEAD_DIM), F32),
            pltpu.VMEM((n_pairs, CHUNK, HEAD_DIM), F32),
            pltpu.VMEM((n_pairs, CHUNK, CHUNK), F32),
            pltpu.VMEM((n_pairs, CHUNK, CHUNK), F32),
            pltpu.VMEM((n_pairs, CHUNK, CHUNK), F32),
            pltpu.VMEM((n_pairs, CHUNK, 1), F32),
            pltpu.VMEM((n_pairs, HEAD_DIM, CHUNK), F32),
            pltpu.VMEM((n_pairs, HEAD_DIM, CHUNK), F32),
            pltpu.VMEM((n_pairs, SUBLANES, CHUNK), F32),
        ],
        compiler_params=pltpu.CompilerParams(
            dimension_semantics=("arbitrary", "arbitrary"), vmem_limit_bytes=VMEM_LIMIT),
        name="ab_prompt",
    )(x, win, wout, convw, gp, ln)


def _ab_decode_kernel(x_ref, win_ref, wout_ref, convw_ref, gp_ref, ln_ref, cbuf_ref, s_in, c_in, n_in, m_in,
                      y_ref, cbuf_out, s_out, c_out, n_out, m_out, proj, merged, *, bb, n_steps):
    i = pl.program_id(0)

    @pl.when(i == 0)
    def _():
        proj[...] = jnp.dot(_bf(x_ref[...]), win_ref[...], preferred_element_type=F32)

    rows = pl.ds(pl.multiple_of(i * bb, bb), bb)
    raw = proj[rows, 0:A_CONV]
    cbuf = cbuf_ref[...]
    conv = raw * convw_ref[CONV_W - 1:CONV_W, :]
    for j in range(CONV_W - 1):
        conv = conv + cbuf[:, j * A_CONV:(j + 1) * A_CONV] * convw_ref[j:j + 1, :]
    cbuf_out[:, 0:(CONV_W - 2) * A_CONV] = cbuf[:, A_CONV:(CONV_W - 1) * A_CONV]
    cbuf_out[:, (CONV_W - 2) * A_CONV:(CONV_W - 1) * A_CONV] = raw
    act = _silu(conv)
    gt = _gate_transform(proj[rows, GT:GT + LANES], gp_ref[...])
    gdn_w = gp_ref[4:5, :]
    ml_w = gp_ref[5:6, :]
    m_all = m_in[...]

    for h in range(N_HEADS):
        def head(off):
            return act[:, off + h * HEAD_DIM:off + (h + 1) * HEAD_DIM]
        q = head(QA)
        q = q * lax.rsqrt(jnp.sum(q * q, axis=-1, keepdims=True) + RMS_EPS) * HEAD_DIM ** -0.5
        k = head(KA)
        k = k * lax.rsqrt(jnp.sum(k * k, axis=-1, keepdims=True) + RMS_EPS)
        v = head(VA)
        q_t = q.T
        k_t = k.T
        qk = jnp.sum(q * k, axis=-1, keepdims=True)
        qb = proj[rows, QB + h * HEAD_DIM:QB + (h + 1) * HEAD_DIM]
        kbm = proj[rows, KB + h * HEAD_DIM:KB + (h + 1) * HEAD_DIM] * HEAD_DIM ** -0.5
        vbm = proj[rows, VB + h * HEAD_DIM:VB + (h + 1) * HEAD_DIM]
        qb_t = qb.T
        kb_t = kbm.T
        qkb = jnp.sum(qb * kbm, axis=-1, keepdims=True)
        n_old = n_in[:, h * HEAD_DIM:(h + 1) * HEAD_DIM]
        qn = jnp.sum(qb * n_old, axis=-1, keepdims=True)
        o_rows = []
        hb_rows = []
        n_rows = []
        m_rows = []
        for b in range(bb):
            s_old = s_in[b, h]
            k_c = k_t[:, b:b + 1]
            q_c = q_t[:, b:b + 1]
            e_g = jnp.exp(gt[b:b + 1, G_DEC + h:G_DEC + h + 1])
            beta = gt[b:b + 1, G_BETA + h:G_BETA + h + 1]
            k_s = jnp.sum(k_c * s_old, axis=0, keepdims=True)
            q_s = jnp.sum(q_c * s_old, axis=0, keepdims=True)
            v_new = v[b:b + 1, :] * beta - (beta * e_g) * k_s
            o_rows.append(e_g * q_s + qk[b:b + 1, :] * v_new)
            s_out[b, h] = s_old * e_g + k_c * v_new
            c_old = c_in[b, h]
            m_old = m_all[b:b + 1, h:h + 1]
            i_pre = gt[b:b + 1, G_IN + h:G_IN + h + 1]
            logf = gt[b:b + 1, G_FG + h:G_FG + h + 1]
            inter = logf + m_old
            mt = jnp.maximum(inter, i_pre)
            w_in = jnp.exp(i_pre - mt)
            sc = jnp.exp(inter - mt)
            wts = w_in * qkb[b:b + 1, :]
            q_cm = jnp.sum(qb_t[:, b:b + 1] * c_old, axis=0, keepdims=True)
            num = sc * q_cm + wts * vbm[b:b + 1, :]
            den = sc * qn[b:b + 1, :] + wts
            hb_rows.append(num / jnp.maximum(jnp.abs(den), jnp.exp(-mt)))
            c_out[b, h] = sc * c_old + (w_in * kb_t[:, b:b + 1]) * vbm[b:b + 1, :]
            n_rows.append(sc * n_old[b:b + 1, :] + w_in * kbm[b:b + 1, :])
            m_rows.append(mt)
        o_a = jnp.concatenate(o_rows, axis=0)
        h_b = jnp.concatenate(hb_rows, axis=0)
        n_out[:, h * HEAD_DIM:(h + 1) * HEAD_DIM] = jnp.concatenate(n_rows, axis=0)
        m_out[:, h:h + 1] = jnp.concatenate(m_rows, axis=0)
        z = proj[rows, ZA + h * HEAD_DIM:ZA + (h + 1) * HEAD_DIM]
        merged[rows, h * HEAD_DIM:(h + 1) * HEAD_DIM] = _rms(o_a, gdn_w) * _silu(z)
        o_gate = proj[rows, OB + h * HEAD_DIM:OB + (h + 1) * HEAD_DIM]
        merged[rows, N_HEADS * HEAD_DIM + h * HEAD_DIM:N_HEADS * HEAD_DIM + (h + 1) * HEAD_DIM] = (
            _rms(h_b, ml_w) * _sigmoid(o_gate))

    @pl.when(i == n_steps - 1)
    def _():
        mix = jnp.dot(_bf(merged[...]), wout_ref[...], preferred_element_type=F32)
        y_ref[...] = _layer_norm(DN_ALPHA * x_ref[...] + mix, ln_ref[0:1, :], ln_ref[1:2, :])


def _ab_decode(x, win, wout, convw, gp, ln, cbuf, s0, c0, n0, m0):
    nb = x.shape[0]
    bb = SUBLANES
    assert nb % bb == 0
    n_steps = nb // bb
    const = lambda shape: pl.BlockSpec(shape, lambda i: (0,) * len(shape))
    blk = lambda shape: pl.BlockSpec((bb,) + shape, lambda i: (i,) + (0,) * len(shape))
    hist = (CONV_W - 1) * A_CONV
    width = N_HEADS * HEAD_DIM
    return pl.pallas_call(
        functools.partial(_ab_decode_kernel, bb=bb, n_steps=n_steps),
        grid=(n_steps,),
        in_specs=[
            const((nb, D_MODEL)), const((D_MODEL, D_IN_PAD)), const((2 * width, D_MODEL)),
            const((CONV_W, A_CONV)), const((SUBLANES, LANES)), const((2, D_MODEL)),
            blk((hist,)), blk((N_HEADS, HEAD_DIM, HEAD_DIM)), blk((N_HEADS, HEAD_DIM, HEAD_DIM)),
            blk((width,)), blk((N_HEADS,)),
        ],
        out_specs=[
            const((nb, D_MODEL)), blk((hist,)), blk((N_HEADS, HEAD_DIM, HEAD_DIM)),
            blk((N_HEADS, HEAD_DIM, HEAD_DIM)), blk((width,)), blk((N_HEADS,)),
        ],
        out_shape=[
            jax.ShapeDtypeStruct((nb, D_MODEL), F32),
            jax.ShapeDtypeStruct((nb, hist), F32),
            jax.ShapeDtypeStruct((nb, N_HEADS, HEAD_DIM, HEAD_DIM), F32),
            jax.ShapeDtypeStruct((nb, N_HEADS, HEAD_DIM, HEAD_DIM), F32),
            jax.ShapeDtypeStruct((nb, width), F32),
            jax.ShapeDtypeStruct((nb, N_HEADS), F32),
        ],
        scratch_shapes=[pltpu.VMEM((nb, D_IN_PAD), F32), pltpu.VMEM((nb, 2 * width), F32)],
        compiler_params=pltpu.CompilerParams(dimension_semantics=("arbitrary",), vmem_limit_bytes=VMEM_LIMIT),
        name="ab_decode",
    )(x, win, wout, convw, gp, ln, cbuf, s0, c0, n0, m0)


def _second_largest_sum(a, b, c, d):
    hi1, lo1 = jnp.maximum(a, b), jnp.minimum(a, b)
    hi2, lo2 = jnp.maximum(c, d), jnp.minimum(c, d)
    return jnp.maximum(hi1, hi2) + jnp.maximum(jnp.minimum(hi1, hi2), jnp.maximum(lo1, lo2))


def _first_argmax(vals):
    best_v = vals[0]
    best_i = jnp.zeros(vals[0].shape, jnp.int32)
    for j in range(1, len(vals)):
        better = vals[j] > best_v
        best_v = jnp.where(better, vals[j], best_v)
        best_i = jnp.where(better, j, best_i)
    return best_i


def _router_kernel(x_ref, rw_ref, rb_ref, meta_ref, cnt_ref, carry, *, n_steps):
    i = pl.program_id(0)
    tm = x_ref.shape[0]

    @pl.when(i == 0)
    def _():
        carry[...] = jnp.zeros_like(carry)

    x = x_ref[...]
    rw = rw_ref[...]
    x_hi = _bf(x)
    x_lo = _bf(x - x_hi.astype(F32))
    w_hi = _bf(rw)
    w_lo = _bf(rw - w_hi.astype(F32))
    nt = lambda a, b: lax.dot_general(a, b, (((1,), (1,)), ((), ())), preferred_element_type=F32)
    logits = nt(w_hi, x_hi) + (nt(w_hi, x_lo) + nt(w_lo, x_hi))
    ex = jnp.exp(logits - jnp.max(logits, axis=0, keepdims=True))
    probs = ex / jnp.sum(ex, axis=0, keepdims=True)
    sel = probs + rb_ref[...]
    p = [probs[j:j + 1, :] for j in range(N_EXPERTS)]
    s = [sel[j:j + 1, :] for j in range(N_EXPERTS)]
    scores = [_second_largest_sum(*s[EXPERTS_PER_GROUP * g:EXPERTS_PER_GROUP * (g + 1)])
              for g in range(N_EXPERT_GROUPS)]
    best = _first_argmax(scores)
    masked = [jnp.where(best == j // EXPERTS_PER_GROUP, s[j], -jnp.inf) for j in range(N_EXPERTS)]
    i1 = _first_argmax(masked)
    i2 = _first_argmax([jnp.where(i1 == j, -jnp.inf, masked[j]) for j in range(N_EXPERTS)])
    zero = jnp.zeros_like(p[0])
    p1 = functools.reduce(lambda a, b: a + b, [jnp.where(i1 == j, p[j], zero) for j in range(N_EXPERTS)])
    p2 = functools.reduce(lambda a, b: a + b, [jnp.where(i2 == j, p[j], zero) for j in range(N_EXPERTS)])
    tot = p1 + p2
    rows = [jnp.where(i1 == j, p1 / tot, zero) + jnp.where(i2 == j, p2 / tot, zero) for j in range(N_EXPERTS)]
    in_group = [best == g for g in range(N_EXPERT_GROUPS)]
    local = [functools.reduce(lambda a, b: a + b,
                              [jnp.where(in_group[g], rows[EXPERTS_PER_GROUP * g + e], zero)
                               for g in range(N_EXPERT_GROUPS)])
             for e in range(EXPERTS_PER_GROUP)]
    onehot = jnp.concatenate([jnp.where(m, 1.0, 0.0) for m in in_group]
                             + [jnp.zeros((SUBLANES - N_EXPERT_GROUPS, tm), F32)], axis=0)
    ri = lax.broadcasted_iota(jnp.int32, (tm, tm), 0)
    ci = lax.broadcasted_iota(jnp.int32, (tm, tm), 1)
    incl = jnp.dot(_bf(onehot), _bf(jnp.where(ri <= ci, 1.0, 0.0)), preferred_element_type=F32)
    prev = carry[...]
    rank = jnp.sum(onehot * (incl - 1.0 + prev[:, 0:1]), axis=0, keepdims=True)
    carry[...] = prev + incl[:, tm - 1:tm]
    row_id = (i * tm + lax.broadcasted_iota(jnp.int32, (1, tm), 1)).astype(F32)
    meta_ref[...] = jnp.concatenate([row_id] + local + [best.astype(F32), rank, zero], axis=0).T

    @pl.when(i == n_steps - 1)
    def _():
        cnt_ref[...] = carry[...]


def _router(x, rw_t, rb_col):
    t = x.shape[0]
    tm = min(256, t)
    assert t % tm == 0
    n_steps = t // tm
    return pl.pallas_call(
        functools.partial(_router_kernel, n_steps=n_steps),
        grid=(n_steps,),
        in_specs=[pl.BlockSpec((tm, D_MODEL), lambda i: (i, 0)),
                  pl.BlockSpec((N_EXPERTS, D_MODEL), lambda i: (0, 0)),
                  pl.BlockSpec((N_EXPERTS, 1), lambda i: (0, 0))],
        out_specs=[pl.BlockSpec((tm, SUBLANES), lambda i: (i, 0)),
                   pl.BlockSpec((SUBLANES, LANES), lambda i: (0, 0))],
        out_shape=[jax.ShapeDtypeStruct((t, SUBLANES), F32), jax.ShapeDtypeStruct((SUBLANES, LANES), F32)],
        scratch_shapes=[pltpu.VMEM((SUBLANES, LANES), F32)],
        compiler_params=pltpu.CompilerParams(dimension_semantics=("arbitrary",), vmem_limit_bytes=VMEM_LIMIT),
        name="router",
    )(x, rw_t, rb_col)


def _moe_kernel(tg_ref, nv_ref, src_cur, dst_cur, src_nxt, cw_ref, x_hbm, wg_ref, wu_ref, wd_ref, ln_ref,
                y_hbm, xg, stage, gsem, ssem, *, tm, n_tiles):
    i = pl.program_id(0)
    slot = lax.rem(i, 2)
    other = 1 - slot
    n_valid = nv_ref[i]
    n_prev = jnp.where(i >= 1, nv_ref[jnp.maximum(i - 1, 0)], 0)
    n_back2 = jnp.where(i >= 2, nv_ref[jnp.maximum(i - 2, 0)], 0)
    has_next = i + 1 < n_tiles

    def gather_row(tok_ref, r, dst_slot):
        return pltpu.make_async_copy(x_hbm.at[pl.ds(tok_ref[0, r], 1), :], xg.at[dst_slot, pl.ds(r, 1), :],
                                     gsem.at[dst_slot])

    def scatter_row(tok_ref, r, src_slot):
        return pltpu.make_async_copy(stage.at[src_slot, pl.ds(r, 1), :], y_hbm.at[pl.ds(tok_ref[0, r], 1), :],
                                     ssem.at[src_slot])

    def scatter_wait(count, src_slot):
        for p in [1 << b for b in range(tm.bit_length())]:
            @pl.when(lax.bitwise_and(count, p) != 0)
            def _():
                pltpu.make_async_copy(stage.at[src_slot, pl.ds(0, p), :], y_hbm.at[pl.ds(0, p), :],
                                      ssem.at[src_slot]).wait()

    @pl.when(i == 0)
    def _():
        def body(r, c):
            gather_row(src_cur, r, 0).start()
            return c
        lax.fori_loop(0, tm, body, 0, unroll=8)

    pltpu.make_async_copy(x_hbm.at[pl.ds(0, tm), :], xg.at[slot], gsem.at[slot]).wait()

    for parity in range(2):
        @pl.when(jnp.logical_and(has_next, slot == parity))
        def _():
            for r in range(tm):
                gather_row(src_nxt, r, 1 - parity).start(priority=r % 2)

    @pl.when(n_valid > 0)
    def _():
        x = xg[slot]
        xb = _bf(x)
        cw = cw_ref[...]
        acc = jnp.zeros((tm, D_MODEL), F32)
        for e in range(EXPERTS_PER_GROUP):
            g = jnp.dot(xb, wg_ref[e], preferred_element_type=F32)
            u = jnp.dot(xb, wu_ref[e], preferred_element_type=F32)
            hid = _silu(g) * u * cw[:, 1 + e:2 + e]
            acc = acc + jnp.dot(_bf(hid), wd_ref[e], preferred_element_type=F32)
        y = _layer_norm(DN_ALPHA * x + acc, ln_ref[0:1, :], ln_ref[1:2, :])

        @pl.when(i >= 2)
        def _():
            scatter_wait(n_back2, slot)
        stage[slot] = y

    @pl.when(jnp.logical_and(n_valid <= 0, i >= 2))
    def _():
        scatter_wait(n_back2, slot)

    for parity in range(2):
        @pl.when(jnp.logical_and(n_valid == tm, slot == parity))
        def _():
            for r in range(tm):
                scatter_row(dst_cur, r, parity).start(priority=r % 2)

    @pl.when(n_valid < tm)
    def _():
        for r in range(tm):
            @pl.when(r < n_valid)
            def _():
                scatter_row(dst_cur, r, slot).start(priority=r % 2)

    @pl.when(i == n_tiles - 1)
    def _():
        scatter_wait(n_prev, other)
        scatter_wait(n_valid, slot)


def _moe(x, meta, counts, wg, wu, wd, ln, layer, out_row):
    t = x.shape[0]
    tm = min(512, t)
    assert t % tm == 0 and tm & (tm - 1) == 0
    n_tiles = t // tm + N_EXPERT_GROUPS
    n_slots = n_tiles * tm
    grp = meta[:, 1 + EXPERTS_PER_GROUP].astype(jnp.int32)
    rank = meta[:, 2 + EXPERTS_PER_GROUP].astype(jnp.int32)
    cnt = counts[:N_EXPERT_GROUPS, 0].astype(jnp.int32)
    padded = ((cnt + tm - 1) // tm) * tm
    g_end = jnp.cumsum(padded)
    g_off = g_end - padded
    pos = jnp.take(g_off, grp) + rank
    slot_meta = jnp.zeros((n_slots, SUBLANES), F32).at[pos].set(meta)
    tok_of_slot = slot_meta[:, 0].astype(jnp.int32)
    tile_start = jnp.arange(n_tiles, dtype=jnp.int32) * tm
    tile_grp = jnp.minimum(jnp.sum(tile_start[:, None] >= g_end[None, :], axis=1), N_EXPERT_GROUPS - 1).astype(jnp.int32)
    tile_valid = jnp.clip(jnp.take(g_off + cnt, tile_grp) - tile_start, 0, tm).astype(jnp.int32)
    src3 = tok_of_slot.reshape(n_tiles, 1, tm)
    dst3 = out_row(tok_of_slot).reshape(n_tiles, 1, tm)

    grid_spec = pltpu.PrefetchScalarGridSpec(
        num_scalar_prefetch=2,
        grid=(n_tiles,),
        in_specs=[
            pl.BlockSpec((None, 1, tm), lambda i, tg, nv: (i, 0, 0), memory_space=pltpu.SMEM),
            pl.BlockSpec((None, 1, tm), lambda i, tg, nv: (i, 0, 0), memory_space=pltpu.SMEM),
            pl.BlockSpec((None, 1, tm), lambda i, tg, nv: (jnp.minimum(i + 1, n_tiles - 1), 0, 0),
                         memory_space=pltpu.SMEM),
            pl.BlockSpec((tm, SUBLANES), lambda i, tg, nv: (i, 0)),
            pl.BlockSpec(memory_space=pl.ANY),
            pl.BlockSpec((None, EXPERTS_PER_GROUP, D_MODEL, D_FF), lambda i, tg, nv: (layer, tg[i], 0, 0)),
            pl.BlockSpec((None, EXPERTS_PER_GROUP, D_MODEL, D_FF), lambda i, tg, nv: (layer, tg[i], 0, 0)),
            pl.BlockSpec((None, EXPERTS_PER_GROUP, D_FF, D_MODEL), lambda i, tg, nv: (layer, tg[i], 0, 0)),
            pl.BlockSpec((2, D_MODEL), lambda i, tg, nv: (0, 0)),
        ],
        out_specs=pl.BlockSpec(memory_space=pl.ANY),
        scratch_shapes=[pltpu.VMEM((2, tm, D_MODEL), F32), pltpu.VMEM((2, tm, D_MODEL), F32),
                        pltpu.SemaphoreType.DMA((2,)), pltpu.SemaphoreType.DMA((2,))],
    )
    return pl.pallas_call(
        functools.partial(_moe_kernel, tm=tm, n_tiles=n_tiles),
        grid_spec=grid_spec,
        out_shape=jax.ShapeDtypeStruct((t, D_MODEL), F32),
        compiler_params=pltpu.CompilerParams(dimension_semantics=("arbitrary",), vmem_limit_bytes=VMEM_LIMIT),
        name="moe",
    )(tile_grp, tile_valid, src3, dst3, src3, slot_meta, x, wg, wu, wd, ln)


def _s5_prep_kernel(are_ref, aim_ref, ldt_ref, bre_ref, bim_ref, abre_ref, abim_ref, bbre_ref, bbim_ref):
    a_r = are_ref[...]
    a_i = aim_ref[...]
    dt = jnp.exp(ldt_ref[...])
    mag = jnp.exp(dt * a_r)
    ab_re = mag * jnp.cos(dt * a_i)
    ab_im = mag * jnp.sin(dt * a_i)
    den = a_r * a_r + a_i * a_i
    nr = ab_re - 1.0
    z_re = (nr * a_r + ab_im * a_i) / den
    z_im = (ab_im * a_r - nr * a_i) / den
    abre_ref[...] = ab_re
    abim_ref[...] = ab_im
    bbre_ref[...] = z_re * bre_ref[...] - z_im * bim_ref[...]
    bbim_ref[...] = z_re * bim_ref[...] + z_im * bre_ref[...]


def _s5_prep(a_re, a_im, log_dt, b_re, b_im):
    n = S5_GROUPS * S5_STATE
    col = jax.ShapeDtypeStruct((n, 1), F32)
    mat = jax.ShapeDtypeStruct((n, S5_GROUP), F32)
    ldt = jnp.broadcast_to(log_dt[:, None], (S5_GROUPS, S5_STATE)).reshape(n, 1)
    return pl.pallas_call(_s5_prep_kernel, out_shape=[col, col, mat, mat], name="s5_prep")(
        a_re.reshape(n, 1), a_im.reshape(n, 1), ldt, b_re.reshape(n, S5_GROUP), b_im.reshape(n, S5_GROUP))


def _gelu_tanh(x):
    return 0.5 * x * (1.0 + jnp.tanh(0.7978845608028654 * (x + 0.044715 * (x * x * x))))


def _s5_kernel(x_ref, h0re_ref, h0im_ref, wre_ref, wim_ref, cre_ref, cim_ref, abre_ref, abim_ref, d_ref,
               gy_ref, hre_out, him_out, bu_re, bu_im, hre_s, him_s, *, nb, tt, n_t):
    t = pl.program_id(0)
    sw = wre_ref.shape[2]

    @pl.when(t == 0)
    def _():
        hre_s[...] = h0re_ref[...]
        him_s[...] = h0im_ref[...]

    for k in range(D_MODEL // LANES):
        ls = slice(k * LANES, (k + 1) * LANES)
        ss = slice(k * sw, (k + 1) * sw)
        buf = k % 2
        xv = x_ref[:, ls]
        xb = _bf(xv)
        bu_re[buf] = jnp.dot(xb, wre_ref[k], preferred_element_type=F32)
        bu_im[buf] = jnp.dot(xb, wim_ref[k], preferred_element_type=F32)
        a_re = jnp.broadcast_to(abre_ref[:, ss], (nb, sw))
        a_im = jnp.broadcast_to(abim_ref[:, ss], (nb, sw))

        def step(s, carry, buf=buf, a_re=a_re, a_im=a_im):
            h_re, h_im = carry
            rows = pl.ds(pl.multiple_of(s * nb, nb), nb)
            n_re = a_re * h_re - a_im * h_im + bu_re[buf, rows, :]
            n_im = a_re * h_im + a_im * h_re + bu_im[buf, rows, :]
            bu_re[buf, rows, :] = n_re
            bu_im[buf, rows, :] = n_im
            return n_re, n_im

        h_re, h_im = lax.fori_loop(0, tt, step, (hre_s[:, ss], him_s[:, ss]), unroll=min(tt, S5_UNROLL))
        hre_s[:, ss] = h_re
        him_s[:, ss] = h_im
        y = (jnp.dot(_bf(bu_re[buf]), cre_ref[k], preferred_element_type=F32)
             - jnp.dot(_bf(bu_im[buf]), cim_ref[k], preferred_element_type=F32)
             + d_ref[:, ls] * xv)
        gy_ref[:, ls] = _gelu_tanh(y).astype(gy_ref.dtype)

    @pl.when(t == n_t - 1)
    def _():
        hre_out[...] = hre_s[...]
        him_out[...] = him_s[...]


def _s5_scan(x, nb, h0_re, h0_im, wre, wim, cre, cim, ab_re, ab_im, d_skip):
    length = x.shape[0] // nb
    tt = min(S5_TT, length)
    assert length % tt == 0 and nb % SUBLANES == 0
    n_t = length // tt
    n_k = D_MODEL // LANES
    sw = (LANES // S5_GROUP) * S5_STATE
    n_state = S5_GROUPS * S5_STATE
    const = lambda shape: pl.BlockSpec(shape, lambda t: (0,) * len(shape))
    return pl.pallas_call(
        functools.partial(_s5_kernel, nb=nb, tt=tt, n_t=n_t),
        grid=(n_t,),
        in_specs=[
            pl.BlockSpec((tt * nb, D_MODEL), lambda t: (t, 0)),
            const((nb, n_state)), const((nb, n_state)),
            const((n_k, LANES, sw)), const((n_k, LANES, sw)), const((n_k, sw, LANES)), const((n_k, sw, LANES)),
            const((1, n_state)), const((1, n_state)), const((1, D_MODEL)),
        ],
        out_specs=[pl.BlockSpec((tt * nb, D_MODEL), lambda t: (t, 0)), const((nb, n_state)), const((nb, n_state))],
        out_shape=[
            jax.ShapeDtypeStruct((length * nb, D_MODEL), BF16),
            jax.ShapeDtypeStruct((nb, n_state), F32),
            jax.ShapeDtypeStruct((nb, n_state), F32),
        ],
        scratch_shapes=[
            pltpu.VMEM((2, nb * tt, sw), F32), pltpu.VMEM((2, nb * tt, sw), F32),
            pltpu.VMEM((nb, n_state), F32), pltpu.VMEM((nb, n_state), F32),
        ],
        compiler_params=pltpu.CompilerParams(dimension_semantics=("arbitrary",), vmem_limit_bytes=VMEM_LIMIT),
        name="s5_scan",
    )(x, h0_re, h0_im, wre, wim, cre, cim, ab_re, ab_im, d_skip)


def _glu_ln_kernel(gy_ref, h_ref, wa_ref, wb_ref, ln_ref, o_ref):
    gy = gy_ref[...]
    a = jnp.dot(gy, wa_ref[...], preferred_element_type=F32)
    b = jnp.dot(gy, wb_ref[...], preferred_element_type=F32)
    o_ref[...] = _layer_norm(DN_ALPHA * h_ref[...] + a * _sigmoid(b), ln_ref[0:1, :], ln_ref[1:2, :])


def _glu_ln(gy, h, wa, wb, ln):
    t = h.shape[0]
    tm = min(512, t)
    assert t % tm == 0
    return pl.pallas_call(
        _glu_ln_kernel,
        grid=(t // tm,),
        in_specs=[pl.BlockSpec((tm, D_MODEL), lambda i: (i, 0)), pl.BlockSpec((tm, D_MODEL), lambda i: (i, 0)),
                  pl.BlockSpec((D_MODEL, D_MODEL), lambda i: (0, 0)), pl.BlockSpec((D_MODEL, D_MODEL), lambda i: (0, 0)),
                  pl.BlockSpec((2, D_MODEL), lambda i: (0, 0))],
        out_specs=pl.BlockSpec((tm, D_MODEL), lambda i: (i, 0)),
        out_shape=jax.ShapeDtypeStruct((t, D_MODEL), F32),
        compiler_params=pltpu.CompilerParams(dimension_semantics=("arbitrary",), vmem_limit_bytes=VMEM_LIMIT),
        name="glu_ln",
    )(gy, h, wa, wb, ln)


def _block_diag_slices(m, rows_per_group, cols_per_group):
    gps = LANES // S5_GROUP
    m = m.reshape(S5_GROUPS // gps, gps, rows_per_group, cols_per_group)
    eye = jnp.eye(gps, dtype=m.dtype)
    out = m[:, :, :, None, :] * eye[None, :, None, :, None]
    return out.reshape(S5_GROUPS // gps, gps * rows_per_group, gps * cols_per_group)


def _prepare(p):
    w = p['w_in'][0]
    win = jnp.concatenate(
        [w[:, 0:1536], w[:, 1544:2056], w[:, 2056:3592], w[:, 3600:4112], w[:, 1536:1544], w[:, 3592:3600],
         jnp.zeros((D_MODEL, D_IN_PAD - 4112), w.dtype)], axis=1).astype(BF16)
    at_lane = lambda v, lane0: jnp.pad(v, (lane0, LANES - lane0 - v.shape[0]))
    gp = jnp.stack([at_lane(p['gdn_A_log'][0], G_DEC), at_lane(p['gdn_dt_bias'][0], G_DEC),
                    at_lane(p['ml_b_i'][0], G_IN), at_lane(p['ml_b_f'][0], G_FG),
                    p['gdn_norm_w'][0], p['ml_norm_w'][0], jnp.zeros((LANES,), F32), jnp.zeros((LANES,), F32)])
    ab_re, ab_im, bb_re, bb_im = _s5_prep(p['s5_A_re'][0], p['s5_A_im'][0], p['s5_log_dt'][0],
                                          p['s5_B_re'][0], p['s5_B_im'][0])
    to_in = lambda bb: _block_diag_slices(
        bb.reshape(S5_GROUPS, S5_STATE, S5_GROUP).transpose(0, 2, 1), S5_GROUP, S5_STATE).astype(BF16)
    to_out = lambda c: _block_diag_slices(c.transpose(0, 2, 1), S5_STATE, S5_GROUP).astype(BF16)
    return dict(
        win=win, wout=p['w_out'][0].astype(BF16), convw=p['gdn_conv_w'][0], gp=gp,
        ln_mix=[jnp.stack([p['ln_mix_g'][l], p['ln_mix_b'][l]]) for l in range(DEPTH)],
        ln_ffn=[jnp.stack([p['ln_ffn_g'][l], p['ln_ffn_b'][l]]) for l in range(DEPTH)],
        rw_t=p['router_w'].T, rb_col=p['router_b'][:, None],
        wg=p['moe_w_gate'].astype(BF16), wu=p['moe_w_up'].astype(BF16), wd=p['moe_w_down'].astype(BF16),
        s5_wre=to_in(bb_re), s5_wim=to_in(bb_im),
        s5_cre=to_out(p['s5_C_re'][0]), s5_cim=to_out(p['s5_C_im'][0]),
        s5_abre=ab_re.reshape(1, -1), s5_abim=ab_im.reshape(1, -1), s5_d=p['s5_D'][0][None, :],
        glu_a=p['s5_w_glu_a'][0].astype(BF16), glu_b=p['s5_w_glu_b'][0].astype(BF16),
    )


def _ffn(h, w, layer, out_row):
    meta, counts = _router(h, w['rw_t'], w['rb_col'])
    return _moe(h, meta, counts, w['wg'], w['wu'], w['wd'], w['ln_ffn'][layer], layer, out_row)


def _s5_layer(h, nb, h0_re, h0_im, w):
    gy, re, im = _s5_scan(h, nb, h0_re, h0_im, w['s5_wre'], w['s5_wim'],
                          w['s5_cre'], w['s5_cim'], w['s5_abre'], w['s5_abim'], w['s5_d'])
    h = _glu_ln(gy, h, w['glu_a'], w['glu_b'], w['ln_mix'][1])
    return h, re, im


def kernel(x_prompt, x_sample, state_gdn_conv, state_gdn_S, state_mlstm_C, state_mlstm_n, state_mlstm_m,
           state_s5_re, state_s5_im, w_in, gdn_conv_w, gdn_A_log, gdn_dt_bias, gdn_norm_w, ml_b_i, ml_b_f,
           ml_norm_w, w_out, s5_A_re, s5_A_im, s5_log_dt, s5_B_re, s5_B_im, s5_C_re, s5_C_im, s5_D,
           s5_w_glu_a, s5_w_glu_b, router_w, router_b, moe_w_gate, moe_w_up, moe_w_down,
           ln_mix_g, ln_mix_b, ln_ffn_g, ln_ffn_b):
    w = _prepare(dict(
        w_in=w_in, gdn_conv_w=gdn_conv_w, gdn_A_log=gdn_A_log, gdn_dt_bias=gdn_dt_bias, gdn_norm_w=gdn_norm_w,
        ml_b_i=ml_b_i, ml_b_f=ml_b_f, ml_norm_w=ml_norm_w, w_out=w_out, s5_A_re=s5_A_re, s5_A_im=s5_A_im,
        s5_log_dt=s5_log_dt, s5_B_re=s5_B_re, s5_B_im=s5_B_im, s5_C_re=s5_C_re, s5_C_im=s5_C_im, s5_D=s5_D,
        s5_w_glu_a=s5_w_glu_a, s5_w_glu_b=s5_w_glu_b, router_w=router_w, router_b=router_b,
        moe_w_gate=moe_w_gate, moe_w_up=moe_w_up, moe_w_down=moe_w_down,
        ln_mix_g=ln_mix_g, ln_mix_b=ln_mix_b, ln_ffn_g=ln_ffn_g, ln_ffn_b=ln_ffn_b))
    bp, lp, _ = x_prompt.shape
    bs, ls, _ = x_sample.shape
    assert ls == 1
    n_state = S5_GROUPS * S5_STATE

    h, p_hist, p_s, p_c, p_n, p_m = _ab_prompt(x_prompt, w['win'], w['wout'], w['convw'], w['gp'], w['ln_mix'][0])
    h = _ffn(h.reshape(bp * lp, D_MODEL), w, 0, lambda r: (r % lp) * bp + r // lp)
    zeros = jnp.zeros((bp, n_state), F32)
    h, p_re, p_im = _s5_layer(h, bp, zeros, zeros, w)
    y_prompt = _ffn(h, w, 1, lambda r: (r % bp) * lp + r // bp).reshape(bp, lp, D_MODEL)

    hs, s_cbuf, s_s, s_c, s_n, s_m = _ab_decode(
        x_sample[:, 0], w['win'], w['wout'], w['convw'], w['gp'], w['ln_mix'][0],
        state_gdn_conv.reshape(bs, (CONV_W - 1) * A_CONV), state_gdn_S[:, 0], state_mlstm_C[:, 0],
        state_mlstm_n.reshape(bs, N_HEADS * HEAD_DIM), state_mlstm_m[:, 0])
    hs = _ffn(hs, w, 0, lambda r: r)
    hs, s_re, s_im = _s5_layer(hs, bs, state_s5_re.reshape(bs, n_state), state_s5_im.reshape(bs, n_state), w)
    y_sample = _ffn(hs, w, 1, lambda r: r).reshape(bs, 1, D_MODEL)

    grp = lambda a, n: a.reshape(n, 1, S5_GROUPS, S5_STATE)
    return (
        y_prompt, y_sample,
        p_hist[:, None, SUBLANES - (CONV_W - 1):, :], p_s[:, None], p_c[:, None],
        p_n[:, None, :, 0, :], p_m[:, None, :, 0, 0], grp(p_re, bp), grp(p_im, bp),
        s_cbuf.reshape(bs, 1, CONV_W - 1, A_CONV), s_s[:, None], s_c[:, None],
        s_n.reshape(bs, 1, N_HEADS, HEAD_DIM), s_m[:, None], grp(s_re, bs), grp(s_im, bs),
    )
```

```python
import functools

import jax
import jax.numpy as jnp
from jax import lax
from jax.experimental import pallas as pl
from jax.experimental.pallas import tpu as pltpu

F32 = jnp.float32
BF16 = jnp.bfloat16
HIGHEST = lax.Precision.HIGHEST

D_MODEL = 1024
DEPTH = 2
N_HEADS = 4
HEAD_DIM = 128
CONV_W = 4
CHUNK = 64
A_CONV = 3 * N_HEADS * HEAD_DIM
S5_GROUP = 16
S5_GROUPS = D_MODEL // S5_GROUP
S5_STATE = 64
N_EXPERTS = 16
EXPERTS_PER_GROUP = 4
N_EXPERT_GROUPS = N_EXPERTS // EXPERTS_PER_GROUP
D_FF = 512
DN_ALPHA = (2 * DEPTH) ** 0.25
LN_EPS = 1e-5
RMS_EPS = 1e-6
NEG_BIG = -1e30

QA, KA, VA, ZA, QB, KB, VB, OB, GT = 0, 512, 1024, 1536, 2048, 2560, 3072, 3584, 4096
D_IN_PAD = 4224
G_DEC, G_BETA, G_IN, G_FG = 0, 4, 8, 12

S5_TT = 64
S5_UNROLL = 64

LANES = 128
SUBLANES = 8
VMEM_LIMIT = 56 * 1024 * 1024


def _bf(x):
    return x.astype(BF16)


def _nn(a, b):
    return jnp.dot(_bf(a), _bf(b), preferred_element_type=F32)


def _nt(a, b):
    return lax.dot_general(_bf(a), _bf(b), (((1,), (1,)), ((), ())), preferred_element_type=F32)


def _tn(a, b):
    return lax.dot_general(_bf(a), _bf(b), (((0,), (0,)), ((), ())), preferred_element_type=F32)


def _sigmoid(x):
    return 1.0 / (1.0 + jnp.exp(-x))


def _softplus(x):
    return jnp.maximum(x, 0.0) + jnp.log(1.0 + jnp.exp(-jnp.abs(x)))


def _silu(x):
    return x * _sigmoid(x)


def _layer_norm(y, g, b):
    mu = jnp.mean(y, axis=-1, keepdims=True)
    yc = y - mu
    var = jnp.mean(yc * yc, axis=-1, keepdims=True)
    return yc * lax.rsqrt(var + LN_EPS) * g + b


def _rms(x, w):
    return x * lax.rsqrt(jnp.mean(x * x, axis=-1, keepdims=True) + RMS_EPS) * w


def _gate_transform(raw, gp):
    lane = lax.broadcasted_iota(jnp.int32, raw.shape, 1)
    dec = -jnp.exp(gp[0:1, :]) * _softplus(raw + gp[1:2, :])
    beta = _sigmoid(raw)
    ipre = raw + gp[2:3, :]
    logf = -_softplus(-(raw + gp[3:4, :]))
    return jnp.where(lane < G_BETA, dec,
                     jnp.where(lane < G_IN, beta,
                               jnp.where(lane < G_FG, ipre,
                                         jnp.where(lane < G_FG + N_HEADS, logf, 0.0))))


def _bnn(a, b):
    return lax.dot_general(_bf(a), _bf(b), (((2,), (1,)), ((0,), (0,))), preferred_element_type=F32)


def _bnt(a, b):
    return lax.dot_general(_bf(a), _bf(b), (((2,), (2,)), ((0,), (0,))), preferred_element_type=F32)


def _btn(a, b):
    return lax.dot_general(_bf(a), _bf(b), (((1,), (1,)), ((0,), (0,))), preferred_element_type=F32)


def _unit_lower_inverse_minus_eye(a, pack):
    p, c, _ = a.shape
    nb = p // pack
    w = pack * c
    r = -jnp.stack([jnp.concatenate([a[b * pack + j] for j in range(pack)], axis=1) for b in range(nb)])
    assert c & (c - 1) == 0
    block_of = lambda dim: lax.shift_right_logical(lax.broadcasted_iota(jnp.int32, (w, w), dim), c.bit_length() - 1)
    on_diag = block_of(0) == block_of(1)

    def block_diag(x):
        return jnp.where(on_diag, jnp.concatenate([x] * pack, axis=1), 0.0)

    steps = max(1, (c - 1).bit_length()) - 1
    q = _bnn(r, block_diag(r))
    for i in range(steps):
        bd = block_diag(q)
        if i + 1 < steps:
            both = _bnn(jnp.concatenate([r, q], axis=1), bd)
            rq, qq = both[:, :c, :], both[:, c:, :]
        else:
            rq, qq = _bnn(r, bd), None
        r = r + q + rq
        q = qq
    return jnp.stack([r[b][:, j * c:(j + 1) * c] for b in range(nb) for j in range(pack)])


def _ab_prompt_kernel(x_ref, win_ref, wout_ref, convw_ref, gp_ref, ln_ref,
                      h_ref, hist_ref, s_out, c_out, n_out, m_out,
                      proj, qkv, gates, gcum, merged, s_s, c_s, n_s, m_s,
                      u_s, w_s, attn_s, dlog_s, qk_s, dmax_s, kt_s, kbt_s, rows_s, *, tb, n_t):
    t = pl.program_id(1)
    nc = tb // CHUNK

    @pl.when(t == 0)
    def _():
        proj[0:SUBLANES, :] = jnp.zeros((SUBLANES, D_IN_PAD), F32)
        s_s[...] = jnp.zeros_like(s_s)
        c_s[...] = jnp.zeros_like(c_s)
        n_s[...] = jnp.zeros_like(n_s)
        m_s[...] = jnp.zeros_like(m_s)

    x = x_ref[...]
    xb = _bf(x)
    proj[SUBLANES:SUBLANES + tb, 0:A_CONV] = jnp.dot(xb, win_ref[:, 0:A_CONV], preferred_element_type=F32)
    proj[SUBLANES:SUBLANES + tb, A_CONV:] = jnp.dot(xb, win_ref[:, A_CONV:], preferred_element_type=F32)

    for blk in range(A_CONV // LANES):
        cs = slice(blk * LANES, (blk + 1) * LANES)
        acc = proj[SUBLANES:SUBLANES + tb, cs] * convw_ref[CONV_W - 1:CONV_W, cs]
        for j in range(1, CONV_W):
            acc = acc + proj[SUBLANES - j:SUBLANES - j + tb, cs] * convw_ref[CONV_W - 1 - j:CONV_W - j, cs]
        y = _silu(acc)
        if blk < 2 * N_HEADS:
            y = y * lax.rsqrt(jnp.sum(y * y, axis=-1, keepdims=True) + RMS_EPS)
            if blk < N_HEADS:
                y = y * HEAD_DIM ** -0.5
        qkv[:, cs] = y

    gt = _gate_transform(proj[SUBLANES:SUBLANES + tb, GT:GT + LANES], gp_ref[...])
    gates[...] = gt
    ri = lax.broadcasted_iota(jnp.int32, (tb, tb), 0)
    ci = lax.broadcasted_iota(jnp.int32, (tb, tb), 1)
    same_chunk = lax.shift_right_logical(ri, 6) == lax.shift_right_logical(ci, 6)
    ltri = jnp.where(same_chunk, jnp.where(ri >= ci, 1.0, 0.0), 0.0)
    gcum[...] = jnp.dot(ltri, gt, preferred_element_type=F32, precision=HIGHEST)

    ii = lax.broadcasted_iota(jnp.int32, (CHUNK, CHUNK), 0)
    jj = lax.broadcasted_iota(jnp.int32, (CHUNK, CHUNK), 1)
    incl = ii >= jj
    strict = ii > jj
    gdn_w = gp_ref[4:5, :]
    ml_w = gp_ref[5:6, :]

    pairs = [(c, h) for c in range(nc) for h in range(N_HEADS)]

    def tile_heads(ref, row0, col0):
        return jnp.stack([ref[row0 + c * CHUNK:row0 + (c + 1) * CHUNK, col0 + h * HEAD_DIM:col0 + (h + 1) * HEAD_DIM]
                          for c, h in pairs])

    def tile_cols(ref, lane0):
        return jnp.stack([ref[c * CHUNK:(c + 1) * CHUNK, lane0 + h:lane0 + h + 1] for c, h in pairs])

    def tile_rows(transposed, lane0):
        return jnp.stack([transposed[c][lane0 + h:lane0 + h + 1, :] for c, h in pairs])

    cs_t = [gcum[c * CHUNK:(c + 1) * CHUNK, :].T for c in range(nc)]
    gt_t = [gates[c * CHUNK:(c + 1) * CHUNK, :].T for c in range(nc)]
    q3 = tile_heads(qkv, 0, QA)
    k3 = tile_heads(qkv, 0, KA)
    v3 = tile_heads(qkv, 0, VA)
    g_col3 = tile_cols(gcum, G_DEC)
    beta3 = tile_cols(gates, G_BETA)
    decay3 = jnp.where(incl, jnp.exp(jnp.where(incl, g_col3 - tile_rows(cs_t, G_DEC), 0.0)), 0.0)
    kb3 = k3 * beta3
    a_low3 = jnp.where(strict, _bnt(kb3, k3) * decay3, 0.0)
    attn_s[...] = _bnt(q3, k3) * decay3
    kbm3 = tile_heads(proj, SUBLANES, KB) * HEAD_DIM ** -0.5
    qk_s[...] = _bnt(tile_heads(proj, SUBLANES, QB), kbm3)
    kt_s[...] = jnp.stack([k3[p].T for p in range(len(pairs))])
    kbt_s[...] = jnp.stack([kbm3[p].T for p in range(len(pairs))])
    zero_row = jnp.zeros((1, CHUNK), F32)
    rows_s[...] = jnp.stack([jnp.concatenate(
        [cs_t[c][G_DEC + h:G_DEC + h + 1, :], cs_t[c][G_FG + h:G_FG + h + 1, :], gt_t[c][G_IN + h:G_IN + h + 1, :]]
        + [zero_row] * (SUBLANES - 3), axis=0) for c, h in pairs])
    r3 = _unit_lower_inverse_minus_eye(a_low3, N_HEADS)
    rhs3 = jnp.concatenate([v3 * beta3, kb3 * jnp.exp(g_col3)], axis=2)
    uw3 = rhs3 + _bnn(r3, rhs3)
    u_s[...] = uw3[:, :, :HEAD_DIM]
    w_s[...] = uw3[:, :, HEAD_DIM:]
    dlog3 = jnp.where(incl, tile_cols(gcum, G_FG) - tile_rows(cs_t, G_FG) + tile_rows(gt_t, G_IN), NEG_BIG)
    dlog_s[...] = dlog3
    dmax_s[...] = jnp.max(dlog3, axis=-1, keepdims=True)

    def chunk_body(c, carry):
        r0 = pl.multiple_of(c * CHUNK, CHUNK)
        rows = pl.ds(r0, CHUNK)
        prow = pl.ds(pl.multiple_of(r0 + SUBLANES, SUBLANES), CHUNK)
        last = pl.ds(r0 + CHUNK - 1, 1)
        pc = pl.ds(pl.multiple_of(c * N_HEADS, N_HEADS), N_HEADS)

        def heads(ref, rws, col0):
            return jnp.stack([ref[rws, col0 + h * HEAD_DIM:col0 + (h + 1) * HEAD_DIM] for h in range(N_HEADS)])

        def cols(ref, rws, lane0):
            return jnp.stack([ref[rws, lane0 + h:lane0 + h + 1] for h in range(N_HEADS)])

        q = heads(qkv, rows, QA)
        g_col = cols(gcum, rows, G_DEC)
        g_last = cols(gcum, last, G_DEC)
        qb = heads(proj, prow, QB)
        kbm = heads(proj, prow, KB) * HEAD_DIM ** -0.5
        vbm = heads(proj, prow, VB)
        b_col = cols(gcum, rows, G_FG)
        b_last = cols(gcum, last, G_FG)
        row_forms = rows_s[pc]
        g_row = row_forms[:, 0:1, :]
        b_row = row_forms[:, 1:2, :]
        i_row = row_forms[:, 2:3, :]
        s_old = s_s[...]
        c_old = c_s[...]
        n_old = n_s[...]
        m_old = m_s[:, 0:1, 0:1]
        inter = b_col + m_old
        mt = jnp.maximum(inter, dmax_s[pc])
        wts = jnp.exp(dlog_s[pc] - mt) * qk_s[pc]
        sc = jnp.exp(inter - mt)
        m_new = mt[:, CHUNK - 1:CHUNK, :]
        sd = jnp.exp(b_last + m_old - m_new)
        wk_row = jnp.exp(b_last - b_row + i_row - m_new)
        w_state = _bnn(w_s[pc], s_old)
        q_state = _bnn(q * jnp.exp(g_col), s_old)
        q_mem = _bnn(qb, c_old)
        w_val = _bnn(wts, vbm)
        kv = _bnn(kbt_s[pc] * wk_row, vbm)
        k_sum = _bnn(jnp.broadcast_to(wk_row, (N_HEADS, SUBLANES, CHUNK)), kbm)
        v_new = u_s[pc] - w_state
        o_a = q_state + _bnn(attn_s[pc], v_new)
        s_s[...] = s_old * jnp.exp(g_last) + _bnn(kt_s[pc] * jnp.exp(g_last - g_row), v_new)
        num = sc * q_mem + w_val
        den = sc * jnp.sum(qb * n_old[:, 0:1, :], axis=-1, keepdims=True) + jnp.sum(wts, axis=-1, keepdims=True)
        h_b = num / jnp.maximum(jnp.abs(den), jnp.exp(-mt))
        c_s[...] = sd * c_old + kv
        n_s[...] = sd * n_old + k_sum
        m_s[...] = jnp.broadcast_to(m_new, m_s.shape)
        o_n = _rms(o_a, gdn_w)
        h_n = _rms(h_b, ml_w)
        for h in range(N_HEADS):
            hs = slice(h * HEAD_DIM, (h + 1) * HEAD_DIM)
            hs2 = slice((N_HEADS + h) * HEAD_DIM, (N_HEADS + h + 1) * HEAD_DIM)
            merged[rows, hs] = o_n[h] * _silu(proj[prow, ZA + h * HEAD_DIM:ZA + (h + 1) * HEAD_DIM])
            merged[rows, hs2] = h_n[h] * _sigmoid(proj[prow, OB + h * HEAD_DIM:OB + (h + 1) * HEAD_DIM])
        return carry

    lax.fori_loop(0, nc, chunk_body, 0)

    mix = jnp.dot(_bf(merged[...]), wout_ref[...], preferred_element_type=F32)
    h_ref[...] = _layer_norm(DN_ALPHA * x + mix, ln_ref[0:1, :], ln_ref[1:2, :])

    proj[0:SUBLANES, 0:A_CONV] = proj[tb:tb + SUBLANES, 0:A_CONV]

    @pl.when(t == n_t - 1)
    def _():
        hist_ref[...] = proj[tb:tb + SUBLANES, 0:A_CONV]
        s_out[...] = s_s[...]
        c_out[...] = c_s[...]
        n_out[...] = n_s[...]
        m_out[...] = m_s[...]


def _ab_prompt(x, win, wout, convw, gp, ln):
    bsz, length, _ = x.shape
    tb = min(512, length)
    assert length % tb == 0 and tb % CHUNK == 0 and length >= SUBLANES
    n_t = length // tb
    n_pairs = (tb // CHUNK) * N_HEADS
    const = lambda shape: pl.BlockSpec(shape, lambda b, t: (0,) * len(shape))
    per_b =lambda shape: pl.BlockSpec((None,) + shape, lambda b, t: (b,) + (0,) * len(shape))
    return pl.pallas_call(
        functools.partial(_ab_prompt_kernel, tb=tb, n_t=n_t),
        grid=(bsz, n_t),
        in_specs=[
            pl.BlockSpec((None, tb, D_MODEL), lambda b, t: (b, t, 0)),
            const((D_MODEL, D_IN_PAD)), const((2 * N_HEADS * HEAD_DIM, D_MODEL)),
            const((CONV_W, A_CONV)), const((SUBLANES, LANES)), const((2, D_MODEL)),
        ],
        out_specs=[
            pl.BlockSpec((None, tb, D_MODEL), lambda b, t: (b, t, 0)),
            per_b((SUBLANES, A_CONV)), per_b((N_HEADS, HEAD_DIM, HEAD_DIM)), per_b((N_HEADS, HEAD_DIM, HEAD_DIM)),
            per_b((N_HEADS, SUBLANES, LANES)), per_b((N_HEADS, SUBLANES, LANES)),
        ],
        out_shape=[
            jax.ShapeDtypeStruct((bsz, length, D_MODEL), F32),
            jax.ShapeDtypeStruct((bsz, SUBLANES, A_CONV), F32),
            jax.ShapeDtypeStruct((bsz, N_HEADS, HEAD_DIM, HEAD_DIM), F32),
            jax.ShapeDtypeStruct((bsz, N_HEADS, HEAD_DIM, HEAD_DIM), F32),
            jax.ShapeDtypeStruct((bsz, N_HEADS, SUBLANES, LANES), F32),
            jax.ShapeDtypeStruct((bsz, N_HEADS, SUBLANES, LANES), F32),
        ],
        scratch_shapes=[
            pltpu.VMEM((tb + SUBLANES, D_IN_PAD), F32),
            pltpu.VMEM((tb, A_CONV), F32),
            pltpu.VMEM((tb, LANES), F32),
            pltpu.VMEM((tb, LANES), F32),
            pltpu.VMEM((tb, 2 * N_HEADS * HEAD_DIM), F32),
            pltpu.VMEM((N_HEADS, HEAD_DIM, HEAD_DIM), F32),
            pltpu.VMEM((N_HEADS, HEAD_DIM, HEAD_DIM), F32),
            pltpu.VMEM((N_HEADS, SUBLANES, LANES), F32),
            pltpu.VMEM((N_HEADS, SUBLANES, LANES), F32),
            pltpu.VMEM((n_pairs, CHUNK, HEAD_DIM), F32),
            pltpu.VMEM((n_pairs, CHUNK, HEAD_DIM), F32),
            pltpu.VMEM((n_pairs, CHUNK, CHUNK), F32),
            pltpu.VMEM((n_pairs, CHUNK, CHUNK), F32),
            pltpu.VMEM((n_pairs, CHUNK, CHUNK), F32),
            pltpu.VMEM((n_pairs, CHUNK, 1), F32),
            pltpu.VMEM((n_pairs, HEAD_DIM, CHUNK), F32),
            pltpu.VMEM((n_pairs, HEAD_DIM, CHUNK), F32),
            pltpu.VMEM((n_pairs, SUBLANES, CHUNK), F32),
        ],
        compiler_params=pltpu.CompilerParams(
            dimension_semantics=("arbitrary", "arbitrary"), vmem_limit_bytes=VMEM_LIMIT),
        name="ab_prompt",
    )(x, win, wout, convw, gp, ln)


def _ab_decode_kernel(x_ref, win_ref, wout_ref, convw_ref, gp_ref, ln_ref, cbuf_ref, s_in, c_in, n_in, m_in,
                      y_ref, cbuf_out, s_out, c_out, n_out, m_out, proj, merged, *, bb, n_steps):
    i = pl.program_id(0)

    @pl.when(i == 0)
    def _():
        proj[...] = jnp.dot(_bf(x_ref[...]), win_ref[...], preferred_element_type=F32)

    rows = pl.ds(pl.multiple_of(i * bb, bb), bb)
    raw = proj[rows, 0:A_CONV]
    cbuf = cbuf_ref[...]
    conv = raw * convw_ref[CONV_W - 1:CONV_W, :]
    for j in range(CONV_W - 1):
        conv = conv + cbuf[:, j * A_CONV:(j + 1) * A_CONV] * convw_ref[j:j + 1, :]
    cbuf_out[:, 0:(CONV_W - 2) * A_CONV] = cbuf[:, A_CONV:(CONV_W - 1) * A_CONV]
    cbuf_out[:, (CONV_W - 2) * A_CONV:(CONV_W - 1) * A_CONV] = raw
    act = _silu(conv)
    gt = _gate_transform(proj[rows, GT:GT + LANES], gp_ref[...])
    gdn_w = gp_ref[4:5, :]
    ml_w = gp_ref[5:6, :]
    m_all = m_in[...]

    for h in range(N_HEADS):
        def head(off):
            return act[:, off + h * HEAD_DIM:off + (h + 1) * HEAD_DIM]
        q = head(QA)
        q = q * lax.rsqrt(jnp.sum(q * q, axis=-1, keepdims=True) + RMS_EPS) * HEAD_DIM ** -0.5
        k = head(KA)
        k = k * lax.rsqrt(jnp.sum(k * k, axis=-1, keepdims=True) + RMS_EPS)
        v = head(VA)
        q_t = q.T
        k_t = k.T
        qk = jnp.sum(q * k, axis=-1, keepdims=True)
        qb = proj[rows, QB + h * HEAD_DIM:QB + (h + 1) * HEAD_DIM]
        kbm = proj[rows, KB + h * HEAD_DIM:KB + (h + 1) * HEAD_DIM] * HEAD_DIM ** -0.5
        vbm = proj[rows, VB + h * HEAD_DIM:VB + (h + 1) * HEAD_DIM]
        qb_t = qb.T
        kb_t = kbm.T
        qkb = jnp.sum(qb * kbm, axis=-1, keepdims=True)
        n_old = n_in[:, h * HEAD_DIM:(h + 1) * HEAD_DIM]
        qn = jnp.sum(qb * n_old, axis=-1, keepdims=True)
        o_rows = []
        hb_rows = []
        n_rows = []
        m_rows = []
        for b in range(bb):
            s_old = s_in[b, h]
            k_c = k_t[:, b:b + 1]
            q_c = q_t[:, b:b + 1]
            e_g = jnp.exp(gt[b:b + 1, G_DEC + h:G_DEC + h + 1])
            beta = gt[b:b + 1, G_BETA + h:G_BETA + h + 1]
            k_s = jnp.sum(k_c * s_old, axis=0, keepdims=True)
            q_s = jnp.sum(q_c * s_old, axis=0, keepdims=True)
            v_new = v[b:b + 1, :] * beta - (beta * e_g) * k_s
            o_rows.append(e_g * q_s + qk[b:b + 1, :] * v_new)
            s_out[b, h] = s_old * e_g + k_c * v_new
            c_old = c_in[b, h]
            m_old = m_all[b:b + 1, h:h + 1]
            i_pre = gt[b:b + 1, G_IN + h:G_IN + h + 1]
            logf = gt[b:b + 1, G_FG + h:G_FG + h + 1]
            inter = logf + m_old
            mt = jnp.maximum(inter, i_pre)
            w_in = jnp.exp(i_pre - mt)
            sc = jnp.exp(inter - mt)
            wts = w_in * qkb[b:b + 1, :]
            q_cm = jnp.sum(qb_t[:, b:b + 1] * c_old, axis=0, keepdims=True)
            num = sc * q_cm + wts * vbm[b:b + 1, :]
            den = sc * qn[b:b + 1, :] + wts
            hb_rows.append(num / jnp.maximum(jnp.abs(den), jnp.exp(-mt)))
            c_out[b, h] = sc * c_old + (w_in * kb_t[:, b:b + 1]) * vbm[b:b + 1, :]
            n_rows.append(sc * n_old[b:b + 1, :] + w_in * kbm[b:b + 1, :])
            m_rows.append(mt)
        o_a = jnp.concatenate(o_rows, axis=0)
        h_b = jnp.concatenate(hb_rows, axis=0)
        n_out[:, h * HEAD_DIM:(h + 1) * HEAD_DIM] = jnp.concatenate(n_rows, axis=0)
        m_out[:, h:h + 1] = jnp.concatenate(m_rows, axis=0)
        z = proj[rows, ZA + h * HEAD_DIM:ZA + (h + 1) * HEAD_DIM]
        merged[rows, h * HEAD_DIM:(h + 1) * HEAD_DIM] = _rms(o_a, gdn_w) * _silu(z)
        o_gate = proj[rows, OB + h * HEAD_DIM:OB + (h + 1) * HEAD_DIM]
        merged[rows, N_HEADS * HEAD_DIM + h * HEAD_DIM:N_HEADS * HEAD_DIM + (h + 1) * HEAD_DIM] = (
            _rms(h_b, ml_w) * _sigmoid(o_gate))

    @pl.when(i == n_steps - 1)
    def _():
        mix = jnp.dot(_bf(merged[...]), wout_ref[...], preferred_element_type=F32)
        y_ref[...] = _layer_norm(DN_ALPHA * x_ref[...] + mix, ln_ref[0:1, :], ln_ref[1:2, :])


def _ab_decode(x, win, wout, convw, gp, ln, cbuf, s0, c0, n0, m0):
    nb = x.shape[0]
    bb = SUBLANES
    assert nb % bb == 0
    n_steps = nb // bb
    const = lambda shape: pl.BlockSpec(shape, lambda i: (0,) * len(shape))
    blk = lambda shape: pl.BlockSpec((bb,) + shape, lambda i: (i,) + (0,) * len(shape))
    hist = (CONV_W - 1) * A_CONV
    width = N_HEADS * HEAD_DIM
    return pl.pallas_call(
        functools.partial(_ab_decode_kernel, bb=bb, n_steps=n_steps),
        grid=(n_steps,),
        in_specs=[
            const((nb, D_MODEL)), const((D_MODEL, D_IN_PAD)), const((2 * width, D_MODEL)),
            const((CONV_W, A_CONV)), const((SUBLANES, LANES)), const((2, D_MODEL)),
            blk((hist,)), blk((N_HEADS, HEAD_DIM, HEAD_DIM)), blk((N_HEADS, HEAD_DIM, HEAD_DIM)),
            blk((width,)), blk((N_HEADS,)),
        ],
        out_specs=[
            const((nb, D_MODEL)), blk((hist,)), blk((N_HEADS, HEAD_DIM, HEAD_DIM)),
            blk((N_HEADS, HEAD_DIM, HEAD_DIM)), blk((width,)), blk((N_HEADS,)),
        ],
        out_shape=[
            jax.ShapeDtypeStruct((nb, D_MODEL), F32),
            jax.ShapeDtypeStruct((nb, hist), F32),
            jax.ShapeDtypeStruct((nb, N_HEADS, HEAD_DIM, HEAD_DIM), F32),
            jax.ShapeDtypeStruct((nb, N_HEADS, HEAD_DIM, HEAD_DIM), F32),
            jax.ShapeDtypeStruct((nb, width), F32),
            jax.ShapeDtypeStruct((nb, N_HEADS), F32),
        ],
        scratch_shapes=[pltpu.VMEM((nb, D_IN_PAD), F32), pltpu.VMEM((nb, 2 * width), F32)],
        compiler_params=pltpu.CompilerParams(dimension_semantics=("arbitrary",), vmem_limit_bytes=VMEM_LIMIT),
        name="ab_decode",
    )(x, win, wout, convw, gp, ln, cbuf, s0, c0, n0, m0)


def _second_largest_sum(a, b, c, d):
    hi1, lo1 = jnp.maximum(a, b), jnp.minimum(a, b)
    hi2, lo2 = jnp.maximum(c, d), jnp.minimum(c, d)
    return jnp.maximum(hi1, hi2) + jnp.maximum(jnp.minimum(hi1, hi2), jnp.maximum(lo1, lo2))


def _first_argmax(vals):
    best_v = vals[0]
    best_i = jnp.zeros(vals[0].shape, jnp.int32)
    for j in range(1, len(vals)):
        better = vals[j] > best_v
        best_v = jnp.where(better, vals[j], best_v)
        best_i = jnp.where(better, j, best_i)
    return best_i


def _router_kernel(x_ref, rw_ref, rb_ref, meta_ref, cnt_ref, carry, *, n_steps):
    i = pl.program_id(0)
    tm = x_ref.shape[0]

    @pl.when(i == 0)
    def _():
        carry[...] = jnp.zeros_like(carry)

    x = x_ref[...]
    rw = rw_ref[...]
    x_hi = _bf(x)
    x_lo = _bf(x - x_hi.astype(F32))
    w_hi = _bf(rw)
    w_lo = _bf(rw - w_hi.astype(F32))
    nt = lambda a, b: lax.dot_general(a, b, (((1,), (1,)), ((), ())), preferred_element_type=F32)
    logits = nt(w_hi, x_hi) + (nt(w_hi, x_lo) + nt(w_lo, x_hi))
    ex = jnp.exp(logits - jnp.max(logits, axis=0, keepdims=True))
    probs = ex / jnp.sum(ex, axis=0, keepdims=True)
    sel = probs + rb_ref[...]
    p = [probs[j:j + 1, :] for j in range(N_EXPERTS)]
    s = [sel[j:j + 1, :] for j in range(N_EXPERTS)]
    scores = [_second_largest_sum(*s[EXPERTS_PER_GROUP * g:EXPERTS_PER_GROUP * (g + 1)])
              for g in range(N_EXPERT_GROUPS)]
    best = _first_argmax(scores)
    masked = [jnp.where(best == j // EXPERTS_PER_GROUP, s[j], -jnp.inf) for j in range(N_EXPERTS)]
    i1 = _first_argmax(masked)
    i2 = _first_argmax([jnp.where(i1 == j, -jnp.inf, masked[j]) for j in range(N_EXPERTS)])
    zero = jnp.zeros_like(p[0])
    p1 = functools.reduce(lambda a, b: a + b, [jnp.where(i1 == j, p[j], zero) for j in range(N_EXPERTS)])
    p2 = functools.reduce(lambda a, b: a + b, [jnp.where(i2 == j, p[j], zero) for j in range(N_EXPERTS)])
    tot = p1 + p2
    rows = [jnp.where(i1 == j, p1 / tot, zero) + jnp.where(i2 == j, p2 / tot, zero) for j in range(N_EXPERTS)]
    in_group = [best == g for g in range(N_EXPERT_GROUPS)]
    local = [functools.reduce(lambda a, b: a + b,
                              [jnp.where(in_group[g], rows[EXPERTS_PER_GROUP * g + e], zero)
                               for g in range(N_EXPERT_GROUPS)])
             for e in range(EXPERTS_PER_GROUP)]
    onehot = jnp.concatenate([jnp.where(m, 1.0, 0.0) for m in in_group]
                             + [jnp.zeros((SUBLANES - N_EXPERT_GROUPS, tm), F32)], axis=0)
    ri = lax.broadcasted_iota(jnp.int32, (tm, tm), 0)
    ci = lax.broadcasted_iota(jnp.int32, (tm, tm), 1)
    incl = jnp.dot(_bf(onehot), _bf(jnp.where(ri <= ci, 1.0, 0.0)), preferred_element_type=F32)
    prev = carry[...]
    rank = jnp.sum(onehot * (incl - 1.0 + prev[:, 0:1]), axis=0, keepdims=True)
    carry[...] = prev + incl[:, tm - 1:tm]
    row_id = (i * tm + lax.broadcasted_iota(jnp.int32, (1, tm), 1)).astype(F32)
    meta_ref[...] = jnp.concatenate([row_id] + local + [best.astype(F32), rank, zero], axis=0).T

    @pl.when(i == n_steps - 1)
    def _():
        cnt_ref[...] = carry[...]


def _router(x, rw_t, rb_col):
    t = x.shape[0]
    tm = min(256, t)
    assert t % tm == 0
    n_steps = t // tm
    return pl.pallas_call(
        functools.partial(_router_kernel, n_steps=n_steps),
        grid=(n_steps,),
        in_specs=[pl.BlockSpec((tm, D_MODEL), lambda i: (i, 0)),
                  pl.BlockSpec((N_EXPERTS, D_MODEL), lambda i: (0, 0)),
                  pl.BlockSpec((N_EXPERTS, 1), lambda i: (0, 0))],
        out_specs=[pl.BlockSpec((tm, SUBLANES), lambda i: (i, 0)),
                   pl.BlockSpec((SUBLANES, LANES), lambda i: (0, 0))],
        out_shape=[jax.ShapeDtypeStruct((t, SUBLANES), F32), jax.ShapeDtypeStruct((SUBLANES, LANES), F32)],
        scratch_shapes=[pltpu.VMEM((SUBLANES, LANES), F32)],
        compiler_params=pltpu.CompilerParams(dimension_semantics=("arbitrary",), vmem_limit_bytes=VMEM_LIMIT),
        name="router",
    )(x, rw_t, rb_col)


def _moe_kernel(tg_ref, nv_ref, src_cur, dst_cur, src_nxt, cw_ref, x_hbm, wg_ref, wu_ref, wd_ref, ln_ref,
                y_hbm, xg, stage, gsem, ssem, *, tm, n_tiles):
    i = pl.program_id(0)
    slot = lax.rem(i, 2)
    other = 1 - slot
    n_valid = nv_ref[i]
    n_prev = jnp.where(i >= 1, nv_ref[jnp.maximum(i - 1, 0)], 0)
    n_back2 = jnp.where(i >= 2, nv_ref[jnp.maximum(i - 2, 0)], 0)
    has_next = i + 1 < n_tiles

    def gather_row(tok_ref, r, dst_slot):
        return pltpu.make_async_copy(x_hbm.at[pl.ds(tok_ref[0, r], 1), :], xg.at[dst_slot, pl.ds(r, 1), :],
                                     gsem.at[dst_slot])

    def scatter_row(tok_ref, r, src_slot):
        return pltpu.make_async_copy(stage.at[src_slot, pl.ds(r, 1), :], y_hbm.at[pl.ds(tok_ref[0, r], 1), :],
                                     ssem.at[src_slot])

    def scatter_wait(count, src_slot):
        for p in [1 << b for b in range(tm.bit_length())]:
            @pl.when(lax.bitwise_and(count, p) != 0)
            def _():
                pltpu.make_async_copy(stage.at[src_slot, pl.ds(0, p), :], y_hbm.at[pl.ds(0, p), :],
                                      ssem.at[src_slot]).wait()

    @pl.when(i == 0)
    def _():
        def body(r, c):
            gather_row(src_cur, r, 0).start()
            return c
        lax.fori_loop(0, tm, body, 0, unroll=8)

    pltpu.make_async_copy(x_hbm.at[pl.ds(0, tm), :], xg.at[slot], gsem.at[slot]).wait()

    for parity in range(2):
        @pl.when(jnp.logical_and(has_next, slot == parity))
        def _():
            for r in range(tm):
                gather_row(src_nxt, r, 1 - parity).start(priority=r % 2)

    @pl.when(n_valid > 0)
    def _():
        x = xg[slot]
        xb = _bf(x)
        cw = cw_ref[...]
        acc = jnp.zeros((tm, D_MODEL), F32)
        for e in range(EXPERTS_PER_GROUP):
            g = jnp.dot(xb, wg_ref[e], preferred_element_type=F32)
            u = jnp.dot(xb, wu_ref[e], preferred_element_type=F32)
            hid = _silu(g) * u * cw[:, 1 + e:2 + e]
            acc = acc + jnp.dot(_bf(hid), wd_ref[e], preferred_element_type=F32)
        y = _layer_norm(DN_ALPHA * x + acc, ln_ref[0:1, :], ln_ref[1:2, :])

        @pl.when(i >= 2)
        def _():
            scatter_wait(n_back2, slot)
        stage[slot] = y

    @pl.when(jnp.logical_and(n_valid <= 0, i >= 2))
    def _():
        scatter_wait(n_back2, slot)

    for parity in range(2):
        @pl.when(jnp.logical_and(n_valid == tm, slot == parity))
        def _():
            for r in range(tm):
                scatter_row(dst_cur, r, parity).start(priority=r % 2)

    @pl.when(n_valid < tm)
    def _():
        for r in range(tm):
            @pl.when(r < n_valid)
            def _():
                scatter_row(dst_cur, r, slot).start(priority=r % 2)

    @pl.when(i == n_tiles - 1)
    def _():
        scatter_wait(n_prev, other)
        scatter_wait(n_valid, slot)


def _moe(x, meta, counts, wg, wu, wd, ln, layer, out_row):
    t = x.shape[0]
    tm = min(512, t)
    assert t % tm == 0 and tm & (tm - 1) == 0
    n_tiles = t // tm + N_EXPERT_GROUPS
    n_slots = n_tiles * tm
    grp = meta[:, 1 + EXPERTS_PER_GROUP].astype(jnp.int32)
    rank = meta[:, 2 + EXPERTS_PER_GROUP].astype(jnp.int32)
    cnt = counts[:N_EXPERT_GROUPS, 0].astype(jnp.int32)
    padded = ((cnt + tm - 1) // tm) * tm
    g_end = jnp.cumsum(padded)
    g_off = g_end - padded
    pos = jnp.take(g_off, grp) + rank
    slot_meta = jnp.zeros((n_slots, SUBLANES), F32).at[pos].set(meta)
    tok_of_slot = slot_meta[:, 0].astype(jnp.int32)
    tile_start = jnp.arange(n_tiles, dtype=jnp.int32) * tm
    tile_grp = jnp.minimum(jnp.sum(tile_start[:, None] >= g_end[None, :], axis=1), N_EXPERT_GROUPS - 1).astype(jnp.int32)
    tile_valid = jnp.clip(jnp.take(g_off + cnt, tile_grp) - tile_start, 0, tm).astype(jnp.int32)
    src3 = tok_of_slot.reshape(n_tiles, 1, tm)
    dst3 = out_row(tok_of_slot).reshape(n_tiles, 1, tm)

    grid_spec = pltpu.PrefetchScalarGridSpec(
        num_scalar_prefetch=2,
        grid=(n_tiles,),
        in_specs=[
            pl.BlockSpec((None, 1, tm), lambda i, tg, nv: (i, 0, 0), memory_space=pltpu.SMEM),
            pl.BlockSpec((None, 1, tm), lambda i, tg, nv: (i, 0, 0), memory_space=pltpu.SMEM),
            pl.BlockSpec((None, 1, tm), lambda i, tg, nv: (jnp.minimum(i + 1, n_tiles - 1), 0, 0),
                         memory_space=pltpu.SMEM),
            pl.BlockSpec((tm, SUBLANES), lambda i, tg, nv: (i, 0)),
            pl.BlockSpec(memory_space=pl.ANY),
            pl.BlockSpec((None, EXPERTS_PER_GROUP, D_MODEL, D_FF), lambda i, tg, nv: (layer, tg[i], 0, 0)),
            pl.BlockSpec((None, EXPERTS_PER_GROUP, D_MODEL, D_FF), lambda i, tg, nv: (layer, tg[i], 0, 0)),
            pl.BlockSpec((None, EXPERTS_PER_GROUP, D_FF, D_MODEL), lambda i, tg, nv: (layer, tg[i], 0, 0)),
            pl.BlockSpec((2, D_MODEL), lambda i, tg, nv: (0, 0)),
        ],
        out_specs=pl.BlockSpec(memory_space=pl.ANY),
        scratch_shapes=[pltpu.VMEM((2, tm, D_MODEL), F32), pltpu.VMEM((2, tm, D_MODEL), F32),
                        pltpu.SemaphoreType.DMA((2,)), pltpu.SemaphoreType.DMA((2,))],
    )
    return pl.pallas_call(
        functools.partial(_moe_kernel, tm=tm, n_tiles=n_tiles),
        grid_spec=grid_spec,
        out_shape=jax.ShapeDtypeStruct((t, D_MODEL), F32),
        compiler_params=pltpu.CompilerParams(dimension_semantics=("arbitrary",), vmem_limit_bytes=VMEM_LIMIT),
        name="moe",
    )(tile_grp, tile_valid, src3, dst3, src3, slot_meta, x, wg, wu, wd, ln)


def _s5_prep_kernel(are_ref, aim_ref, ldt_ref, bre_ref, bim_ref, abre_ref, abim_ref, bbre_ref, bbim_ref):
    a_r = are_ref[...]
    a_i = aim_ref[...]
    dt = jnp.exp(ldt_ref[...])
    mag = jnp.exp(dt * a_r)
    ab_re = mag * jnp.cos(dt * a_i)
    ab_im = mag * jnp.sin(dt * a_i)
    den = a_r * a_r + a_i * a_i
    nr = ab_re - 1.0
    z_re = (nr * a_r + ab_im * a_i) / den
    z_im = (ab_im * a_r - nr * a_i) / den
    abre_ref[...] = ab_re
    abim_ref[...] = ab_im
    bbre_ref[...] = z_re * bre_ref[...] - z_im * bim_ref[...]
    bbim_ref[...] = z_re * bim_ref[...] + z_im * bre_ref[...]


def _s5_prep(a_re, a_im, log_dt, b_re, b_im):
    n = S5_GROUPS * S5_STATE
    col = jax.ShapeDtypeStruct((n, 1), F32)
    mat = jax.ShapeDtypeStruct((n, S5_GROUP), F32)
    ldt = jnp.broadcast_to(log_dt[:, None], (S5_GROUPS, S5_STATE)).reshape(n, 1)
    return pl.pallas_call(_s5_prep_kernel, out_shape=[col, col, mat, mat], name="s5_prep")(
        a_re.reshape(n, 1), a_im.reshape(n, 1), ldt, b_re.reshape(n, S5_GROUP), b_im.reshape(n, S5_GROUP))


def _gelu_tanh(x):
    return 0.5 * x * (1.0 + jnp.tanh(0.7978845608028654 * (x + 0.044715 * (x * x * x))))


def _s5_kernel(x_ref, h0re_ref, h0im_ref, wre_ref, wim_ref, cre_ref, cim_ref, abre_ref, abim_ref, d_ref,
               wa_ref, wb_ref, ln_ref, o_ref, hre_out, him_out, bu_re, bu_im, hre_s, him_s, gy_ref, glu,
               *, nb, tt, n_t):
    t = pl.program_id(0)
    sw = wre_ref.shape[2]

    @pl.when(t == 0)
    def _():
        hre_s[...] = h0re_ref[...]
        him_s[...] = h0im_ref[...]

    for k in range(D_MODEL // LANES):
        ls = slice(k * LANES, (k + 1) * LANES)
        ss = slice(k * sw, (k + 1) * sw)
        buf = k % 2
        xv = x_ref[:, ls]
        xb = _bf(xv)
        bu_re[buf] = jnp.dot(xb, wre_ref[k], preferred_element_type=F32)
        bu_im[buf] = jnp.dot(xb, wim_ref[k], preferred_element_type=F32)
        a_re = jnp.broadcast_to(abre_ref[:, ss], (nb, sw))
        a_im = jnp.broadcast_to(abim_ref[:, ss], (nb, sw))

        def step(s, carry, buf=buf, a_re=a_re, a_im=a_im):
            h_re, h_im = carry
            rows = pl.ds(pl.multiple_of(s * nb, nb), nb)
            n_re = a_re * h_re - a_im * h_im + bu_re[buf, rows, :]
            n_im = a_re * h_im + a_im * h_re + bu_im[buf, rows, :]
            bu_re[buf, rows, :] = n_re
            bu_im[buf, rows, :] = n_im
            return n_re, n_im

        h_re, h_im = lax.fori_loop(0, tt, step, (hre_s[:, ss], him_s[:, ss]), unroll=min(tt, S5_UNROLL))
        hre_s[:, ss] = h_re
        him_s[:, ss] = h_im
        y = (jnp.dot(_bf(bu_re[buf]), cre_ref[k], preferred_element_type=F32)
             - jnp.dot(_bf(bu_im[buf]), cim_ref[k], preferred_element_type=F32)
             + d_ref[:, ls] * xv)
        gy_ref[:, ls] = _gelu_tanh(y).astype(gy_ref.dtype)

    gy = gy_ref[...]
    width = 2 * LANES
    for n in range(D_MODEL // width):
        ns = slice(n * width, (n + 1) * width)
        a = jnp.dot(gy, wa_ref[:, ns], preferred_element_type=F32)
        b = jnp.dot(gy, wb_ref[:, ns], preferred_element_type=F32)
        glu[:, ns] = a * _sigmoid(b)
    o_ref[...] = _layer_norm(DN_ALPHA * x_ref[...] + glu[...], ln_ref[0:1, :], ln_ref[1:2, :])

    @pl.when(t == n_t - 1)
    def _():
        hre_out[...] = hre_s[...]
        him_out[...] = him_s[...]


def _s5_scan(x, nb, h0_re, h0_im, wre, wim, cre, cim, ab_re, ab_im, d_skip, wa, wb, ln):
    length = x.shape[0] // nb
    tt = min(S5_TT, length)
    assert length % tt == 0 and nb % SUBLANES == 0
    n_t = length // tt
    n_k = D_MODEL // LANES
    sw = (LANES // S5_GROUP) * S5_STATE
    n_state = S5_GROUPS * S5_STATE
    const = lambda shape: pl.BlockSpec(shape, lambda t: (0,) * len(shape))
    return pl.pallas_call(
        functools.partial(_s5_kernel, nb=nb, tt=tt, n_t=n_t),
        grid=(n_t,),
        in_specs=[
            pl.BlockSpec((tt * nb, D_MODEL), lambda t: (t, 0)),
            const((nb, n_state)), const((nb, n_state)),
            const((n_k, LANES, sw)), const((n_k, LANES, sw)), const((n_k, sw, LANES)), const((n_k, sw, LANES)),
            const((1, n_state)), const((1, n_state)), const((1, D_MODEL)),
            const((D_MODEL, D_MODEL)), const((D_MODEL, D_MODEL)), const((2, D_MODEL)),
        ],
        out_specs=[pl.BlockSpec((tt * nb, D_MODEL), lambda t: (t, 0)), const((nb, n_state)), const((nb, n_state))],
        out_shape=[
            jax.ShapeDtypeStruct((length * nb, D_MODEL), F32),
            jax.ShapeDtypeStruct((nb, n_state), F32),
            jax.ShapeDtypeStruct((nb, n_state), F32),
        ],
        scratch_shapes=[
            pltpu.VMEM((2, nb * tt, sw), F32), pltpu.VMEM((2, nb * tt, sw), F32),
            pltpu.VMEM((nb, n_state), F32), pltpu.VMEM((nb, n_state), F32),
            pltpu.VMEM((nb * tt, D_MODEL), BF16), pltpu.VMEM((nb * tt, D_MODEL), F32),
        ],
        compiler_params=pltpu.CompilerParams(dimension_semantics=("arbitrary",), vmem_limit_bytes=VMEM_LIMIT),
        name="s5_glu",
    )(x, h0_re, h0_im, wre, wim, cre, cim, ab_re, ab_im, d_skip, wa, wb, ln)


def _block_diag_slices(m, rows_per_group, cols_per_group):
    gps = LANES // S5_GROUP
    m = m.reshape(S5_GROUPS // gps, gps, rows_per_group, cols_per_group)
    eye = jnp.eye(gps, dtype=m.dtype)
    out = m[:, :, :, None, :] * eye[None, :, None, :, None]
    return out.reshape(S5_GROUPS // gps, gps * rows_per_group, gps * cols_per_group)


def _prepare(p):
    w = p['w_in'][0]
    win = jnp.concatenate(
        [w[:, 0:1536], w[:, 1544:2056], w[:, 2056:3592], w[:, 3600:4112], w[:, 1536:1544], w[:, 3592:3600],
         jnp.zeros((D_MODEL, D_IN_PAD - 4112), w.dtype)], axis=1).astype(BF16)
    at_lane = lambda v, lane0: jnp.pad(v, (lane0, LANES - lane0 - v.shape[0]))
    gp = jnp.stack([at_lane(p['gdn_A_log'][0], G_DEC), at_lane(p['gdn_dt_bias'][0], G_DEC),
                    at_lane(p['ml_b_i'][0], G_IN), at_lane(p['ml_b_f'][0], G_FG),
                    p['gdn_norm_w'][0], p['ml_norm_w'][0], jnp.zeros((LANES,), F32), jnp.zeros((LANES,), F32)])
    ab_re, ab_im, bb_re, bb_im = _s5_prep(p['s5_A_re'][0], p['s5_A_im'][0], p['s5_log_dt'][0],
                                          p['s5_B_re'][0], p['s5_B_im'][0])
    to_in = lambda bb: _block_diag_slices(
        bb.reshape(S5_GROUPS, S5_STATE, S5_GROUP).transpose(0, 2, 1), S5_GROUP, S5_STATE).astype(BF16)
    to_out = lambda c: _block_diag_slices(c.transpose(0, 2, 1), S5_STATE, S5_GROUP).astype(BF16)
    return dict(
        win=win, wout=p['w_out'][0].astype(BF16), convw=p['gdn_conv_w'][0], gp=gp,
        ln_mix=[jnp.stack([p['ln_mix_g'][l], p['ln_mix_b'][l]]) for l in range(DEPTH)],
        ln_ffn=[jnp.stack([p['ln_ffn_g'][l], p['ln_ffn_b'][l]]) for l in range(DEPTH)],
        rw_t=p['router_w'].T, rb_col=p['router_b'][:, None],
        wg=p['moe_w_gate'].astype(BF16), wu=p['moe_w_up'].astype(BF16), wd=p['moe_w_down'].astype(BF16),
        s5_wre=to_in(bb_re), s5_wim=to_in(bb_im),
        s5_cre=to_out(p['s5_C_re'][0]), s5_cim=to_out(p['s5_C_im'][0]),
        s5_abre=ab_re.reshape(1, -1), s5_abim=ab_im.reshape(1, -1), s5_d=p['s5_D'][0][None, :],
        glu_a=p['s5_w_glu_a'][0].astype(BF16), glu_b=p['s5_w_glu_b'][0].astype(BF16),
    )


def _ffn(h, w, layer, out_row):
    meta, counts = _router(h, w['rw_t'], w['rb_col'])
    return _moe(h, meta, counts, w['wg'], w['wu'], w['wd'], w['ln_ffn'][layer], layer, out_row)


def _s5_layer(h, nb, h0_re, h0_im, w):
    return _s5_scan(h, nb, h0_re, h0_im, w['s5_wre'], w['s5_wim'], w['s5_cre'], w['s5_cim'],
                    w['s5_abre'], w['s5_abim'], w['s5_d'], w['glu_a'], w['glu_b'], w['ln_mix'][1])


def kernel(x_prompt, x_sample, state_gdn_conv, state_gdn_S, state_mlstm_C, state_mlstm_n, state_mlstm_m,
           state_s5_re, state_s5_im, w_in, gdn_conv_w, gdn_A_log, gdn_dt_bias, gdn_norm_w, ml_b_i, ml_b_f,
           ml_norm_w, w_out, s5_A_re, s5_A_im, s5_log_dt, s5_B_re, s5_B_im, s5_C_re, s5_C_im, s5_D,
           s5_w_glu_a, s5_w_glu_b, router_w, router_b, moe_w_gate, moe_w_up, moe_w_down,
           ln_mix_g, ln_mix_b, ln_ffn_g, ln_ffn_b):
    w = _prepare(dict(
        w_in=w_in, gdn_conv_w=gdn_conv_w, gdn_A_log=gdn_A_log, gdn_dt_bias=gdn_dt_bias, gdn_norm_w=gdn_norm_w,
        ml_b_i=ml_b_i, ml_b_f=ml_b_f, ml_norm_w=ml_norm_w, w_out=w_out, s5_A_re=s5_A_re, s5_A_im=s5_A_im,
        s5_log_dt=s5_log_dt, s5_B_re=s5_B_re, s5_B_im=s5_B_im, s5_C_re=s5_C_re, s5_C_im=s5_C_im, s5_D=s5_D,
        s5_w_glu_a=s5_w_glu_a, s5_w_glu_b=s5_w_glu_b, router_w=router_w, router_b=router_b,
        moe_w_gate=moe_w_gate, moe_w_up=moe_w_up, moe_w_down=moe_w_down,
        ln_mix_g=ln_mix_g, ln_mix_b=ln_mix_b, ln_ffn_g=ln_ffn_g, ln_ffn_b=ln_ffn_b))
    bp, lp, _ = x_prompt.shape
    bs, ls, _ = x_sample.shape
    assert ls == 1
    n_state = S5_GROUPS * S5_STATE

    h, p_hist, p_s, p_c, p_n, p_m = _ab_prompt(x_prompt, w['win'], w['wout'], w['convw'], w['gp'], w['ln_mix'][0])
    h = _ffn(h.reshape(bp * lp, D_MODEL), w, 0, lambda r: (r % lp) * bp + r // lp)
    zeros = jnp.zeros((bp, n_state), F32)
    h, p_re, p_im = _s5_layer(h, bp, zeros, zeros, w)
    y_prompt = _ffn(h, w, 1, lambda r: (r % bp) * lp + r // bp).reshape(bp, lp, D_MODEL)

    hs, s_cbuf, s_s, s_c, s_n, s_m = _ab_decode(
        x_sample[:, 0], w['win'], w['wout'], w['convw'], w['gp'], w['ln_mix'][0],
        state_gdn_conv.reshape(bs, (CONV_W - 1) * A_CONV), state_gdn_S[:, 0], state_mlstm_C[:, 0],
        state_mlstm_n.reshape(bs, N_HEADS * HEAD_DIM), state_mlstm_m[:, 0])
    hs = _ffn(hs, w, 0, lambda r: r)
    hs, s_re, s_im = _s5_layer(hs, bs, state_s5_re.reshape(bs, n_state), state_s5_im.reshape(bs, n_state), w)
    y_sample = _ffn(hs, w, 1, lambda r: r).reshape(bs, 1, D_MODEL)

    grp = lambda a, n: a.reshape(n, 1, S5_GROUPS, S5_STATE)
    return (
        y_prompt, y_sample,
        p_hist[:, None, SUBLANES - (CONV_W - 1):, :], p_s[:, None], p_c[:, None],
        p_n[:, None, :, 0, :], p_m[:, None, :, 0, 0], grp(p_re, bp), grp(p_im, bp),
        s_cbuf.reshape(bs, 1, CONV_W - 1, A_CONV), s_s[:, None], s_c[:, None],
        s_n.reshape(bs, 1, N_HEADS, HEAD_DIM), s_m[:, None], grp(s_re, bs), grp(s_im, bs),
    )
```

```python
import functools

import jax
import jax.numpy as jnp
from jax import lax
from jax.experimental import pallas as pl
from jax.experimental.pallas import tpu as pltpu

F32 = jnp.float32
BF16 = jnp.bfloat16
HIGHEST = lax.Precision.HIGHEST

D_MODEL = 1024
DEPTH = 2
N_HEADS = 4
HEAD_DIM = 128
CONV_W = 4
CHUNK = 64
A_CONV = 3 * N_HEADS * HEAD_DIM
S5_GROUP = 16
S5_GROUPS = D_MODEL // S5_GROUP
S5_STATE = 64
N_EXPERTS = 16
EXPERTS_PER_GROUP = 4
N_EXPERT_GROUPS = N_EXPERTS // EXPERTS_PER_GROUP
D_FF = 512
DN_ALPHA = (2 * DEPTH) ** 0.25
LN_EPS = 1e-5
RMS_EPS = 1e-6
NEG_BIG = -1e30

QA, KA, VA, ZA, QB, KB, VB, OB, GT = 0, 512, 1024, 1536, 2048, 2560, 3072, 3584, 4096
D_IN_PAD = 4224
G_DEC, G_BETA, G_IN, G_FG = 0, 4, 8, 12

S5_TT = 64
S5_UNROLL = 64

LANES = 128
SUBLANES = 8
VMEM_LIMIT = 56 * 1024 * 1024


def _bf(x):
    return x.astype(BF16)


def _nn(a, b):
    return jnp.dot(_bf(a), _bf(b), preferred_element_type=F32)


def _nt(a, b):
    return lax.dot_general(_bf(a), _bf(b), (((1,), (1,)), ((), ())), preferred_element_type=F32)


def _tn(a, b):
    return lax.dot_general(_bf(a), _bf(b), (((0,), (0,)), ((), ())), preferred_element_type=F32)


def _sigmoid(x):
    return 1.0 / (1.0 + jnp.exp(-x))


def _softplus(x):
    return jnp.maximum(x, 0.0) + jnp.log(1.0 + jnp.exp(-jnp.abs(x)))


def _silu(x):
    return x * _sigmoid(x)


def _layer_norm(y, g, b):
    mu = jnp.mean(y, axis=-1, keepdims=True)
    yc = y - mu
    var = jnp.mean(yc * yc, axis=-1, keepdims=True)
    return yc * lax.rsqrt(var + LN_EPS) * g + b


def _rms(x, w):
    return x * lax.rsqrt(jnp.mean(x * x, axis=-1, keepdims=True) + RMS_EPS) * w


def _gate_transform(raw, gp):
    lane = lax.broadcasted_iota(jnp.int32, raw.shape, 1)
    dec = -jnp.exp(gp[0:1, :]) * _softplus(raw + gp[1:2, :])
    beta = _sigmoid(raw)
    ipre = raw + gp[2:3, :]
    logf = -_softplus(-(raw + gp[3:4, :]))
    return jnp.where(lane < G_BETA, dec,
                     jnp.where(lane < G_IN, beta,
                               jnp.where(lane < G_FG, ipre,
                                         jnp.where(lane < G_FG + N_HEADS, logf, 0.0))))


def _bnn(a, b):
    return lax.dot_general(_bf(a), _bf(b), (((2,), (1,)), ((0,), (0,))), preferred_element_type=F32)


def _bnt(a, b):
    return lax.dot_general(_bf(a), _bf(b), (((2,), (2,)), ((0,), (0,))), preferred_element_type=F32)


def _btn(a, b):
    return lax.dot_general(_bf(a), _bf(b), (((1,), (1,)), ((0,), (0,))), preferred_element_type=F32)


def _unit_lower_inverse_minus_eye(a, pack):
    p, c, _ = a.shape
    nb = p // pack
    w = pack * c
    r = -jnp.stack([jnp.concatenate([a[b * pack + j] for j in range(pack)], axis=1) for b in range(nb)])
    assert c & (c - 1) == 0
    block_of = lambda dim: lax.shift_right_logical(lax.broadcasted_iota(jnp.int32, (w, w), dim), c.bit_length() - 1)
    on_diag = block_of(0) == block_of(1)

    def block_diag(x):
        return jnp.where(on_diag, jnp.concatenate([x] * pack, axis=1), 0.0)

    steps = max(1, (c - 1).bit_length()) - 1
    q = _bnn(r, block_diag(r))
    for i in range(steps):
        bd = block_diag(q)
        if i + 1 < steps:
            both = _bnn(jnp.concatenate([r, q], axis=1), bd)
            rq, qq = both[:, :c, :], both[:, c:, :]
        else:
            rq, qq = _bnn(r, bd), None
        r = r + q + rq
        q = qq
    return jnp.stack([r[b][:, j * c:(j + 1) * c] for b in range(nb) for j in range(pack)])


def _ab_prompt_kernel(x_ref, win_ref, wout_ref, convw_ref, gp_ref, ln_ref,
                      h_ref, hist_ref, s_out, c_out, n_out, m_out,
                      proj, qkv, gates, gcum, merged, s_s, c_s, n_s, m_s,
                      u_s, w_s, attn_s, dlog_s, qk_s, dmax_s, kt_s, kbt_s, rows_s, *, tb, n_t):
    t = pl.program_id(1)
    nc = tb // CHUNK

    @pl.when(t == 0)
    def _():
        proj[0:SUBLANES, :] = jnp.zeros((SUBLANES, D_IN_PAD), F32)
        s_s[...] = jnp.zeros_like(s_s)
        c_s[...] = jnp.zeros_like(c_s)
        n_s[...] = jnp.zeros_like(n_s)
        m_s[...] = jnp.zeros_like(m_s)

    x = x_ref[...]
    xb = _bf(x)
    proj[SUBLANES:SUBLANES + tb, 0:A_CONV] = jnp.dot(xb, win_ref[:, 0:A_CONV], preferred_element_type=F32)
    proj[SUBLANES:SUBLANES + tb, A_CONV:] = jnp.dot(xb, win_ref[:, A_CONV:], preferred_element_type=F32)

    for blk in range(A_CONV // LANES):
        cs = slice(blk * LANES, (blk + 1) * LANES)
        acc = proj[SUBLANES:SUBLANES + tb, cs] * convw_ref[CONV_W - 1:CONV_W, cs]
        for j in range(1, CONV_W):
            acc = acc + proj[SUBLANES - j:SUBLANES - j + tb, cs] * convw_ref[CONV_W - 1 - j:CONV_W - j, cs]
        y = _silu(acc)
        if blk < 2 * N_HEADS:
            y = y * lax.rsqrt(jnp.sum(y * y, axis=-1, keepdims=True) + RMS_EPS)
            if blk < N_HEADS:
                y = y * HEAD_DIM ** -0.5
        qkv[:, cs] = y

    gt = _gate_transform(proj[SUBLANES:SUBLANES + tb, GT:GT + LANES], gp_ref[...])
    gates[...] = gt
    ri = lax.broadcasted_iota(jnp.int32, (tb, tb), 0)
    ci = lax.broadcasted_iota(jnp.int32, (tb, tb), 1)
    same_chunk = lax.shift_right_logical(ri, 6) == lax.shift_right_logical(ci, 6)
    ltri = jnp.where(same_chunk, jnp.where(ri >= ci, 1.0, 0.0), 0.0)
    gcum[...] = jnp.dot(ltri, gt, preferred_element_type=F32, precision=HIGHEST)

    ii = lax.broadcasted_iota(jnp.int32, (CHUNK, CHUNK), 0)
    jj = lax.broadcasted_iota(jnp.int32, (CHUNK, CHUNK), 1)
    incl = ii >= jj
    strict = ii > jj
    gdn_w = gp_ref[4:5, :]
    ml_w = gp_ref[5:6, :]

    pairs = [(c, h) for c in range(nc) for h in range(N_HEADS)]

    def tile_heads(ref, row0, col0):
        return jnp.stack([ref[row0 + c * CHUNK:row0 + (c + 1) * CHUNK, col0 + h * HEAD_DIM:col0 + (h + 1) * HEAD_DIM]
                          for c, h in pairs])

    def tile_cols(ref, lane0):
        return jnp.stack([ref[c * CHUNK:(c + 1) * CHUNK, lane0 + h:lane0 + h + 1] for c, h in pairs])

    def tile_rows(transposed, lane0):
        return jnp.stack([transposed[c][lane0 + h:lane0 + h + 1, :] for c, h in pairs])

    cs_t = [gcum[c * CHUNK:(c + 1) * CHUNK, :].T for c in range(nc)]
    gt_t = [gates[c * CHUNK:(c + 1) * CHUNK, :].T for c in range(nc)]
    q3 = tile_heads(qkv, 0, QA)
    k3 = tile_heads(qkv, 0, KA)
    v3 = tile_heads(qkv, 0, VA)
    g_col3 = tile_cols(gcum, G_DEC)
    beta3 = tile_cols(gates, G_BETA)
    decay3 = jnp.where(incl, jnp.exp(jnp.where(incl, g_col3 - tile_rows(cs_t, G_DEC), 0.0)), 0.0)
    kb3 = k3 * beta3
    a_low3 = jnp.where(strict, _bnt(kb3, k3) * decay3, 0.0)
    attn_s[...] = _bnt(q3, k3) * decay3
    kbm3 = tile_heads(proj, SUBLANES, KB) * HEAD_DIM ** -0.5
    qk_s[...] = _bnt(tile_heads(proj, SUBLANES, QB), kbm3)
    kt_s[...] = jnp.stack([k3[p].T for p in range(len(pairs))])
    kbt_s[...] = jnp.stack([kbm3[p].T for p in range(len(pairs))])
    zero_row = jnp.zeros((1, CHUNK), F32)
    rows_s[...] = jnp.stack([jnp.concatenate(
        [cs_t[c][G_DEC + h:G_DEC + h + 1, :], cs_t[c][G_FG + h:G_FG + h + 1, :], gt_t[c][G_IN + h:G_IN + h + 1, :]]
        + [zero_row] * (SUBLANES - 3), axis=0) for c, h in pairs])
    r3 = _unit_lower_inverse_minus_eye(a_low3, N_HEADS)
    rhs3 = jnp.concatenate([v3 * beta3, kb3 * jnp.exp(g_col3)], axis=2)
    uw3 = rhs3 + _bnn(r3, rhs3)
    u_s[...] = uw3[:, :, :HEAD_DIM]
    w_s[...] = uw3[:, :, HEAD_DIM:]
    dlog3 = jnp.where(incl, tile_cols(gcum, G_FG) - tile_rows(cs_t, G_FG) + tile_rows(gt_t, G_IN), NEG_BIG)
    dlog_s[...] = dlog3
    dmax_s[...] = jnp.max(dlog3, axis=-1, keepdims=True)

    def chunk_body(c, carry):
        r0 = pl.multiple_of(c * CHUNK, CHUNK)
        rows = pl.ds(r0, CHUNK)
        prow = pl.ds(pl.multiple_of(r0 + SUBLANES, SUBLANES), CHUNK)
        last = pl.ds(r0 + CHUNK - 1, 1)
        pc = pl.ds(pl.multiple_of(c * N_HEADS, N_HEADS), N_HEADS)

        def heads(ref, rws, col0):
            return jnp.stack([ref[rws, col0 + h * HEAD_DIM:col0 + (h + 1) * HEAD_DIM] for h in range(N_HEADS)])

        def cols(ref, rws, lane0):
            return jnp.stack([ref[rws, lane0 + h:lane0 + h + 1] for h in range(N_HEADS)])

        q = heads(qkv, rows, QA)
        g_col = cols(gcum, rows, G_DEC)
        g_last = cols(gcum, last, G_DEC)
        qb = heads(proj, prow, QB)
        kbm = heads(proj, prow, KB) * HEAD_DIM ** -0.5
        vbm = heads(proj, prow, VB)
        b_col = cols(gcum, rows, G_FG)
        b_last = cols(gcum, last, G_FG)
        row_forms = rows_s[pc]
        g_row = row_forms[:, 0:1, :]
        b_row = row_forms[:, 1:2, :]
        i_row = row_forms[:, 2:3, :]
        s_old = s_s[...]
        c_old = c_s[...]
        n_old = n_s[...]
        m_old = m_s[:, 0:1, 0:1]
        inter = b_col + m_old
        mt = jnp.maximum(inter, dmax_s[pc])
        wts = jnp.exp(dlog_s[pc] - mt) * qk_s[pc]
        sc = jnp.exp(inter - mt)
        m_new = mt[:, CHUNK - 1:CHUNK, :]
        sd = jnp.exp(b_last + m_old - m_new)
        wk_row = jnp.exp(b_last - b_row + i_row - m_new)
        w_state = _bnn(w_s[pc], s_old)
        q_state = _bnn(q * jnp.exp(g_col), s_old)
        q_mem = _bnn(qb, c_old)
        w_val = _bnn(wts, vbm)
        kv = _bnn(kbt_s[pc] * wk_row, vbm)
        k_sum = _bnn(jnp.broadcast_to(wk_row, (N_HEADS, SUBLANES, CHUNK)), kbm)
        v_new = u_s[pc] - w_state
        o_a = q_state + _bnn(attn_s[pc], v_new)
        s_s[...] = s_old * jnp.exp(g_last) + _bnn(kt_s[pc] * jnp.exp(g_last - g_row), v_new)
        num = sc * q_mem + w_val
        den = sc * jnp.sum(qb * n_old[:, 0:1, :], axis=-1, keepdims=True) + jnp.sum(wts, axis=-1, keepdims=True)
        h_b = num / jnp.maximum(jnp.abs(den), jnp.exp(-mt))
        c_s[...] = sd * c_old + kv
        n_s[...] = sd * n_old + k_sum
        m_s[...] = jnp.broadcast_to(m_new, m_s.shape)
        o_n = _rms(o_a, gdn_w)
        h_n = _rms(h_b, ml_w)
        for h in range(N_HEADS):
            hs = slice(h * HEAD_DIM, (h + 1) * HEAD_DIM)
            hs2 = slice((N_HEADS + h) * HEAD_DIM, (N_HEADS + h + 1) * HEAD_DIM)
            merged[rows, hs] = o_n[h] * _silu(proj[prow, ZA + h * HEAD_DIM:ZA + (h + 1) * HEAD_DIM])
            merged[rows, hs2] = h_n[h] * _sigmoid(proj[prow, OB + h * HEAD_DIM:OB + (h + 1) * HEAD_DIM])
        return carry

    lax.fori_loop(0, nc, chunk_body, 0)

    mix = jnp.dot(_bf(merged[...]), wout_ref[...], preferred_element_type=F32)
    h_ref[...] = _layer_norm(DN_ALPHA * x + mix, ln_ref[0:1, :], ln_ref[1:2, :])

    proj[0:SUBLANES, 0:A_CONV] = proj[tb:tb + SUBLANES, 0:A_CONV]

    @pl.when(t == n_t - 1)
    def _():
        hist_ref[...] = proj[tb:tb + SUBLANES, 0:A_CONV]
        s_out[...] = s_s[...]
        c_out[...] = c_s[...]
        n_out[...] = n_s[...]
        m_out[...] = m_s[...]


def _expert_weight_cast_specs(moe_w, layer, n_steps, step_of):
    ops, in_specs, out_specs, out_shapes, shapes = [], [], [], [], []
    for w in moe_w:
        _, n_e, rows, cols = w.shape
        per_layer = n_e * rows
        blk = per_layer // n_steps
        assert per_layer % n_steps == 0 and blk % (2 * SUBLANES) == 0
        ops.append(w.reshape(w.shape[0] * per_layer, cols))
        in_specs.append(pl.BlockSpec((blk, cols), lambda *g: (layer * n_steps + step_of(*g), 0)))
        out_specs.append(pl.BlockSpec((blk, cols), lambda *g: (step_of(*g), 0)))
        out_shapes.append(jax.ShapeDtypeStruct((per_layer, cols), BF16))
        shapes.append((n_e, rows, cols))
    return ops, in_specs, out_specs, out_shapes, shapes


def _ab_prompt(x, win, wout, convw, gp, ln):
    bsz, length, _ = x.shape
    tb = min(512, length)
    assert length % tb == 0 and tb % CHUNK == 0 and length >= SUBLANES
    n_t = length // tb
    n_pairs = (tb // CHUNK) * N_HEADS
    const = lambda shape: pl.BlockSpec(shape, lambda b, t: (0,) * len(shape))
    per_b = lambda shape: pl.BlockSpec((None,) + shape, lambda b, t: (b,) + (0,) * len(shape))
    return pl.pallas_call(
        functools.partial(_ab_prompt_kernel, tb=tb, n_t=n_t),
        grid=(bsz, n_t),
        in_specs=[
            pl.BlockSpec((None, tb, D_MODEL), lambda b, t: (b, t, 0)),
            const((D_MODEL, D_IN_PAD)), const((2 * N_HEADS * HEAD_DIM, D_MODEL)),
            const((CONV_W, A_CONV)), const((SUBLANES, LANES)), const((2, D_MODEL)),
        ],
        out_specs=[
            pl.BlockSpec((None, tb, D_MODEL), lambda b, t: (b, t, 0)),
            per_b((SUBLANES, A_CONV)), per_b((N_HEADS, HEAD_DIM, HEAD_DIM)), per_b((N_HEADS, HEAD_DIM, HEAD_DIM)),
            per_b((N_HEADS, SUBLANES, LANES)), per_b((N_HEADS, SUBLANES, LANES)),
        ],
        out_shape=[
            jax.ShapeDtypeStruct((bsz, length, D_MODEL), F32),
            jax.ShapeDtypeStruct((bsz, SUBLANES, A_CONV), F32),
            jax.ShapeDtypeStruct((bsz, N_HEADS, HEAD_DIM, HEAD_DIM), F32),
            jax.ShapeDtypeStruct((bsz, N_HEADS, HEAD_DIM, HEAD_DIM), F32),
            jax.ShapeDtypeStruct((bsz, N_HEADS, SUBLANES, LANES), F32),
            jax.ShapeDtypeStruct((bsz, N_HEADS, SUBLANES, LANES), F32),
        ],
        scratch_shapes=[
            pltpu.VMEM((tb + SUBLANES, D_IN_PAD), F32),
            pltpu.VMEM((tb, A_CONV), F32),
            pltpu.VMEM((tb, LANES), F32),
            pltpu.VMEM((tb, LANES), F32),
            pltpu.VMEM((tb, 2 * N_HEADS * HEAD_DIM), F32),
            pltpu.VMEM((N_HEADS, HEAD_DIM, HEAD_DIM), F32),
            pltpu.VMEM((N_HEADS, HEAD_DIM, HEAD_DIM), F32),
            pltpu.VMEM((N_HEADS, SUBLANES, LANES), F32),
            pltpu.VMEM((N_HEADS, SUBLANES, LANES), F32),
            pltpu.VMEM((n_pairs, CHUNK, HEAD_DIM), F32),
            pltpu.VMEM((n_pairs, CHUNK, HEAD_DIM), F32),
            pltpu.VMEM((n_pairs, CHUNK, CHUNK), F32),
            pltpu.VMEM((n_pairs, CHUNK, CHUNK), F32),
            pltpu.VMEM((n_pairs, CHUNK, CHUNK), F32),
            pltpu.VMEM((n_pairs, CHUNK, 1), F32),
            pltpu.VMEM((n_pairs, HEAD_DIM, CHUNK), F32),
            pltpu.VMEM((n_pairs, HEAD_DIM, CHUNK), F32),
            pltpu.VMEM((n_pairs, SUBLANES, CHUNK), F32),
        ],
        compiler_params=pltpu.CompilerParams(
            dimension_semantics=("arbitrary", "arbitrary"), vmem_limit_bytes=VMEM_LIMIT),
        name="ab_prompt",
    )(x, win, wout, convw, gp, ln)


def _ab_decode_kernel(x_ref, win_ref, wout_ref, convw_ref, gp_ref, ln_ref, cbuf_ref, s_in, c_in, n_in, m_in,
                      y_ref, cbuf_out, s_out, c_out, n_out, m_out, proj, merged, *, bb, n_steps):
    i = pl.program_id(0)

    @pl.when(i == 0)
    def _():
        proj[...] = jnp.dot(_bf(x_ref[...]), win_ref[...], preferred_element_type=F32)

    rows = pl.ds(pl.multiple_of(i * bb, bb), bb)
    raw = proj[rows, 0:A_CONV]
    cbuf = cbuf_ref[...]
    conv = raw * convw_ref[CONV_W - 1:CONV_W, :]
    for j in range(CONV_W - 1):
        conv = conv + cbuf[:, j * A_CONV:(j + 1) * A_CONV] * convw_ref[j:j + 1, :]
    cbuf_out[:, 0:(CONV_W - 2) * A_CONV] = cbuf[:, A_CONV:(CONV_W - 1) * A_CONV]
    cbuf_out[:, (CONV_W - 2) * A_CONV:(CONV_W - 1) * A_CONV] = raw
    act = _silu(conv)
    gt = _gate_transform(proj[rows, GT:GT + LANES], gp_ref[...])
    gdn_w = gp_ref[4:5, :]
    ml_w = gp_ref[5:6, :]
    m_all = m_in[...]

    for h in range(N_HEADS):
        def head(off):
            return act[:, off + h * HEAD_DIM:off + (h + 1) * HEAD_DIM]
        q = head(QA)
        q = q * lax.rsqrt(jnp.sum(q * q, axis=-1, keepdims=True) + RMS_EPS) * HEAD_DIM ** -0.5
        k = head(KA)
        k = k * lax.rsqrt(jnp.sum(k * k, axis=-1, keepdims=True) + RMS_EPS)
        v = head(VA)
        q_t = q.T
        k_t = k.T
        qk = jnp.sum(q * k, axis=-1, keepdims=True)
        qb = proj[rows, QB + h * HEAD_DIM:QB + (h + 1) * HEAD_DIM]
        kbm = proj[rows, KB + h * HEAD_DIM:KB + (h + 1) * HEAD_DIM] * HEAD_DIM ** -0.5
        vbm = proj[rows, VB + h * HEAD_DIM:VB + (h + 1) * HEAD_DIM]
        qb_t = qb.T
        kb_t = kbm.T
        qkb = jnp.sum(qb * kbm, axis=-1, keepdims=True)
        n_old = n_in[:, h * HEAD_DIM:(h + 1) * HEAD_DIM]
        qn = jnp.sum(qb * n_old, axis=-1, keepdims=True)
        o_rows = []
        hb_rows = []
        n_rows = []
        m_rows = []
        for b in range(bb):
            s_old = s_in[b, h]
            k_c = k_t[:, b:b + 1]
            q_c = q_t[:, b:b + 1]
            e_g = jnp.exp(gt[b:b + 1, G_DEC + h:G_DEC + h + 1])
            beta = gt[b:b + 1, G_BETA + h:G_BETA + h + 1]
            k_s = jnp.sum(k_c * s_old, axis=0, keepdims=True)
            q_s = jnp.sum(q_c * s_old, axis=0, keepdims=True)
            v_new = v[b:b + 1, :] * beta - (beta * e_g) * k_s
            o_rows.append(e_g * q_s + qk[b:b + 1, :] * v_new)
            s_out[b, h] = s_old * e_g + k_c * v_new
            c_old = c_in[b, h]
            m_old = m_all[b:b + 1, h:h + 1]
            i_pre = gt[b:b + 1, G_IN + h:G_IN + h + 1]
            logf = gt[b:b + 1, G_FG + h:G_FG + h + 1]
            inter = logf + m_old
            mt = jnp.maximum(inter, i_pre)
            w_in = jnp.exp(i_pre - mt)
            sc = jnp.exp(inter - mt)
            wts = w_in * qkb[b:b + 1, :]
            q_cm = jnp.sum(qb_t[:, b:b + 1] * c_old, axis=0, keepdims=True)
            num = sc * q_cm + wts * vbm[b:b + 1, :]
            den = sc * qn[b:b + 1, :] + wts
            hb_rows.append(num / jnp.maximum(jnp.abs(den), jnp.exp(-mt)))
            c_out[b, h] = sc * c_old + (w_in * kb_t[:, b:b + 1]) * vbm[b:b + 1, :]
            n_rows.append(sc * n_old[b:b + 1, :] + w_in * kbm[b:b + 1, :])
            m_rows.append(mt)
        o_a = jnp.concatenate(o_rows, axis=0)
        h_b = jnp.concatenate(hb_rows, axis=0)
        n_out[:, h * HEAD_DIM:(h + 1) * HEAD_DIM] = jnp.concatenate(n_rows, axis=0)
        m_out[:, h:h + 1] = jnp.concatenate(m_rows, axis=0)
        z = proj[rows, ZA + h * HEAD_DIM:ZA + (h + 1) * HEAD_DIM]
        merged[rows, h * HEAD_DIM:(h + 1) * HEAD_DIM] = _rms(o_a, gdn_w) * _silu(z)
        o_gate = proj[rows, OB + h * HEAD_DIM:OB + (h + 1) * HEAD_DIM]
        merged[rows, N_HEADS * HEAD_DIM + h * HEAD_DIM:N_HEADS * HEAD_DIM + (h + 1) * HEAD_DIM] = (
            _rms(h_b, ml_w) * _sigmoid(o_gate))

    @pl.when(i == n_steps - 1)
    def _():
        mix = jnp.dot(_bf(merged[...]), wout_ref[...], preferred_element_type=F32)
        y_ref[...] = _layer_norm(DN_ALPHA * x_ref[...] + mix, ln_ref[0:1, :], ln_ref[1:2, :])


def _ab_decode(x, win, wout, convw, gp, ln, cbuf, s0, c0, n0, m0):
    nb = x.shape[0]
    bb = SUBLANES
    assert nb % bb == 0
    n_steps = nb // bb
    const = lambda shape: pl.BlockSpec(shape, lambda i: (0,) * len(shape))
    blk = lambda shape: pl.BlockSpec((bb,) + shape, lambda i: (i,) + (0,) * len(shape))
    hist = (CONV_W - 1) * A_CONV
    width = N_HEADS * HEAD_DIM
    return pl.pallas_call(
        functools.partial(_ab_decode_kernel, bb=bb, n_steps=n_steps),
        grid=(n_steps,),
        in_specs=[
            const((nb, D_MODEL)), const((D_MODEL, D_IN_PAD)), const((2 * width, D_MODEL)),
            const((CONV_W, A_CONV)), const((SUBLANES, LANES)), const((2, D_MODEL)),
            blk((hist,)), blk((N_HEADS, HEAD_DIM, HEAD_DIM)), blk((N_HEADS, HEAD_DIM, HEAD_DIM)),
            blk((width,)), blk((N_HEADS,)),
        ],
        out_specs=[
            const((nb, D_MODEL)), blk((hist,)), blk((N_HEADS, HEAD_DIM, HEAD_DIM)),
            blk((N_HEADS, HEAD_DIM, HEAD_DIM)), blk((width,)), blk((N_HEADS,)),
        ],
        out_shape=[
            jax.ShapeDtypeStruct((nb, D_MODEL), F32),
            jax.ShapeDtypeStruct((nb, hist), F32),
            jax.ShapeDtypeStruct((nb, N_HEADS, HEAD_DIM, HEAD_DIM), F32),
            jax.ShapeDtypeStruct((nb, N_HEADS, HEAD_DIM, HEAD_DIM), F32),
            jax.ShapeDtypeStruct((nb, width), F32),
            jax.ShapeDtypeStruct((nb, N_HEADS), F32),
        ],
        scratch_shapes=[pltpu.VMEM((nb, D_IN_PAD), F32), pltpu.VMEM((nb, 2 * width), F32)],
        compiler_params=pltpu.CompilerParams(dimension_semantics=("arbitrary",), vmem_limit_bytes=VMEM_LIMIT),
        name="ab_decode",
    )(x, win, wout, convw, gp, ln, cbuf, s0, c0, n0, m0)


def _second_largest_sum(a, b, c, d):
    hi1, lo1 = jnp.maximum(a, b), jnp.minimum(a, b)
    hi2, lo2 = jnp.maximum(c, d), jnp.minimum(c, d)
    return jnp.maximum(hi1, hi2) + jnp.maximum(jnp.minimum(hi1, hi2), jnp.maximum(lo1, lo2))


def _first_argmax(vals):
    best_v = vals[0]
    best_i = jnp.zeros(vals[0].shape, jnp.int32)
    for j in range(1, len(vals)):
        better = vals[j] > best_v
        best_v = jnp.where(better, vals[j], best_v)
        best_i = jnp.where(better, j, best_i)
    return best_i


def _router_kernel(*refs, n_steps, n_cast):
    x_ref, rw_ref, rb_ref = refs[:3]
    meta_ref, cnt_ref = refs[3 + n_cast:5 + n_cast]
    carry = refs[5 + 2 * n_cast]
    for src, dst in zip(refs[3:3 + n_cast], refs[5 + n_cast:5 + 2 * n_cast]):
        dst[...] = _bf(src[...])
    i = pl.program_id(0)
    tm = x_ref.shape[0]

    @pl.when(i == 0)
    def _():
        carry[...] = jnp.zeros_like(carry)

    x = x_ref[...]
    rw = rw_ref[...]
    x_hi = _bf(x)
    x_lo = _bf(x - x_hi.astype(F32))
    w_hi = _bf(rw)
    w_lo = _bf(rw - w_hi.astype(F32))
    nt = lambda a, b: lax.dot_general(a, b, (((1,), (1,)), ((), ())), preferred_element_type=F32)
    logits = nt(w_hi, x_hi) + (nt(w_hi, x_lo) + nt(w_lo, x_hi))
    ex = jnp.exp(logits - jnp.max(logits, axis=0, keepdims=True))
    probs = ex / jnp.sum(ex, axis=0, keepdims=True)
    sel = probs + rb_ref[...]
    p = [probs[j:j + 1, :] for j in range(N_EXPERTS)]
    s = [sel[j:j + 1, :] for j in range(N_EXPERTS)]
    scores = [_second_largest_sum(*s[EXPERTS_PER_GROUP * g:EXPERTS_PER_GROUP * (g + 1)])
              for g in range(N_EXPERT_GROUPS)]
    best = _first_argmax(scores)
    masked = [jnp.where(best == j // EXPERTS_PER_GROUP, s[j], -jnp.inf) for j in range(N_EXPERTS)]
    i1 = _first_argmax(masked)
    i2 = _first_argmax([jnp.where(i1 == j, -jnp.inf, masked[j]) for j in range(N_EXPERTS)])
    zero = jnp.zeros_like(p[0])
    p1 = functools.reduce(lambda a, b: a + b, [jnp.where(i1 == j, p[j], zero) for j in range(N_EXPERTS)])
    p2 = functools.reduce(lambda a, b: a + b, [jnp.where(i2 == j, p[j], zero) for j in range(N_EXPERTS)])
    tot = p1 + p2
    rows = [jnp.where(i1 == j, p1 / tot, zero) + jnp.where(i2 == j, p2 / tot, zero) for j in range(N_EXPERTS)]
    in_group = [best == g for g in range(N_EXPERT_GROUPS)]
    local = [functools.reduce(lambda a, b: a + b,
                              [jnp.where(in_group[g], rows[EXPERTS_PER_GROUP * g + e], zero)
                               for g in range(N_EXPERT_GROUPS)])
             for e in range(EXPERTS_PER_GROUP)]
    onehot = jnp.concatenate([jnp.where(m, 1.0, 0.0) for m in in_group]
                             + [jnp.zeros((SUBLANES - N_EXPERT_GROUPS, tm), F32)], axis=0)
    ri = lax.broadcasted_iota(jnp.int32, (tm, tm), 0)
    ci = lax.broadcasted_iota(jnp.int32, (tm, tm), 1)
    incl = jnp.dot(_bf(onehot), _bf(jnp.where(ri <= ci, 1.0, 0.0)), preferred_element_type=F32)
    prev = carry[...]
    rank = jnp.sum(onehot * (incl - 1.0 + prev[:, 0:1]), axis=0, keepdims=True)
    carry[...] = prev + incl[:, tm - 1:tm]
    row_id = (i * tm + lax.broadcasted_iota(jnp.int32, (1, tm), 1)).astype(F32)
    meta_ref[...] = jnp.concatenate([row_id] + local + [best.astype(F32), rank, zero], axis=0).T

    @pl.when(i == n_steps - 1)
    def _():
        cnt_ref[...] = carry[...]


def _router(x, rw_t, rb_col, moe_w=None, layer=0):
    t = x.shape[0]
    tm = min(256, t)
    assert t % tm == 0
    n_steps = t // tm
    if moe_w is None:
        w_ops, w_in, w_out, w_shapes, w_final = [], [], [], [], []
    else:
        w_ops, w_in, w_out, w_shapes, w_final = _expert_weight_cast_specs(moe_w, layer, n_steps, lambda i: i)
    outs = pl.pallas_call(
        functools.partial(_router_kernel, n_steps=n_steps, n_cast=len(w_ops)),
        grid=(n_steps,),
        in_specs=[pl.BlockSpec((tm, D_MODEL), lambda i: (i, 0)),
                  pl.BlockSpec((N_EXPERTS, D_MODEL), lambda i: (0, 0)),
                  pl.BlockSpec((N_EXPERTS, 1), lambda i: (0, 0))] + w_in,
        out_specs=[pl.BlockSpec((tm, SUBLANES), lambda i: (i, 0)),
                   pl.BlockSpec((SUBLANES, LANES), lambda i: (0, 0))] + w_out,
        out_shape=[jax.ShapeDtypeStruct((t, SUBLANES), F32), jax.ShapeDtypeStruct((SUBLANES, LANES), F32)] + w_shapes,
        scratch_shapes=[pltpu.VMEM((SUBLANES, LANES), F32)],
        compiler_params=pltpu.CompilerParams(dimension_semantics=("arbitrary",), vmem_limit_bytes=VMEM_LIMIT),
        name="router",
    )(x, rw_t, rb_col, *w_ops)
    return outs[0], outs[1], [o.reshape(s) for o, s in zip(outs[2:], w_final)]


def _moe_kernel(tg_ref, nv_ref, src_cur, dst_cur, src_nxt, cw_ref, x_hbm, wg_ref, wu_ref, wd_ref, ln_ref,
                y_hbm, xg, stage, gsem, ssem, *, tm, n_tiles):
    i = pl.program_id(0)
    slot = lax.rem(i, 2)
    other = 1 - slot
    n_valid = nv_ref[i]
    n_prev = jnp.where(i >= 1, nv_ref[jnp.maximum(i - 1, 0)], 0)
    n_back2 = jnp.where(i >= 2, nv_ref[jnp.maximum(i - 2, 0)], 0)
    has_next = i + 1 < n_tiles

    def gather_row(tok_ref, r, dst_slot):
        return pltpu.make_async_copy(x_hbm.at[pl.ds(tok_ref[0, r], 1), :], xg.at[dst_slot, pl.ds(r, 1), :],
                                     gsem.at[dst_slot])

    def scatter_row(tok_ref, r, src_slot):
        return pltpu.make_async_copy(stage.at[src_slot, pl.ds(r, 1), :], y_hbm.at[pl.ds(tok_ref[0, r], 1), :],
                                     ssem.at[src_slot])

    def scatter_wait(count, src_slot):
        for p in [1 << b for b in range(tm.bit_length())]:
            @pl.when(lax.bitwise_and(count, p) != 0)
            def _():
                pltpu.make_async_copy(stage.at[src_slot, pl.ds(0, p), :], y_hbm.at[pl.ds(0, p), :],
                                      ssem.at[src_slot]).wait()

    @pl.when(i == 0)
    def _():
        def body(r, c):
            gather_row(src_cur, r, 0).start()
            return c
        lax.fori_loop(0, tm, body, 0, unroll=8)

    pltpu.make_async_copy(x_hbm.at[pl.ds(0, tm), :], xg.at[slot], gsem.at[slot]).wait()

    for parity in range(2):
        @pl.when(jnp.logical_and(has_next, slot == parity))
        def _():
            for r in range(tm):
                gather_row(src_nxt, r, 1 - parity).start(priority=r % 2)

    @pl.when(n_valid > 0)
    def _():
        x = xg[slot]
        xb = _bf(x)
        cw = cw_ref[...]
        acc = jnp.zeros((tm, D_MODEL), F32)
        for e in range(EXPERTS_PER_GROUP):
            g = jnp.dot(xb, wg_ref[e], preferred_element_type=F32)
            u = jnp.dot(xb, wu_ref[e], preferred_element_type=F32)
            hid = _silu(g) * u * cw[:, 1 + e:2 + e]
            acc = acc + jnp.dot(_bf(hid), wd_ref[e], preferred_element_type=F32)
        y = _layer_norm(DN_ALPHA * x + acc, ln_ref[0:1, :], ln_ref[1:2, :])

        @pl.when(i >= 2)
        def _():
            scatter_wait(n_back2, slot)
        stage[slot] = y

    @pl.when(jnp.logical_and(n_valid <= 0, i >= 2))
    def _():
        scatter_wait(n_back2, slot)

    for parity in range(2):
        @pl.when(jnp.logical_and(n_valid == tm, slot == parity))
        def _():
            for r in range(tm):
                scatter_row(dst_cur, r, parity).start(priority=r % 2)

    @pl.when(n_valid < tm)
    def _():
        for r in range(tm):
            @pl.when(r < n_valid)
            def _():
                scatter_row(dst_cur, r, slot).start(priority=r % 2)

    @pl.when(i == n_tiles - 1)
    def _():
        scatter_wait(n_prev, other)
        scatter_wait(n_valid, slot)


def _moe(x, meta, counts, wg, wu, wd, ln, out_row):
    t = x.shape[0]
    tm = min(512, t)
    assert t % tm == 0 and tm & (tm - 1) == 0
    n_tiles = t // tm + N_EXPERT_GROUPS
    n_slots = n_tiles * tm
    grp = meta[:, 1 + EXPERTS_PER_GROUP].astype(jnp.int32)
    rank = meta[:, 2 + EXPERTS_PER_GROUP].astype(jnp.int32)
    cnt = counts[:N_EXPERT_GROUPS, 0].astype(jnp.int32)
    padded = ((cnt + tm - 1) // tm) * tm
    g_end = jnp.cumsum(padded)
    g_off = g_end - padded
    pos = jnp.take(g_off, grp) + rank
    slot_meta = jnp.zeros((n_slots, SUBLANES), F32).at[pos].set(meta)
    tok_of_slot = slot_meta[:, 0].astype(jnp.int32)
    tile_start = jnp.arange(n_tiles, dtype=jnp.int32) * tm
    tile_grp = jnp.minimum(jnp.sum(tile_start[:, None] >= g_end[None, :], axis=1), N_EXPERT_GROUPS - 1).astype(jnp.int32)
    tile_valid = jnp.clip(jnp.take(g_off + cnt, tile_grp) - tile_start, 0, tm).astype(jnp.int32)
    src3 = tok_of_slot.reshape(n_tiles, 1, tm)
    dst3 = out_row(tok_of_slot).reshape(n_tiles, 1, tm)

    grid_spec = pltpu.PrefetchScalarGridSpec(
        num_scalar_prefetch=2,
        grid=(n_tiles,),
        in_specs=[
            pl.BlockSpec((None, 1, tm), lambda i, tg, nv: (i, 0, 0), memory_space=pltpu.SMEM),
            pl.BlockSpec((None, 1, tm), lambda i, tg, nv: (i, 0, 0), memory_space=pltpu.SMEM),
            pl.BlockSpec((None, 1, tm), lambda i, tg, nv: (jnp.minimum(i + 1, n_tiles - 1), 0, 0),
                         memory_space=pltpu.SMEM),
            pl.BlockSpec((tm, SUBLANES), lambda i, tg, nv: (i, 0)),
            pl.BlockSpec(memory_space=pl.ANY),
            pl.BlockSpec((EXPERTS_PER_GROUP, D_MODEL, D_FF), lambda i, tg, nv: (tg[i], 0, 0)),
            pl.BlockSpec((EXPERTS_PER_GROUP, D_MODEL, D_FF), lambda i, tg, nv: (tg[i], 0, 0)),
            pl.BlockSpec((EXPERTS_PER_GROUP, D_FF, D_MODEL), lambda i, tg, nv: (tg[i], 0, 0)),
            pl.BlockSpec((2, D_MODEL), lambda i, tg, nv: (0, 0)),
        ],
        out_specs=pl.BlockSpec(memory_space=pl.ANY),
        scratch_shapes=[pltpu.VMEM((2, tm, D_MODEL), F32), pltpu.VMEM((2, tm, D_MODEL), F32),
                        pltpu.SemaphoreType.DMA((2,)), pltpu.SemaphoreType.DMA((2,))],
    )
    return pl.pallas_call(
        functools.partial(_moe_kernel, tm=tm, n_tiles=n_tiles),
        grid_spec=grid_spec,
        out_shape=jax.ShapeDtypeStruct((t, D_MODEL), F32),
        compiler_params=pltpu.CompilerParams(dimension_semantics=("arbitrary",), vmem_limit_bytes=VMEM_LIMIT),
        name="moe",
    )(tile_grp, tile_valid, src3, dst3, src3, slot_meta, x, wg, wu, wd, ln)


def _s5_prep_kernel(are_ref, aim_ref, ldt_ref, bre_ref, bim_ref, abre_ref, abim_ref, bbre_ref, bbim_ref):
    a_r = are_ref[...]
    a_i = aim_ref[...]
    dt = jnp.exp(ldt_ref[...])
    mag = jnp.exp(dt * a_r)
    ab_re = mag * jnp.cos(dt * a_i)
    ab_im = mag * jnp.sin(dt * a_i)
    den = a_r * a_r + a_i * a_i
    nr = ab_re - 1.0
    z_re = (nr * a_r + ab_im * a_i) / den
    z_im = (ab_im * a_r - nr * a_i) / den
    abre_ref[...] = ab_re
    abim_ref[...] = ab_im
    bbre_ref[...] = z_re * bre_ref[...] - z_im * bim_ref[...]
    bbim_ref[...] = z_re * bim_ref[...] + z_im * bre_ref[...]


def _s5_prep(a_re, a_im, log_dt, b_re, b_im):
    n = S5_GROUPS * S5_STATE
    col = jax.ShapeDtypeStruct((n, 1), F32)
    mat = jax.ShapeDtypeStruct((n, S5_GROUP), F32)
    ldt = jnp.broadcast_to(log_dt[:, None], (S5_GROUPS, S5_STATE)).reshape(n, 1)
    return pl.pallas_call(_s5_prep_kernel, out_shape=[col, col, mat, mat], name="s5_prep")(
        a_re.reshape(n, 1), a_im.reshape(n, 1), ldt, b_re.reshape(n, S5_GROUP), b_im.reshape(n, S5_GROUP))


def _gelu_tanh(x):
    return 0.5 * x * (1.0 + jnp.tanh(0.7978845608028654 * (x + 0.044715 * (x * x * x))))


def _s5_kernel(*refs, nb, tt, n_t, n_cast):
    (x_ref, h0re_ref, h0im_ref, wre_ref, wim_ref, cre_ref, cim_ref, abre_ref, abim_ref, d_ref,
     wa_ref, wb_ref, ln_ref) = refs[:13]
    cast_in = refs[13:13 + n_cast]
    o_ref, hre_out, him_out = refs[13 + n_cast:16 + n_cast]
    cast_out = refs[16 + n_cast:16 + 2 * n_cast]
    bu_re, bu_im, hre_s, him_s, gy_ref, glu = refs[16 + 2 * n_cast:]
    t = pl.program_id(0)
    sw = wre_ref.shape[2]
    for src, dst in zip(cast_in, cast_out):
        dst[...] = _bf(src[...])

    @pl.when(t == 0)
    def _():
        hre_s[...] = h0re_ref[...]
        him_s[...] = h0im_ref[...]

    for k in range(D_MODEL // LANES):
        ls = slice(k * LANES, (k + 1) * LANES)
        ss = slice(k * sw, (k + 1) * sw)
        buf = k % 2
        xv = x_ref[:, ls]
        xb = _bf(xv)
        bu_re[buf] = jnp.dot(xb, wre_ref[k], preferred_element_type=F32)
        bu_im[buf] = jnp.dot(xb, wim_ref[k], preferred_element_type=F32)
        a_re = jnp.broadcast_to(abre_ref[:, ss], (nb, sw))
        a_im = jnp.broadcast_to(abim_ref[:, ss], (nb, sw))

        def step(s, carry, buf=buf, a_re=a_re, a_im=a_im):
            h_re, h_im = carry
            rows = pl.ds(pl.multiple_of(s * nb, nb), nb)
            n_re = a_re * h_re - a_im * h_im + bu_re[buf, rows, :]
            n_im = a_re * h_im + a_im * h_re + bu_im[buf, rows, :]
            bu_re[buf, rows, :] = n_re
            bu_im[buf, rows, :] = n_im
            return n_re, n_im

        h_re, h_im = lax.fori_loop(0, tt, step, (hre_s[:, ss], him_s[:, ss]), unroll=min(tt, S5_UNROLL))
        hre_s[:, ss] = h_re
        him_s[:, ss] = h_im
        y = (jnp.dot(_bf(bu_re[buf]), cre_ref[k], preferred_element_type=F32)
             - jnp.dot(_bf(bu_im[buf]), cim_ref[k], preferred_element_type=F32)
             + d_ref[:, ls] * xv)
        gy_ref[:, ls] = _gelu_tanh(y).astype(gy_ref.dtype)

    gy = gy_ref[...]
    width = 2 * LANES
    for n in range(D_MODEL // width):
        ns = slice(n * width, (n + 1) * width)
        a = jnp.dot(gy, wa_ref[:, ns], preferred_element_type=F32)
        b = jnp.dot(gy, wb_ref[:, ns], preferred_element_type=F32)
        glu[:, ns] = a * _sigmoid(b)
    o_ref[...] = _layer_norm(DN_ALPHA * x_ref[...] + glu[...], ln_ref[0:1, :], ln_ref[1:2, :])

    @pl.when(t == n_t - 1)
    def _():
        hre_out[...] = hre_s[...]
        him_out[...] = him_s[...]


def _s5_scan(x, nb, h0_re, h0_im, wre, wim, cre, cim, ab_re, ab_im, d_skip, wa, wb, ln, moe_w=None):
    length = x.shape[0] // nb
    tt = min(S5_TT, length)
    assert length % tt == 0 and nb % SUBLANES == 0
    n_t = length // tt
    n_k = D_MODEL // LANES
    sw = (LANES // S5_GROUP) * S5_STATE
    n_state = S5_GROUPS * S5_STATE
    const = lambda shape: pl.BlockSpec(shape, lambda t: (0,) * len(shape))
    if moe_w is None:
        w_ops, w_in, w_out, w_shapes, w_final = [], [], [], [], []
    else:
        w_ops, w_in, w_out, w_shapes, w_final = _expert_weight_cast_specs(moe_w, 1, n_t, lambda t: t)
    outs = pl.pallas_call(
        functools.partial(_s5_kernel, nb=nb, tt=tt, n_t=n_t, n_cast=len(w_ops)),
        grid=(n_t,),
        in_specs=[
            pl.BlockSpec((tt * nb, D_MODEL), lambda t: (t, 0)),
            const((nb, n_state)), const((nb, n_state)),
            const((n_k, LANES, sw)), const((n_k, LANES, sw)), const((n_k, sw, LANES)), const((n_k, sw, LANES)),
            const((1, n_state)), const((1, n_state)), const((1, D_MODEL)),
            const((D_MODEL, D_MODEL)), const((D_MODEL, D_MODEL)), const((2, D_MODEL)),
        ] + w_in,
        out_specs=[pl.BlockSpec((tt * nb, D_MODEL), lambda t: (t, 0)), const((nb, n_state)),
                   const((nb, n_state))] + w_out,
        out_shape=[
            jax.ShapeDtypeStruct((length * nb, D_MODEL), F32),
            jax.ShapeDtypeStruct((nb, n_state), F32),
            jax.ShapeDtypeStruct((nb, n_state), F32),
        ] + w_shapes,
        scratch_shapes=[
            pltpu.VMEM((2, nb * tt, sw), F32), pltpu.VMEM((2, nb * tt, sw), F32),
            pltpu.VMEM((nb, n_state), F32), pltpu.VMEM((nb, n_state), F32),
            pltpu.VMEM((nb * tt, D_MODEL), BF16), pltpu.VMEM((nb * tt, D_MODEL), F32),
        ],
        compiler_params=pltpu.CompilerParams(dimension_semantics=("arbitrary",), vmem_limit_bytes=VMEM_LIMIT),
        name="s5_glu",
    )(x, h0_re, h0_im, wre, wim, cre, cim, ab_re, ab_im, d_skip, wa, wb, ln, *w_ops)
    return outs[:3], [o.reshape(s) for o, s in zip(outs[3:], w_final)]


def _block_diag_slices(m, rows_per_group, cols_per_group):
    gps = LANES // S5_GROUP
    m = m.reshape(S5_GROUPS // gps, gps, rows_per_group, cols_per_group)
    eye = jnp.eye(gps, dtype=m.dtype)
    out = m[:, :, :, None, :] * eye[None, :, None, :, None]
    return out.reshape(S5_GROUPS // gps, gps * rows_per_group, gps * cols_per_group)


def _prepare(p):
    w = p['w_in'][0]
    win = jnp.concatenate(
        [w[:, 0:1536], w[:, 1544:2056], w[:, 2056:3592], w[:, 3600:4112], w[:, 1536:1544], w[:, 3592:3600],
         jnp.zeros((D_MODEL, D_IN_PAD - 4112), w.dtype)], axis=1).astype(BF16)
    at_lane = lambda v, lane0: jnp.pad(v, (lane0, LANES - lane0 - v.shape[0]))
    gp = jnp.stack([at_lane(p['gdn_A_log'][0], G_DEC), at_lane(p['gdn_dt_bias'][0], G_DEC),
                    at_lane(p['ml_b_i'][0], G_IN), at_lane(p['ml_b_f'][0], G_FG),
                    p['gdn_norm_w'][0], p['ml_norm_w'][0], jnp.zeros((LANES,), F32), jnp.zeros((LANES,), F32)])
    ab_re, ab_im, bb_re, bb_im = _s5_prep(p['s5_A_re'][0], p['s5_A_im'][0], p['s5_log_dt'][0],
                                          p['s5_B_re'][0], p['s5_B_im'][0])
    to_in = lambda bb: _block_diag_slices(
        bb.reshape(S5_GROUPS, S5_STATE, S5_GROUP).transpose(0, 2, 1), S5_GROUP, S5_STATE).astype(BF16)
    to_out = lambda c: _block_diag_slices(c.transpose(0, 2, 1), S5_STATE, S5_GROUP).astype(BF16)
    return dict(
        win=win, wout=p['w_out'][0].astype(BF16), convw=p['gdn_conv_w'][0], gp=gp,
        ln_mix=[jnp.stack([p['ln_mix_g'][l], p['ln_mix_b'][l]]) for l in range(DEPTH)],
        ln_ffn=[jnp.stack([p['ln_ffn_g'][l], p['ln_ffn_b'][l]]) for l in range(DEPTH)],
        rw_t=p['router_w'].T, rb_col=p['router_b'][:, None],
        s5_wre=to_in(bb_re), s5_wim=to_in(bb_im),
        s5_cre=to_out(p['s5_C_re'][0]), s5_cim=to_out(p['s5_C_im'][0]),
        s5_abre=ab_re.reshape(1, -1), s5_abim=ab_im.reshape(1, -1), s5_d=p['s5_D'][0][None, :],
        glu_a=p['s5_w_glu_a'][0].astype(BF16), glu_b=p['s5_w_glu_b'][0].astype(BF16),
    )


def _ffn(h, w, layer, out_row, experts=None, moe_w=None):
    meta, counts, cast = _router(h, w['rw_t'], w['rb_col'], moe_w if experts is None else None, layer)
    experts = cast if experts is None else experts
    return _moe(h, meta, counts, *experts, w['ln_ffn'][layer], out_row), experts


def _s5_layer(h, nb, h0_re, h0_im, w, moe_w=None):
    return _s5_scan(h, nb, h0_re, h0_im, w['s5_wre'], w['s5_wim'], w['s5_cre'], w['s5_cim'],
                    w['s5_abre'], w['s5_abim'], w['s5_d'], w['glu_a'], w['glu_b'], w['ln_mix'][1], moe_w)


def kernel(x_prompt, x_sample, state_gdn_conv, state_gdn_S, state_mlstm_C, state_mlstm_n, state_mlstm_m,
           state_s5_re, state_s5_im, w_in, gdn_conv_w, gdn_A_log, gdn_dt_bias, gdn_norm_w, ml_b_i, ml_b_f,
           ml_norm_w, w_out, s5_A_re, s5_A_im, s5_log_dt, s5_B_re, s5_B_im, s5_C_re, s5_C_im, s5_D,
           s5_w_glu_a, s5_w_glu_b, router_w, router_b, moe_w_gate, moe_w_up, moe_w_down,
           ln_mix_g, ln_mix_b, ln_ffn_g, ln_ffn_b):
    w = _prepare(dict(
        w_in=w_in, gdn_conv_w=gdn_conv_w, gdn_A_log=gdn_A_log, gdn_dt_bias=gdn_dt_bias, gdn_norm_w=gdn_norm_w,
        ml_b_i=ml_b_i, ml_b_f=ml_b_f, ml_norm_w=ml_norm_w, w_out=w_out, s5_A_re=s5_A_re, s5_A_im=s5_A_im,
        s5_log_dt=s5_log_dt, s5_B_re=s5_B_re, s5_B_im=s5_B_im, s5_C_re=s5_C_re, s5_C_im=s5_C_im, s5_D=s5_D,
        s5_w_glu_a=s5_w_glu_a, s5_w_glu_b=s5_w_glu_b, router_w=router_w, router_b=router_b,
        moe_w_gate=moe_w_gate, moe_w_up=moe_w_up, moe_w_down=moe_w_down,
        ln_mix_g=ln_mix_g, ln_mix_b=ln_mix_b, ln_ffn_g=ln_ffn_g, ln_ffn_b=ln_ffn_b))
    bp, lp, _ = x_prompt.shape
    bs, ls, _ = x_sample.shape
    assert ls == 1
    n_state = S5_GROUPS * S5_STATE

    moe_w = (moe_w_gate, moe_w_up, moe_w_down)
    h, p_hist, p_s, p_c, p_n, p_m = _ab_prompt(x_prompt, w['win'], w['wout'], w['convw'], w['gp'], w['ln_mix'][0])
    h, experts0 = _ffn(h.reshape(bp * lp, D_MODEL), w, 0, lambda r: (r % lp) * bp + r // lp, moe_w=moe_w)
    zeros = jnp.zeros((bp, n_state), F32)
    (h, p_re, p_im), experts1 = _s5_layer(h, bp, zeros, zeros, w, moe_w)
    y_prompt = _ffn(h, w, 1, lambda r: (r % bp) * lp + r // bp, experts=experts1)[0].reshape(bp, lp, D_MODEL)

    hs, s_cbuf, s_s, s_c, s_n, s_m = _ab_decode(
        x_sample[:, 0], w['win'], w['wout'], w['convw'], w['gp'], w['ln_mix'][0],
        state_gdn_conv.reshape(bs, (CONV_W - 1) * A_CONV), state_gdn_S[:, 0], state_mlstm_C[:, 0],
        state_mlstm_n.reshape(bs, N_HEADS * HEAD_DIM), state_mlstm_m[:, 0])
    hs = _ffn(hs, w, 0, lambda r: r, experts=experts0)[0]
    (hs, s_re, s_im), _ = _s5_layer(hs, bs, state_s5_re.reshape(bs, n_state), state_s5_im.reshape(bs, n_state), w)
    y_sample = _ffn(hs, w, 1, lambda r: r, experts=experts1)[0].reshape(bs, 1, D_MODEL)

    grp = lambda a, n: a.reshape(n, 1, S5_GROUPS, S5_STATE)
    return (
        y_prompt, y_sample,
        p_hist[:, None, SUBLANES - (CONV_W - 1):, :], p_s[:, None], p_c[:, None],
        p_n[:, None, :, 0, :], p_m[:, None, :, 0, 0], grp(p_re, bp), grp(p_im, bp),
        s_cbuf.reshape(bs, 1, CONV_W - 1, A_CONV), s_s[:, None], s_c[:, None],
        s_n.reshape(bs, 1, N_HEADS, HEAD_DIM), s_m[:, None], grp(s_re, bs), grp(s_im, bs),
    )
```

```python
import functools

import jax
import jax.numpy as jnp
from jax import lax
from jax.experimental import pallas as pl
from jax.experimental.pallas import tpu as pltpu

F32 = jnp.float32
BF16 = jnp.bfloat16
HIGHEST = lax.Precision.HIGHEST

D_MODEL = 1024
DEPTH = 2
N_HEADS = 4
HEAD_DIM = 128
CONV_W = 4
CHUNK = 64
A_CONV = 3 * N_HEADS * HEAD_DIM
S5_GROUP = 16
S5_GROUPS = D_MODEL // S5_GROUP
S5_STATE = 64
N_EXPERTS = 16
EXPERTS_PER_GROUP = 4
N_EXPERT_GROUPS = N_EXPERTS // EXPERTS_PER_GROUP
D_FF = 512
DN_ALPHA = (2 * DEPTH) ** 0.25
LN_EPS = 1e-5
RMS_EPS = 1e-6
NEG_BIG = -1e30

QA, KA, VA, ZA, QB, KB, VB, OB, GT = 0, 512, 1024, 1536, 2048, 2560, 3072, 3584, 4096
D_IN_PAD = 4224
G_DEC, G_BETA, G_IN, G_FG = 0, 4, 8, 12

S5_TT = 64
S5_UNROLL = 64

LANES = 128
SUBLANES = 8
VMEM_LIMIT = 56 * 1024 * 1024


def _bf(x):
    return x.astype(BF16)


def _nn(a, b):
    return jnp.dot(_bf(a), _bf(b), preferred_element_type=F32)


def _nt(a, b):
    return lax.dot_general(_bf(a), _bf(b), (((1,), (1,)), ((), ())), preferred_element_type=F32)


def _tn(a, b):
    return lax.dot_general(_bf(a), _bf(b), (((0,), (0,)), ((), ())), preferred_element_type=F32)


def _sigmoid(x):
    return 1.0 / (1.0 + jnp.exp(-x))


def _softplus(x):
    return jnp.maximum(x, 0.0) + jnp.log(1.0 + jnp.exp(-jnp.abs(x)))


def _silu(x):
    return x * _sigmoid(x)


def _layer_norm(y, g, b):
    mu = jnp.mean(y, axis=-1, keepdims=True)
    yc = y - mu
    var = jnp.mean(yc * yc, axis=-1, keepdims=True)
    return yc * lax.rsqrt(var + LN_EPS) * g + b


def _rms(x, w):
    return x * lax.rsqrt(jnp.mean(x * x, axis=-1, keepdims=True) + RMS_EPS) * w


def _gate_transform(raw, gp):
    lane = lax.broadcasted_iota(jnp.int32, raw.shape, 1)
    dec = -jnp.exp(gp[0:1, :]) * _softplus(raw + gp[1:2, :])
    beta = _sigmoid(raw)
    ipre = raw + gp[2:3, :]
    logf = -_softplus(-(raw + gp[3:4, :]))
    return jnp.where(lane < G_BETA, dec,
                     jnp.where(lane < G_IN, beta,
                               jnp.where(lane < G_FG, ipre,
                                         jnp.where(lane < G_FG + N_HEADS, logf, 0.0))))


def _bnn(a, b):
    return lax.dot_general(_bf(a), _bf(b), (((2,), (1,)), ((0,), (0,))), preferred_element_type=F32)


def _bnt(a, b):
    return lax.dot_general(_bf(a), _bf(b), (((2,), (2,)), ((0,), (0,))), preferred_element_type=F32)


def _btn(a, b):
    return lax.dot_general(_bf(a), _bf(b), (((1,), (1,)), ((0,), (0,))), preferred_element_type=F32)


def _unit_lower_inverse_minus_eye(a, pack):
    p, c, _ = a.shape
    nb = p // pack
    w = pack * c
    r = -jnp.stack([jnp.concatenate([a[b * pack + j] for j in range(pack)], axis=1) for b in range(nb)])
    assert c & (c - 1) == 0
    block_of = lambda dim: lax.shift_right_logical(lax.broadcasted_iota(jnp.int32, (w, w), dim), c.bit_length() - 1)
    on_diag = block_of(0) == block_of(1)

    def block_diag(x):
        return jnp.where(on_diag, jnp.concatenate([x] * pack, axis=1), 0.0)

    steps = max(1, (c - 1).bit_length()) - 1
    q = _bnn(r, block_diag(r))
    for i in range(steps):
        bd = block_diag(q)
        if i + 1 < steps:
            both = _bnn(jnp.concatenate([r, q], axis=1), bd)
            rq, qq = both[:, :c, :], both[:, c:, :]
        else:
            rq, qq = _bnn(r, bd), None
        r = r + q + rq
        q = qq
    return jnp.stack([r[b][:, j * c:(j + 1) * c] for b in range(nb) for j in range(pack)])


def _ab_prompt_kernel(x_ref, win_ref, wout_ref, convw_ref, gp_ref, ln_ref,
                      h_ref, hist_ref, s_out, c_out, n_out, m_out,
                      proj, qkv, gates, gcum, merged, s_s, c_s, n_s, m_s,
                      u_s, w_s, attn_s, dlog_s, qk_s, dmax_s, kt_s, kbt_s, rows_s, *, tb, n_t):
    t = pl.program_id(1)
    nc = tb // CHUNK

    @pl.when(t == 0)
    def _():
        proj[0:SUBLANES, :] = jnp.zeros((SUBLANES, D_IN_PAD), F32)
        s_s[...] = jnp.zeros_like(s_s)
        c_s[...] = jnp.zeros_like(c_s)
        n_s[...] = jnp.zeros_like(n_s)
        m_s[...] = jnp.zeros_like(m_s)

    x = x_ref[...]
    xb = _bf(x)
    proj[SUBLANES:SUBLANES + tb, 0:A_CONV] = jnp.dot(xb, win_ref[:, 0:A_CONV], preferred_element_type=F32)
    proj[SUBLANES:SUBLANES + tb, A_CONV:] = jnp.dot(xb, win_ref[:, A_CONV:], preferred_element_type=F32)

    for blk in range(A_CONV // LANES):
        cs = slice(blk * LANES, (blk + 1) * LANES)
        acc = proj[SUBLANES:SUBLANES + tb, cs] * convw_ref[CONV_W - 1:CONV_W, cs]
        for j in range(1, CONV_W):
            acc = acc + proj[SUBLANES - j:SUBLANES - j + tb, cs] * convw_ref[CONV_W - 1 - j:CONV_W - j, cs]
        y = _silu(acc)
        if blk < 2 * N_HEADS:
            y = y * lax.rsqrt(jnp.sum(y * y, axis=-1, keepdims=True) + RMS_EPS)
            if blk < N_HEADS:
                y = y * HEAD_DIM ** -0.5
        qkv[:, cs] = y

    gt = _gate_transform(proj[SUBLANES:SUBLANES + tb, GT:GT + LANES], gp_ref[...])
    gates[...] = gt
    ri = lax.broadcasted_iota(jnp.int32, (tb, tb), 0)
    ci = lax.broadcasted_iota(jnp.int32, (tb, tb), 1)
    same_chunk = lax.shift_right_logical(ri, 6) == lax.shift_right_logical(ci, 6)
    ltri = jnp.where(same_chunk, jnp.where(ri >= ci, 1.0, 0.0), 0.0)
    gcum[...] = jnp.dot(ltri, gt, preferred_element_type=F32, precision=HIGHEST)

    ii = lax.broadcasted_iota(jnp.int32, (CHUNK, CHUNK), 0)
    jj = lax.broadcasted_iota(jnp.int32, (CHUNK, CHUNK), 1)
    incl = ii >= jj
    strict = ii > jj
    gdn_w = gp_ref[4:5, :]
    ml_w = gp_ref[5:6, :]

    pairs = [(c, h) for c in range(nc) for h in range(N_HEADS)]

    def tile_heads(ref, row0, col0):
        return jnp.stack([ref[row0 + c * CHUNK:row0 + (c + 1) * CHUNK, col0 + h * HEAD_DIM:col0 + (h + 1) * HEAD_DIM]
                          for c, h in pairs])

    def tile_cols(ref, lane0):
        return jnp.stack([ref[c * CHUNK:(c + 1) * CHUNK, lane0 + h:lane0 + h + 1] for c, h in pairs])

    def tile_rows(transposed, lane0):
        return jnp.stack([transposed[c][lane0 + h:lane0 + h + 1, :] for c, h in pairs])

    cs_t = [gcum[c * CHUNK:(c + 1) * CHUNK, :].T for c in range(nc)]
    gt_t = [gates[c * CHUNK:(c + 1) * CHUNK, :].T for c in range(nc)]
    q3 = tile_heads(qkv, 0, QA)
    k3 = tile_heads(qkv, 0, KA)
    v3 = tile_heads(qkv, 0, VA)
    g_col3 = tile_cols(gcum, G_DEC)
    beta3 = tile_cols(gates, G_BETA)
    decay3 = jnp.where(incl, jnp.exp(jnp.where(incl, g_col3 - tile_rows(cs_t, G_DEC), 0.0)), 0.0)
    kb3 = k3 * beta3
    a_low3 = jnp.where(strict, _bnt(kb3, k3) * decay3, 0.0)
    attn_s[...] = _bnt(q3, k3) * decay3
    q_dec3 = q3 * jnp.exp(g_col3)
    for p, (c, h) in enumerate(pairs):
        qkv[c * CHUNK:(c + 1) * CHUNK, QA + h * HEAD_DIM:QA + (h + 1) * HEAD_DIM] = q_dec3[p]
    kbm3 = tile_heads(proj, SUBLANES, KB) * HEAD_DIM ** -0.5
    qk_s[...] = _bnt(tile_heads(proj, SUBLANES, QB), kbm3)
    kt_s[...] = jnp.stack([k3[p].T for p in range(len(pairs))])
    kbt_s[...] = jnp.stack([kbm3[p].T for p in range(len(pairs))])
    zero_row = jnp.zeros((1, CHUNK), F32)
    rows_s[...] = jnp.stack([jnp.concatenate(
        [cs_t[c][G_DEC + h:G_DEC + h + 1, :], cs_t[c][G_FG + h:G_FG + h + 1, :], gt_t[c][G_IN + h:G_IN + h + 1, :]]
        + [zero_row] * (SUBLANES - 3), axis=0) for c, h in pairs])
    r3 = _unit_lower_inverse_minus_eye(a_low3, N_HEADS)
    rhs3 = jnp.concatenate([v3 * beta3, kb3 * jnp.exp(g_col3)], axis=2)
    uw3 = rhs3 + _bnn(r3, rhs3)
    u_s[...] = uw3[:, :, :HEAD_DIM]
    w_s[...] = uw3[:, :, HEAD_DIM:]
    dlog3 = jnp.where(incl, tile_cols(gcum, G_FG) - tile_rows(cs_t, G_FG) + tile_rows(gt_t, G_IN), NEG_BIG)
    dlog_s[...] = dlog3
    dmax_s[...] = jnp.max(dlog3, axis=-1, keepdims=True)

    def chunk_body(c, carry):
        r0 = pl.multiple_of(c * CHUNK, CHUNK)
        rows = pl.ds(r0, CHUNK)
        prow = pl.ds(pl.multiple_of(r0 + SUBLANES, SUBLANES), CHUNK)
        last = pl.ds(r0 + CHUNK - 1, 1)
        pc = pl.ds(pl.multiple_of(c * N_HEADS, N_HEADS), N_HEADS)

        def heads(ref, rws, col0):
            return jnp.stack([ref[rws, col0 + h * HEAD_DIM:col0 + (h + 1) * HEAD_DIM] for h in range(N_HEADS)])

        def cols(ref, rws, lane0):
            return jnp.stack([ref[rws, lane0 + h:lane0 + h + 1] for h in range(N_HEADS)])

        q_dec = heads(qkv, rows, QA)
        g_last = cols(gcum, last, G_DEC)
        qb = heads(proj, prow, QB)
        kbm = heads(proj, prow, KB) * HEAD_DIM ** -0.5
        vbm = heads(proj, prow, VB)
        b_col = cols(gcum, rows, G_FG)
        b_last = cols(gcum, last, G_FG)
        row_forms = rows_s[pc]
        g_row = row_forms[:, 0:1, :]
        b_row = row_forms[:, 1:2, :]
        i_row = row_forms[:, 2:3, :]
        s_old = s_s[...]
        c_old = c_s[...]
        n_old = n_s[...]
        m_old = m_s[:, 0:1, 0:1]
        inter = b_col + m_old
        mt = jnp.maximum(inter, dmax_s[pc])
        wts = jnp.exp(dlog_s[pc] - mt) * qk_s[pc]
        sc = jnp.exp(inter - mt)
        m_new = mt[:, CHUNK - 1:CHUNK, :]
        sd = jnp.exp(b_last + m_old - m_new)
        wk_row = jnp.exp(b_last - b_row + i_row - m_new)
        w_state = _bnn(w_s[pc], s_old)
        q_state = _bnn(q_dec, s_old)
        q_mem = _bnn(qb, c_old)
        w_val = _bnn(wts, vbm)
        kv = _bnn(kbt_s[pc] * wk_row, vbm)
        k_sum = _bnn(jnp.broadcast_to(wk_row, (N_HEADS, SUBLANES, CHUNK)), kbm)
        v_new = u_s[pc] - w_state
        o_a = q_state + _bnn(attn_s[pc], v_new)
        s_s[...] = s_old * jnp.exp(g_last) + _bnn(kt_s[pc] * jnp.exp(g_last - g_row), v_new)
        num = sc * q_mem + w_val
        q_norm = _bnt(qb, n_old)[:, :, 0:1]
        den = sc * q_norm + jnp.sum(wts, axis=-1, keepdims=True)
        h_b = num / jnp.maximum(jnp.abs(den), jnp.exp(-mt))
        c_s[...] = sd * c_old + kv
        n_s[...] = sd * n_old + k_sum
        m_s[...] = jnp.broadcast_to(m_new, m_s.shape)
        o_n = _rms(o_a, gdn_w)
        h_n = _rms(h_b, ml_w)
        for h in range(N_HEADS):
            hs = slice(h * HEAD_DIM, (h + 1) * HEAD_DIM)
            hs2 = slice((N_HEADS + h) * HEAD_DIM, (N_HEADS + h + 1) * HEAD_DIM)
            merged[rows, hs] = o_n[h] * _silu(proj[prow, ZA + h * HEAD_DIM:ZA + (h + 1) * HEAD_DIM])
            merged[rows, hs2] = h_n[h] * _sigmoid(proj[prow, OB + h * HEAD_DIM:OB + (h + 1) * HEAD_DIM])
        return carry

    lax.fori_loop(0, nc, chunk_body, 0)

    mix = jnp.dot(_bf(merged[...]), wout_ref[...], preferred_element_type=F32)
    h_ref[...] = _layer_norm(DN_ALPHA * x + mix, ln_ref[0:1, :], ln_ref[1:2, :])

    proj[0:SUBLANES, 0:A_CONV] = proj[tb:tb + SUBLANES, 0:A_CONV]

    @pl.when(t == n_t - 1)
    def _():
        hist_ref[...] = proj[tb:tb + SUBLANES, 0:A_CONV]
        s_out[...] = s_s[...]
        c_out[...] = c_s[...]
        n_out[...] = n_s[...]
        m_out[...] = m_s[...]


def _expert_weight_cast_specs(moe_w, layer, n_steps, step_of):
    ops, in_specs, out_specs, out_shapes, shapes = [], [], [], [], []
    for w in moe_w:
        _, n_e, rows, cols = w.shape
        per_layer = n_e * rows
        blk = per_layer // n_steps
        assert per_layer % n_steps == 0 and blk % (2 * SUBLANES) == 0
        ops.append(w.reshape(w.shape[0] * per_layer, cols))
        in_specs.append(pl.BlockSpec((blk, cols), lambda *g: (layer * n_steps + step_of(*g), 0)))
        out_specs.append(pl.BlockSpec((blk, cols), lambda *g: (step_of(*g), 0)))
        out_shapes.append(jax.ShapeDtypeStruct((per_layer, cols), BF16))
        shapes.append((n_e, rows, cols))
    return ops, in_specs, out_specs, out_shapes, shapes


def _ab_prompt(x, win, wout, convw, gp, ln):
    bsz, length, _ = x.shape
    tb = min(512, length)
    assert length % tb == 0 and tb % CHUNK == 0 and length >= SUBLANES
    n_t = length // tb
    n_pairs = (tb // CHUNK) * N_HEADS
    const = lambda shape: pl.BlockSpec(shape, lambda b, t: (0,) * len(shape))
    per_b = lambda shape: pl.BlockSpec((None,) + shape, lambda b, t: (b,) + (0,) * len(shape))
    return pl.pallas_call(
        functools.partial(_ab_prompt_kernel, tb=tb, n_t=n_t),
        grid=(bsz, n_t),
        in_specs=[
            pl.BlockSpec((None, tb, D_MODEL), lambda b, t: (b, t, 0)),
            const((D_MODEL, D_IN_PAD)), const((2 * N_HEADS * HEAD_DIM, D_MODEL)),
            const((CONV_W, A_CONV)), const((SUBLANES, LANES)), const((2, D_MODEL)),
        ],
        out_specs=[
            pl.BlockSpec((None, tb, D_MODEL), lambda b, t: (b, t, 0)),
            per_b((SUBLANES, A_CONV)), per_b((N_HEADS, HEAD_DIM, HEAD_DIM)), per_b((N_HEADS, HEAD_DIM, HEAD_DIM)),
            per_b((N_HEADS, SUBLANES, LANES)), per_b((N_HEADS, SUBLANES, LANES)),
        ],
        out_shape=[
            jax.ShapeDtypeStruct((bsz, length, D_MODEL), F32),
            jax.ShapeDtypeStruct((bsz, SUBLANES, A_CONV), F32),
            jax.ShapeDtypeStruct((bsz, N_HEADS, HEAD_DIM, HEAD_DIM), F32),
            jax.ShapeDtypeStruct((bsz, N_HEADS, HEAD_DIM, HEAD_DIM), F32),
            jax.ShapeDtypeStruct((bsz, N_HEADS, SUBLANES, LANES), F32),
            jax.ShapeDtypeStruct((bsz, N_HEADS, SUBLANES, LANES), F32),
        ],
        scratch_shapes=[
            pltpu.VMEM((tb + SUBLANES, D_IN_PAD), F32),
            pltpu.VMEM((tb, A_CONV), F32),
            pltpu.VMEM((tb, LANES), F32),
            pltpu.VMEM((tb, LANES), F32),
            pltpu.VMEM((tb, 2 * N_HEADS * HEAD_DIM), F32),
            pltpu.VMEM((N_HEADS, HEAD_DIM, HEAD_DIM), F32),
            pltpu.VMEM((N_HEADS, HEAD_DIM, HEAD_DIM), F32),
            pltpu.VMEM((N_HEADS, SUBLANES, LANES), F32),
            pltpu.VMEM((N_HEADS, SUBLANES, LANES), F32),
            pltpu.VMEM((n_pairs, CHUNK, HEAD_DIM), F32),
            pltpu.VMEM((n_pairs, CHUNK, HEAD_DIM), F32),
            pltpu.VMEM((n_pairs, CHUNK, CHUNK), F32),
            pltpu.VMEM((n_pairs, CHUNK, CHUNK), F32),
            pltpu.VMEM((n_pairs, CHUNK, CHUNK), F32),
            pltpu.VMEM((n_pairs, CHUNK, 1), F32),
            pltpu.VMEM((n_pairs, HEAD_DIM, CHUNK), F32),
            pltpu.VMEM((n_pairs, HEAD_DIM, CHUNK), F32),
            pltpu.VMEM((n_pairs, SUBLANES, CHUNK), F32),
        ],
        compiler_params=pltpu.CompilerParams(
            dimension_semantics=("arbitrary", "arbitrary"), vmem_limit_bytes=VMEM_LIMIT),
        name="ab_prompt",
    )(x, win, wout, convw, gp, ln)


def _ab_decode_kernel(*refs, bb, n_steps, n_cast):
    x_ref, win_ref, wout_ref, convw_ref, gp_ref, ln_ref, cbuf_ref, s_in, c_in, n_in, m_in = refs[:11]
    y_ref, cbuf_out, s_out, c_out, n_out, m_out = refs[11 + n_cast:17 + n_cast]
    proj, merged = refs[17 + 2 * n_cast:]
    for src, dst in zip(refs[11:11 + n_cast], refs[17 + n_cast:17 + 2 * n_cast]):
        dst[...] = _bf(src[...])
    i = pl.program_id(0)

    @pl.when(i == 0)
    def _():
        proj[...] = jnp.dot(_bf(x_ref[...]), win_ref[...], preferred_element_type=F32)

    rows = pl.ds(pl.multiple_of(i * bb, bb), bb)
    raw = proj[rows, 0:A_CONV]
    cbuf = cbuf_ref[...]
    conv = raw * convw_ref[CONV_W - 1:CONV_W, :]
    for j in range(CONV_W - 1):
        conv = conv + cbuf[:, j * A_CONV:(j + 1) * A_CONV] * convw_ref[j:j + 1, :]
    cbuf_out[:, 0:(CONV_W - 2) * A_CONV] = cbuf[:, A_CONV:(CONV_W - 1) * A_CONV]
    cbuf_out[:, (CONV_W - 2) * A_CONV:(CONV_W - 1) * A_CONV] = raw
    act = _silu(conv)
    gt = _gate_transform(proj[rows, GT:GT + LANES], gp_ref[...])
    gdn_w = gp_ref[4:5, :]
    ml_w = gp_ref[5:6, :]
    m_all = m_in[...]

    for h in range(N_HEADS):
        def head(off):
            return act[:, off + h * HEAD_DIM:off + (h + 1) * HEAD_DIM]
        q = head(QA)
        q = q * lax.rsqrt(jnp.sum(q * q, axis=-1, keepdims=True) + RMS_EPS) * HEAD_DIM ** -0.5
        k = head(KA)
        k = k * lax.rsqrt(jnp.sum(k * k, axis=-1, keepdims=True) + RMS_EPS)
        v = head(VA)
        q_t = q.T
        k_t = k.T
        qk = jnp.sum(q * k, axis=-1, keepdims=True)
        qb = proj[rows, QB + h * HEAD_DIM:QB + (h + 1) * HEAD_DIM]
        kbm = proj[rows, KB + h * HEAD_DIM:KB + (h + 1) * HEAD_DIM] * HEAD_DIM ** -0.5
        vbm = proj[rows, VB + h * HEAD_DIM:VB + (h + 1) * HEAD_DIM]
        qb_t = qb.T
        kb_t = kbm.T
        qkb = jnp.sum(qb * kbm, axis=-1, keepdims=True)
        n_old = n_in[:, h * HEAD_DIM:(h + 1) * HEAD_DIM]
        qn = jnp.sum(qb * n_old, axis=-1, keepdims=True)
        o_rows = []
        hb_rows = []
        n_rows = []
        m_rows = []
        for b in range(bb):
            s_old = s_in[b, h]
            k_c = k_t[:, b:b + 1]
            q_c = q_t[:, b:b + 1]
            e_g = jnp.exp(gt[b:b + 1, G_DEC + h:G_DEC + h + 1])
            beta = gt[b:b + 1, G_BETA + h:G_BETA + h + 1]
            k_s = jnp.sum(k_c * s_old, axis=0, keepdims=True)
            q_s = jnp.sum(q_c * s_old, axis=0, keepdims=True)
            v_new = v[b:b + 1, :] * beta - (beta * e_g) * k_s
            o_rows.append(e_g * q_s + qk[b:b + 1, :] * v_new)
            s_out[b, h] = s_old * e_g + k_c * v_new
            c_old = c_in[b, h]
            m_old = m_all[b:b + 1, h:h + 1]
            i_pre = gt[b:b + 1, G_IN + h:G_IN + h + 1]
            logf = gt[b:b + 1, G_FG + h:G_FG + h + 1]
            inter = logf + m_old
            mt = jnp.maximum(inter, i_pre)
            w_in = jnp.exp(i_pre - mt)
            sc = jnp.exp(inter - mt)
            wts = w_in * qkb[b:b + 1, :]
            q_cm = jnp.sum(qb_t[:, b:b + 1] * c_old, axis=0, keepdims=True)
            num = sc * q_cm + wts * vbm[b:b + 1, :]
            den = sc * qn[b:b + 1, :] + wts
            hb_rows.append(num / jnp.maximum(jnp.abs(den), jnp.exp(-mt)))
            c_out[b, h] = sc * c_old + (w_in * kb_t[:, b:b + 1]) * vbm[b:b + 1, :]
            n_rows.append(sc * n_old[b:b + 1, :] + w_in * kbm[b:b + 1, :])
            m_rows.append(mt)
        o_a = jnp.concatenate(o_rows, axis=0)
        h_b = jnp.concatenate(hb_rows, axis=0)
        n_out[:, h * HEAD_DIM:(h + 1) * HEAD_DIM] = jnp.concatenate(n_rows, axis=0)
        m_out[:, h:h + 1] = jnp.concatenate(m_rows, axis=0)
        z = proj[rows, ZA + h * HEAD_DIM:ZA + (h + 1) * HEAD_DIM]
        merged[rows, h * HEAD_DIM:(h + 1) * HEAD_DIM] = _rms(o_a, gdn_w) * _silu(z)
        o_gate = proj[rows, OB + h * HEAD_DIM:OB + (h + 1) * HEAD_DIM]
        merged[rows, N_HEADS * HEAD_DIM + h * HEAD_DIM:N_HEADS * HEAD_DIM + (h + 1) * HEAD_DIM] = (
            _rms(h_b, ml_w) * _sigmoid(o_gate))

    @pl.when(i == n_steps - 1)
    def _():
        mix = jnp.dot(_bf(merged[...]), wout_ref[...], preferred_element_type=F32)
        y_ref[...] = _layer_norm(DN_ALPHA * x_ref[...] + mix, ln_ref[0:1, :], ln_ref[1:2, :])


def _ab_decode(x, win, wout, convw, gp, ln, cbuf, s0, c0, n0, m0, moe_w):
    nb = x.shape[0]
    bb = SUBLANES
    assert nb % bb == 0
    n_steps = nb // bb
    const = lambda shape: pl.BlockSpec(shape, lambda i: (0,) * len(shape))
    blk = lambda shape: pl.BlockSpec((bb,) + shape, lambda i: (i,) + (0,) * len(shape))
    hist = (CONV_W - 1) * A_CONV
    width = N_HEADS * HEAD_DIM
    w_ops, w_in, w_out, w_shapes, w_final = _expert_weight_cast_specs(moe_w, 0, n_steps, lambda i: i)
    outs = pl.pallas_call(
        functools.partial(_ab_decode_kernel, bb=bb, n_steps=n_steps, n_cast=len(w_ops)),
        grid=(n_steps,),
        in_specs=[
            const((nb, D_MODEL)), const((D_MODEL, D_IN_PAD)), const((2 * width, D_MODEL)),
            const((CONV_W, A_CONV)), const((SUBLANES, LANES)), const((2, D_MODEL)),
            blk((hist,)), blk((N_HEADS, HEAD_DIM, HEAD_DIM)), blk((N_HEADS, HEAD_DIM, HEAD_DIM)),
            blk((width,)), blk((N_HEADS,)),
        ] + w_in,
        out_specs=[
            const((nb, D_MODEL)), blk((hist,)), blk((N_HEADS, HEAD_DIM, HEAD_DIM)),
            blk((N_HEADS, HEAD_DIM, HEAD_DIM)), blk((width,)), blk((N_HEADS,)),
        ] + w_out,
        out_shape=[
            jax.ShapeDtypeStruct((nb, D_MODEL), F32),
            jax.ShapeDtypeStruct((nb, hist), F32),
            jax.ShapeDtypeStruct((nb, N_HEADS, HEAD_DIM, HEAD_DIM), F32),
            jax.ShapeDtypeStruct((nb, N_HEADS, HEAD_DIM, HEAD_DIM), F32),
            jax.ShapeDtypeStruct((nb, width), F32),
            jax.ShapeDtypeStruct((nb, N_HEADS), F32),
        ] + w_shapes,
        scratch_shapes=[pltpu.VMEM((nb, D_IN_PAD), F32), pltpu.VMEM((nb, 2 * width), F32)],
        compiler_params=pltpu.CompilerParams(dimension_semantics=("arbitrary",), vmem_limit_bytes=VMEM_LIMIT),
        name="ab_decode",
    )(x, win, wout, convw, gp, ln, cbuf, s0, c0, n0, m0, *w_ops)
    return outs[:6], [o.reshape(s) for o, s in zip(outs[6:], w_final)]


def _second_largest_sum(a, b, c, d):
    hi1, lo1 = jnp.maximum(a, b), jnp.minimum(a, b)
    hi2, lo2 = jnp.maximum(c, d), jnp.minimum(c, d)
    return jnp.maximum(hi1, hi2) + jnp.maximum(jnp.minimum(hi1, hi2), jnp.maximum(lo1, lo2))


def _first_argmax(vals):
    best_v = vals[0]
    best_i = jnp.zeros(vals[0].shape, jnp.int32)
    for j in range(1, len(vals)):
        better = vals[j] > best_v
        best_v = jnp.where(better, vals[j], best_v)
        best_i = jnp.where(better, j, best_i)
    return best_i


def _router_kernel(*refs, n_steps, n_cast):
    x_ref, rw_ref, rb_ref = refs[:3]
    meta_ref, cnt_ref = refs[3 + n_cast:5 + n_cast]
    carry = refs[5 + 2 * n_cast]
    for src, dst in zip(refs[3:3 + n_cast], refs[5 + n_cast:5 + 2 * n_cast]):
        dst[...] = _bf(src[...])
    i = pl.program_id(0)
    tm = x_ref.shape[0]

    @pl.when(i == 0)
    def _():
        carry[...] = jnp.zeros_like(carry)

    x = x_ref[...]
    rw = rw_ref[...]
    x_hi = _bf(x)
    x_lo = _bf(x - x_hi.astype(F32))
    w_hi = _bf(rw)
    w_lo = _bf(rw - w_hi.astype(F32))
    nt = lambda a, b: lax.dot_general(a, b, (((1,), (1,)), ((), ())), preferred_element_type=F32)
    logits = nt(w_hi, x_hi) + (nt(w_hi, x_lo) + nt(w_lo, x_hi))
    ex = jnp.exp(logits - jnp.max(logits, axis=0, keepdims=True))
    probs = ex / jnp.sum(ex, axis=0, keepdims=True)
    sel = probs + rb_ref[...]
    p = [probs[j:j + 1, :] for j in range(N_EXPERTS)]
    s = [sel[j:j + 1, :] for j in range(N_EXPERTS)]
    scores = [_second_largest_sum(*s[EXPERTS_PER_GROUP * g:EXPERTS_PER_GROUP * (g + 1)])
              for g in range(N_EXPERT_GROUPS)]
    best = _first_argmax(scores)
    masked = [jnp.where(best == j // EXPERTS_PER_GROUP, s[j], -jnp.inf) for j in range(N_EXPERTS)]
    i1 = _first_argmax(masked)
    i2 = _first_argmax([jnp.where(i1 == j, -jnp.inf, masked[j]) for j in range(N_EXPERTS)])
    zero = jnp.zeros_like(p[0])
    p1 = functools.reduce(lambda a, b: a + b, [jnp.where(i1 == j, p[j], zero) for j in range(N_EXPERTS)])
    p2 = functools.reduce(lambda a, b: a + b, [jnp.where(i2 == j, p[j], zero) for j in range(N_EXPERTS)])
    tot = p1 + p2
    rows = [jnp.where(i1 == j, p1 / tot, zero) + jnp.where(i2 == j, p2 / tot, zero) for j in range(N_EXPERTS)]
    in_group = [best == g for g in range(N_EXPERT_GROUPS)]
    local = [functools.reduce(lambda a, b: a + b,
                              [jnp.where(in_group[g], rows[EXPERTS_PER_GROUP * g + e], zero)
                               for g in range(N_EXPERT_GROUPS)])
             for e in range(EXPERTS_PER_GROUP)]
    onehot = jnp.concatenate([jnp.where(m, 1.0, 0.0) for m in in_group]
                             + [jnp.zeros((SUBLANES - N_EXPERT_GROUPS, tm), F32)], axis=0)
    ri = lax.broadcasted_iota(jnp.int32, (tm, tm), 0)
    ci = lax.broadcasted_iota(jnp.int32, (tm, tm), 1)
    incl = jnp.dot(_bf(onehot), _bf(jnp.where(ri <= ci, 1.0, 0.0)), preferred_element_type=F32)
    prev = carry[...]
    rank = jnp.sum(onehot * (incl - 1.0 + prev[:, 0:1]), axis=0, keepdims=True)
    carry[...] = prev + incl[:, tm - 1:tm]
    row_id = (i * tm + lax.broadcasted_iota(jnp.int32, (1, tm), 1)).astype(F32)
    meta_ref[...] = jnp.concatenate([row_id] + local + [best.astype(F32), rank, zero], axis=0).T

    @pl.when(i == n_steps - 1)
    def _():
        cnt_ref[...] = carry[...]


def _router(x, rw_t, rb_col, moe_w=None, layer=0):
    t = x.shape[0]
    tm = min(256, t)
    assert t % tm == 0
    n_steps = t // tm
    if moe_w is None:
        w_ops, w_in, w_out, w_shapes, w_final = [], [], [], [], []
    else:
        w_ops, w_in, w_out, w_shapes, w_final = _expert_weight_cast_specs(moe_w, layer, n_steps, lambda i: i)
    outs = pl.pallas_call(
        functools.partial(_router_kernel, n_steps=n_steps, n_cast=len(w_ops)),
        grid=(n_steps,),
        in_specs=[pl.BlockSpec((tm, D_MODEL), lambda i: (i, 0)),
                  pl.BlockSpec((N_EXPERTS, D_MODEL), lambda i: (0, 0)),
                  pl.BlockSpec((N_EXPERTS, 1), lambda i: (0, 0))] + w_in,
        out_specs=[pl.BlockSpec((tm, SUBLANES), lambda i: (i, 0)),
                   pl.BlockSpec((SUBLANES, LANES), lambda i: (0, 0))] + w_out,
        out_shape=[jax.ShapeDtypeStruct((t, SUBLANES), F32), jax.ShapeDtypeStruct((SUBLANES, LANES), F32)] + w_shapes,
        scratch_shapes=[pltpu.VMEM((SUBLANES, LANES), F32)],
        compiler_params=pltpu.CompilerParams(dimension_semantics=("arbitrary",), vmem_limit_bytes=VMEM_LIMIT),
        name="router",
    )(x, rw_t, rb_col, *w_ops)
    return outs[0], outs[1], [o.reshape(s) for o, s in zip(outs[2:], w_final)]


def _moe_kernel(tg_ref, nv_ref, src_cur, dst_cur, src_nxt, cw_ref, x_hbm, wg_ref, wu_ref, wd_ref, ln_ref,
                y_hbm, xg, stage, gsem, ssem, *, tm, n_tiles):
    i = pl.program_id(0)
    slot = lax.rem(i, 2)
    other = 1 - slot
    n_valid = nv_ref[i]
    n_prev = jnp.where(i >= 1, nv_ref[jnp.maximum(i - 1, 0)], 0)
    n_back2 = jnp.where(i >= 2, nv_ref[jnp.maximum(i - 2, 0)], 0)
    has_next = i + 1 < n_tiles

    def gather_row(tok_ref, r, dst_slot):
        return pltpu.make_async_copy(x_hbm.at[pl.ds(tok_ref[0, r], 1), :], xg.at[dst_slot, pl.ds(r, 1), :],
                                     gsem.at[dst_slot])

    def scatter_row(tok_ref, r, src_slot):
        return pltpu.make_async_copy(stage.at[src_slot, pl.ds(r, 1), :], y_hbm.at[pl.ds(tok_ref[0, r], 1), :],
                                     ssem.at[src_slot])

    def scatter_wait(count, src_slot):
        for p in [1 << b for b in range(tm.bit_length())]:
            @pl.when(lax.bitwise_and(count, p) != 0)
            def _():
                pltpu.make_async_copy(stage.at[src_slot, pl.ds(0, p), :], y_hbm.at[pl.ds(0, p), :],
                                      ssem.at[src_slot]).wait()

    @pl.when(i == 0)
    def _():
        def body(r, c):
            gather_row(src_cur, r, 0).start()
            return c
        lax.fori_loop(0, tm, body, 0, unroll=8)

    pltpu.make_async_copy(x_hbm.at[pl.ds(0, tm), :], xg.at[slot], gsem.at[slot]).wait()

    for parity in range(2):
        @pl.when(jnp.logical_and(has_next, slot == parity))
        def _():
            for r in range(tm):
                gather_row(src_nxt, r, 1 - parity).start(priority=r % 2)

    @pl.when(n_valid > 0)
    def _():
        x = xg[slot]
        xb = _bf(x)
        cw = cw_ref[...]
        acc = jnp.zeros((tm, D_MODEL), F32)
        for e in range(EXPERTS_PER_GROUP):
            g = jnp.dot(xb, wg_ref[e], preferred_element_type=F32)
            u = jnp.dot(xb, wu_ref[e], preferred_element_type=F32)
            hid = _silu(g) * u * cw[:, 1 + e:2 + e]
            acc = acc + jnp.dot(_bf(hid), wd_ref[e], preferred_element_type=F32)
        y = _layer_norm(DN_ALPHA * x + acc, ln_ref[0:1, :], ln_ref[1:2, :])

        @pl.when(i >= 2)
        def _():
            scatter_wait(n_back2, slot)
        stage[slot] = y

    @pl.when(jnp.logical_and(n_valid <= 0, i >= 2))
    def _():
        scatter_wait(n_back2, slot)

    for parity in range(2):
        @pl.when(jnp.logical_and(n_valid == tm, slot == parity))
        def _():
            for r in range(tm):
                scatter_row(dst_cur, r, parity).start(priority=r % 2)

    @pl.when(n_valid < tm)
    def _():
        for r in range(tm):
            @pl.when(r < n_valid)
            def _():
                scatter_row(dst_cur, r, slot).start(priority=r % 2)

    @pl.when(i == n_tiles - 1)
    def _():
        scatter_wait(n_prev, other)
        scatter_wait(n_valid, slot)


def _moe(x, meta, counts, wg, wu, wd, ln, out_row):
    t = x.shape[0]
    tm = min(512, t)
    assert t % tm == 0 and tm & (tm - 1) == 0
    n_tiles = t // tm + N_EXPERT_GROUPS
    n_slots = n_tiles * tm
    grp = meta[:, 1 + EXPERTS_PER_GROUP].astype(jnp.int32)
    rank = meta[:, 2 + EXPERTS_PER_GROUP].astype(jnp.int32)
    cnt = counts[:N_EXPERT_GROUPS, 0].astype(jnp.int32)
    padded = ((cnt + tm - 1) // tm) * tm
    g_end = jnp.cumsum(padded)
    g_off = g_end - padded
    pos = jnp.take(g_off, grp) + rank
    slot_meta = jnp.zeros((n_slots, SUBLANES), F32).at[pos].set(meta)
    tok_of_slot = slot_meta[:, 0].astype(jnp.int32)
    tile_start = jnp.arange(n_tiles, dtype=jnp.int32) * tm
    tile_grp = jnp.minimum(jnp.sum(tile_start[:, None] >= g_end[None, :], axis=1), N_EXPERT_GROUPS - 1).astype(jnp.int32)
    tile_valid = jnp.clip(jnp.take(g_off + cnt, tile_grp) - tile_start, 0, tm).astype(jnp.int32)
    src3 = tok_of_slot.reshape(n_tiles, 1, tm)
    dst3 = out_row(tok_of_slot).reshape(n_tiles, 1, tm)

    grid_spec = pltpu.PrefetchScalarGridSpec(
        num_scalar_prefetch=2,
        grid=(n_tiles,),
        in_specs=[
            pl.BlockSpec((None, 1, tm), lambda i, tg, nv: (i, 0, 0), memory_space=pltpu.SMEM),
            pl.BlockSpec((None, 1, tm), lambda i, tg, nv: (i, 0, 0), memory_space=pltpu.SMEM),
            pl.BlockSpec((None, 1, tm), lambda i, tg, nv: (jnp.minimum(i + 1, n_tiles - 1), 0, 0),
                         memory_space=pltpu.SMEM),
            pl.BlockSpec((tm, SUBLANES), lambda i, tg, nv: (i, 0)),
            pl.BlockSpec(memory_space=pl.ANY),
            pl.BlockSpec((EXPERTS_PER_GROUP, D_MODEL, D_FF), lambda i, tg, nv: (tg[i], 0, 0)),
            pl.BlockSpec((EXPERTS_PER_GROUP, D_MODEL, D_FF), lambda i, tg, nv: (tg[i], 0, 0)),
            pl.BlockSpec((EXPERTS_PER_GROUP, D_FF, D_MODEL), lambda i, tg, nv: (tg[i], 0, 0)),
            pl.BlockSpec((2, D_MODEL), lambda i, tg, nv: (0, 0)),
        ],
        out_specs=pl.BlockSpec(memory_space=pl.ANY),
        scratch_shapes=[pltpu.VMEM((2, tm, D_MODEL), F32), pltpu.VMEM((2, tm, D_MODEL), F32),
                        pltpu.SemaphoreType.DMA((2,)), pltpu.SemaphoreType.DMA((2,))],
    )
    return pl.pallas_call(
        functools.partial(_moe_kernel, tm=tm, n_tiles=n_tiles),
        grid_spec=grid_spec,
        out_shape=jax.ShapeDtypeStruct((t, D_MODEL), F32),
        compiler_params=pltpu.CompilerParams(dimension_semantics=("arbitrary",), vmem_limit_bytes=VMEM_LIMIT),
        name="moe",
    )(tile_grp, tile_valid, src3, dst3, src3, slot_meta, x, wg, wu, wd, ln)


def _s5_prep_kernel(are_ref, aim_ref, ldt_ref, bre_ref, bim_ref, abre_ref, abim_ref, bbre_ref, bbim_ref):
    a_r = are_ref[...]
    a_i = aim_ref[...]
    dt = jnp.exp(ldt_ref[...])
    mag = jnp.exp(dt * a_r)
    ab_re = mag * jnp.cos(dt * a_i)
    ab_im = mag * jnp.sin(dt * a_i)
    den = a_r * a_r + a_i * a_i
    nr = ab_re - 1.0
    z_re = (nr * a_r + ab_im * a_i) / den
    z_im = (ab_im * a_r - nr * a_i) / den
    abre_ref[...] = ab_re
    abim_ref[...] = ab_im
    bbre_ref[...] = z_re * bre_ref[...] - z_im * bim_ref[...]
    bbim_ref[...] = z_re * bim_ref[...] + z_im * bre_ref[...]


def _s5_prep(a_re, a_im, log_dt, b_re, b_im):
    n = S5_GROUPS * S5_STATE
    col = jax.ShapeDtypeStruct((n, 1), F32)
    mat = jax.ShapeDtypeStruct((n, S5_GROUP), F32)
    ldt = jnp.broadcast_to(log_dt[:, None], (S5_GROUPS, S5_STATE)).reshape(n, 1)
    return pl.pallas_call(_s5_prep_kernel, out_shape=[col, col, mat, mat], name="s5_prep")(
        a_re.reshape(n, 1), a_im.reshape(n, 1), ldt, b_re.reshape(n, S5_GROUP), b_im.reshape(n, S5_GROUP))


def _gelu_tanh(x):
    return 0.5 * x * (1.0 + jnp.tanh(0.7978845608028654 * (x + 0.044715 * (x * x * x))))


def _s5_kernel(*refs, nb, tt, n_t, n_cast):
    (x_ref, h0re_ref, h0im_ref, wre_ref, wim_ref, cre_ref, cim_ref, abre_ref, abim_ref, d_ref,
     wa_ref, wb_ref, ln_ref) = refs[:13]
    cast_in = refs[13:13 + n_cast]
    o_ref, hre_out, him_out = refs[13 + n_cast:16 + n_cast]
    cast_out = refs[16 + n_cast:16 + 2 * n_cast]
    bu_re, bu_im, hre_s, him_s, gy_ref, glu = refs[16 + 2 * n_cast:]
    t = pl.program_id(0)
    sw = wre_ref.shape[2]
    for src, dst in zip(cast_in, cast_out):
        dst[...] = _bf(src[...])

    @pl.when(t == 0)
    def _():
        hre_s[...] = h0re_ref[...]
        him_s[...] = h0im_ref[...]

    for k in range(D_MODEL // LANES):
        ls = slice(k * LANES, (k + 1) * LANES)
        ss = slice(k * sw, (k + 1) * sw)
        buf = k % 2
        xv = x_ref[:, ls]
        xb = _bf(xv)
        bu_re[buf] = jnp.dot(xb, wre_ref[k], preferred_element_type=F32)
        bu_im[buf] = jnp.dot(xb, wim_ref[k], preferred_element_type=F32)
        a_re = jnp.broadcast_to(abre_ref[:, ss], (nb, sw))
        a_im = jnp.broadcast_to(abim_ref[:, ss], (nb, sw))

        def step(s, carry, buf=buf, a_re=a_re, a_im=a_im):
            h_re, h_im = carry
            rows = pl.ds(pl.multiple_of(s * nb, nb), nb)
            n_re = a_re * h_re - a_im * h_im + bu_re[buf, rows, :]
            n_im = a_re * h_im + a_im * h_re + bu_im[buf, rows, :]
            bu_re[buf, rows, :] = n_re
            bu_im[buf, rows, :] = n_im
            return n_re, n_im

        h_re, h_im = lax.fori_loop(0, tt, step, (hre_s[:, ss], him_s[:, ss]), unroll=min(tt, S5_UNROLL))
        hre_s[:, ss] = h_re
        him_s[:, ss] = h_im
        y = (jnp.dot(_bf(bu_re[buf]), cre_ref[k], preferred_element_type=F32)
             - jnp.dot(_bf(bu_im[buf]), cim_ref[k], preferred_element_type=F32)
             + d_ref[:, ls] * xv)
        gy_ref[:, ls] = _gelu_tanh(y).astype(gy_ref.dtype)

    gy = gy_ref[...]
    width = 2 * LANES
    for n in range(D_MODEL // width):
        ns = slice(n * width, (n + 1) * width)
        a = jnp.dot(gy, wa_ref[:, ns], preferred_element_type=F32)
        b = jnp.dot(gy, wb_ref[:, ns], preferred_element_type=F32)
        glu[:, ns] = a * _sigmoid(b)
    o_ref[...] = _layer_norm(DN_ALPHA * x_ref[...] + glu[...], ln_ref[0:1, :], ln_ref[1:2, :])

    @pl.when(t == n_t - 1)
    def _():
        hre_out[...] = hre_s[...]
        him_out[...] = him_s[...]


def _s5_scan(x, nb, h0_re, h0_im, wre, wim, cre, cim, ab_re, ab_im, d_skip, wa, wb, ln, moe_w=None):
    length = x.shape[0] // nb
    tt = min(S5_TT, length)
    assert length % tt == 0 and nb % SUBLANES == 0
    n_t = length // tt
    n_k = D_MODEL // LANES
    sw = (LANES // S5_GROUP) * S5_STATE
    n_state = S5_GROUPS * S5_STATE
    const = lambda shape: pl.BlockSpec(shape, lambda t: (0,) * len(shape))
    if moe_w is None:
        w_ops, w_in, w_out, w_shapes, w_final = [], [], [], [], []
    else:
        w_ops, w_in, w_out, w_shapes, w_final = _expert_weight_cast_specs(moe_w, 1, n_t, lambda t: t)
    outs = pl.pallas_call(
        functools.partial(_s5_kernel, nb=nb, tt=tt, n_t=n_t, n_cast=len(w_ops)),
        grid=(n_t,),
        in_specs=[
            pl.BlockSpec((tt * nb, D_MODEL), lambda t: (t, 0)),
            const((nb, n_state)), const((nb, n_state)),
            const((n_k, LANES, sw)), const((n_k, LANES, sw)), const((n_k, sw, LANES)), const((n_k, sw, LANES)),
            const((1, n_state)), const((1, n_state)), const((1, D_MODEL)),
            const((D_MODEL, D_MODEL)), const((D_MODEL, D_MODEL)), const((2, D_MODEL)),
        ] + w_in,
        out_specs=[pl.BlockSpec((tt * nb, D_MODEL), lambda t: (t, 0)), const((nb, n_state)),
                   const((nb, n_state))] + w_out,
        out_shape=[
            jax.ShapeDtypeStruct((length * nb, D_MODEL), F32),
            jax.ShapeDtypeStruct((nb, n_state), F32),
            jax.ShapeDtypeStruct((nb, n_state), F32),
        ] + w_shapes,
        scratch_shapes=[
            pltpu.VMEM((2, nb * tt, sw), F32), pltpu.VMEM((2, nb * tt, sw), F32),
            pltpu.VMEM((nb, n_state), F32), pltpu.VMEM((nb, n_state), F32),
            pltpu.VMEM((nb * tt, D_MODEL), BF16), pltpu.VMEM((nb * tt, D_MODEL), F32),
        ],
        compiler_params=pltpu.CompilerParams(dimension_semantics=("arbitrary",), vmem_limit_bytes=VMEM_LIMIT),
        name="s5_glu",
    )(x, h0_re, h0_im, wre, wim, cre, cim, ab_re, ab_im, d_skip, wa, wb, ln, *w_ops)
    return outs[:3], [o.reshape(s) for o, s in zip(outs[3:], w_final)]


def _block_diag_slices(m, rows_per_group, cols_per_group):
    gps = LANES // S5_GROUP
    m = m.reshape(S5_GROUPS // gps, gps, rows_per_group, cols_per_group)
    eye = jnp.eye(gps, dtype=m.dtype)
    out = m[:, :, :, None, :] * eye[None, :, None, :, None]
    return out.reshape(S5_GROUPS // gps, gps * rows_per_group, gps * cols_per_group)


def _prepare(p):
    w = p['w_in'][0]
    win = jnp.concatenate(
        [w[:, 0:1536], w[:, 1544:2056], w[:, 2056:3592], w[:, 3600:4112], w[:, 1536:1544], w[:, 3592:3600],
         jnp.zeros((D_MODEL, D_IN_PAD - 4112), w.dtype)], axis=1).astype(BF16)
    at_lane = lambda v, lane0: jnp.pad(v, (lane0, LANES - lane0 - v.shape[0]))
    gp = jnp.stack([at_lane(p['gdn_A_log'][0], G_DEC), at_lane(p['gdn_dt_bias'][0], G_DEC),
                    at_lane(p['ml_b_i'][0], G_IN), at_lane(p['ml_b_f'][0], G_FG),
                    p['gdn_norm_w'][0], p['ml_norm_w'][0], jnp.zeros((LANES,), F32), jnp.zeros((LANES,), F32)])
    ab_re, ab_im, bb_re, bb_im = _s5_prep(p['s5_A_re'][0], p['s5_A_im'][0], p['s5_log_dt'][0],
                                          p['s5_B_re'][0], p['s5_B_im'][0])
    to_in = lambda bb: _block_diag_slices(
        bb.reshape(S5_GROUPS, S5_STATE, S5_GROUP).transpose(0, 2, 1), S5_GROUP, S5_STATE).astype(BF16)
    to_out = lambda c: _block_diag_slices(c.transpose(0, 2, 1), S5_STATE, S5_GROUP).astype(BF16)
    return dict(
        win=win, wout=p['w_out'][0].astype(BF16), convw=p['gdn_conv_w'][0], gp=gp,
        ln_mix=[jnp.stack([p['ln_mix_g'][l], p['ln_mix_b'][l]]) for l in range(DEPTH)],
        ln_ffn=[jnp.stack([p['ln_ffn_g'][l], p['ln_ffn_b'][l]]) for l in range(DEPTH)],
        rw_t=p['router_w'].T, rb_col=p['router_b'][:, None],
        s5_wre=to_in(bb_re), s5_wim=to_in(bb_im),
        s5_cre=to_out(p['s5_C_re'][0]), s5_cim=to_out(p['s5_C_im'][0]),
        s5_abre=ab_re.reshape(1, -1), s5_abim=ab_im.reshape(1, -1), s5_d=p['s5_D'][0][None, :],
        glu_a=p['s5_w_glu_a'][0].astype(BF16), glu_b=p['s5_w_glu_b'][0].astype(BF16),
    )


def _ffn(h, w, layer, out_row, experts=None, moe_w=None):
    meta, counts, cast = _router(h, w['rw_t'], w['rb_col'], moe_w if experts is None else None, layer)
    experts = cast if experts is None else experts
    return _moe(h, meta, counts, *experts, w['ln_ffn'][layer], out_row), experts


def _s5_layer(h, nb, h0_re, h0_im, w, moe_w=None):
    return _s5_scan(h, nb, h0_re, h0_im, w['s5_wre'], w['s5_wim'], w['s5_cre'], w['s5_cim'],
                    w['s5_abre'], w['s5_abim'], w['s5_d'], w['glu_a'], w['glu_b'], w['ln_mix'][1], moe_w)


def kernel(x_prompt, x_sample, state_gdn_conv, state_gdn_S, state_mlstm_C, state_mlstm_n, state_mlstm_m,
           state_s5_re, state_s5_im, w_in, gdn_conv_w, gdn_A_log, gdn_dt_bias, gdn_norm_w, ml_b_i, ml_b_f,
           ml_norm_w, w_out, s5_A_re, s5_A_im, s5_log_dt, s5_B_re, s5_B_im, s5_C_re, s5_C_im, s5_D,
           s5_w_glu_a, s5_w_glu_b, router_w, router_b, moe_w_gate, moe_w_up, moe_w_down,
           ln_mix_g, ln_mix_b, ln_ffn_g, ln_ffn_b):
    w = _prepare(dict(
        w_in=w_in, gdn_conv_w=gdn_conv_w, gdn_A_log=gdn_A_log, gdn_dt_bias=gdn_dt_bias, gdn_norm_w=gdn_norm_w,
        ml_b_i=ml_b_i, ml_b_f=ml_b_f, ml_norm_w=ml_norm_w, w_out=w_out, s5_A_re=s5_A_re, s5_A_im=s5_A_im,
        s5_log_dt=s5_log_dt, s5_B_re=s5_B_re, s5_B_im=s5_B_im, s5_C_re=s5_C_re, s5_C_im=s5_C_im, s5_D=s5_D,
        s5_w_glu_a=s5_w_glu_a, s5_w_glu_b=s5_w_glu_b, router_w=router_w, router_b=router_b,
        moe_w_gate=moe_w_gate, moe_w_up=moe_w_up, moe_w_down=moe_w_down,
        ln_mix_g=ln_mix_g, ln_mix_b=ln_mix_b, ln_ffn_g=ln_ffn_g, ln_ffn_b=ln_ffn_b))
    bp, lp, _ = x_prompt.shape
    bs, ls, _ = x_sample.shape
    assert ls == 1
    n_state = S5_GROUPS * S5_STATE

    moe_w = (moe_w_gate, moe_w_up, moe_w_down)
    (hs, s_cbuf, s_s, s_c, s_n, s_m), experts0 = _ab_decode(
        x_sample[:, 0], w['win'], w['wout'], w['convw'], w['gp'], w['ln_mix'][0],
        state_gdn_conv.reshape(bs, (CONV_W - 1) * A_CONV), state_gdn_S[:, 0], state_mlstm_C[:, 0],
        state_mlstm_n.reshape(bs, N_HEADS * HEAD_DIM), state_mlstm_m[:, 0], moe_w)

    h, p_hist, p_s, p_c, p_n, p_m = _ab_prompt(x_prompt, w['win'], w['wout'], w['convw'], w['gp'], w['ln_mix'][0])
    h = _ffn(h.reshape(bp * lp, D_MODEL), w, 0, lambda r: (r % lp) * bp + r // lp, experts=experts0)[0]
    zeros = jnp.zeros((bp, n_state), F32)
    (h, p_re, p_im), experts1 = _s5_layer(h, bp, zeros, zeros, w, moe_w)
    y_prompt = _ffn(h, w, 1, lambda r: (r % bp) * lp + r // bp, experts=experts1)[0].reshape(bp, lp, D_MODEL)

    hs = _ffn(hs, w, 0, lambda r: r, experts=experts0)[0]
    (hs, s_re, s_im), _ = _s5_layer(hs, bs, state_s5_re.reshape(bs, n_state), state_s5_im.reshape(bs, n_state), w)
    y_sample = _ffn(hs, w, 1, lambda r: r, experts=experts1)[0].reshape(bs, 1, D_MODEL)

    grp = lambda a, n: a.reshape(n, 1, S5_GROUPS, S5_STATE)
    return (
        y_prompt, y_sample,
        p_hist[:, None, SUBLANES - (CONV_W - 1):, :], p_s[:, None], p_c[:, None],
        p_n[:, None, :, 0, :], p_m[:, None, :, 0, 0], grp(p_re, bp), grp(p_im, bp),
        s_cbuf.reshape(bs, 1, CONV_W - 1, A_CONV), s_s[:, None], s_c[:, None],
        s_n.reshape(bs, 1, N_HEADS, HEAD_DIM), s_m[:, None], grp(s_re, bs), grp(s_im, bs),
    )
```

```python
import functools

import jax
import jax.numpy as jnp
from jax import lax
from jax.experimental import pallas as pl
from jax.experimental.pallas import tpu as pltpu

F32 = jnp.float32
BF16 = jnp.bfloat16
HIGHEST = lax.Precision.HIGHEST

D_MODEL = 1024
DEPTH = 2
N_HEADS = 4
HEAD_DIM = 128
CONV_W = 4
CHUNK = 64
A_CONV = 3 * N_HEADS * HEAD_DIM
S5_GROUP = 16
S5_GROUPS = D_MODEL // S5_GROUP
S5_STATE = 64
N_EXPERTS = 16
EXPERTS_PER_GROUP = 4
N_EXPERT_GROUPS = N_EXPERTS // EXPERTS_PER_GROUP
D_FF = 512
DN_ALPHA = (2 * DEPTH) ** 0.25
LN_EPS = 1e-5
RMS_EPS = 1e-6
NEG_BIG = -1e30

QA, KA, VA, ZA, QB, KB, VB, OB, GT = 0, 512, 1024, 1536, 2048, 2560, 3072, 3584, 4096
D_IN_PAD = 4224
G_DEC, G_BETA, G_IN, G_FG = 0, 4, 8, 12

S5_TT = 64
S5_UNROLL = 64

LANES = 128
SUBLANES = 8
VMEM_LIMIT = 56 * 1024 * 1024


def _bf(x):
    return x.astype(BF16)


def _nn(a, b):
    return jnp.dot(_bf(a), _bf(b), preferred_element_type=F32)


def _nt(a, b):
    return lax.dot_general(_bf(a), _bf(b), (((1,), (1,)), ((), ())), preferred_element_type=F32)


def _tn(a, b):
    return lax.dot_general(_bf(a), _bf(b), (((0,), (0,)), ((), ())), preferred_element_type=F32)


def _sigmoid(x):
    return 1.0 / (1.0 + jnp.exp(-x))


def _softplus(x):
    return jnp.maximum(x, 0.0) + jnp.log(1.0 + jnp.exp(-jnp.abs(x)))


def _silu(x):
    return x * _sigmoid(x)


def _layer_norm(y, g, b):
    mu = jnp.mean(y, axis=-1, keepdims=True)
    yc = y - mu
    var = jnp.mean(yc * yc, axis=-1, keepdims=True)
    return yc * lax.rsqrt(var + LN_EPS) * g + b


def _rms(x, w):
    return x * lax.rsqrt(jnp.mean(x * x, axis=-1, keepdims=True) + RMS_EPS) * w


def _gate_transform(raw, gp):
    lane = lax.broadcasted_iota(jnp.int32, raw.shape, 1)
    dec = -jnp.exp(gp[0:1, :]) * _softplus(raw + gp[1:2, :])
    beta = _sigmoid(raw)
    ipre = raw + gp[2:3, :]
    logf = -_softplus(-(raw + gp[3:4, :]))
    return jnp.where(lane < G_BETA, dec,
                     jnp.where(lane < G_IN, beta,
                               jnp.where(lane < G_FG, ipre,
                                         jnp.where(lane < G_FG + N_HEADS, logf, 0.0))))


def _bnn(a, b):
    return lax.dot_general(_bf(a), _bf(b), (((2,), (1,)), ((0,), (0,))), preferred_element_type=F32)


def _bnt(a, b):
    return lax.dot_general(_bf(a), _bf(b), (((2,), (2,)), ((0,), (0,))), preferred_element_type=F32)


def _btn(a, b):
    return lax.dot_general(_bf(a), _bf(b), (((1,), (1,)), ((0,), (0,))), preferred_element_type=F32)


def _unit_lower_inverse_minus_eye(a, pack):
    p, c, _ = a.shape
    nb = p // pack
    w = pack * c
    r = -jnp.stack([jnp.concatenate([a[b * pack + j] for j in range(pack)], axis=1) for b in range(nb)])
    assert c & (c - 1) == 0
    block_of = lambda dim: lax.shift_right_logical(lax.broadcasted_iota(jnp.int32, (w, w), dim), c.bit_length() - 1)
    on_diag = block_of(0) == block_of(1)

    def block_diag(x):
        return jnp.where(on_diag, jnp.concatenate([x] * pack, axis=1), 0.0)

    steps = max(1, (c - 1).bit_length()) - 1
    q = _bnn(r, block_diag(r))
    for i in range(steps):
        bd = block_diag(q)
        if i + 1 < steps:
            both = _bnn(jnp.concatenate([r, q], axis=1), bd)
            rq, qq = both[:, :c, :], both[:, c:, :]
        else:
            rq, qq = _bnn(r, bd), None
        r = r + q + rq
        q = qq
    return jnp.stack([r[b][:, j * c:(j + 1) * c] for b in range(nb) for j in range(pack)])


def _ab_prompt_kernel(x_ref, win_ref, wout_ref, convw_ref, gp_ref, ln_ref,
                      h_ref, hist_ref, s_out, c_out, n_out, m_out,
                      proj, qkv, gates, gcum, merged, s_s, c_s, n_s, m_s,
                      u_s, w_s, attn_s, dlog_s, qk_s, dmax_s, kt_s, kbt_s, rows_s, *, tb, n_t):
    t = pl.program_id(1)
    nc = tb // CHUNK

    @pl.when(t == 0)
    def _():
        proj[0:SUBLANES, :] = jnp.zeros((SUBLANES, D_IN_PAD), F32)
        s_s[...] = jnp.zeros_like(s_s)
        c_s[...] = jnp.zeros_like(c_s)
        n_s[...] = jnp.zeros_like(n_s)
        m_s[...] = jnp.zeros_like(m_s)

    x = x_ref[...]
    xb = _bf(x)
    proj[SUBLANES:SUBLANES + tb, 0:A_CONV] = jnp.dot(xb, win_ref[:, 0:A_CONV], preferred_element_type=F32)
    proj[SUBLANES:SUBLANES + tb, A_CONV:] = jnp.dot(xb, win_ref[:, A_CONV:], preferred_element_type=F32)

    for blk in range(A_CONV // LANES):
        cs = slice(blk * LANES, (blk + 1) * LANES)
        acc = proj[SUBLANES:SUBLANES + tb, cs] * convw_ref[CONV_W - 1:CONV_W, cs]
        for j in range(1, CONV_W):
            acc = acc + proj[SUBLANES - j:SUBLANES - j + tb, cs] * convw_ref[CONV_W - 1 - j:CONV_W - j, cs]
        y = _silu(acc)
        if blk < 2 * N_HEADS:
            y = y * lax.rsqrt(jnp.sum(y * y, axis=-1, keepdims=True) + RMS_EPS)
            if blk < N_HEADS:
                y = y * HEAD_DIM ** -0.5
        qkv[:, cs] = y

    gt = _gate_transform(proj[SUBLANES:SUBLANES + tb, GT:GT + LANES], gp_ref[...])
    gates[...] = gt
    ri = lax.broadcasted_iota(jnp.int32, (tb, tb), 0)
    ci = lax.broadcasted_iota(jnp.int32, (tb, tb), 1)
    same_chunk = lax.shift_right_logical(ri, 6) == lax.shift_right_logical(ci, 6)
    ltri = jnp.where(same_chunk, jnp.where(ri >= ci, 1.0, 0.0), 0.0)
    gcum[...] = jnp.dot(ltri, gt, preferred_element_type=F32, precision=HIGHEST)

    ii = lax.broadcasted_iota(jnp.int32, (CHUNK, CHUNK), 0)
    jj = lax.broadcasted_iota(jnp.int32, (CHUNK, CHUNK), 1)
    incl = ii >= jj
    strict = ii > jj
    gdn_w = gp_ref[4:5, :]
    ml_w = gp_ref[5:6, :]

    pairs = [(c, h) for c in range(nc) for h in range(N_HEADS)]

    def tile_heads(ref, row0, col0):
        return jnp.stack([ref[row0 + c * CHUNK:row0 + (c + 1) * CHUNK, col0 + h * HEAD_DIM:col0 + (h + 1) * HEAD_DIM]
                          for c, h in pairs])

    def tile_cols(ref, lane0):
        return jnp.stack([ref[c * CHUNK:(c + 1) * CHUNK, lane0 + h:lane0 + h + 1] for c, h in pairs])

    def tile_rows(transposed, lane0):
        return jnp.stack([transposed[c][lane0 + h:lane0 + h + 1, :] for c, h in pairs])

    cs_t = [gcum[c * CHUNK:(c + 1) * CHUNK, :].T for c in range(nc)]
    gt_t = [gates[c * CHUNK:(c + 1) * CHUNK, :].T for c in range(nc)]
    q3 = tile_heads(qkv, 0, QA)
    k3 = tile_heads(qkv, 0, KA)
    v3 = tile_heads(qkv, 0, VA)
    g_col3 = tile_cols(gcum, G_DEC)
    beta3 = tile_cols(gates, G_BETA)
    decay3 = jnp.where(incl, jnp.exp(jnp.where(incl, g_col3 - tile_rows(cs_t, G_DEC), 0.0)), 0.0)
    kb3 = k3 * beta3
    a_low3 = jnp.where(strict, _bnt(kb3, k3) * decay3, 0.0)
    attn_s[...] = _bnt(q3, k3) * decay3
    kbm3 = tile_heads(proj, SUBLANES, KB) * HEAD_DIM ** -0.5
    qk_s[...] = _bnt(tile_heads(proj, SUBLANES, QB), kbm3)
    kt_s[...] = jnp.stack([k3[p].T for p in range(len(pairs))])
    kbt_s[...] = jnp.stack([kbm3[p].T for p in range(len(pairs))])
    zero_row = jnp.zeros((1, CHUNK), F32)
    rows_s[...] = jnp.stack([jnp.concatenate(
        [cs_t[c][G_DEC + h:G_DEC + h + 1, :], cs_t[c][G_FG + h:G_FG + h + 1, :], gt_t[c][G_IN + h:G_IN + h + 1, :]]
        + [zero_row] * (SUBLANES - 3), axis=0) for c, h in pairs])
    r3 = _unit_lower_inverse_minus_eye(a_low3, N_HEADS)
    rhs3 = jnp.concatenate([v3 * beta3, kb3 * jnp.exp(g_col3)], axis=2)
    uw3 = rhs3 + _bnn(r3, rhs3)
    u_s[...] = uw3[:, :, :HEAD_DIM]
    w_s[...] = uw3[:, :, HEAD_DIM:]
    dlog3 = jnp.where(incl, tile_cols(gcum, G_FG) - tile_rows(cs_t, G_FG) + tile_rows(gt_t, G_IN), NEG_BIG)
    dlog_s[...] = dlog3
    dmax_s[...] = jnp.max(dlog3, axis=-1, keepdims=True)

    def chunk_body(c, carry):
        r0 = pl.multiple_of(c * CHUNK, CHUNK)
        rows = pl.ds(r0, CHUNK)
        prow = pl.ds(pl.multiple_of(r0 + SUBLANES, SUBLANES), CHUNK)
        last = pl.ds(r0 + CHUNK - 1, 1)
        pc = pl.ds(pl.multiple_of(c * N_HEADS, N_HEADS), N_HEADS)

        def heads(ref, rws, col0):
            return jnp.stack([ref[rws, col0 + h * HEAD_DIM:col0 + (h + 1) * HEAD_DIM] for h in range(N_HEADS)])

        def cols(ref, rws, lane0):
            return jnp.stack([ref[rws, lane0 + h:lane0 + h + 1] for h in range(N_HEADS)])

        q = heads(qkv, rows, QA)
        g_col = cols(gcum, rows, G_DEC)
        g_last = cols(gcum, last, G_DEC)
        qb = heads(proj, prow, QB)
        kbm = heads(proj, prow, KB) * HEAD_DIM ** -0.5
        vbm = heads(proj, prow, VB)
        b_col = cols(gcum, rows, G_FG)
        b_last = cols(gcum, last, G_FG)
        row_forms = rows_s[pc]
        g_row = row_forms[:, 0:1, :]
        b_row = row_forms[:, 1:2, :]
        i_row = row_forms[:, 2:3, :]
        s_old = s_s[...]
        c_old = c_s[...]
        n_old = n_s[...]
        m_old = m_s[:, 0:1, 0:1]
        inter = b_col + m_old
        mt = jnp.maximum(inter, dmax_s[pc])
        wts = jnp.exp(dlog_s[pc] - mt) * qk_s[pc]
        sc = jnp.exp(inter - mt)
        m_new = mt[:, CHUNK - 1:CHUNK, :]
        sd = jnp.exp(b_last + m_old - m_new)
        wk_row = jnp.exp(b_last - b_row + i_row - m_new)
        w_state = _bnn(w_s[pc], s_old)
        q_state = _bnn(q * jnp.exp(g_col), s_old)
        q_mem = _bnn(qb, c_old)
        w_val = _bnn(wts, vbm)
        kv = _bnn(kbt_s[pc] * wk_row, vbm)
        k_sum = _bnn(jnp.broadcast_to(wk_row, (N_HEADS, SUBLANES, CHUNK)), kbm)
        v_new = u_s[pc] - w_state
        o_a = q_state + _bnn(attn_s[pc], v_new)
        s_s[...] = s_old * jnp.exp(g_last) + _bnn(kt_s[pc] * jnp.exp(g_last - g_row), v_new)
        num = sc * q_mem + w_val
        den = sc * jnp.sum(qb * n_old[:, 0:1, :], axis=-1, keepdims=True) + jnp.sum(wts, axis=-1, keepdims=True)
        h_b = num / jnp.maximum(jnp.abs(den), jnp.exp(-mt))
        c_s[...] = sd * c_old + kv
        n_s[...] = sd * n_old + k_sum
        m_s[...] = jnp.broadcast_to(m_new, m_s.shape)
        o_n = _rms(o_a, gdn_w)
        h_n = _rms(h_b, ml_w)
        for h in range(N_HEADS):
            hs = slice(h * HEAD_DIM, (h + 1) * HEAD_DIM)
            hs2 = slice((N_HEADS + h) * HEAD_DIM, (N_HEADS + h + 1) * HEAD_DIM)
            merged[rows, hs] = o_n[h] * _silu(proj[prow, ZA + h * HEAD_DIM:ZA + (h + 1) * HEAD_DIM])
            merged[rows, hs2] = h_n[h] * _sigmoid(proj[prow, OB + h * HEAD_DIM:OB + (h + 1) * HEAD_DIM])
        return carry

    lax.fori_loop(0, nc, chunk_body, 0)

    mix = jnp.dot(_bf(merged[...]), wout_ref[...], preferred_element_type=F32)
    h_ref[...] = _layer_norm(DN_ALPHA * x + mix, ln_ref[0:1, :], ln_ref[1:2, :])

    proj[0:SUBLANES, 0:A_CONV] = proj[tb:tb + SUBLANES, 0:A_CONV]

    @pl.when(t == n_t - 1)
    def _():
        hist_ref[...] = proj[tb:tb + SUBLANES, 0:A_CONV]
        s_out[...] = s_s[...]
        c_out[...] = c_s[...]
        n_out[...] = n_s[...]
        m_out[...] = m_s[...]


def _expert_weight_cast_specs(moe_w, layer, n_steps, step_of):
    ops, in_specs, out_specs, out_shapes, shapes = [], [], [], [], []
    for w in moe_w:
        _, n_e, rows, cols = w.shape
        per_layer = n_e * rows
        blk = per_layer // n_steps
        assert per_layer % n_steps == 0 and blk % (2 * SUBLANES) == 0
        ops.append(w.reshape(w.shape[0] * per_layer, cols))
        in_specs.append(pl.BlockSpec((blk, cols), lambda *g: (layer * n_steps + step_of(*g), 0)))
        out_specs.append(pl.BlockSpec((blk, cols), lambda *g: (step_of(*g), 0)))
        out_shapes.append(jax.ShapeDtypeStruct((per_layer, cols), BF16))
        shapes.append((n_e, rows, cols))
    return ops, in_specs, out_specs, out_shapes, shapes


def _ab_prompt(x, win, wout, convw, gp, ln):
    bsz, length, _ = x.shape
    tb = min(512, length)
    assert length % tb == 0 and tb % CHUNK == 0 and length >= SUBLANES
    n_t = length // tb
    n_pairs = (tb // CHUNK) * N_HEADS
    const = lambda shape: pl.BlockSpec(shape, lambda b, t: (0,) * len(shape))
    per_b = lambda shape: pl.BlockSpec((None,) + shape, lambda b, t: (b,) + (0,) * len(shape))
    return pl.pallas_call(
        functools.partial(_ab_prompt_kernel, tb=tb, n_t=n_t),
        grid=(bsz, n_t),
        in_specs=[
            pl.BlockSpec((None, tb, D_MODEL), lambda b, t: (b, t, 0)),
            const((D_MODEL, D_IN_PAD)), const((2 * N_HEADS * HEAD_DIM, D_MODEL)),
            const((CONV_W, A_CONV)), const((SUBLANES, LANES)), const((2, D_MODEL)),
        ],
        out_specs=[
            pl.BlockSpec((None, tb, D_MODEL), lambda b, t: (b, t, 0)),
            per_b((SUBLANES, A_CONV)), per_b((N_HEADS, HEAD_DIM, HEAD_DIM)), per_b((N_HEADS, HEAD_DIM, HEAD_DIM)),
            per_b((N_HEADS, SUBLANES, LANES)), per_b((N_HEADS, SUBLANES, LANES)),
        ],
        out_shape=[
            jax.ShapeDtypeStruct((bsz, length, D_MODEL), F32),
            jax.ShapeDtypeStruct((bsz, SUBLANES, A_CONV), F32),
            jax.ShapeDtypeStruct((bsz, N_HEADS, HEAD_DIM, HEAD_DIM), F32),
            jax.ShapeDtypeStruct((bsz, N_HEADS, HEAD_DIM, HEAD_DIM), F32),
            jax.ShapeDtypeStruct((bsz, N_HEADS, SUBLANES, LANES), F32),
            jax.ShapeDtypeStruct((bsz, N_HEADS, SUBLANES, LANES), F32),
        ],
        scratch_shapes=[
            pltpu.VMEM((tb + SUBLANES, D_IN_PAD), F32),
            pltpu.VMEM((tb, A_CONV), F32),
            pltpu.VMEM((tb, LANES), F32),
            pltpu.VMEM((tb, LANES), F32),
            pltpu.VMEM((tb, 2 * N_HEADS * HEAD_DIM), F32),
            pltpu.VMEM((N_HEADS, HEAD_DIM, HEAD_DIM), F32),
            pltpu.VMEM((N_HEADS, HEAD_DIM, HEAD_DIM), F32),
            pltpu.VMEM((N_HEADS, SUBLANES, LANES), F32),
            pltpu.VMEM((N_HEADS, SUBLANES, LANES), F32),
            pltpu.VMEM((n_pairs, CHUNK, HEAD_DIM), F32),
            pltpu.VMEM((n_pairs, CHUNK, HEAD_DIM), F32),
            pltpu.VMEM((n_pairs, CHUNK, CHUNK), F32),
            pltpu.VMEM((n_pairs, CHUNK, CHUNK), F32),
            pltpu.VMEM((n_pairs, CHUNK, CHUNK), F32),
            pltpu.VMEM((n_pairs, CHUNK, 1), F32),
            pltpu.VMEM((n_pairs, HEAD_DIM, CHUNK), F32),
            pltpu.VMEM((n_pairs, HEAD_DIM, CHUNK), F32),
            pltpu.VMEM((n_pairs, SUBLANES, CHUNK), F32),
        ],
        compiler_params=pltpu.CompilerParams(
            dimension_semantics=("arbitrary", "arbitrary"), vmem_limit_bytes=VMEM_LIMIT),
        name="ab_prompt",
    )(x, win, wout, convw, gp, ln)


def _ab_decode_kernel(*refs, bb, n_steps, n_cast):
    x_ref, win_ref, wout_ref, convw_ref, gp_ref, ln_ref, cbuf_ref, s_in, c_in, n_in, m_in = refs[:11]
    y_ref, cbuf_out, s_out, c_out, n_out, m_out = refs[11 + n_cast:17 + n_cast]
    proj, merged = refs[17 + 2 * n_cast:]
    for src, dst in zip(refs[11:11 + n_cast], refs[17 + n_cast:17 + 2 * n_cast]):
        dst[...] = _bf(src[...])
    i = pl.program_id(0)

    @pl.when(i == 0)
    def _():
        proj[...] = jnp.dot(_bf(x_ref[...]), win_ref[...], preferred_element_type=F32)

    rows = pl.ds(pl.multiple_of(i * bb, bb), bb)
    raw = proj[rows, 0:A_CONV]
    cbuf = cbuf_ref[...]
    conv = raw * convw_ref[CONV_W - 1:CONV_W, :]
    for j in range(CONV_W - 1):
        conv = conv + cbuf[:, j * A_CONV:(j + 1) * A_CONV] * convw_ref[j:j + 1, :]
    cbuf_out[:, 0:(CONV_W - 2) * A_CONV] = cbuf[:, A_CONV:(CONV_W - 1) * A_CONV]
    cbuf_out[:, (CONV_W - 2) * A_CONV:(CONV_W - 1) * A_CONV] = raw
    act = _silu(conv)
    gt = _gate_transform(proj[rows, GT:GT + LANES], gp_ref[...])
    gdn_w = gp_ref[4:5, :]
    ml_w = gp_ref[5:6, :]
    m_all = m_in[...]

    for h in range(N_HEADS):
        def head(off):
            return act[:, off + h * HEAD_DIM:off + (h + 1) * HEAD_DIM]
        q = head(QA)
        q = q * lax.rsqrt(jnp.sum(q * q, axis=-1, keepdims=True) + RMS_EPS) * HEAD_DIM ** -0.5
        k = head(KA)
        k = k * lax.rsqrt(jnp.sum(k * k, axis=-1, keepdims=True) + RMS_EPS)
        v = head(VA)
        q_t = q.T
        k_t = k.T
        qk = jnp.sum(q * k, axis=-1, keepdims=True)
        qb = proj[rows, QB + h * HEAD_DIM:QB + (h + 1) * HEAD_DIM]
        kbm = proj[rows, KB + h * HEAD_DIM:KB + (h + 1) * HEAD_DIM] * HEAD_DIM ** -0.5
        vbm = proj[rows, VB + h * HEAD_DIM:VB + (h + 1) * HEAD_DIM]
        qb_t = qb.T
        kb_t = kbm.T
        qkb = jnp.sum(qb * kbm, axis=-1, keepdims=True)
        n_old = n_in[:, h * HEAD_DIM:(h + 1) * HEAD_DIM]
        qn = jnp.sum(qb * n_old, axis=-1, keepdims=True)
        o_rows = []
        hb_rows = []
        n_rows = []
        m_rows = []
        for b in range(bb):
            s_old = s_in[b, h]
            k_c = k_t[:, b:b + 1]
            q_c = q_t[:, b:b + 1]
            e_g = jnp.exp(gt[b:b + 1, G_DEC + h:G_DEC + h + 1])
            beta = gt[b:b + 1, G_BETA + h:G_BETA + h + 1]
            k_s = jnp.sum(k_c * s_old, axis=0, keepdims=True)
            q_s = jnp.sum(q_c * s_old, axis=0, keepdims=True)
            v_new = v[b:b + 1, :] * beta - (beta * e_g) * k_s
            o_rows.append(e_g * q_s + qk[b:b + 1, :] * v_new)
            s_out[b, h] = s_old * e_g + k_c * v_new
            c_old = c_in[b, h]
            m_old = m_all[b:b + 1, h:h + 1]
            i_pre = gt[b:b + 1, G_IN + h:G_IN + h + 1]
            logf = gt[b:b + 1, G_FG + h:G_FG + h + 1]
            inter = logf + m_old
            mt = jnp.maximum(inter, i_pre)
            w_in = jnp.exp(i_pre - mt)
            sc = jnp.exp(inter - mt)
            wts = w_in * qkb[b:b + 1, :]
            q_cm = jnp.sum(qb_t[:, b:b + 1] * c_old, axis=0, keepdims=True)
            num = sc * q_cm + wts * vbm[b:b + 1, :]
            den = sc * qn[b:b + 1, :] + wts
            hb_rows.append(num / jnp.maximum(jnp.abs(den), jnp.exp(-mt)))
            c_out[b, h] = sc * c_old + (w_in * kb_t[:, b:b + 1]) * vbm[b:b + 1, :]
            n_rows.append(sc * n_old[b:b + 1, :] + w_in * kbm[b:b + 1, :])
            m_rows.append(mt)
        o_a = jnp.concatenate(o_rows, axis=0)
        h_b = jnp.concatenate(hb_rows, axis=0)
        n_out[:, h * HEAD_DIM:(h + 1) * HEAD_DIM] = jnp.concatenate(n_rows, axis=0)
        m_out[:, h:h + 1] = jnp.concatenate(m_rows, axis=0)
        z = proj[rows, ZA + h * HEAD_DIM:ZA + (h + 1) * HEAD_DIM]
        merged[rows, h * HEAD_DIM:(h + 1) * HEAD_DIM] = _rms(o_a, gdn_w) * _silu(z)
        o_gate = proj[rows, OB + h * HEAD_DIM:OB + (h + 1) * HEAD_DIM]
        merged[rows, N_HEADS * HEAD_DIM + h * HEAD_DIM:N_HEADS * HEAD_DIM + (h + 1) * HEAD_DIM] = (
            _rms(h_b, ml_w) * _sigmoid(o_gate))

    @pl.when(i == n_steps - 1)
    def _():
        mix = jnp.dot(_bf(merged[...]), wout_ref[...], preferred_element_type=F32)
        y_ref[...] = _layer_norm(DN_ALPHA * x_ref[...] + mix, ln_ref[0:1, :], ln_ref[1:2, :])


def _ab_decode(x, win, wout, convw, gp, ln, cbuf, s0, c0, n0, m0, moe_w):
    nb = x.shape[0]
    bb = SUBLANES
    assert nb % bb == 0
    n_steps = nb // bb
    const = lambda shape: pl.BlockSpec(shape, lambda i: (0,) * len(shape))
    blk = lambda shape: pl.BlockSpec((bb,) + shape, lambda i: (i,) + (0,) * len(shape))
    hist = (CONV_W - 1) * A_CONV
    width = N_HEADS * HEAD_DIM
    w_ops, w_in, w_out, w_shapes, w_final = _expert_weight_cast_specs(moe_w, 0, n_steps, lambda i: i)
    outs = pl.pallas_call(
        functools.partial(_ab_decode_kernel, bb=bb, n_steps=n_steps, n_cast=len(w_ops)),
        grid=(n_steps,),
        in_specs=[
            const((nb, D_MODEL)), const((D_MODEL, D_IN_PAD)), const((2 * width, D_MODEL)),
            const((CONV_W, A_CONV)), const((SUBLANES, LANES)), const((2, D_MODEL)),
            blk((hist,)), blk((N_HEADS, HEAD_DIM, HEAD_DIM)), blk((N_HEADS, HEAD_DIM, HEAD_DIM)),
            blk((width,)), blk((N_HEADS,)),
        ] + w_in,
        out_specs=[
            const((nb, D_MODEL)), blk((hist,)), blk((N_HEADS, HEAD_DIM, HEAD_DIM)),
            blk((N_HEADS, HEAD_DIM, HEAD_DIM)), blk((width,)), blk((N_HEADS,)),
        ] + w_out,
        out_shape=[
            jax.ShapeDtypeStruct((nb, D_MODEL), F32),
            jax.ShapeDtypeStruct((nb, hist), F32),
            jax.ShapeDtypeStruct((nb, N_HEADS, HEAD_DIM, HEAD_DIM), F32),
            jax.ShapeDtypeStruct((nb, N_HEADS, HEAD_DIM, HEAD_DIM), F32),
            jax.ShapeDtypeStruct((nb, width), F32),
            jax.ShapeDtypeStruct((nb, N_HEADS), F32),
        ] + w_shapes,
        scratch_shapes=[pltpu.VMEM((nb, D_IN_PAD), F32), pltpu.VMEM((nb, 2 * width), F32)],
        compiler_params=pltpu.CompilerParams(dimension_semantics=("arbitrary",), vmem_limit_bytes=VMEM_LIMIT),
        name="ab_decode",
    )(x, win, wout, convw, gp, ln, cbuf, s0, c0, n0, m0, *w_ops)
    return outs[:6], [o.reshape(s) for o, s in zip(outs[6:], w_final)]


def _second_largest_sum(a, b, c, d):
    hi1, lo1 = jnp.maximum(a, b), jnp.minimum(a, b)
    hi2, lo2 = jnp.maximum(c, d), jnp.minimum(c, d)
    return jnp.maximum(hi1, hi2) + jnp.maximum(jnp.minimum(hi1, hi2), jnp.maximum(lo1, lo2))


def _first_argmax(vals):
    best_v = vals[0]
    best_i = jnp.zeros(vals[0].shape, jnp.int32)
    for j in range(1, len(vals)):
        better = vals[j] > best_v
        best_v = jnp.where(better, vals[j], best_v)
        best_i = jnp.where(better, j, best_i)
    return best_i


def _router_kernel(*refs, n_steps, n_cast):
    x_ref, rw_ref, rb_ref = refs[:3]
    meta_ref, cnt_ref = refs[3 + n_cast:5 + n_cast]
    carry = refs[5 + 2 * n_cast]
    for src, dst in zip(refs[3:3 + n_cast], refs[5 + n_cast:5 + 2 * n_cast]):
        dst[...] = _bf(src[...])
    i = pl.program_id(0)
    tm = x_ref.shape[0]

    @pl.when(i == 0)
    def _():
        carry[...] = jnp.zeros_like(carry)

    x = x_ref[...]
    rw = rw_ref[...]
    x_hi = _bf(x)
    x_lo = _bf(x - x_hi.astype(F32))
    w_hi = _bf(rw)
    w_lo = _bf(rw - w_hi.astype(F32))
    nt = lambda a, b: lax.dot_general(a, b, (((1,), (1,)), ((), ())), preferred_element_type=F32)
    logits = nt(w_hi, x_hi) + (nt(w_hi, x_lo) + nt(w_lo, x_hi))
    ex = jnp.exp(logits - jnp.max(logits, axis=0, keepdims=True))
    probs = ex / jnp.sum(ex, axis=0, keepdims=True)
    sel = probs + rb_ref[...]
    p = [probs[j:j + 1, :] for j in range(N_EXPERTS)]
    s = [sel[j:j + 1, :] for j in range(N_EXPERTS)]
    scores = [_second_largest_sum(*s[EXPERTS_PER_GROUP * g:EXPERTS_PER_GROUP * (g + 1)])
              for g in range(N_EXPERT_GROUPS)]
    best = _first_argmax(scores)
    masked = [jnp.where(best == j // EXPERTS_PER_GROUP, s[j], -jnp.inf) for j in range(N_EXPERTS)]
    i1 = _first_argmax(masked)
    i2 = _first_argmax([jnp.where(i1 == j, -jnp.inf, masked[j]) for j in range(N_EXPERTS)])
    zero = jnp.zeros_like(p[0])
    p1 = functools.reduce(lambda a, b: a + b, [jnp.where(i1 == j, p[j], zero) for j in range(N_EXPERTS)])
    p2 = functools.reduce(lambda a, b: a + b, [jnp.where(i2 == j, p[j], zero) for j in range(N_EXPERTS)])
    tot = p1 + p2
    rows = [jnp.where(i1 == j, p1 / tot, zero) + jnp.where(i2 == j, p2 / tot, zero) for j in range(N_EXPERTS)]
    in_group = [best == g for g in range(N_EXPERT_GROUPS)]
    local = [functools.reduce(lambda a, b: a + b,
                              [jnp.where(in_group[g], rows[EXPERTS_PER_GROUP * g + e], zero)
                               for g in range(N_EXPERT_GROUPS)])
             for e in range(EXPERTS_PER_GROUP)]
    onehot = jnp.concatenate([jnp.where(m, 1.0, 0.0) for m in in_group]
                             + [jnp.zeros((SUBLANES - N_EXPERT_GROUPS, tm), F32)], axis=0)
    ri = lax.broadcasted_iota(jnp.int32, (tm, tm), 0)
    ci = lax.broadcasted_iota(jnp.int32, (tm, tm), 1)
    incl = jnp.dot(_bf(onehot), _bf(jnp.where(ri <= ci, 1.0, 0.0)), preferred_element_type=F32)
    prev = carry[...]
    rank = jnp.sum(onehot * (incl - 1.0 + prev[:, 0:1]), axis=0, keepdims=True)
    carry[...] = prev + incl[:, tm - 1:tm]
    row_id = (i * tm + lax.broadcasted_iota(jnp.int32, (1, tm), 1)).astype(F32)
    meta_ref[...] = jnp.concatenate([row_id] + local + [best.astype(F32), rank, zero], axis=0).T

    @pl.when(i == n_steps - 1)
    def _():
        cnt_ref[...] = carry[...]


def _router(x, rw_t, rb_col, moe_w=None, layer=0):
    t = x.shape[0]
    tm = min(256, t)
    assert t % tm == 0
    n_steps = t // tm
    if moe_w is None:
        w_ops, w_in, w_out, w_shapes, w_final = [], [], [], [], []
    else:
        w_ops, w_in, w_out, w_shapes, w_final = _expert_weight_cast_specs(moe_w, layer, n_steps, lambda i: i)
    outs = pl.pallas_call(
        functools.partial(_router_kernel, n_steps=n_steps, n_cast=len(w_ops)),
        grid=(n_steps,),
        in_specs=[pl.BlockSpec((tm, D_MODEL), lambda i: (i, 0)),
                  pl.BlockSpec((N_EXPERTS, D_MODEL), lambda i: (0, 0)),
                  pl.BlockSpec((N_EXPERTS, 1), lambda i: (0, 0))] + w_in,
        out_specs=[pl.BlockSpec((tm, SUBLANES), lambda i: (i, 0)),
                   pl.BlockSpec((SUBLANES, LANES), lambda i: (0, 0))] + w_out,
        out_shape=[jax.ShapeDtypeStruct((t, SUBLANES), F32), jax.ShapeDtypeStruct((SUBLANES, LANES), F32)] + w_shapes,
        scratch_shapes=[pltpu.VMEM((SUBLANES, LANES), F32)],
        compiler_params=pltpu.CompilerParams(dimension_semantics=("arbitrary",), vmem_limit_bytes=VMEM_LIMIT),
        name="router",
    )(x, rw_t, rb_col, *w_ops)
    return outs[0], outs[1], [o.reshape(s) for o, s in zip(outs[2:], w_final)]


def _moe_kernel(tg_ref, nv_ref, src_cur, dst_cur, src_nxt, cw_ref, x_hbm, wg_ref, wu_ref, wd_ref, ln_ref,
                y_hbm, xg, stage, gsem, ssem, *, tm, n_tiles):
    i = pl.program_id(0)
    slot = lax.rem(i, 2)
    other = 1 - slot
    n_valid = nv_ref[i]
    n_prev = jnp.where(i >= 1, nv_ref[jnp.maximum(i - 1, 0)], 0)
    n_back2 = jnp.where(i >= 2, nv_ref[jnp.maximum(i - 2, 0)], 0)
    has_next = i + 1 < n_tiles

    def gather_row(tok_ref, r, dst_slot):
        return pltpu.make_async_copy(x_hbm.at[pl.ds(tok_ref[0, r], 1), :], xg.at[dst_slot, pl.ds(r, 1), :],
                                     gsem.at[dst_slot])

    def scatter_row(tok_ref, r, src_slot):
        return pltpu.make_async_copy(stage.at[src_slot, pl.ds(r, 1), :], y_hbm.at[pl.ds(tok_ref[0, r], 1), :],
                                     ssem.at[src_slot])

    def scatter_wait(count, src_slot):
        for p in [1 << b for b in range(tm.bit_length())]:
            @pl.when(lax.bitwise_and(count, p) != 0)
            def _():
                pltpu.make_async_copy(stage.at[src_slot, pl.ds(0, p), :], y_hbm.at[pl.ds(0, p), :],
                                      ssem.at[src_slot]).wait()

    @pl.when(i == 0)
    def _():
        def body(r, c):
            gather_row(src_cur, r, 0).start()
            return c
        lax.fori_loop(0, tm, body, 0, unroll=8)

    pltpu.make_async_copy(x_hbm.at[pl.ds(0, tm), :], xg.at[slot], gsem.at[slot]).wait()

    for parity in range(2):
        @pl.when(jnp.logical_and(has_next, slot == parity))
        def _():
            for r in range(tm):
                gather_row(src_nxt, r, 1 - parity).start(priority=r % 2)

    @pl.when(n_valid > 0)
    def _():
        x = xg[slot]
        xb = _bf(x)
        cw = cw_ref[...]
        acc = jnp.zeros((tm, D_MODEL), F32)
        for e in range(EXPERTS_PER_GROUP):
            g = jnp.dot(xb, wg_ref[e], preferred_element_type=F32)
            u = jnp.dot(xb, wu_ref[e], preferred_element_type=F32)
            hid = _silu(g) * u * cw[:, 1 + e:2 + e]
            acc = acc + jnp.dot(_bf(hid), wd_ref[e], preferred_element_type=F32)
        y = _layer_norm(DN_ALPHA * x + acc, ln_ref[0:1, :], ln_ref[1:2, :])

        @pl.when(i >= 2)
        def _():
            scatter_wait(n_back2, slot)
        stage[slot] = y

    @pl.when(jnp.logical_and(n_valid <= 0, i >= 2))
    def _():
        scatter_wait(n_back2, slot)

    for parity in range(2):
        @pl.when(jnp.logical_and(n_valid == tm, slot == parity))
        def _():
            for r in range(tm):
                scatter_row(dst_cur, r, parity).start(priority=r % 2)

    @pl.when(n_valid < tm)
    def _():
        for r in range(tm):
            @pl.when(r < n_valid)
            def _():
                scatter_row(dst_cur, r, slot).start(priority=r % 2)

    @pl.when(i == n_tiles - 1)
    def _():
        scatter_wait(n_prev, other)
        scatter_wait(n_valid, slot)


def _moe(x, meta, counts, wg, wu, wd, ln, out_row):
    t = x.shape[0]
    tm = min(512, t)
    assert t % tm == 0 and tm & (tm - 1) == 0
    n_tiles = t // tm + N_EXPERT_GROUPS
    n_slots = n_tiles * tm
    grp = meta[:, 1 + EXPERTS_PER_GROUP].astype(jnp.int32)
    rank = meta[:, 2 + EXPERTS_PER_GROUP].astype(jnp.int32)
    cnt = counts[:N_EXPERT_GROUPS, 0].astype(jnp.int32)
    padded = ((cnt + tm - 1) // tm) * tm
    g_end = jnp.cumsum(padded)
    g_off = g_end - padded
    pos = jnp.take(g_off, grp) + rank
    slot_meta = jnp.zeros((n_slots, SUBLANES), F32).at[pos].set(meta)
    tok_of_slot = slot_meta[:, 0].astype(jnp.int32)
    tile_start = jnp.arange(n_tiles, dtype=jnp.int32) * tm
    tile_grp = jnp.minimum(jnp.sum(tile_start[:, None] >= g_end[None, :], axis=1), N_EXPERT_GROUPS - 1).astype(jnp.int32)
    tile_valid = jnp.clip(jnp.take(g_off + cnt, tile_grp) - tile_start, 0, tm).astype(jnp.int32)
    src3 = tok_of_slot.reshape(n_tiles, 1, tm)
    dst3 = out_row(tok_of_slot).reshape(n_tiles, 1, tm)

    grid_spec = pltpu.PrefetchScalarGridSpec(
        num_scalar_prefetch=2,
        grid=(n_tiles,),
        in_specs=[
            pl.BlockSpec((None, 1, tm), lambda i, tg, nv: (i, 0, 0), memory_space=pltpu.SMEM),
            pl.BlockSpec((None, 1, tm), lambda i, tg, nv: (i, 0, 0), memory_space=pltpu.SMEM),
            pl.BlockSpec((None, 1, tm), lambda i, tg, nv: (jnp.minimum(i + 1, n_tiles - 1), 0, 0),
                         memory_space=pltpu.SMEM),
            pl.BlockSpec((tm, SUBLANES), lambda i, tg, nv: (i, 0)),
            pl.BlockSpec(memory_space=pl.ANY),
            pl.BlockSpec((EXPERTS_PER_GROUP, D_MODEL, D_FF), lambda i, tg, nv: (tg[i], 0, 0)),
            pl.BlockSpec((EXPERTS_PER_GROUP, D_MODEL, D_FF), lambda i, tg, nv: (tg[i], 0, 0)),
            pl.BlockSpec((EXPERTS_PER_GROUP, D_FF, D_MODEL), lambda i, tg, nv: (tg[i], 0, 0)),
            pl.BlockSpec((2, D_MODEL), lambda i, tg, nv: (0, 0)),
        ],
        out_specs=pl.BlockSpec(memory_space=pl.ANY),
        scratch_shapes=[pltpu.VMEM((2, tm, D_MODEL), F32), pltpu.VMEM((2, tm, D_MODEL), F32),
                        pltpu.SemaphoreType.DMA((2,)), pltpu.SemaphoreType.DMA((2,))],
    )
    return pl.pallas_call(
        functools.partial(_moe_kernel, tm=tm, n_tiles=n_tiles),
        grid_spec=grid_spec,
        out_shape=jax.ShapeDtypeStruct((t, D_MODEL), F32),
        compiler_params=pltpu.CompilerParams(dimension_semantics=("arbitrary",), vmem_limit_bytes=VMEM_LIMIT),
        name="moe",
    )(tile_grp, tile_valid, src3, dst3, src3, slot_meta, x, wg, wu, wd, ln)


def _s5_prep_kernel(are_ref, aim_ref, ldt_ref, bre_ref, bim_ref, abre_ref, abim_ref, bbre_ref, bbim_ref):
    a_r = are_ref[...]
    a_i = aim_ref[...]
    dt = jnp.exp(ldt_ref[...])
    mag = jnp.exp(dt * a_r)
    ab_re = mag * jnp.cos(dt * a_i)
    ab_im = mag * jnp.sin(dt * a_i)
    den = a_r * a_r + a_i * a_i
    nr = ab_re - 1.0
    z_re = (nr * a_r + ab_im * a_i) / den
    z_im = (ab_im * a_r - nr * a_i) / den
    abre_ref[...] = ab_re
    abim_ref[...] = ab_im
    bbre_ref[...] = z_re * bre_ref[...] - z_im * bim_ref[...]
    bbim_ref[...] = z_re * bim_ref[...] + z_im * bre_ref[...]


def _s5_prep(a_re, a_im, log_dt, b_re, b_im):
    n = S5_GROUPS * S5_STATE
    col = jax.ShapeDtypeStruct((n, 1), F32)
    mat = jax.ShapeDtypeStruct((n, S5_GROUP), F32)
    ldt = jnp.broadcast_to(log_dt[:, None], (S5_GROUPS, S5_STATE)).reshape(n, 1)
    return pl.pallas_call(_s5_prep_kernel, out_shape=[col, col, mat, mat], name="s5_prep")(
        a_re.reshape(n, 1), a_im.reshape(n, 1), ldt, b_re.reshape(n, S5_GROUP), b_im.reshape(n, S5_GROUP))


def _gelu_tanh(x):
    return 0.5 * x * (1.0 + jnp.tanh(0.7978845608028654 * (x + 0.044715 * (x * x * x))))


def _s5_kernel(*refs, nb, tt, n_t, n_cast):
    (x_ref, h0re_ref, h0im_ref, wre_ref, wim_ref, cre_ref, cim_ref, abre_ref, abim_ref, d_ref,
     wa_ref, wb_ref, ln_ref) = refs[:13]
    cast_in = refs[13:13 + n_cast]
    o_ref, hre_out, him_out = refs[13 + n_cast:16 + n_cast]
    cast_out = refs[16 + n_cast:16 + 2 * n_cast]
    bu_re, bu_im, hre_s, him_s, gy_ref, glu = refs[16 + 2 * n_cast:]
    t = pl.program_id(0)
    sw = wre_ref.shape[2]
    for src, dst in zip(cast_in, cast_out):
        dst[...] = _bf(src[...])

    @pl.when(t == 0)
    def _():
        hre_s[...] = h0re_ref[...]
        him_s[...] = h0im_ref[...]

    for k in range(D_MODEL // LANES):
        ls = slice(k * LANES, (k + 1) * LANES)
        ss = slice(k * sw, (k + 1) * sw)
        buf = k % 2
        xv = x_ref[:, ls]
        xb = _bf(xv)
        bu_re[buf] = jnp.dot(xb, wre_ref[k], preferred_element_type=F32)
        bu_im[buf] = jnp.dot(xb, wim_ref[k], preferred_element_type=F32)
        a_re = jnp.broadcast_to(abre_ref[:, ss], (nb, sw))
        a_im = jnp.broadcast_to(abim_ref[:, ss], (nb, sw))

        def step(s, carry, buf=buf, a_re=a_re, a_im=a_im):
            h_re, h_im = carry
            rows = pl.ds(pl.multiple_of(s * nb, nb), nb)
            n_re = a_re * h_re - a_im * h_im + bu_re[buf, rows, :]
            n_im = a_re * h_im + a_im * h_re + bu_im[buf, rows, :]
            bu_re[buf, rows, :] = n_re
            bu_im[buf, rows, :] = n_im
            return n_re, n_im

        h_re, h_im = lax.fori_loop(0, tt, step, (hre_s[:, ss], him_s[:, ss]), unroll=min(tt, S5_UNROLL))
        hre_s[:, ss] = h_re
        him_s[:, ss] = h_im
        y = (jnp.dot(_bf(bu_re[buf]), cre_ref[k], preferred_element_type=F32)
             - jnp.dot(_bf(bu_im[buf]), cim_ref[k], preferred_element_type=F32)
             + d_ref[:, ls] * xv)
        gy_ref[:, ls] = _gelu_tanh(y).astype(gy_ref.dtype)

    gy = gy_ref[...]
    width = 2 * LANES
    for n in range(D_MODEL // width):
        ns = slice(n * width, (n + 1) * width)
        a = jnp.dot(gy, wa_ref[:, ns], preferred_element_type=F32)
        b = jnp.dot(gy, wb_ref[:, ns], preferred_element_type=F32)
        glu[:, ns] = a * _sigmoid(b)
    o_ref[...] = _layer_norm(DN_ALPHA * x_ref[...] + glu[...], ln_ref[0:1, :], ln_ref[1:2, :])

    @pl.when(t == n_t - 1)
    def _():
        hre_out[...] = hre_s[...]
        him_out[...] = him_s[...]


def _s5_scan(x, nb, h0_re, h0_im, wre, wim, cre, cim, ab_re, ab_im, d_skip, wa, wb, ln, moe_w=None):
    length = x.shape[0] // nb
    tt = min(S5_TT, length)
    assert length % tt == 0 and nb % SUBLANES == 0
    n_t = length // tt
    n_k = D_MODEL // LANES
    sw = (LANES // S5_GROUP) * S5_STATE
    n_state = S5_GROUPS * S5_STATE
    const = lambda shape: pl.BlockSpec(shape, lambda t: (0,) * len(shape))
    if moe_w is None:
        w_ops, w_in, w_out, w_shapes, w_final = [], [], [], [], []
    else:
        w_ops, w_in, w_out, w_shapes, w_final = _expert_weight_cast_specs(moe_w, 1, n_t, lambda t: t)
    outs = pl.pallas_call(
        functools.partial(_s5_kernel, nb=nb, tt=tt, n_t=n_t, n_cast=len(w_ops)),
        grid=(n_t,),
        in_specs=[
            pl.BlockSpec((tt * nb, D_MODEL), lambda t: (t, 0)),
            const((nb, n_state)), const((nb, n_state)),
            const((n_k, LANES, sw)), const((n_k, LANES, sw)), const((n_k, sw, LANES)), const((n_k, sw, LANES)),
            const((1, n_state)), const((1, n_state)), const((1, D_MODEL)),
            const((D_MODEL, D_MODEL)), const((D_MODEL, D_MODEL)), const((2, D_MODEL)),
        ] + w_in,
        out_specs=[pl.BlockSpec((tt * nb, D_MODEL), lambda t: (t, 0)), const((nb, n_state)),
                   const((nb, n_state))] + w_out,
        out_shape=[
            jax.ShapeDtypeStruct((length * nb, D_MODEL), F32),
            jax.ShapeDtypeStruct((nb, n_state), F32),
            jax.ShapeDtypeStruct((nb, n_state), F32),
        ] + w_shapes,
        scratch_shapes=[
            pltpu.VMEM((2, nb * tt, sw), F32), pltpu.VMEM((2, nb * tt, sw), F32),
            pltpu.VMEM((nb, n_state), F32), pltpu.VMEM((nb, n_state), F32),
            pltpu.VMEM((nb * tt, D_MODEL), BF16), pltpu.VMEM((nb * tt, D_MODEL), F32),
        ],
        compiler_params=pltpu.CompilerParams(dimension_semantics=("arbitrary",), vmem_limit_bytes=VMEM_LIMIT),
        name="s5_glu",
    )(x, h0_re, h0_im, wre, wim, cre, cim, ab_re, ab_im, d_skip, wa, wb, ln, *w_ops)
    return outs[:3], [o.reshape(s) for o, s in zip(outs[3:], w_final)]


def _block_diag_slices(m, rows_per_group, cols_per_group):
    gps = LANES // S5_GROUP
    m = m.reshape(S5_GROUPS // gps, gps, rows_per_group, cols_per_group)
    eye = jnp.eye(gps, dtype=m.dtype)
    out = m[:, :, :, None, :] * eye[None, :, None, :, None]
    return out.reshape(S5_GROUPS // gps, gps * rows_per_group, gps * cols_per_group)


def _prepare(p):
    w = p['w_in'][0]
    win = jnp.concatenate(
        [w[:, 0:1536], w[:, 1544:2056], w[:, 2056:3592], w[:, 3600:4112], w[:, 1536:1544], w[:, 3592:3600],
         jnp.zeros((D_MODEL, D_IN_PAD - 4112), w.dtype)], axis=1).astype(BF16)
    at_lane = lambda v, lane0: jnp.pad(v, (lane0, LANES - lane0 - v.shape[0]))
    gp = jnp.stack([at_lane(p['gdn_A_log'][0], G_DEC), at_lane(p['gdn_dt_bias'][0], G_DEC),
                    at_lane(p['ml_b_i'][0], G_IN), at_lane(p['ml_b_f'][0], G_FG),
                    p['gdn_norm_w'][0], p['ml_norm_w'][0], jnp.zeros((LANES,), F32), jnp.zeros((LANES,), F32)])
    ab_re, ab_im, bb_re, bb_im = _s5_prep(p['s5_A_re'][0], p['s5_A_im'][0], p['s5_log_dt'][0],
                                          p['s5_B_re'][0], p['s5_B_im'][0])
    to_in = lambda bb: _block_diag_slices(
        bb.reshape(S5_GROUPS, S5_STATE, S5_GROUP).transpose(0, 2, 1), S5_GROUP, S5_STATE).astype(BF16)
    to_out = lambda c: _block_diag_slices(c.transpose(0, 2, 1), S5_STATE, S5_GROUP).astype(BF16)
    return dict(
        win=win, wout=p['w_out'][0].astype(BF16), convw=p['gdn_conv_w'][0], gp=gp,
        ln_mix=[jnp.stack([p['ln_mix_g'][l], p['ln_mix_b'][l]]) for l in range(DEPTH)],
        ln_ffn=[jnp.stack([p['ln_ffn_g'][l], p['ln_ffn_b'][l]]) for l in range(DEPTH)],
        rw_t=p['router_w'].T, rb_col=p['router_b'][:, None],
        s5_wre=to_in(bb_re), s5_wim=to_in(bb_im),
        s5_cre=to_out(p['s5_C_re'][0]), s5_cim=to_out(p['s5_C_im'][0]),
        s5_abre=ab_re.reshape(1, -1), s5_abim=ab_im.reshape(1, -1), s5_d=p['s5_D'][0][None, :],
        glu_a=p['s5_w_glu_a'][0].astype(BF16), glu_b=p['s5_w_glu_b'][0].astype(BF16),
    )


def _ffn(h, w, layer, out_row, experts=None, moe_w=None):
    meta, counts, cast = _router(h, w['rw_t'], w['rb_col'], moe_w if experts is None else None, layer)
    experts = cast if experts is None else experts
    return _moe(h, meta, counts, *experts, w['ln_ffn'][layer], out_row), experts


def _s5_layer(h, nb, h0_re, h0_im, w, moe_w=None):
    return _s5_scan(h, nb, h0_re, h0_im, w['s5_wre'], w['s5_wim'], w['s5_cre'], w['s5_cim'],
                    w['s5_abre'], w['s5_abim'], w['s5_d'], w['glu_a'], w['glu_b'], w['ln_mix'][1], moe_w)


def kernel(x_prompt, x_sample, state_gdn_conv, state_gdn_S, state_mlstm_C, state_mlstm_n, state_mlstm_m,
           state_s5_re, state_s5_im, w_in, gdn_conv_w, gdn_A_log, gdn_dt_bias, gdn_norm_w, ml_b_i, ml_b_f,
           ml_norm_w, w_out, s5_A_re, s5_A_im, s5_log_dt, s5_B_re, s5_B_im, s5_C_re, s5_C_im, s5_D,
           s5_w_glu_a, s5_w_glu_b, router_w, router_b, moe_w_gate, moe_w_up, moe_w_down,
           ln_mix_g, ln_mix_b, ln_ffn_g, ln_ffn_b):
    w = _prepare(dict(
        w_in=w_in, gdn_conv_w=gdn_conv_w, gdn_A_log=gdn_A_log, gdn_dt_bias=gdn_dt_bias, gdn_norm_w=gdn_norm_w,
        ml_b_i=ml_b_i, ml_b_f=ml_b_f, ml_norm_w=ml_norm_w, w_out=w_out, s5_A_re=s5_A_re, s5_A_im=s5_A_im,
        s5_log_dt=s5_log_dt, s5_B_re=s5_B_re, s5_B_im=s5_B_im, s5_C_re=s5_C_re, s5_C_im=s5_C_im, s5_D=s5_D,
        s5_w_glu_a=s5_w_glu_a, s5_w_glu_b=s5_w_glu_b, router_w=router_w, router_b=router_b,
        moe_w_gate=moe_w_gate, moe_w_up=moe_w_up, moe_w_down=moe_w_down,
        ln_mix_g=ln_mix_g, ln_mix_b=ln_mix_b, ln_ffn_g=ln_ffn_g, ln_ffn_b=ln_ffn_b))
    bp, lp, _ = x_prompt.shape
    bs, ls, _ = x_sample.shape
    assert ls == 1
    n_state = S5_GROUPS * S5_STATE

    moe_w = (moe_w_gate, moe_w_up, moe_w_down)
    (hs, s_cbuf, s_s, s_c, s_n, s_m), experts0 = _ab_decode(
        x_sample[:, 0], w['win'], w['wout'], w['convw'], w['gp'], w['ln_mix'][0],
        state_gdn_conv.reshape(bs, (CONV_W - 1) * A_CONV), state_gdn_S[:, 0], state_mlstm_C[:, 0],
        state_mlstm_n.reshape(bs, N_HEADS * HEAD_DIM), state_mlstm_m[:, 0], moe_w)

    h, p_hist, p_s, p_c, p_n, p_m = _ab_prompt(x_prompt, w['win'], w['wout'], w['convw'], w['gp'], w['ln_mix'][0])
    h = _ffn(h.reshape(bp * lp, D_MODEL), w, 0, lambda r: (r % lp) * bp + r // lp, experts=experts0)[0]
    zeros = jnp.zeros((bp, n_state), F32)
    (h, p_re, p_im), experts1 = _s5_layer(h, bp, zeros, zeros, w, moe_w)
    y_prompt = _ffn(h, w, 1, lambda r: (r % bp) * lp + r // bp, experts=experts1)[0].reshape(bp, lp, D_MODEL)

    hs = _ffn(hs, w, 0, lambda r: r, experts=experts0)[0]
    (hs, s_re, s_im), _ = _s5_layer(hs, bs, state_s5_re.reshape(bs, n_state), state_s5_im.reshape(bs, n_state), w)
    y_sample = _ffn(hs, w, 1, lambda r: r, experts=experts1)[0].reshape(bs, 1, D_MODEL)

    grp = lambda a, n: a.reshape(n, 1, S5_GROUPS, S5_STATE)
    return (
        y_prompt, y_sample,
        p_hist[:, None, SUBLANES - (CONV_W - 1):, :], p_s[:, None], p_c[:, None],
        p_n[:, None, :, 0, :], p_m[:, None, :, 0, 0], grp(p_re, bp), grp(p_im, bp),
        s_cbuf.reshape(bs, 1, CONV_W - 1, A_CONV), s_s[:, None], s_c[:, None],
        s_n.reshape(bs, 1, N_HEADS, HEAD_DIM), s_m[:, None], grp(s_re, bs), grp(s_im, bs),
    )
```

```python
import functools

import jax
import jax.numpy as jnp
from jax import lax
from jax.experimental import pallas as pl
from jax.experimental.pallas import tpu as pltpu

F32 = jnp.float32
BF16 = jnp.bfloat16
HIGHEST = lax.Precision.HIGHEST

D_MODEL = 1024
DEPTH = 2
N_HEADS = 4
HEAD_DIM = 128
CONV_W = 4
CHUNK = 64
A_CONV = 3 * N_HEADS * HEAD_DIM
S5_GROUP = 16
S5_GROUPS = D_MODEL // S5_GROUP
S5_STATE = 64
N_EXPERTS = 16
EXPERTS_PER_GROUP = 4
N_EXPERT_GROUPS = N_EXPERTS // EXPERTS_PER_GROUP
D_FF = 512
DN_ALPHA = (2 * DEPTH) ** 0.25
LN_EPS = 1e-5
RMS_EPS = 1e-6
NEG_BIG = -1e30

QA, KA, VA, ZA, QB, KB, VB, OB, GT = 0, 512, 1024, 1536, 2048, 2560, 3072, 3584, 4096
D_IN_PAD = 4224
G_DEC, G_BETA, G_IN, G_FG = 0, 4, 8, 12

S5_TT = 64
S5_UNROLL = 64

LANES = 128
SUBLANES = 8
VMEM_LIMIT = 56 * 1024 * 1024


def _bf(x):
    return x.astype(BF16)


def _nn(a, b):
    return jnp.dot(_bf(a), _bf(b), preferred_element_type=F32)


def _nt(a, b):
    return lax.dot_general(_bf(a), _bf(b), (((1,), (1,)), ((), ())), preferred_element_type=F32)


def _tn(a, b):
    return lax.dot_general(_bf(a), _bf(b), (((0,), (0,)), ((), ())), preferred_element_type=F32)


def _sigmoid(x):
    return 1.0 / (1.0 + jnp.exp(-x))


def _softplus(x):
    return jnp.maximum(x, 0.0) + jnp.log(1.0 + jnp.exp(-jnp.abs(x)))


def _silu(x):
    return x * _sigmoid(x)


def _layer_norm(y, g, b):
    mu = jnp.mean(y, axis=-1, keepdims=True)
    yc = y - mu
    var = jnp.mean(yc * yc, axis=-1, keepdims=True)
    return yc * lax.rsqrt(var + LN_EPS) * g + b


def _rms(x, w):
    return x * lax.rsqrt(jnp.mean(x * x, axis=-1, keepdims=True) + RMS_EPS) * w


def _gate_transform(raw, gp):
    lane = lax.broadcasted_iota(jnp.int32, raw.shape, 1)
    dec = -jnp.exp(gp[0:1, :]) * _softplus(raw + gp[1:2, :])
    beta = _sigmoid(raw)
    ipre = raw + gp[2:3, :]
    logf = -_softplus(-(raw + gp[3:4, :]))
    return jnp.where(lane < G_BETA, dec,
                     jnp.where(lane < G_IN, beta,
                               jnp.where(lane < G_FG, ipre,
                                         jnp.where(lane < G_FG + N_HEADS, logf, 0.0))))


def _bnn(a, b):
    return lax.dot_general(_bf(a), _bf(b), (((2,), (1,)), ((0,), (0,))), preferred_element_type=F32)


def _bnt(a, b):
    return lax.dot_general(_bf(a), _bf(b), (((2,), (2,)), ((0,), (0,))), preferred_element_type=F32)


def _btn(a, b):
    return lax.dot_general(_bf(a), _bf(b), (((1,), (1,)), ((0,), (0,))), preferred_element_type=F32)


def _unit_lower_inverse_minus_eye(a, pack):
    p, c, _ = a.shape
    nb = p // pack
    w = pack * c
    r = -jnp.stack([jnp.concatenate([a[b * pack + j] for j in range(pack)], axis=1) for b in range(nb)])
    assert c & (c - 1) == 0
    block_of = lambda dim: lax.shift_right_logical(lax.broadcasted_iota(jnp.int32, (w, w), dim), c.bit_length() - 1)
    on_diag = block_of(0) == block_of(1)

    def block_diag(x):
        return jnp.where(on_diag, jnp.concatenate([x] * pack, axis=1), 0.0)

    steps = max(1, (c - 1).bit_length()) - 1
    q = _bnn(r, block_diag(r))
    for i in range(steps):
        bd = block_diag(q)
        if i + 1 < steps:
            both = _bnn(jnp.concatenate([r, q], axis=1), bd)
            rq, qq = both[:, :c, :], both[:, c:, :]
        else:
            rq, qq = _bnn(r, bd), None
        r = r + q + rq
        q = qq
    return jnp.stack([r[b][:, j * c:(j + 1) * c] for b in range(nb) for j in range(pack)])


def _ab_prompt_kernel(x_ref, win_ref, wout_ref, convw_ref, gp_ref, ln_ref,
                      h_ref, hist_ref, s_out, c_out, n_out, m_out,
                      proj, qkv, gates, gcum, merged, s_s, c_s, n_s, m_s,
                      u_s, w_s, attn_s, dlog_s, qk_s, dmax_s, kt_s, kbt_s, rows_s, *, tb, n_t):
    t = pl.program_id(1)
    nc = tb // CHUNK

    @pl.when(t == 0)
    def _():
        proj[0:SUBLANES, :] = jnp.zeros((SUBLANES, D_IN_PAD), F32)
        s_s[...] = jnp.zeros_like(s_s)
        c_s[...] = jnp.zeros_like(c_s)
        n_s[...] = jnp.zeros_like(n_s)
        m_s[...] = jnp.zeros_like(m_s)

    x = x_ref[...]
    xb = _bf(x)
    proj[SUBLANES:SUBLANES + tb, 0:A_CONV] = jnp.dot(xb, win_ref[:, 0:A_CONV], preferred_element_type=F32)
    proj[SUBLANES:SUBLANES + tb, A_CONV:] = jnp.dot(xb, win_ref[:, A_CONV:], preferred_element_type=F32)

    for blk in range(A_CONV // LANES):
        cs = slice(blk * LANES, (blk + 1) * LANES)
        acc = proj[SUBLANES:SUBLANES + tb, cs] * convw_ref[CONV_W - 1:CONV_W, cs]
        for j in range(1, CONV_W):
            acc = acc + proj[SUBLANES - j:SUBLANES - j + tb, cs] * convw_ref[CONV_W - 1 - j:CONV_W - j, cs]
        y = _silu(acc)
        if blk < 2 * N_HEADS:
            y = y * lax.rsqrt(jnp.sum(y * y, axis=-1, keepdims=True) + RMS_EPS)
            if blk < N_HEADS:
                y = y * HEAD_DIM ** -0.5
        qkv[:, cs] = y

    gt = _gate_transform(proj[SUBLANES:SUBLANES + tb, GT:GT + LANES], gp_ref[...])
    gates[...] = gt
    ri = lax.broadcasted_iota(jnp.int32, (tb, tb), 0)
    ci = lax.broadcasted_iota(jnp.int32, (tb, tb), 1)
    same_chunk = lax.shift_right_logical(ri, 6) == lax.shift_right_logical(ci, 6)
    ltri = jnp.where(same_chunk, jnp.where(ri >= ci, 1.0, 0.0), 0.0)
    gcum[...] = jnp.dot(ltri, gt, preferred_element_type=F32, precision=HIGHEST)

    ii = lax.broadcasted_iota(jnp.int32, (CHUNK, CHUNK), 0)
    jj = lax.broadcasted_iota(jnp.int32, (CHUNK, CHUNK), 1)
    incl = ii >= jj
    strict = ii > jj
    gdn_w = gp_ref[4:5, :]
    ml_w = gp_ref[5:6, :]

    pairs = [(c, h) for c in range(nc) for h in range(N_HEADS)]

    def tile_heads(ref, row0, col0):
        return jnp.stack([ref[row0 + c * CHUNK:row0 + (c + 1) * CHUNK, col0 + h * HEAD_DIM:col0 + (h + 1) * HEAD_DIM]
                          for c, h in pairs])

    def tile_cols(ref, lane0):
        return jnp.stack([ref[c * CHUNK:(c + 1) * CHUNK, lane0 + h:lane0 + h + 1] for c, h in pairs])

    def tile_rows(transposed, lane0):
        return jnp.stack([transposed[c][lane0 + h:lane0 + h + 1, :] for c, h in pairs])

    cs_t = [gcum[c * CHUNK:(c + 1) * CHUNK, :].T for c in range(nc)]
    gt_t = [gates[c * CHUNK:(c + 1) * CHUNK, :].T for c in range(nc)]
    q3 = tile_heads(qkv, 0, QA)
    k3 = tile_heads(qkv, 0, KA)
    v3 = tile_heads(qkv, 0, VA)
    g_col3 = tile_cols(gcum, G_DEC)
    beta3 = tile_cols(gates, G_BETA)
    decay3 = jnp.where(incl, jnp.exp(jnp.where(incl, g_col3 - tile_rows(cs_t, G_DEC), 0.0)), 0.0)
    kb3 = k3 * beta3
    a_low3 = jnp.where(strict, _bnt(kb3, k3) * decay3, 0.0)
    attn_s[...] = _bnt(q3, k3) * decay3
    kbm3 = tile_heads(proj, SUBLANES, KB) * HEAD_DIM ** -0.5
    qk_s[...] = _bnt(tile_heads(proj, SUBLANES, QB), kbm3)
    kt_s[...] = jnp.stack([k3[p].T for p in range(len(pairs))])
    kbt_s[...] = jnp.stack([kbm3[p].T for p in range(len(pairs))])
    zero_row = jnp.zeros((1, CHUNK), F32)
    rows_s[...] = jnp.stack([jnp.concatenate(
        [cs_t[c][G_DEC + h:G_DEC + h + 1, :], cs_t[c][G_FG + h:G_FG + h + 1, :], gt_t[c][G_IN + h:G_IN + h + 1, :]]
        + [zero_row] * (SUBLANES - 3), axis=0) for c, h in pairs])
    r3 = _unit_lower_inverse_minus_eye(a_low3, N_HEADS)
    rhs3 = jnp.concatenate([v3 * beta3, kb3 * jnp.exp(g_col3)], axis=2)
    uw3 = rhs3 + _bnn(r3, rhs3)
    u_s[...] = uw3[:, :, :HEAD_DIM]
    w_s[...] = uw3[:, :, HEAD_DIM:]
    dlog3 = jnp.where(incl, tile_cols(gcum, G_FG) - tile_rows(cs_t, G_FG) + tile_rows(gt_t, G_IN), NEG_BIG)
    dlog_s[...] = dlog3
    dmax_s[...] = jnp.max(dlog3, axis=-1, keepdims=True)

    def chunk_body(c, carry):
        r0 = pl.multiple_of(c * CHUNK, CHUNK)
        rows = pl.ds(r0, CHUNK)
        prow = pl.ds(pl.multiple_of(r0 + SUBLANES, SUBLANES), CHUNK)
        last = pl.ds(r0 + CHUNK - 1, 1)
        pc = pl.ds(pl.multiple_of(c * N_HEADS, N_HEADS), N_HEADS)

        def heads(ref, rws, col0):
            return jnp.stack([ref[rws, col0 + h * HEAD_DIM:col0 + (h + 1) * HEAD_DIM] for h in range(N_HEADS)])

        def cols(ref, rws, lane0):
            return jnp.stack([ref[rws, lane0 + h:lane0 + h + 1] for h in range(N_HEADS)])

        q = heads(qkv, rows, QA)
        g_col = cols(gcum, rows, G_DEC)
        g_last = cols(gcum, last, G_DEC)
        qb = heads(proj, prow, QB)
        kbm = heads(proj, prow, KB) * HEAD_DIM ** -0.5
        vbm = heads(proj, prow, VB)
        b_col = cols(gcum, rows, G_FG)
        b_last = cols(gcum, last, G_FG)
        row_forms = rows_s[pc]
        g_row = row_forms[:, 0:1, :]
        b_row = row_forms[:, 1:2, :]
        i_row = row_forms[:, 2:3, :]
        s_old = s_s[...]
        c_old = c_s[...]
        n_old = n_s[...]
        m_old = m_s[:, 0:1, 0:1]
        inter = b_col + m_old
        mt = jnp.maximum(inter, dmax_s[pc])
        wts = jnp.exp(dlog_s[pc] - mt) * qk_s[pc]
        sc = jnp.exp(inter - mt)
        m_new = mt[:, CHUNK - 1:CHUNK, :]
        sd = jnp.exp(b_last + m_old - m_new)
        wk_row = jnp.exp(b_last - b_row + i_row - m_new)
        w_state = _bnn(w_s[pc], s_old)
        q_state = _bnn(q * jnp.exp(g_col), s_old)
        q_mem = _bnn(qb, c_old)
        w_val = _bnn(wts, vbm)
        kv = _bnn(kbt_s[pc] * wk_row, vbm)
        k_sum = _bnn(jnp.broadcast_to(wk_row, (N_HEADS, SUBLANES, CHUNK)), kbm)
        v_new = u_s[pc] - w_state
        o_a = q_state + _bnn(attn_s[pc], v_new)
        s_s[...] = s_old * jnp.exp(g_last) + _bnn(kt_s[pc] * jnp.exp(g_last - g_row), v_new)
        num = sc * q_mem + w_val
        den = sc * jnp.sum(qb * n_old[:, 0:1, :], axis=-1, keepdims=True) + jnp.sum(wts, axis=-1, keepdims=True)
        h_b = num / jnp.maximum(jnp.abs(den), jnp.exp(-mt))
        c_s[...] = sd * c_old + kv
        n_s[...] = sd * n_old + k_sum
        m_s[...] = jnp.broadcast_to(m_new, m_s.shape)
        o_n = _rms(o_a, gdn_w)
        h_n = _rms(h_b, ml_w)
        for h in range(N_HEADS):
            hs = slice(h * HEAD_DIM, (h + 1) * HEAD_DIM)
            hs2 = slice((N_HEADS + h) * HEAD_DIM, (N_HEADS + h + 1) * HEAD_DIM)
            merged[rows, hs] = o_n[h] * _silu(proj[prow, ZA + h * HEAD_DIM:ZA + (h + 1) * HEAD_DIM])
            merged[rows, hs2] = h_n[h] * _sigmoid(proj[prow, OB + h * HEAD_DIM:OB + (h + 1) * HEAD_DIM])
        return carry

    lax.fori_loop(0, nc, chunk_body, 0)

    mix = jnp.dot(_bf(merged[...]), wout_ref[...], preferred_element_type=F32)
    h_ref[...] = _layer_norm(DN_ALPHA * x + mix, ln_ref[0:1, :], ln_ref[1:2, :])

    proj[0:SUBLANES, 0:A_CONV] = proj[tb:tb + SUBLANES, 0:A_CONV]

    @pl.when(t == n_t - 1)
    def _():
        hist_ref[...] = proj[tb:tb + SUBLANES, 0:A_CONV]
        s_out[...] = s_s[...]
        c_out[...] = c_s[...]
        n_out[...] = n_s[...]
        m_out[...] = m_s[...]


def _expert_weight_cast_specs(moe_w, layer, n_steps, step_of):
    ops, in_specs, out_specs, out_shapes, shapes = [], [], [], [], []
    for w in moe_w:
        _, n_e, rows, cols = w.shape
        per_layer = n_e * rows
        blk = per_layer // n_steps
        assert per_layer % n_steps == 0 and blk % (2 * SUBLANES) == 0
        ops.append(w.reshape(w.shape[0] * per_layer, cols))
        in_specs.append(pl.BlockSpec((blk, cols), lambda *g: (layer * n_steps + step_of(*g), 0)))
        out_specs.append(pl.BlockSpec((blk, cols), lambda *g: (step_of(*g), 0)))
        out_shapes.append(jax.ShapeDtypeStruct((per_layer, cols), BF16))
        shapes.append((n_e, rows, cols))
    return ops, in_specs, out_specs, out_shapes, shapes


def _ab_prompt(x, win, wout, convw, gp, ln):
    bsz, length, _ = x.shape
    tb = min(512, length)
    assert length % tb == 0 and tb % CHUNK == 0 and length >= SUBLANES
    n_t = length // tb
    n_pairs = (tb // CHUNK) * N_HEADS
    const = lambda shape: pl.BlockSpec(shape, lambda b, t: (0,) * len(shape))
    per_b = lambda shape: pl.BlockSpec((None,) + shape, lambda b, t: (b,) + (0,) * len(shape))
    return pl.pallas_call(
        functools.partial(_ab_prompt_kernel, tb=tb, n_t=n_t),
        grid=(bsz, n_t),
        in_specs=[
            pl.BlockSpec((None, tb, D_MODEL), lambda b, t: (b, t, 0)),
            const((D_MODEL, D_IN_PAD)), const((2 * N_HEADS * HEAD_DIM, D_MODEL)),
            const((CONV_W, A_CONV)), const((SUBLANES, LANES)), const((2, D_MODEL)),
        ],
        out_specs=[
            pl.BlockSpec((None, tb, D_MODEL), lambda b, t: (b, t, 0)),
            per_b((SUBLANES, A_CONV)), per_b((N_HEADS, HEAD_DIM, HEAD_DIM)), per_b((N_HEADS, HEAD_DIM, HEAD_DIM)),
            per_b((N_HEADS, SUBLANES, LANES)), per_b((N_HEADS, SUBLANES, LANES)),
        ],
        out_shape=[
            jax.ShapeDtypeStruct((bsz, length, D_MODEL), F32),
            jax.ShapeDtypeStruct((bsz, SUBLANES, A_CONV), F32),
            jax.ShapeDtypeStruct((bsz, N_HEADS, HEAD_DIM, HEAD_DIM), F32),
            jax.ShapeDtypeStruct((bsz, N_HEADS, HEAD_DIM, HEAD_DIM), F32),
            jax.ShapeDtypeStruct((bsz, N_HEADS, SUBLANES, LANES), F32),
            jax.ShapeDtypeStruct((bsz, N_HEADS, SUBLANES, LANES), F32),
        ],
        scratch_shapes=[
            pltpu.VMEM((tb + SUBLANES, D_IN_PAD), F32),
            pltpu.VMEM((tb, A_CONV), F32),
            pltpu.VMEM((tb, LANES), F32),
            pltpu.VMEM((tb, LANES), F32),
            pltpu.VMEM((tb, 2 * N_HEADS * HEAD_DIM), F32),
            pltpu.VMEM((N_HEADS, HEAD_DIM, HEAD_DIM), F32),
            pltpu.VMEM((N_HEADS, HEAD_DIM, HEAD_DIM), F32),
            pltpu.VMEM((N_HEADS, SUBLANES, LANES), F32),
            pltpu.VMEM((N_HEADS, SUBLANES, LANES), F32),
            pltpu.VMEM((n_pairs, CHUNK, HEAD_DIM), F32),
            pltpu.VMEM((n_pairs, CHUNK, HEAD_DIM), F32),
            pltpu.VMEM((n_pairs, CHUNK, CHUNK), F32),
            pltpu.VMEM((n_pairs, CHUNK, CHUNK), F32),
            pltpu.VMEM((n_pairs, CHUNK, CHUNK), F32),
            pltpu.VMEM((n_pairs, CHUNK, 1), F32),
            pltpu.VMEM((n_pairs, HEAD_DIM, CHUNK), F32),
            pltpu.VMEM((n_pairs, HEAD_DIM, CHUNK), F32),
            pltpu.VMEM((n_pairs, SUBLANES, CHUNK), F32),
        ],
        compiler_params=pltpu.CompilerParams(
            dimension_semantics=("arbitrary", "arbitrary"), vmem_limit_bytes=VMEM_LIMIT),
        name="ab_prompt",
    )(x, win, wout, convw, gp, ln)


def _ab_decode_kernel(*refs, bb, n_steps, n_cast):
    x_ref, win_ref, wout_ref, convw_ref, gp_ref, ln_ref, cbuf_ref, s_in, c_in, n_in, m_in = refs[:11]
    y_ref, cbuf_out, s_out, c_out, n_out, m_out = refs[11 + n_cast:17 + n_cast]
    proj, merged = refs[17 + 2 * n_cast:]
    for src, dst in zip(refs[11:11 + n_cast], refs[17 + n_cast:17 + 2 * n_cast]):
        dst[...] = _bf(src[...])
    i = pl.program_id(0)

    @pl.when(i == 0)
    def _():
        proj[...] = jnp.dot(_bf(x_ref[...]), win_ref[...], preferred_element_type=F32)

    rows = pl.ds(pl.multiple_of(i * bb, bb), bb)
    raw = proj[rows, 0:A_CONV]
    cbuf = cbuf_ref[...]
    conv = raw * convw_ref[CONV_W - 1:CONV_W, :]
    for j in range(CONV_W - 1):
        conv = conv + cbuf[:, j * A_CONV:(j + 1) * A_CONV] * convw_ref[j:j + 1, :]
    cbuf_out[:, 0:(CONV_W - 2) * A_CONV] = cbuf[:, A_CONV:(CONV_W - 1) * A_CONV]
    cbuf_out[:, (CONV_W - 2) * A_CONV:(CONV_W - 1) * A_CONV] = raw
    act = _silu(conv)
    gt = _gate_transform(proj[rows, GT:GT + LANES], gp_ref[...])
    gdn_w = gp_ref[4:5, :]
    ml_w = gp_ref[5:6, :]
    m_all = m_in[...]

    for h in range(N_HEADS):
        def head(off):
            return act[:, off + h * HEAD_DIM:off + (h + 1) * HEAD_DIM]
        q = head(QA)
        q = q * lax.rsqrt(jnp.sum(q * q, axis=-1, keepdims=True) + RMS_EPS) * HEAD_DIM ** -0.5
        k = head(KA)
        k = k * lax.rsqrt(jnp.sum(k * k, axis=-1, keepdims=True) + RMS_EPS)
        v = head(VA)
        k_t = k.T
        qk = jnp.sum(q * k, axis=-1, keepdims=True)
        qb = proj[rows, QB + h * HEAD_DIM:QB + (h + 1) * HEAD_DIM]
        kbm = proj[rows, KB + h * HEAD_DIM:KB + (h + 1) * HEAD_DIM] * HEAD_DIM ** -0.5
        vbm = proj[rows, VB + h * HEAD_DIM:VB + (h + 1) * HEAD_DIM]
        kb_t = kbm.T
        pad = [jnp.zeros((1, HEAD_DIM), F32)] * (SUBLANES - 2)
        read_s = _bnn(jnp.stack([jnp.concatenate([k[b:b + 1], q[b:b + 1]] + pad, axis=0) for b in range(bb)]),
                      s_in[:, h])
        read_c = _bnn(jnp.stack([jnp.concatenate([qb[b:b + 1], qb[b:b + 1]] + pad, axis=0) for b in range(bb)]),
                      c_in[:, h])
        qkb = jnp.sum(qb * kbm, axis=-1, keepdims=True)
        n_old = n_in[:, h * HEAD_DIM:(h + 1) * HEAD_DIM]
        qn = jnp.sum(qb * n_old, axis=-1, keepdims=True)
        o_rows = []
        hb_rows = []
        n_rows = []
        m_rows = []
        for b in range(bb):
            s_old = s_in[b, h]
            k_c = k_t[:, b:b + 1]
            e_g = jnp.exp(gt[b:b + 1, G_DEC + h:G_DEC + h + 1])
            beta = gt[b:b + 1, G_BETA + h:G_BETA + h + 1]
            k_s = read_s[b, 0:1, :]
            q_s = read_s[b, 1:2, :]
            v_new = v[b:b + 1, :] * beta - (beta * e_g) * k_s
            o_rows.append(e_g * q_s + qk[b:b + 1, :] * v_new)
            s_out[b, h] = s_old * e_g + k_c * v_new
            c_old = c_in[b, h]
            m_old = m_all[b:b + 1, h:h + 1]
            i_pre = gt[b:b + 1, G_IN + h:G_IN + h + 1]
            logf = gt[b:b + 1, G_FG + h:G_FG + h + 1]
            inter = logf + m_old
            mt = jnp.maximum(inter, i_pre)
            w_in = jnp.exp(i_pre - mt)
            sc = jnp.exp(inter - mt)
            wts = w_in * qkb[b:b + 1, :]
            q_cm = read_c[b, 0:1, :]
            num = sc * q_cm + wts * vbm[b:b + 1, :]
            den = sc * qn[b:b + 1, :] + wts
            hb_rows.append(num / jnp.maximum(jnp.abs(den), jnp.exp(-mt)))
            c_out[b, h] = sc * c_old + (w_in * kb_t[:, b:b + 1]) * vbm[b:b + 1, :]
            n_rows.append(sc * n_old[b:b + 1, :] + w_in * kbm[b:b + 1, :])
            m_rows.append(mt)
        o_a = jnp.concatenate(o_rows, axis=0)
        h_b = jnp.concatenate(hb_rows, axis=0)
        n_out[:, h * HEAD_DIM:(h + 1) * HEAD_DIM] = jnp.concatenate(n_rows, axis=0)
        m_out[:, h:h + 1] = jnp.concatenate(m_rows, axis=0)
        z = proj[rows, ZA + h * HEAD_DIM:ZA + (h + 1) * HEAD_DIM]
        merged[rows, h * HEAD_DIM:(h + 1) * HEAD_DIM] = _rms(o_a, gdn_w) * _silu(z)
        o_gate = proj[rows, OB + h * HEAD_DIM:OB + (h + 1) * HEAD_DIM]
        merged[rows, N_HEADS * HEAD_DIM + h * HEAD_DIM:N_HEADS * HEAD_DIM + (h + 1) * HEAD_DIM] = (
            _rms(h_b, ml_w) * _sigmoid(o_gate))

    @pl.when(i == n_steps - 1)
    def _():
        mix = jnp.dot(_bf(merged[...]), wout_ref[...], preferred_element_type=F32)
        y_ref[...] = _layer_norm(DN_ALPHA * x_ref[...] + mix, ln_ref[0:1, :], ln_ref[1:2, :])


def _ab_decode(x, win, wout, convw, gp, ln, cbuf, s0, c0, n0, m0, moe_w):
    nb = x.shape[0]
    bb = SUBLANES
    assert nb % bb == 0
    n_steps = nb // bb
    const = lambda shape: pl.BlockSpec(shape, lambda i: (0,) * len(shape))
    blk = lambda shape: pl.BlockSpec((bb,) + shape, lambda i: (i,) + (0,) * len(shape))
    hist = (CONV_W - 1) * A_CONV
    width = N_HEADS * HEAD_DIM
    w_ops, w_in, w_out, w_shapes, w_final = _expert_weight_cast_specs(moe_w, 0, n_steps, lambda i: i)
    outs = pl.pallas_call(
        functools.partial(_ab_decode_kernel, bb=bb, n_steps=n_steps, n_cast=len(w_ops)),
        grid=(n_steps,),
        in_specs=[
            const((nb, D_MODEL)), const((D_MODEL, D_IN_PAD)), const((2 * width, D_MODEL)),
            const((CONV_W, A_CONV)), const((SUBLANES, LANES)), const((2, D_MODEL)),
            blk((hist,)), blk((N_HEADS, HEAD_DIM, HEAD_DIM)), blk((N_HEADS, HEAD_DIM, HEAD_DIM)),
            blk((width,)), blk((N_HEADS,)),
        ] + w_in,
        out_specs=[
            const((nb, D_MODEL)), blk((hist,)), blk((N_HEADS, HEAD_DIM, HEAD_DIM)),
            blk((N_HEADS, HEAD_DIM, HEAD_DIM)), blk((width,)), blk((N_HEADS,)),
        ] + w_out,
        out_shape=[
            jax.ShapeDtypeStruct((nb, D_MODEL), F32),
            jax.ShapeDtypeStruct((nb, hist), F32),
            jax.ShapeDtypeStruct((nb, N_HEADS, HEAD_DIM, HEAD_DIM), F32),
            jax.ShapeDtypeStruct((nb, N_HEADS, HEAD_DIM, HEAD_DIM), F32),
            jax.ShapeDtypeStruct((nb, width), F32),
            jax.ShapeDtypeStruct((nb, N_HEADS), F32),
        ] + w_shapes,
        scratch_shapes=[pltpu.VMEM((nb, D_IN_PAD), F32), pltpu.VMEM((nb, 2 * width), F32)],
        compiler_params=pltpu.CompilerParams(dimension_semantics=("arbitrary",), vmem_limit_bytes=VMEM_LIMIT),
        name="ab_decode",
    )(x, win, wout, convw, gp, ln, cbuf, s0, c0, n0, m0, *w_ops)
    return outs[:6], [o.reshape(s) for o, s in zip(outs[6:], w_final)]


def _second_largest_sum(a, b, c, d):
    hi1, lo1 = jnp.maximum(a, b), jnp.minimum(a, b)
    hi2, lo2 = jnp.maximum(c, d), jnp.minimum(c, d)
    return jnp.maximum(hi1, hi2) + jnp.maximum(jnp.minimum(hi1, hi2), jnp.maximum(lo1, lo2))


def _first_argmax(vals):
    best_v = vals[0]
    best_i = jnp.zeros(vals[0].shape, jnp.int32)
    for j in range(1, len(vals)):
        better = vals[j] > best_v
        best_v = jnp.where(better, vals[j], best_v)
        best_i = jnp.where(better, j, best_i)
    return best_i


def _router_kernel(*refs, n_steps, n_cast):
    x_ref, rw_ref, rb_ref = refs[:3]
    meta_ref, cnt_ref, order_ref = refs[3 + n_cast:6 + n_cast]
    carry = refs[6 + 2 * n_cast]
    for src, dst in zip(refs[3:3 + n_cast], refs[6 + n_cast:6 + 2 * n_cast]):
        dst[...] = _bf(src[...])
    i = pl.program_id(0)
    tm = x_ref.shape[0]

    @pl.when(i == 0)
    def _():
        carry[...] = jnp.zeros_like(carry)

    x = x_ref[...]
    rw = rw_ref[...]
    x_hi = _bf(x)
    x_lo = _bf(x - x_hi.astype(F32))
    w_hi = _bf(rw)
    w_lo = _bf(rw - w_hi.astype(F32))
    nt = lambda a, b: lax.dot_general(a, b, (((1,), (1,)), ((), ())), preferred_element_type=F32)
    logits = nt(w_hi, x_hi) + (nt(w_hi, x_lo) + nt(w_lo, x_hi))
    ex = jnp.exp(logits - jnp.max(logits, axis=0, keepdims=True))
    probs = ex / jnp.sum(ex, axis=0, keepdims=True)
    sel = probs + rb_ref[...]
    p = [probs[j:j + 1, :] for j in range(N_EXPERTS)]
    s = [sel[j:j + 1, :] for j in range(N_EXPERTS)]
    scores = [_second_largest_sum(*s[EXPERTS_PER_GROUP * g:EXPERTS_PER_GROUP * (g + 1)])
              for g in range(N_EXPERT_GROUPS)]
    best = _first_argmax(scores)
    masked = [jnp.where(best == j // EXPERTS_PER_GROUP, s[j], -jnp.inf) for j in range(N_EXPERTS)]
    i1 = _first_argmax(masked)
    i2 = _first_argmax([jnp.where(i1 == j, -jnp.inf, masked[j]) for j in range(N_EXPERTS)])
    zero = jnp.zeros_like(p[0])
    p1 = functools.reduce(lambda a, b: a + b, [jnp.where(i1 == j, p[j], zero) for j in range(N_EXPERTS)])
    p2 = functools.reduce(lambda a, b: a + b, [jnp.where(i2 == j, p[j], zero) for j in range(N_EXPERTS)])
    tot = p1 + p2
    rows = [jnp.where(i1 == j, p1 / tot, zero) + jnp.where(i2 == j, p2 / tot, zero) for j in range(N_EXPERTS)]
    in_group = [best == g for g in range(N_EXPERT_GROUPS)]
    local = [functools.reduce(lambda a, b: a + b,
                              [jnp.where(in_group[g], rows[EXPERTS_PER_GROUP * g + e], zero)
                               for g in range(N_EXPERT_GROUPS)])
             for e in range(EXPERTS_PER_GROUP)]
    onehot = jnp.concatenate([jnp.where(m, 1.0, 0.0) for m in in_group]
                             + [jnp.zeros((SUBLANES - N_EXPERT_GROUPS, tm), F32)], axis=0)
    ri = lax.broadcasted_iota(jnp.int32, (tm, tm), 0)
    ci = lax.broadcasted_iota(jnp.int32, (tm, tm), 1)
    incl = jnp.dot(_bf(onehot), _bf(jnp.where(ri <= ci, 1.0, 0.0)), preferred_element_type=F32)
    prev = carry[...]
    rank = jnp.sum(onehot * (incl - 1.0 + prev[:, 0:1]), axis=0, keepdims=True)
    carry[...] = prev + incl[:, tm - 1:tm]
    row_id = (i * tm + lax.broadcasted_iota(jnp.int32, (1, tm), 1)).astype(F32)
    meta_ref[...] = jnp.concatenate([row_id] + local + [best.astype(F32), rank, zero], axis=0).T
    order_ref[...] = jnp.concatenate([best.astype(F32), rank] + [zero] * (SUBLANES - 2), axis=0)

    @pl.when(i == n_steps - 1)
    def _():
        cnt_ref[...] = carry[...]


def _router(x, rw_t, rb_col, moe_w=None, layer=0):
    t = x.shape[0]
    tm = min(256, t)
    assert t % tm == 0
    n_steps = t // tm
    if moe_w is None:
        w_ops, w_in, w_out, w_shapes, w_final = [], [], [], [], []
    else:
        w_ops, w_in, w_out, w_shapes, w_final = _expert_weight_cast_specs(moe_w, layer, n_steps, lambda i: i)
    outs = pl.pallas_call(
        functools.partial(_router_kernel, n_steps=n_steps, n_cast=len(w_ops)),
        grid=(n_steps,),
        in_specs=[pl.BlockSpec((tm, D_MODEL), lambda i: (i, 0)),
                  pl.BlockSpec((N_EXPERTS, D_MODEL), lambda i: (0, 0)),
                  pl.BlockSpec((N_EXPERTS, 1), lambda i: (0, 0))] + w_in,
        out_specs=[pl.BlockSpec((tm, SUBLANES), lambda i: (i, 0)),
                   pl.BlockSpec((SUBLANES, LANES), lambda i: (0, 0)),
                   pl.BlockSpec((SUBLANES, tm), lambda i: (0, i))] + w_out,
        out_shape=[jax.ShapeDtypeStruct((t, SUBLANES), F32), jax.ShapeDtypeStruct((SUBLANES, LANES), F32),
                   jax.ShapeDtypeStruct((SUBLANES, t), F32)] + w_shapes,
        scratch_shapes=[pltpu.VMEM((SUBLANES, LANES), F32)],
        compiler_params=pltpu.CompilerParams(dimension_semantics=("arbitrary",), vmem_limit_bytes=VMEM_LIMIT),
        name="router",
    )(x, rw_t, rb_col, *w_ops)
    return outs[0], outs[1], outs[2], [o.reshape(s) for o, s in zip(outs[3:], w_final)]


def _moe_kernel(tg_ref, nv_ref, src_cur, dst_cur, src_nxt, cw_ref, x_hbm, wg_ref, wu_ref, wd_ref, ln_ref,
                y_hbm, xg, stage, gsem, ssem, *, tm, n_tiles):
    i = pl.program_id(0)
    slot = lax.rem(i, 2)
    other = 1 - slot
    n_valid = nv_ref[i]
    n_prev = jnp.where(i >= 1, nv_ref[jnp.maximum(i - 1, 0)], 0)
    n_back2 = jnp.where(i >= 2, nv_ref[jnp.maximum(i - 2, 0)], 0)
    has_next = jnp.logical_and(i + 1 < n_tiles, nv_ref[jnp.minimum(i + 1, n_tiles - 1)] > 0)

    def gather_row(tok_ref, r, dst_slot):
        return pltpu.make_async_copy(x_hbm.at[pl.ds(tok_ref[0, r], 1), :], xg.at[dst_slot, pl.ds(r, 1), :],
                                     gsem.at[dst_slot])

    def scatter_row(tok_ref, r, src_slot):
        return pltpu.make_async_copy(stage.at[src_slot, pl.ds(r, 1), :], y_hbm.at[pl.ds(tok_ref[0, r], 1), :],
                                     ssem.at[src_slot])

    def scatter_wait(count, src_slot):
        for p in [1 << b for b in range(tm.bit_length())]:
            @pl.when(lax.bitwise_and(count, p) != 0)
            def _():
                pltpu.make_async_copy(stage.at[src_slot, pl.ds(0, p), :], y_hbm.at[pl.ds(0, p), :],
                                      ssem.at[src_slot]).wait()

    @pl.when(jnp.logical_and(i == 0, n_valid > 0))
    def _():
        def body(r, c):
            gather_row(src_cur, r, 0).start()
            return c
        lax.fori_loop(0, tm, body, 0, unroll=8)

    @pl.when(n_valid > 0)
    def _():
        pltpu.make_async_copy(x_hbm.at[pl.ds(0, tm), :], xg.at[slot], gsem.at[slot]).wait()

    for parity in range(2):
        @pl.when(jnp.logical_and(has_next, slot == parity))
        def _():
            for r in range(tm):
                gather_row(src_nxt, r, 1 - parity).start(priority=r % 2)

    @pl.when(n_valid > 0)
    def _():
        x = xg[slot]
        xb = _bf(x)
        cw = cw_ref[...]
        acc = jnp.zeros((tm, D_MODEL), F32)
        for e in range(EXPERTS_PER_GROUP):
            g = jnp.dot(xb, wg_ref[e], preferred_element_type=F32)
            u = jnp.dot(xb, wu_ref[e], preferred_element_type=F32)
            hid = _silu(g) * u * cw[:, 1 + e:2 + e]
            acc = acc + jnp.dot(_bf(hid), wd_ref[e], preferred_element_type=F32)
        y = _layer_norm(DN_ALPHA * x + acc, ln_ref[0:1, :], ln_ref[1:2, :])

        @pl.when(i >= 2)
        def _():
            scatter_wait(n_back2, slot)
        stage[slot] = y

    @pl.when(jnp.logical_and(n_valid <= 0, i >= 2))
    def _():
        scatter_wait(n_back2, slot)

    for parity in range(2):
        @pl.when(jnp.logical_and(n_valid == tm, slot == parity))
        def _():
            for r in range(tm):
                scatter_row(dst_cur, r, parity).start(priority=r % 2)

    @pl.when(n_valid < tm)
    def _():
        for r in range(tm):
            @pl.when(r < n_valid)
            def _():
                scatter_row(dst_cur, r, slot).start(priority=r % 2)

    @pl.when(i == n_tiles - 1)
    def _():
        scatter_wait(n_prev, other)
        scatter_wait(n_valid, slot)


def _moe(x, meta, counts, order, wg, wu, wd, ln, out_row):
    t = x.shape[0]
    tm = min(512, t)
    assert t % tm == 0 and tm & (tm - 1) == 0
    n_tiles = t // tm + N_EXPERT_GROUPS
    n_slots = n_tiles * tm
    grp = order[0].astype(jnp.int32)
    rank = order[1].astype(jnp.int32)
    cnt = counts[:N_EXPERT_GROUPS, 0].astype(jnp.int32)
    padded = ((cnt + tm - 1) // tm) * tm
    g_end = jnp.cumsum(padded)
    g_off = g_end - padded
    pos = jnp.take(g_off, grp) + rank
    slot_meta = jnp.zeros((n_slots, SUBLANES), F32).at[pos].set(meta, unique_indices=True, mode='promise_in_bounds')
    tok_of_slot = slot_meta[:, 0].astype(jnp.int32)
    tile_start = jnp.arange(n_tiles, dtype=jnp.int32) * tm
    tile_grp = jnp.minimum(jnp.sum(tile_start[:, None] >= g_end[None, :], axis=1), N_EXPERT_GROUPS - 1).astype(jnp.int32)
    tile_valid = jnp.clip(jnp.take(g_off + cnt, tile_grp) - tile_start, 0, tm).astype(jnp.int32)
    src3 = tok_of_slot.reshape(n_tiles, 1, tm)
    dst3 = out_row(tok_of_slot).reshape(n_tiles, 1, tm)

    grid_spec = pltpu.PrefetchScalarGridSpec(
        num_scalar_prefetch=2,
        grid=(n_tiles,),
        in_specs=[
            pl.BlockSpec((None, 1, tm), lambda i, tg, nv: (i, 0, 0), memory_space=pltpu.SMEM),
            pl.BlockSpec((None, 1, tm), lambda i, tg, nv: (i, 0, 0), memory_space=pltpu.SMEM),
            pl.BlockSpec((None, 1, tm), lambda i, tg, nv: (jnp.minimum(i + 1, n_tiles - 1), 0, 0),
                         memory_space=pltpu.SMEM),
            pl.BlockSpec((tm, SUBLANES), lambda i, tg, nv: (i, 0)),
            pl.BlockSpec(memory_space=pl.ANY),
            pl.BlockSpec((EXPERTS_PER_GROUP, D_MODEL, D_FF), lambda i, tg, nv: (tg[i], 0, 0)),
            pl.BlockSpec((EXPERTS_PER_GROUP, D_MODEL, D_FF), lambda i, tg, nv: (tg[i], 0, 0)),
            pl.BlockSpec((EXPERTS_PER_GROUP, D_FF, D_MODEL), lambda i, tg, nv: (tg[i], 0, 0)),
            pl.BlockSpec((2, D_MODEL), lambda i, tg, nv: (0, 0)),
        ],
        out_specs=pl.BlockSpec(memory_space=pl.ANY),
        scratch_shapes=[pltpu.VMEM((2, tm, D_MODEL), F32), pltpu.VMEM((2, tm, D_MODEL), F32),
                        pltpu.SemaphoreType.DMA((2,)), pltpu.SemaphoreType.DMA((2,))],
    )
    return pl.pallas_call(
        functools.partial(_moe_kernel, tm=tm, n_tiles=n_tiles),
        grid_spec=grid_spec,
        out_shape=jax.ShapeDtypeStruct((t, D_MODEL), F32),
        compiler_params=pltpu.CompilerParams(dimension_semantics=("arbitrary",), vmem_limit_bytes=VMEM_LIMIT),
        name="moe",
    )(tile_grp, tile_valid, src3, dst3, src3, slot_meta, x, wg, wu, wd, ln)


def _s5_prep_kernel(are_ref, aim_ref, ldt_ref, bre_ref, bim_ref, abre_ref, abim_ref, bbre_ref, bbim_ref):
    a_r = are_ref[...]
    a_i = aim_ref[...]
    dt = jnp.exp(ldt_ref[...])
    mag = jnp.exp(dt * a_r)
    ab_re = mag * jnp.cos(dt * a_i)
    ab_im = mag * jnp.sin(dt * a_i)
    den = a_r * a_r + a_i * a_i
    nr = ab_re - 1.0
    z_re = (nr * a_r + ab_im * a_i) / den
    z_im = (ab_im * a_r - nr * a_i) / den
    abre_ref[...] = ab_re
    abim_ref[...] = ab_im
    bbre_ref[...] = z_re * bre_ref[...] - z_im * bim_ref[...]
    bbim_ref[...] = z_re * bim_ref[...] + z_im * bre_ref[...]


def _s5_prep(a_re, a_im, log_dt, b_re, b_im):
    n = S5_GROUPS * S5_STATE
    col = jax.ShapeDtypeStruct((n, 1), F32)
    mat = jax.ShapeDtypeStruct((n, S5_GROUP), F32)
    ldt = jnp.broadcast_to(log_dt[:, None], (S5_GROUPS, S5_STATE)).reshape(n, 1)
    return pl.pallas_call(_s5_prep_kernel, out_shape=[col, col, mat, mat], name="s5_prep")(
        a_re.reshape(n, 1), a_im.reshape(n, 1), ldt, b_re.reshape(n, S5_GROUP), b_im.reshape(n, S5_GROUP))


def _gelu_tanh(x):
    return 0.5 * x * (1.0 + jnp.tanh(0.7978845608028654 * (x + 0.044715 * (x * x * x))))


def _s5_kernel(*refs, nb, tt, n_t, n_cast):
    (x_ref, h0re_ref, h0im_ref, wre_ref, wim_ref, cre_ref, cim_ref, abre_ref, abim_ref, d_ref,
     wa_ref, wb_ref, ln_ref) = refs[:13]
    cast_in = refs[13:13 + n_cast]
    o_ref, hre_out, him_out = refs[13 + n_cast:16 + n_cast]
    cast_out = refs[16 + n_cast:16 + 2 * n_cast]
    bu_re, bu_im, hre_s, him_s, gy_ref, glu = refs[16 + 2 * n_cast:]
    t = pl.program_id(0)
    sw = wre_ref.shape[2]
    for src, dst in zip(cast_in, cast_out):
        dst[...] = _bf(src[...])

    @pl.when(t == 0)
    def _():
        hre_s[...] = h0re_ref[...]
        him_s[...] = h0im_ref[...]

    for k in range(D_MODEL // LANES):
        ls = slice(k * LANES, (k + 1) * LANES)
        ss = slice(k * sw, (k + 1) * sw)
        buf = k % 2
        xv = x_ref[:, ls]
        xb = _bf(xv)
        bu_re[buf] = jnp.dot(xb, wre_ref[k], preferred_element_type=F32)
        bu_im[buf] = jnp.dot(xb, wim_ref[k], preferred_element_type=F32)
        a_re = jnp.broadcast_to(abre_ref[:, ss], (nb, sw))
        a_im = jnp.broadcast_to(abim_ref[:, ss], (nb, sw))

        def step(s, carry, buf=buf, a_re=a_re, a_im=a_im):
            h_re, h_im = carry
            rows = pl.ds(pl.multiple_of(s * nb, nb), nb)
            n_re = a_re * h_re - a_im * h_im + bu_re[buf, rows, :]
            n_im = a_re * h_im + a_im * h_re + bu_im[buf, rows, :]
            bu_re[buf, rows, :] = n_re
            bu_im[buf, rows, :] = n_im
            return n_re, n_im

        h_re, h_im = lax.fori_loop(0, tt, step, (hre_s[:, ss], him_s[:, ss]), unroll=min(tt, S5_UNROLL))
        hre_s[:, ss] = h_re
        him_s[:, ss] = h_im
        y = (jnp.dot(_bf(bu_re[buf]), cre_ref[k], preferred_element_type=F32)
             - jnp.dot(_bf(bu_im[buf]), cim_ref[k], preferred_element_type=F32)
             + d_ref[:, ls] * xv)
        gy_ref[:, ls] = _gelu_tanh(y).astype(gy_ref.dtype)

    gy = gy_ref[...]
    width = 2 * LANES
    for n in range(D_MODEL // width):
        ns = slice(n * width, (n + 1) * width)
        a = jnp.dot(gy, wa_ref[:, ns], preferred_element_type=F32)
        b = jnp.dot(gy, wb_ref[:, ns], preferred_element_type=F32)
        glu[:, ns] = a * _sigmoid(b)
    o_ref[...] = _layer_norm(DN_ALPHA * x_ref[...] + glu[...], ln_ref[0:1, :], ln_ref[1:2, :])

    @pl.when(t == n_t - 1)
    def _():
        hre_out[...] = hre_s[...]
        him_out[...] = him_s[...]


def _s5_scan(x, nb, h0_re, h0_im, wre, wim, cre, cim, ab_re, ab_im, d_skip, wa, wb, ln, moe_w=None):
    length = x.shape[0] // nb
    tt = min(S5_TT, length)
    assert length % tt == 0 and nb % SUBLANES == 0
    n_t = length // tt
    n_k = D_MODEL // LANES
    sw = (LANES // S5_GROUP) * S5_STATE
    n_state = S5_GROUPS * S5_STATE
    const = lambda shape: pl.BlockSpec(shape, lambda t: (0,) * len(shape))
    if moe_w is None:
        w_ops, w_in, w_out, w_shapes, w_final = [], [], [], [], []
    else:
        w_ops, w_in, w_out, w_shapes, w_final = _expert_weight_cast_specs(moe_w, 1, n_t, lambda t: t)
    outs = pl.pallas_call(
        functools.partial(_s5_kernel, nb=nb, tt=tt, n_t=n_t, n_cast=len(w_ops)),
        grid=(n_t,),
        in_specs=[
            pl.BlockSpec((tt * nb, D_MODEL), lambda t: (t, 0)),
            const((nb, n_state)), const((nb, n_state)),
            const((n_k, LANES, sw)), const((n_k, LANES, sw)), const((n_k, sw, LANES)), const((n_k, sw, LANES)),
            const((1, n_state)), const((1, n_state)), const((1, D_MODEL)),
            const((D_MODEL, D_MODEL)), const((D_MODEL, D_MODEL)), const((2, D_MODEL)),
        ] + w_in,
        out_specs=[pl.BlockSpec((tt * nb, D_MODEL), lambda t: (t, 0)), const((nb, n_state)),
                   const((nb, n_state))] + w_out,
        out_shape=[
            jax.ShapeDtypeStruct((length * nb, D_MODEL), F32),
            jax.ShapeDtypeStruct((nb, n_state), F32),
            jax.ShapeDtypeStruct((nb, n_state), F32),
        ] + w_shapes,
        scratch_shapes=[
            pltpu.VMEM((2, nb * tt, sw), F32), pltpu.VMEM((2, nb * tt, sw), F32),
            pltpu.VMEM((nb, n_state), F32), pltpu.VMEM((nb, n_state), F32),
            pltpu.VMEM((nb * tt, D_MODEL), BF16), pltpu.VMEM((nb * tt, D_MODEL), F32),
        ],
        compiler_params=pltpu.CompilerParams(dimension_semantics=("arbitrary",), vmem_limit_bytes=VMEM_LIMIT),
        name="s5_glu",
    )(x, h0_re, h0_im, wre, wim, cre, cim, ab_re, ab_im, d_skip, wa, wb, ln, *w_ops)
    return outs[:3], [o.reshape(s) for o, s in zip(outs[3:], w_final)]


def _block_diag_slices(m, rows_per_group, cols_per_group):
    gps = LANES // S5_GROUP
    m = m.reshape(S5_GROUPS // gps, gps, rows_per_group, cols_per_group)
    eye = jnp.eye(gps, dtype=m.dtype)
    out = m[:, :, :, None, :] * eye[None, :, None, :, None]
    return out.reshape(S5_GROUPS // gps, gps * rows_per_group, gps * cols_per_group)


def _prepare(p):
    w = p['w_in'][0]
    win = jnp.concatenate(
        [w[:, 0:1536], w[:, 1544:2056], w[:, 2056:3592], w[:, 3600:4112], w[:, 1536:1544], w[:, 3592:3600],
         jnp.zeros((D_MODEL, D_IN_PAD - 4112), w.dtype)], axis=1).astype(BF16)
    at_lane = lambda v, lane0: jnp.pad(v, (lane0, LANES - lane0 - v.shape[0]))
    gp = jnp.stack([at_lane(p['gdn_A_log'][0], G_DEC), at_lane(p['gdn_dt_bias'][0], G_DEC),
                    at_lane(p['ml_b_i'][0], G_IN), at_lane(p['ml_b_f'][0], G_FG),
                    p['gdn_norm_w'][0], p['ml_norm_w'][0], jnp.zeros((LANES,), F32), jnp.zeros((LANES,), F32)])
    ab_re, ab_im, bb_re, bb_im = _s5_prep(p['s5_A_re'][0], p['s5_A_im'][0], p['s5_log_dt'][0],
                                          p['s5_B_re'][0], p['s5_B_im'][0])
    to_in = lambda bb: _block_diag_slices(
        bb.reshape(S5_GROUPS, S5_STATE, S5_GROUP).transpose(0, 2, 1), S5_GROUP, S5_STATE).astype(BF16)
    to_out = lambda c: _block_diag_slices(c.transpose(0, 2, 1), S5_STATE, S5_GROUP).astype(BF16)
    return dict(
        win=win, wout=p['w_out'][0].astype(BF16), convw=p['gdn_conv_w'][0], gp=gp,
        ln_mix=[jnp.stack([p['ln_mix_g'][l], p['ln_mix_b'][l]]) for l in range(DEPTH)],
        ln_ffn=[jnp.stack([p['ln_ffn_g'][l], p['ln_ffn_b'][l]]) for l in range(DEPTH)],
        rw_t=p['router_w'].T, rb_col=p['router_b'][:, None],
        s5_wre=to_in(bb_re), s5_wim=to_in(bb_im),
        s5_cre=to_out(p['s5_C_re'][0]), s5_cim=to_out(p['s5_C_im'][0]),
        s5_abre=ab_re.reshape(1, -1), s5_abim=ab_im.reshape(1, -1), s5_d=p['s5_D'][0][None, :],
        glu_a=p['s5_w_glu_a'][0].astype(BF16), glu_b=p['s5_w_glu_b'][0].astype(BF16),
    )


def _ffn(h, w, layer, out_row, experts=None, moe_w=None):
    meta, counts, order, cast = _router(h, w['rw_t'], w['rb_col'], moe_w if experts is None else None, layer)
    experts = cast if experts is None else experts
    return _moe(h, meta, counts, order, *experts, w['ln_ffn'][layer], out_row), experts


def _s5_layer(h, nb, h0_re, h0_im, w, moe_w=None):
    return _s5_scan(h, nb, h0_re, h0_im, w['s5_wre'], w['s5_wim'], w['s5_cre'], w['s5_cim'],
                    w['s5_abre'], w['s5_abim'], w['s5_d'], w['glu_a'], w['glu_b'], w['ln_mix'][1], moe_w)


def kernel(x_prompt, x_sample, state_gdn_conv, state_gdn_S, state_mlstm_C, state_mlstm_n, state_mlstm_m,
           state_s5_re, state_s5_im, w_in, gdn_conv_w, gdn_A_log, gdn_dt_bias, gdn_norm_w, ml_b_i, ml_b_f,
           ml_norm_w, w_out, s5_A_re, s5_A_im, s5_log_dt, s5_B_re, s5_B_im, s5_C_re, s5_C_im, s5_D,
           s5_w_glu_a, s5_w_glu_b, router_w, router_b, moe_w_gate, moe_w_up, moe_w_down,
           ln_mix_g, ln_mix_b, ln_ffn_g, ln_ffn_b):
    w = _prepare(dict(
        w_in=w_in, gdn_conv_w=gdn_conv_w, gdn_A_log=gdn_A_log, gdn_dt_bias=gdn_dt_bias, gdn_norm_w=gdn_norm_w,
        ml_b_i=ml_b_i, ml_b_f=ml_b_f, ml_norm_w=ml_norm_w, w_out=w_out, s5_A_re=s5_A_re, s5_A_im=s5_A_im,
        s5_log_dt=s5_log_dt, s5_B_re=s5_B_re, s5_B_im=s5_B_im, s5_C_re=s5_C_re, s5_C_im=s5_C_im, s5_D=s5_D,
        s5_w_glu_a=s5_w_glu_a, s5_w_glu_b=s5_w_glu_b, router_w=router_w, router_b=router_b,
        moe_w_gate=moe_w_gate, moe_w_up=moe_w_up, moe_w_down=moe_w_down,
        ln_mix_g=ln_mix_g, ln_mix_b=ln_mix_b, ln_ffn_g=ln_ffn_g, ln_ffn_b=ln_ffn_b))
    bp, lp, _ = x_prompt.shape
    bs, ls, _ = x_sample.shape
    assert ls == 1
    n_state = S5_GROUPS * S5_STATE

    moe_w = (moe_w_gate, moe_w_up, moe_w_down)
    (hs, s_cbuf, s_s, s_c, s_n, s_m), experts0 = _ab_decode(
        x_sample[:, 0], w['win'], w['wout'], w['convw'], w['gp'], w['ln_mix'][0],
        state_gdn_conv.reshape(bs, (CONV_W - 1) * A_CONV), state_gdn_S[:, 0], state_mlstm_C[:, 0],
        state_mlstm_n.reshape(bs, N_HEADS * HEAD_DIM), state_mlstm_m[:, 0], moe_w)

    h, p_hist, p_s, p_c, p_n, p_m = _ab_prompt(x_prompt, w['win'], w['wout'], w['convw'], w['gp'], w['ln_mix'][0])
    h = _ffn(h.reshape(bp * lp, D_MODEL), w, 0, lambda r: (r % lp) * bp + r // lp, experts=experts0)[0]
    zeros = jnp.zeros((bp, n_state), F32)
    (h, p_re, p_im), experts1 = _s5_layer(h, bp, zeros, zeros, w, moe_w)
    y_prompt = _ffn(h, w, 1, lambda r: (r % bp) * lp + r // bp, experts=experts1)[0].reshape(bp, lp, D_MODEL)

    hs = _ffn(hs, w, 0, lambda r: r, experts=experts0)[0]
    (hs, s_re, s_im), _ = _s5_layer(hs, bs, state_s5_re.reshape(bs, n_state), state_s5_im.reshape(bs, n_state), w)
    y_sample = _ffn(hs, w, 1, lambda r: r, experts=experts1)[0].reshape(bs, 1, D_MODEL)

    grp = lambda a, n: a.reshape(n, 1, S5_GROUPS, S5_STATE)
    return (
        y_prompt, y_sample,
        p_hist[:, None, SUBLANES - (CONV_W - 1):, :], p_s[:, None], p_c[:, None],
        p_n[:, None, :, 0, :], p_m[:, None, :, 0, 0], grp(p_re, bp), grp(p_im, bp),
        s_cbuf.reshape(bs, 1, CONV_W - 1, A_CONV), s_s[:, None], s_c[:, None],
        s_n.reshape(bs, 1, N_HEADS, HEAD_DIM), s_m[:, None], grp(s_re, bs), grp(s_im, bs),
    )
```

```python
import functools

import jax
import jax.numpy as jnp
from jax import lax
from jax.experimental import pallas as pl
from jax.experimental.pallas import tpu as pltpu

F32 = jnp.float32
BF16 = jnp.bfloat16
HIGHEST = lax.Precision.HIGHEST

D_MODEL = 1024
DEPTH = 2
N_HEADS = 4
HEAD_DIM = 128
CONV_W = 4
CHUNK = 64
A_CONV = 3 * N_HEADS * HEAD_DIM
S5_GROUP = 16
S5_GROUPS = D_MODEL // S5_GROUP
S5_STATE = 64
N_EXPERTS = 16
EXPERTS_PER_GROUP = 4
N_EXPERT_GROUPS = N_EXPERTS // EXPERTS_PER_GROUP
D_FF = 512
DN_ALPHA = (2 * DEPTH) ** 0.25
LN_EPS = 1e-5
RMS_EPS = 1e-6
NEG_BIG = -1e30

QA, KA, VA, ZA, QB, KB, VB, OB, GT = 0, 512, 1024, 1536, 2048, 2560, 3072, 3584, 4096
D_IN_PAD = 4224
G_DEC, G_BETA, G_IN, G_FG = 0, 4, 8, 12

S5_TT = 64

LANES = 128
SUBLANES = 8
VMEM_LIMIT = 56 * 1024 * 1024


def _bf(x):
    return x.astype(BF16)


def _sigmoid(x):
    return 1.0 / (1.0 + jnp.exp(-x))


def _softplus(x):
    return jnp.maximum(x, 0.0) + jnp.log(1.0 + jnp.exp(-jnp.abs(x)))


def _silu(x):
    return x * _sigmoid(x)


def _layer_norm(y, g, b):
    mu = jnp.mean(y, axis=-1, keepdims=True)
    yc = y - mu
    var = jnp.mean(yc * yc, axis=-1, keepdims=True)
    return yc * lax.rsqrt(var + LN_EPS) * g + b


def _rms(x, w):
    return x * lax.rsqrt(jnp.mean(x * x, axis=-1, keepdims=True) + RMS_EPS) * w


def _gate_transform(raw, gp):
    lane = lax.broadcasted_iota(jnp.int32, raw.shape, 1)
    dec = -jnp.exp(gp[0:1, :]) * _softplus(raw + gp[1:2, :])
    beta = _sigmoid(raw)
    ipre = raw + gp[2:3, :]
    logf = -_softplus(-(raw + gp[3:4, :]))
    return jnp.where(lane < G_BETA, dec,
                     jnp.where(lane < G_IN, beta,
                               jnp.where(lane < G_FG, ipre,
                                         jnp.where(lane < G_FG + N_HEADS, logf, 0.0))))


def _bnn(a, b):
    return lax.dot_general(_bf(a), _bf(b), (((2,), (1,)), ((0,), (0,))), preferred_element_type=F32)


def _bnt(a, b):
    return lax.dot_general(_bf(a), _bf(b), (((2,), (2,)), ((0,), (0,))), preferred_element_type=F32)


def _btn(a, b):
    return lax.dot_general(_bf(a), _bf(b), (((1,), (1,)), ((0,), (0,))), preferred_element_type=F32)


def _unit_lower_inverse_minus_eye(a, pack):
    p, c, _ = a.shape
    nb = p // pack
    w = pack * c
    r = -jnp.stack([jnp.concatenate([a[b * pack + j] for j in range(pack)], axis=1) for b in range(nb)])
    assert c & (c - 1) == 0
    block_of = lambda dim: lax.shift_right_logical(lax.broadcasted_iota(jnp.int32, (w, w), dim), c.bit_length() - 1)
    on_diag = block_of(0) == block_of(1)

    def block_diag(x):
        return jnp.where(on_diag, jnp.concatenate([x] * pack, axis=1), 0.0)

    steps = max(1, (c - 1).bit_length()) - 1
    q = _bnn(r, block_diag(r))
    for i in range(steps):
        bd = block_diag(q)
        if i + 1 < steps:
            both = _bnn(jnp.concatenate([r, q], axis=1), bd)
            rq, qq = both[:, :c, :], both[:, c:, :]
        else:
            rq, qq = _bnn(r, bd), None
        r = r + q + rq
        q = qq
    return jnp.stack([r[b][:, j * c:(j + 1) * c] for b in range(nb) for j in range(pack)])


def _ab_prompt_kernel(x_ref, win_ref, wout_ref, convw_ref, gp_ref, ln_ref,
                      h_ref, hist_ref, s_out, c_out, n_out, m_out,
                      proj, qkv, gates, gcum, merged, s_s, c_s, n_s, m_s,
                      u_s, w_s, attn_s, dlog_s, qk_s, dmax_s, kt_s, kbt_s, rows_s, *, tb, n_t):
    t = pl.program_id(1)
    nc = tb // CHUNK

    @pl.when(t == 0)
    def _():
        proj[0:SUBLANES, :] = jnp.zeros((SUBLANES, D_IN_PAD), F32)
        s_s[...] = jnp.zeros_like(s_s)
        c_s[...] = jnp.zeros_like(c_s)
        n_s[...] = jnp.zeros_like(n_s)
        m_s[...] = jnp.zeros_like(m_s)

    x = x_ref[...]
    xb = _bf(x)
    proj[SUBLANES:SUBLANES + tb, 0:A_CONV] = jnp.dot(xb, win_ref[:, 0:A_CONV], preferred_element_type=F32)
    proj[SUBLANES:SUBLANES + tb, A_CONV:] = jnp.dot(xb, win_ref[:, A_CONV:], preferred_element_type=F32)

    for blk in range(A_CONV // LANES):
        cs = slice(blk * LANES, (blk + 1) * LANES)
        acc = proj[SUBLANES:SUBLANES + tb, cs] * convw_ref[CONV_W - 1:CONV_W, cs]
        for j in range(1, CONV_W):
            acc = acc + proj[SUBLANES - j:SUBLANES - j + tb, cs] * convw_ref[CONV_W - 1 - j:CONV_W - j, cs]
        y = _silu(acc)
        if blk < 2 * N_HEADS:
            y = y * lax.rsqrt(jnp.sum(y * y, axis=-1, keepdims=True) + RMS_EPS)
            if blk < N_HEADS:
                y = y * HEAD_DIM ** -0.5
        qkv[:, cs] = y

    gt = _gate_transform(proj[SUBLANES:SUBLANES + tb, GT:GT + LANES], gp_ref[...])
    gates[...] = gt
    ri = lax.broadcasted_iota(jnp.int32, (tb, tb), 0)
    ci = lax.broadcasted_iota(jnp.int32, (tb, tb), 1)
    log2_chunk = CHUNK.bit_length() - 1
    same_chunk = lax.shift_right_logical(ri, log2_chunk) == lax.shift_right_logical(ci, log2_chunk)
    ltri = jnp.where(same_chunk, jnp.where(ri >= ci, 1.0, 0.0), 0.0)
    gcum[...] = jnp.dot(ltri, gt, preferred_element_type=F32, precision=HIGHEST)

    ii = lax.broadcasted_iota(jnp.int32, (CHUNK, CHUNK), 0)
    jj = lax.broadcasted_iota(jnp.int32, (CHUNK, CHUNK), 1)
    incl = ii >= jj
    strict = ii > jj
    gdn_w = gp_ref[4:5, :]
    ml_w = gp_ref[5:6, :]

    pairs = [(c, h) for c in range(nc) for h in range(N_HEADS)]

    def tile_heads(ref, row0, col0):
        return jnp.stack([ref[row0 + c * CHUNK:row0 + (c + 1) * CHUNK, col0 + h * HEAD_DIM:col0 + (h + 1) * HEAD_DIM]
                          for c, h in pairs])

    def tile_cols(ref, lane0):
        return jnp.stack([ref[c * CHUNK:(c + 1) * CHUNK, lane0 + h:lane0 + h + 1] for c, h in pairs])

    def tile_rows(transposed, lane0):
        return jnp.stack([transposed[c][lane0 + h:lane0 + h + 1, :] for c, h in pairs])

    cs_t = [gcum[c * CHUNK:(c + 1) * CHUNK, :].T for c in range(nc)]
    gt_t = [gates[c * CHUNK:(c + 1) * CHUNK, :].T for c in range(nc)]
    q3 = tile_heads(qkv, 0, QA)
    k3 = tile_heads(qkv, 0, KA)
    v3 = tile_heads(qkv, 0, VA)
    g_col3 = tile_cols(gcum, G_DEC)
    beta3 = tile_cols(gates, G_BETA)
    decay3 = jnp.where(incl, jnp.exp(jnp.where(incl, g_col3 - tile_rows(cs_t, G_DEC), 0.0)), 0.0)
    kb3 = k3 * beta3
    a_low3 = jnp.where(strict, _bnt(kb3, k3) * decay3, 0.0)
    attn_s[...] = _bnt(q3, k3) * decay3
    kbm3 = tile_heads(proj, SUBLANES, KB) * HEAD_DIM ** -0.5
    qk_s[...] = _bnt(tile_heads(proj, SUBLANES, QB), kbm3)
    kt_s[...] = jnp.stack([k3[p].T for p in range(len(pairs))])
    kbt_s[...] = jnp.stack([kbm3[p].T for p in range(len(pairs))])
    zero_row = jnp.zeros((1, CHUNK), F32)
    rows_s[...] = jnp.stack([jnp.concatenate(
        [cs_t[c][G_DEC + h:G_DEC + h + 1, :], cs_t[c][G_FG + h:G_FG + h + 1, :], gt_t[c][G_IN + h:G_IN + h + 1, :]]
        + [zero_row] * (SUBLANES - 3), axis=0) for c, h in pairs])
    r3 = _unit_lower_inverse_minus_eye(a_low3, N_HEADS)
    rhs3 = jnp.concatenate([v3 * beta3, kb3 * jnp.exp(g_col3)], axis=2)
    uw3 = rhs3 + _bnn(r3, rhs3)
    u_s[...] = uw3[:, :, :HEAD_DIM]
    w_s[...] = uw3[:, :, HEAD_DIM:]
    dlog3 = jnp.where(incl, tile_cols(gcum, G_FG) - tile_rows(cs_t, G_FG) + tile_rows(gt_t, G_IN), NEG_BIG)
    dlog_s[...] = dlog3
    dmax_s[...] = jnp.max(dlog3, axis=-1, keepdims=True)

    def chunk_body(c, carry):
        r0 = pl.multiple_of(c * CHUNK, CHUNK)
        rows = pl.ds(r0, CHUNK)
        prow = pl.ds(pl.multiple_of(r0 + SUBLANES, SUBLANES), CHUNK)
        last = pl.ds(r0 + CHUNK - 1, 1)
        pc = pl.ds(pl.multiple_of(c * N_HEADS, N_HEADS), N_HEADS)

        def heads(ref, rws, col0):
            return jnp.stack([ref[rws, col0 + h * HEAD_DIM:col0 + (h + 1) * HEAD_DIM] for h in range(N_HEADS)])

        def cols(ref, rws, lane0):
            return jnp.stack([ref[rws, lane0 + h:lane0 + h + 1] for h in range(N_HEADS)])

        q = heads(qkv, rows, QA)
        g_col = cols(gcum, rows, G_DEC)
        g_last = cols(gcum, last, G_DEC)
        qb = heads(proj, prow, QB)
        kbm = heads(proj, prow, KB) * HEAD_DIM ** -0.5
        vbm = heads(proj, prow, VB)
        b_col = cols(gcum, rows, G_FG)
        b_last = cols(gcum, last, G_FG)
        row_forms = rows_s[pc]
        g_row = row_forms[:, 0:1, :]
        b_row = row_forms[:, 1:2, :]
        i_row = row_forms[:, 2:3, :]
        s_old = s_s[...]
        c_old = c_s[...]
        n_old = n_s[...]
        m_old = m_s[:, 0:1, 0:1]
        inter = b_col + m_old
        mt = jnp.maximum(inter, dmax_s[pc])
        wts = jnp.exp(dlog_s[pc] - mt) * qk_s[pc]
        sc = jnp.exp(inter - mt)
        m_new = mt[:, CHUNK - 1:CHUNK, :]
        sd = jnp.exp(b_last + m_old - m_new)
        wk_row = jnp.exp(b_last - b_row + i_row - m_new)
        w_state = _bnn(w_s[pc], s_old)
        q_state = _bnn(q * jnp.exp(g_col), s_old)
        q_mem = _bnn(qb, c_old)
        w_val = _bnn(wts, vbm)
        kv = _bnn(kbt_s[pc] * wk_row, vbm)
        k_sum = _bnn(jnp.broadcast_to(wk_row, (N_HEADS, SUBLANES, CHUNK)), kbm)
        v_new = u_s[pc] - w_state
        o_a = q_state + _bnn(attn_s[pc], v_new)
        s_s[...] = s_old * jnp.exp(g_last) + _bnn(kt_s[pc] * jnp.exp(g_last - g_row), v_new)
        num = sc * q_mem + w_val
        den = sc * jnp.sum(qb * n_old[:, 0:1, :], axis=-1, keepdims=True) + jnp.sum(wts, axis=-1, keepdims=True)
        h_b = num / jnp.maximum(jnp.abs(den), jnp.exp(-mt))
        c_s[...] = sd * c_old + kv
        n_s[...] = sd * n_old + k_sum
        m_s[...] = jnp.broadcast_to(m_new, m_s.shape)
        o_n = _rms(o_a, gdn_w)
        h_n = _rms(h_b, ml_w)
        for h in range(N_HEADS):
            hs = slice(h * HEAD_DIM, (h + 1) * HEAD_DIM)
            hs2 = slice((N_HEADS + h) * HEAD_DIM, (N_HEADS + h + 1) * HEAD_DIM)
            merged[rows, hs] = o_n[h] * _silu(proj[prow, ZA + h * HEAD_DIM:ZA + (h + 1) * HEAD_DIM])
            merged[rows, hs2] = h_n[h] * _sigmoid(proj[prow, OB + h * HEAD_DIM:OB + (h + 1) * HEAD_DIM])
        return carry

    lax.fori_loop(0, nc, chunk_body, 0)

    mix = jnp.dot(_bf(merged[...]), wout_ref[...], preferred_element_type=F32)
    h_ref[...] = _layer_norm(DN_ALPHA * x + mix, ln_ref[0:1, :], ln_ref[1:2, :])

    proj[0:SUBLANES, 0:A_CONV] = proj[tb:tb + SUBLANES, 0:A_CONV]

    @pl.when(t == n_t - 1)
    def _():
        hist_ref[...] = proj[tb:tb + SUBLANES, 0:A_CONV]
        s_out[...] = s_s[...]
        c_out[...] = c_s[...]
        n_out[...] = n_s[...]
        m_out[...] = m_s[...]


def _expert_weight_cast_specs(moe_w, layer, n_steps, step_of):
    ops, in_specs, out_specs, out_shapes, shapes = [], [], [], [], []
    for w in moe_w:
        _, n_e, rows, cols = w.shape
        per_layer = n_e * rows
        blk = per_layer // n_steps
        assert per_layer % n_steps == 0 and blk % (2 * SUBLANES) == 0
        ops.append(w.reshape(w.shape[0] * per_layer, cols))
        in_specs.append(pl.BlockSpec((blk, cols), lambda *g: (layer * n_steps + step_of(*g), 0)))
        out_specs.append(pl.BlockSpec((blk, cols), lambda *g: (step_of(*g), 0)))
        out_shapes.append(jax.ShapeDtypeStruct((per_layer, cols), BF16))
        shapes.append((n_e, rows, cols))
    return ops, in_specs, out_specs, out_shapes, shapes


def _ab_prompt(x, win, wout, convw, gp, ln):
    bsz, length, _ = x.shape
    tb = min(512, length)
    assert length % tb == 0 and tb % CHUNK == 0 and length >= SUBLANES
    n_t = length // tb
    n_pairs = (tb // CHUNK) * N_HEADS
    const = lambda shape: pl.BlockSpec(shape, lambda b, t: (0,) * len(shape))
    per_b = lambda shape: pl.BlockSpec((None,) + shape, lambda b, t: (b,) + (0,) * len(shape))
    return pl.pallas_call(
        functools.partial(_ab_prompt_kernel, tb=tb, n_t=n_t),
        grid=(bsz, n_t),
        in_specs=[
            pl.BlockSpec((None, tb, D_MODEL), lambda b, t: (b, t, 0)),
            const((D_MODEL, D_IN_PAD)), const((2 * N_HEADS * HEAD_DIM, D_MODEL)),
            const((CONV_W, A_CONV)), const((SUBLANES, LANES)), const((2, D_MODEL)),
        ],
        out_specs=[
            pl.BlockSpec((None, tb, D_MODEL), lambda b, t: (b, t, 0)),
            per_b((SUBLANES, A_CONV)), per_b((N_HEADS, HEAD_DIM, HEAD_DIM)), per_b((N_HEADS, HEAD_DIM, HEAD_DIM)),
            per_b((N_HEADS, SUBLANES, LANES)), per_b((N_HEADS, SUBLANES, LANES)),
        ],
        out_shape=[
            jax.ShapeDtypeStruct((bsz, length, D_MODEL), F32),
            jax.ShapeDtypeStruct((bsz, SUBLANES, A_CONV), F32),
            jax.ShapeDtypeStruct((bsz, N_HEADS, HEAD_DIM, HEAD_DIM), F32),
            jax.ShapeDtypeStruct((bsz, N_HEADS, HEAD_DIM, HEAD_DIM), F32),
            jax.ShapeDtypeStruct((bsz, N_HEADS, SUBLANES, LANES), F32),
            jax.ShapeDtypeStruct((bsz, N_HEADS, SUBLANES, LANES), F32),
        ],
        scratch_shapes=[
            pltpu.VMEM((tb + SUBLANES, D_IN_PAD), F32),
            pltpu.VMEM((tb, A_CONV), F32),
            pltpu.VMEM((tb, LANES), F32),
            pltpu.VMEM((tb, LANES), F32),
            pltpu.VMEM((tb, 2 * N_HEADS * HEAD_DIM), F32),
            pltpu.VMEM((N_HEADS, HEAD_DIM, HEAD_DIM), F32),
            pltpu.VMEM((N_HEADS, HEAD_DIM, HEAD_DIM), F32),
            pltpu.VMEM((N_HEADS, SUBLANES, LANES), F32),
            pltpu.VMEM((N_HEADS, SUBLANES, LANES), F32),
            pltpu.VMEM((n_pairs, CHUNK, HEAD_DIM), F32),
            pltpu.VMEM((n_pairs, CHUNK, HEAD_DIM), F32),
            pltpu.VMEM((n_pairs, CHUNK, CHUNK), F32),
            pltpu.VMEM((n_pairs, CHUNK, CHUNK), F32),
            pltpu.VMEM((n_pairs, CHUNK, CHUNK), F32),
            pltpu.VMEM((n_pairs, CHUNK, 1), F32),
            pltpu.VMEM((n_pairs, HEAD_DIM, CHUNK), F32),
            pltpu.VMEM((n_pairs, HEAD_DIM, CHUNK), F32),
            pltpu.VMEM((n_pairs, SUBLANES, CHUNK), F32),
        ],
        compiler_params=pltpu.CompilerParams(
            dimension_semantics=("arbitrary", "arbitrary"), vmem_limit_bytes=VMEM_LIMIT),
        name="ab_prompt",
    )(x, win, wout, convw, gp, ln)


def _ab_decode_kernel(*refs, bb, n_steps, n_cast):
    x_ref, win_ref, wout_ref, convw_ref, gp_ref, ln_ref, cbuf_ref, s_in, c_in, n_in, m_in = refs[:11]
    y_ref, cbuf_out, s_out, c_out, n_out, m_out = refs[11 + n_cast:17 + n_cast]
    proj, merged = refs[17 + 2 * n_cast:]
    for src, dst in zip(refs[11:11 + n_cast], refs[17 + n_cast:17 + 2 * n_cast]):
        dst[...] = _bf(src[...])
    i = pl.program_id(0)

    @pl.when(i == 0)
    def _():
        proj[...] = jnp.dot(_bf(x_ref[...]), win_ref[...], preferred_element_type=F32)

    rows = pl.ds(pl.multiple_of(i * bb, bb), bb)
    raw = proj[rows, 0:A_CONV]
    cbuf = cbuf_ref[...]
    conv = raw * convw_ref[CONV_W - 1:CONV_W, :]
    for j in range(CONV_W - 1):
        conv = conv + cbuf[:, j * A_CONV:(j + 1) * A_CONV] * convw_ref[j:j + 1, :]
    cbuf_out[:, 0:(CONV_W - 2) * A_CONV] = cbuf[:, A_CONV:(CONV_W - 1) * A_CONV]
    cbuf_out[:, (CONV_W - 2) * A_CONV:(CONV_W - 1) * A_CONV] = raw
    act = _silu(conv)
    gt = _gate_transform(proj[rows, GT:GT + LANES], gp_ref[...])
    gdn_w = gp_ref[4:5, :]
    ml_w = gp_ref[5:6, :]
    m_all = m_in[...]

    for h in range(N_HEADS):
        def head(off):
            return act[:, off + h * HEAD_DIM:off + (h + 1) * HEAD_DIM]
        q = head(QA)
        q = q * lax.rsqrt(jnp.sum(q * q, axis=-1, keepdims=True) + RMS_EPS) * HEAD_DIM ** -0.5
        k = head(KA)
        k = k * lax.rsqrt(jnp.sum(k * k, axis=-1, keepdims=True) + RMS_EPS)
        v = head(VA)
        k_t = k.T
        qk = jnp.sum(q * k, axis=-1, keepdims=True)
        qb = proj[rows, QB + h * HEAD_DIM:QB + (h + 1) * HEAD_DIM]
        kbm = proj[rows, KB + h * HEAD_DIM:KB + (h + 1) * HEAD_DIM] * HEAD_DIM ** -0.5
        vbm = proj[rows, VB + h * HEAD_DIM:VB + (h + 1) * HEAD_DIM]
        kb_t = kbm.T
        pad = [jnp.zeros((1, HEAD_DIM), F32)] * (SUBLANES - 2)
        read_s = _bnn(jnp.stack([jnp.concatenate([k[b:b + 1], q[b:b + 1]] + pad, axis=0) for b in range(bb)]),
                      s_in[:, h])
        read_c = _bnn(jnp.stack([jnp.concatenate([qb[b:b + 1], qb[b:b + 1]] + pad, axis=0) for b in range(bb)]),
                      c_in[:, h])
        qkb = jnp.sum(qb * kbm, axis=-1, keepdims=True)
        n_old = n_in[:, h * HEAD_DIM:(h + 1) * HEAD_DIM]
        qn = jnp.sum(qb * n_old, axis=-1, keepdims=True)
        o_rows = []
        hb_rows = []
        n_rows = []
        m_rows = []
        for b in range(bb):
            s_old = s_in[b, h]
            k_c = k_t[:, b:b + 1]
            e_g = jnp.exp(gt[b:b + 1, G_DEC + h:G_DEC + h + 1])
            beta = gt[b:b + 1, G_BETA + h:G_BETA + h + 1]
            k_s = read_s[b, 0:1, :]
            q_s = read_s[b, 1:2, :]
            v_new = v[b:b + 1, :] * beta - (beta * e_g) * k_s
            o_rows.append(e_g * q_s + qk[b:b + 1, :] * v_new)
            s_out[b, h] = s_old * e_g + k_c * v_new
            c_old = c_in[b, h]
            m_old = m_all[b:b + 1, h:h + 1]
            i_pre = gt[b:b + 1, G_IN + h:G_IN + h + 1]
            logf = gt[b:b + 1, G_FG + h:G_FG + h + 1]
            inter = logf + m_old
            mt = jnp.maximum(inter, i_pre)
            w_in = jnp.exp(i_pre - mt)
            sc = jnp.exp(inter - mt)
            wts = w_in * qkb[b:b + 1, :]
            q_cm = read_c[b, 0:1, :]
            num = sc * q_cm + wts * vbm[b:b + 1, :]
            den = sc * qn[b:b + 1, :] + wts
            hb_rows.append(num / jnp.maximum(jnp.abs(den), jnp.exp(-mt)))
            c_out[b, h] = sc * c_old + (w_in * kb_t[:, b:b + 1]) * vbm[b:b + 1, :]
            n_rows.append(sc * n_old[b:b + 1, :] + w_in * kbm[b:b + 1, :])
            m_rows.append(mt)
        o_a = jnp.concatenate(o_rows, axis=0)
        h_b = jnp.concatenate(hb_rows, axis=0)
        n_out[:, h * HEAD_DIM:(h + 1) * HEAD_DIM] = jnp.concatenate(n_rows, axis=0)
        m_out[:, h:h + 1] = jnp.concatenate(m_rows, axis=0)
        z = proj[rows, ZA + h * HEAD_DIM:ZA + (h + 1) * HEAD_DIM]
        merged[rows, h * HEAD_DIM:(h + 1) * HEAD_DIM] = _rms(o_a, gdn_w) * _silu(z)
        o_gate = proj[rows, OB + h * HEAD_DIM:OB + (h + 1) * HEAD_DIM]
        merged[rows, N_HEADS * HEAD_DIM + h * HEAD_DIM:N_HEADS * HEAD_DIM + (h + 1) * HEAD_DIM] = (
            _rms(h_b, ml_w) * _sigmoid(o_gate))

    @pl.when(i == n_steps - 1)
    def _():
        mix = jnp.dot(_bf(merged[...]), wout_ref[...], preferred_element_type=F32)
        y_ref[...] = _layer_norm(DN_ALPHA * x_ref[...] + mix, ln_ref[0:1, :], ln_ref[1:2, :])


def _ab_decode(x, win, wout, convw, gp, ln, cbuf, s0, c0, n0, m0, moe_w):
    nb = x.shape[0]
    bb = SUBLANES
    assert nb % bb == 0
    n_steps = nb // bb
    const = lambda shape: pl.BlockSpec(shape, lambda i: (0,) * len(shape))
    blk = lambda shape: pl.BlockSpec((bb,) + shape, lambda i: (i,) + (0,) * len(shape))
    hist = (CONV_W - 1) * A_CONV
    width = N_HEADS * HEAD_DIM
    w_ops, w_in, w_out, w_shapes, w_final = _expert_weight_cast_specs(moe_w, 0, n_steps, lambda i: i)
    outs = pl.pallas_call(
        functools.partial(_ab_decode_kernel, bb=bb, n_steps=n_steps, n_cast=len(w_ops)),
        grid=(n_steps,),
        in_specs=[
            const((nb, D_MODEL)), const((D_MODEL, D_IN_PAD)), const((2 * width, D_MODEL)),
            const((CONV_W, A_CONV)), const((SUBLANES, LANES)), const((2, D_MODEL)),
            blk((hist,)), blk((N_HEADS, HEAD_DIM, HEAD_DIM)), blk((N_HEADS, HEAD_DIM, HEAD_DIM)),
            blk((width,)), blk((N_HEADS,)),
        ] + w_in,
        out_specs=[
            const((nb, D_MODEL)), blk((hist,)), blk((N_HEADS, HEAD_DIM, HEAD_DIM)),
            blk((N_HEADS, HEAD_DIM, HEAD_DIM)), blk((width,)), blk((N_HEADS,)),
        ] + w_out,
        out_shape=[
            jax.ShapeDtypeStruct((nb, D_MODEL), F32),
            jax.ShapeDtypeStruct((nb, hist), F32),
            jax.ShapeDtypeStruct((nb, N_HEADS, HEAD_DIM, HEAD_DIM), F32),
            jax.ShapeDtypeStruct((nb, N_HEADS, HEAD_DIM, HEAD_DIM), F32),
            jax.ShapeDtypeStruct((nb, width), F32),
            jax.ShapeDtypeStruct((nb, N_HEADS), F32),
        ] + w_shapes,
        scratch_shapes=[pltpu.VMEM((nb, D_IN_PAD), F32), pltpu.VMEM((nb, 2 * width), F32)],
        compiler_params=pltpu.CompilerParams(dimension_semantics=("arbitrary",), vmem_limit_bytes=VMEM_LIMIT),
        name="ab_decode",
    )(x, win, wout, convw, gp, ln, cbuf, s0, c0, n0, m0, *w_ops)
    return outs[:6], [o.reshape(s) for o, s in zip(outs[6:], w_final)]


def _second_largest_sum(a, b, c, d):
    hi1, lo1 = jnp.maximum(a, b), jnp.minimum(a, b)
    hi2, lo2 = jnp.maximum(c, d), jnp.minimum(c, d)
    return jnp.maximum(hi1, hi2) + jnp.maximum(jnp.minimum(hi1, hi2), jnp.maximum(lo1, lo2))


def _first_argmax(vals):
    best_v = vals[0]
    best_i = jnp.zeros(vals[0].shape, jnp.int32)
    for j in range(1, len(vals)):
        better = vals[j] > best_v
        best_v = jnp.where(better, vals[j], best_v)
        best_i = jnp.where(better, j, best_i)
    return best_i


def _router_kernel(*refs, n_steps, n_cast):
    x_ref, rw_ref, rb_ref = refs[:3]
    meta_ref, cnt_ref, order_ref = refs[3 + n_cast:6 + n_cast]
    carry = refs[6 + 2 * n_cast]
    for src, dst in zip(refs[3:3 + n_cast], refs[6 + n_cast:6 + 2 * n_cast]):
        dst[...] = _bf(src[...])
    i = pl.program_id(0)
    tm = x_ref.shape[0]

    @pl.when(i == 0)
    def _():
        carry[...] = jnp.zeros_like(carry)

    x = x_ref[...]
    rw = rw_ref[...]
    x_hi = _bf(x)
    x_lo = _bf(x - x_hi.astype(F32))
    w_hi = _bf(rw)
    w_lo = _bf(rw - w_hi.astype(F32))
    nt = lambda a, b: lax.dot_general(a, b, (((1,), (1,)), ((), ())), preferred_element_type=F32)
    logits = nt(w_hi, x_hi) + (nt(w_hi, x_lo) + nt(w_lo, x_hi))
    ex = jnp.exp(logits - jnp.max(logits, axis=0, keepdims=True))
    probs = ex / jnp.sum(ex, axis=0, keepdims=True)
    sel = probs + rb_ref[...]
    p = [probs[j:j + 1, :] for j in range(N_EXPERTS)]
    s = [sel[j:j + 1, :] for j in range(N_EXPERTS)]
    scores = [_second_largest_sum(*s[EXPERTS_PER_GROUP * g:EXPERTS_PER_GROUP * (g + 1)])
              for g in range(N_EXPERT_GROUPS)]
    best = _first_argmax(scores)
    masked = [jnp.where(best == j // EXPERTS_PER_GROUP, s[j], -jnp.inf) for j in range(N_EXPERTS)]
    i1 = _first_argmax(masked)
    i2 = _first_argmax([jnp.where(i1 == j, -jnp.inf, masked[j]) for j in range(N_EXPERTS)])
    zero = jnp.zeros_like(p[0])
    p1 = functools.reduce(lambda a, b: a + b, [jnp.where(i1 == j, p[j], zero) for j in range(N_EXPERTS)])
    p2 = functools.reduce(lambda a, b: a + b, [jnp.where(i2 == j, p[j], zero) for j in range(N_EXPERTS)])
    tot = p1 + p2
    rows = [jnp.where(i1 == j, p1 / tot, zero) + jnp.where(i2 == j, p2 / tot, zero) for j in range(N_EXPERTS)]
    in_group = [best == g for g in range(N_EXPERT_GROUPS)]
    local = [functools.reduce(lambda a, b: a + b,
                              [jnp.where(in_group[g], rows[EXPERTS_PER_GROUP * g + e], zero)
                               for g in range(N_EXPERT_GROUPS)])
             for e in range(EXPERTS_PER_GROUP)]
    onehot = jnp.concatenate([jnp.where(m, 1.0, 0.0) for m in in_group]
                             + [jnp.zeros((SUBLANES - N_EXPERT_GROUPS, tm), F32)], axis=0)
    ri = lax.broadcasted_iota(jnp.int32, (tm, tm), 0)
    ci = lax.broadcasted_iota(jnp.int32, (tm, tm), 1)
    incl = jnp.dot(_bf(onehot), _bf(jnp.where(ri <= ci, 1.0, 0.0)), preferred_element_type=F32)
    prev = carry[...]
    rank = jnp.sum(onehot * (incl - 1.0 + prev[:, 0:1]), axis=0, keepdims=True)
    carry[...] = prev + incl[:, tm - 1:tm]
    row_id = (i * tm + lax.broadcasted_iota(jnp.int32, (1, tm), 1)).astype(F32)
    meta_ref[...] = jnp.concatenate([row_id] + local + [best.astype(F32), rank, zero], axis=0).T
    order_ref[...] = jnp.concatenate([best.astype(F32), rank] + [zero] * (SUBLANES - 2), axis=0)

    @pl.when(i == n_steps - 1)
    def _():
        cnt_ref[...] = carry[...]


def _router(x, rw_t, rb_col, moe_w=None, layer=0):
    t = x.shape[0]
    tm = min(512, t)
    assert t % tm == 0
    n_steps = t // tm
    if moe_w is None:
        w_ops, w_in, w_out, w_shapes, w_final = [], [], [], [], []
    else:
        w_ops, w_in, w_out, w_shapes, w_final = _expert_weight_cast_specs(moe_w, layer, n_steps, lambda i: i)
    outs = pl.pallas_call(
        functools.partial(_router_kernel, n_steps=n_steps, n_cast=len(w_ops)),
        grid=(n_steps,),
        in_specs=[pl.BlockSpec((tm, D_MODEL), lambda i: (i, 0)),
                  pl.BlockSpec((N_EXPERTS, D_MODEL), lambda i: (0, 0)),
                  pl.BlockSpec((N_EXPERTS, 1), lambda i: (0, 0))] + w_in,
        out_specs=[pl.BlockSpec((tm, SUBLANES), lambda i: (i, 0)),
                   pl.BlockSpec((SUBLANES, LANES), lambda i: (0, 0)),
                   pl.BlockSpec((SUBLANES, tm), lambda i: (0, i))] + w_out,
        out_shape=[jax.ShapeDtypeStruct((t, SUBLANES), F32), jax.ShapeDtypeStruct((SUBLANES, LANES), F32),
                   jax.ShapeDtypeStruct((SUBLANES, t), F32)] + w_shapes,
        scratch_shapes=[pltpu.VMEM((SUBLANES, LANES), F32)],
        compiler_params=pltpu.CompilerParams(dimension_semantics=("arbitrary",), vmem_limit_bytes=VMEM_LIMIT),
        name="router",
    )(x, rw_t, rb_col, *w_ops)
    return outs[0], outs[1], outs[2], [o.reshape(s) for o, s in zip(outs[3:], w_final)]


def _moe_kernel(tg_ref, nv_ref, src_cur, dst_cur, src_nxt, cw_ref, x_hbm, wg_ref, wu_ref, wd_ref, ln_ref,
                y_hbm, xg, stage, gsem, ssem, *, tm, n_tiles):
    i = pl.program_id(0)
    slot = lax.rem(i, 2)
    other = 1 - slot
    n_valid = nv_ref[i]
    n_prev = jnp.where(i >= 1, nv_ref[jnp.maximum(i - 1, 0)], 0)
    n_back2 = jnp.where(i >= 2, nv_ref[jnp.maximum(i - 2, 0)], 0)
    has_next = jnp.logical_and(i + 1 < n_tiles, nv_ref[jnp.minimum(i + 1, n_tiles - 1)] > 0)

    def gather_row(tok_ref, r, dst_slot):
        return pltpu.make_async_copy(x_hbm.at[pl.ds(tok_ref[0, r], 1), :], xg.at[dst_slot, pl.ds(r, 1), :],
                                     gsem.at[dst_slot])

    def scatter_row(tok_ref, r, src_slot):
        return pltpu.make_async_copy(stage.at[src_slot, pl.ds(r, 1), :], y_hbm.at[pl.ds(tok_ref[0, r], 1), :],
                                     ssem.at[src_slot])

    def scatter_wait(count, src_slot):
        for p in [1 << b for b in range(tm.bit_length())]:
            @pl.when(lax.bitwise_and(count, p) != 0)
            def _():
                pltpu.make_async_copy(stage.at[src_slot, pl.ds(0, p), :], y_hbm.at[pl.ds(0, p), :],
                                      ssem.at[src_slot]).wait()

    @pl.when(jnp.logical_and(i == 0, n_valid > 0))
    def _():
        def body(r, c):
            gather_row(src_cur, r, 0).start()
            return c
        lax.fori_loop(0, tm, body, 0, unroll=8)

    @pl.when(n_valid > 0)
    def _():
        pltpu.make_async_copy(x_hbm.at[pl.ds(0, tm), :], xg.at[slot], gsem.at[slot]).wait()

    for parity in range(2):
        @pl.when(jnp.logical_and(has_next, slot == parity))
        def _():
            for r in range(tm):
                gather_row(src_nxt, r, 1 - parity).start(priority=r % 2)

    @pl.when(i >= 2)
    def _():
        scatter_wait(n_back2, slot)

    def experts_on_rows(n_rows):
        rs = slice(0, n_rows)
        x = xg[slot, rs, :]
        xb = _bf(x)
        cw = cw_ref[rs, :]
        acc = jnp.zeros((n_rows, D_MODEL), F32)
        for e in range(EXPERTS_PER_GROUP):
            g = jnp.dot(xb, wg_ref[e], preferred_element_type=F32)
            u = jnp.dot(xb, wu_ref[e], preferred_element_type=F32)
            hid = _silu(g) * u * cw[:, 1 + e:2 + e]
            acc = acc + jnp.dot(_bf(hid), wd_ref[e], preferred_element_type=F32)
        stage[slot, rs, :] = _layer_norm(DN_ALPHA * x + acc, ln_ref[0:1, :], ln_ref[1:2, :])

    half = tm // 2 if tm >= 512 else 0

    @pl.when(n_valid > half)
    def _():
        experts_on_rows(tm)

    if half:
        @pl.when(jnp.logical_and(n_valid > 0, n_valid <= half))
        def _():
            experts_on_rows(half)

    for parity in range(2):
        @pl.when(jnp.logical_and(n_valid == tm, slot == parity))
        def _():
            for r in range(tm):
                scatter_row(dst_cur, r, parity).start(priority=r % 2)

    @pl.when(n_valid < tm)
    def _():
        for r in range(tm):
            @pl.when(r < n_valid)
            def _():
                scatter_row(dst_cur, r, slot).start(priority=r % 2)

    @pl.when(i == n_tiles - 1)
    def _():
        scatter_wait(n_prev, other)
        scatter_wait(n_valid, slot)


def _moe(x, meta, counts, order, wg, wu, wd, ln, out_row):
    t = x.shape[0]
    tm = min(512, t)
    assert t % tm == 0 and tm & (tm - 1) == 0
    n_tiles = t // tm + N_EXPERT_GROUPS
    n_slots = n_tiles * tm
    grp = order[0].astype(jnp.int32)
    rank = order[1].astype(jnp.int32)
    cnt = counts[:N_EXPERT_GROUPS, 0].astype(jnp.int32)
    padded = ((cnt + tm - 1) // tm) * tm
    g_end = jnp.cumsum(padded)
    g_off = g_end - padded
    pos = jnp.take(g_off, grp) + rank
    slot_meta = jnp.zeros((n_slots, SUBLANES), F32).at[pos].set(meta, unique_indices=True, mode='promise_in_bounds')
    tok_of_slot = slot_meta[:, 0].astype(jnp.int32)
    tile_start = jnp.arange(n_tiles, dtype=jnp.int32) * tm
    tile_grp = jnp.minimum(jnp.sum(tile_start[:, None] >= g_end[None, :], axis=1), N_EXPERT_GROUPS - 1).astype(jnp.int32)
    tile_valid = jnp.clip(jnp.take(g_off + cnt, tile_grp) - tile_start, 0, tm).astype(jnp.int32)
    src3 = tok_of_slot.reshape(n_tiles, 1, tm)
    dst3 = out_row(tok_of_slot).reshape(n_tiles, 1, tm)

    grid_spec = pltpu.PrefetchScalarGridSpec(
        num_scalar_prefetch=2,
        grid=(n_tiles,),
        in_specs=[
            pl.BlockSpec((None, 1, tm), lambda i, tg, nv: (i, 0, 0), memory_space=pltpu.SMEM),
            pl.BlockSpec((None, 1, tm), lambda i, tg, nv: (i, 0, 0), memory_space=pltpu.SMEM),
            pl.BlockSpec((None, 1, tm), lambda i, tg, nv: (jnp.minimum(i + 1, n_tiles - 1), 0, 0),
                         memory_space=pltpu.SMEM),
            pl.BlockSpec((tm, SUBLANES), lambda i, tg, nv: (i, 0)),
            pl.BlockSpec(memory_space=pl.ANY),
            pl.BlockSpec((EXPERTS_PER_GROUP, D_MODEL, D_FF), lambda i, tg, nv: (tg[i], 0, 0)),
            pl.BlockSpec((EXPERTS_PER_GROUP, D_MODEL, D_FF), lambda i, tg, nv: (tg[i], 0, 0)),
            pl.BlockSpec((EXPERTS_PER_GROUP, D_FF, D_MODEL), lambda i, tg, nv: (tg[i], 0, 0)),
            pl.BlockSpec((2, D_MODEL), lambda i, tg, nv: (0, 0)),
        ],
        out_specs=pl.BlockSpec(memory_space=pl.ANY),
        scratch_shapes=[pltpu.VMEM((2, tm, D_MODEL), F32), pltpu.VMEM((2, tm, D_MODEL), F32),
                        pltpu.SemaphoreType.DMA((2,)), pltpu.SemaphoreType.DMA((2,))],
    )
    return pl.pallas_call(
        functools.partial(_moe_kernel, tm=tm, n_tiles=n_tiles),
        grid_spec=grid_spec,
        out_shape=jax.ShapeDtypeStruct((t, D_MODEL), F32),
        compiler_params=pltpu.CompilerParams(dimension_semantics=("arbitrary",), vmem_limit_bytes=VMEM_LIMIT),
        name="moe",
    )(tile_grp, tile_valid, src3, dst3, src3, slot_meta, x, wg, wu, wd, ln)


def _s5_prep_kernel(are_ref, aim_ref, ldt_ref, bre_ref, bim_ref, abre_ref, abim_ref, bbre_ref, bbim_ref):
    a_r = are_ref[...]
    a_i = aim_ref[...]
    dt = jnp.exp(ldt_ref[...])
    mag = jnp.exp(dt * a_r)
    ab_re = mag * jnp.cos(dt * a_i)
    ab_im = mag * jnp.sin(dt * a_i)
    den = a_r * a_r + a_i * a_i
    nr = ab_re - 1.0
    z_re = (nr * a_r + ab_im * a_i) / den
    z_im = (ab_im * a_r - nr * a_i) / den
    abre_ref[...] = ab_re
    abim_ref[...] = ab_im
    bbre_ref[...] = z_re * bre_ref[...] - z_im * bim_ref[...]
    bbim_ref[...] = z_re * bim_ref[...] + z_im * bre_ref[...]


def _s5_prep(a_re, a_im, log_dt, b_re, b_im):
    n = S5_GROUPS * S5_STATE
    col = jax.ShapeDtypeStruct((n, 1), F32)
    mat = jax.ShapeDtypeStruct((n, S5_GROUP), F32)
    ldt = jnp.broadcast_to(log_dt[:, None], (S5_GROUPS, S5_STATE)).reshape(n, 1)
    return pl.pallas_call(_s5_prep_kernel, out_shape=[col, col, mat, mat], name="s5_prep")(
        a_re.reshape(n, 1), a_im.reshape(n, 1), ldt, b_re.reshape(n, S5_GROUP), b_im.reshape(n, S5_GROUP))


def _gelu_tanh(x):
    return 0.5 * x * (1.0 + jnp.tanh(0.7978845608028654 * (x + 0.044715 * (x * x * x))))


def _s5_kernel(*refs, nb, tt, n_t, n_cast):
    (x_ref, h0re_ref, h0im_ref, wre_ref, wim_ref, cre_ref, cim_ref, abre_ref, abim_ref, d_ref,
     wa_ref, wb_ref, ln_ref) = refs[:13]
    cast_in = refs[13:13 + n_cast]
    o_ref, hre_out, him_out = refs[13 + n_cast:16 + n_cast]
    cast_out = refs[16 + n_cast:16 + 2 * n_cast]
    bu_re, bu_im, hre_s, him_s, gy_ref, glu = refs[16 + 2 * n_cast:]
    t = pl.program_id(0)
    sw = wre_ref.shape[2]
    for src, dst in zip(cast_in, cast_out):
        dst[...] = _bf(src[...])

    @pl.when(t == 0)
    def _():
        hre_s[...] = h0re_ref[...]
        him_s[...] = h0im_ref[...]

    for k in range(D_MODEL // LANES):
        ls = slice(k * LANES, (k + 1) * LANES)
        ss = slice(k * sw, (k + 1) * sw)
        buf = k % 2
        xv = x_ref[:, ls]
        xb = _bf(xv)
        bu_re[buf] = jnp.dot(xb, wre_ref[k], preferred_element_type=F32)
        bu_im[buf] = jnp.dot(xb, wim_ref[k], preferred_element_type=F32)
        a_re = jnp.broadcast_to(abre_ref[:, ss], (nb, sw))
        a_im = jnp.broadcast_to(abim_ref[:, ss], (nb, sw))

        def step(s, carry, buf=buf, a_re=a_re, a_im=a_im):
            h_re, h_im = carry
            rows = pl.ds(pl.multiple_of(s * nb, nb), nb)
            n_re = a_re * h_re - a_im * h_im + bu_re[buf, rows, :]
            n_im = a_re * h_im + a_im * h_re + bu_im[buf, rows, :]
            bu_re[buf, rows, :] = n_re
            bu_im[buf, rows, :] = n_im
            return n_re, n_im

        h_re, h_im = lax.fori_loop(0, tt, step, (hre_s[:, ss], him_s[:, ss]), unroll=True)
        hre_s[:, ss] = h_re
        him_s[:, ss] = h_im
        y = (jnp.dot(_bf(bu_re[buf]), cre_ref[k], preferred_element_type=F32)
             - jnp.dot(_bf(bu_im[buf]), cim_ref[k], preferred_element_type=F32)
             + d_ref[:, ls] * xv)
        gy_ref[:, ls] = _gelu_tanh(y).astype(gy_ref.dtype)

    gy = gy_ref[...]
    width = 2 * LANES
    for n in range(D_MODEL // width):
        ns = slice(n * width, (n + 1) * width)
        a = jnp.dot(gy, wa_ref[:, ns], preferred_element_type=F32)
        b = jnp.dot(gy, wb_ref[:, ns], preferred_element_type=F32)
        glu[:, ns] = a * _sigmoid(b)
    o_ref[...] = _layer_norm(DN_ALPHA * x_ref[...] + glu[...], ln_ref[0:1, :], ln_ref[1:2, :])

    @pl.when(t == n_t - 1)
    def _():
        hre_out[...] = hre_s[...]
        him_out[...] = him_s[...]


def _s5_scan(x, nb, h0_re, h0_im, wre, wim, cre, cim, ab_re, ab_im, d_skip, wa, wb, ln, moe_w=None):
    length = x.shape[0] // nb
    tt = min(S5_TT, length)
    assert length % tt == 0 and nb % SUBLANES == 0
    n_t = length // tt
    n_k = D_MODEL // LANES
    sw = (LANES // S5_GROUP) * S5_STATE
    n_state = S5_GROUPS * S5_STATE
    const = lambda shape: pl.BlockSpec(shape, lambda t: (0,) * len(shape))
    if moe_w is None:
        w_ops, w_in, w_out, w_shapes, w_final = [], [], [], [], []
    else:
        w_ops, w_in, w_out, w_shapes, w_final = _expert_weight_cast_specs(moe_w, 1, n_t, lambda t: t)
    outs = pl.pallas_call(
        functools.partial(_s5_kernel, nb=nb, tt=tt, n_t=n_t, n_cast=len(w_ops)),
        grid=(n_t,),
        in_specs=[
            pl.BlockSpec((tt * nb, D_MODEL), lambda t: (t, 0)),
            const((nb, n_state)), const((nb, n_state)),
            const((n_k, LANES, sw)), const((n_k, LANES, sw)), const((n_k, sw, LANES)), const((n_k, sw, LANES)),
            const((1, n_state)), const((1, n_state)), const((1, D_MODEL)),
            const((D_MODEL, D_MODEL)), const((D_MODEL, D_MODEL)), const((2, D_MODEL)),
        ] + w_in,
        out_specs=[pl.BlockSpec((tt * nb, D_MODEL), lambda t: (t, 0)), const((nb, n_state)),
                   const((nb, n_state))] + w_out,
        out_shape=[
            jax.ShapeDtypeStruct((length * nb, D_MODEL), F32),
            jax.ShapeDtypeStruct((nb, n_state), F32),
            jax.ShapeDtypeStruct((nb, n_state), F32),
        ] + w_shapes,
        scratch_shapes=[
            pltpu.VMEM((2, nb * tt, sw), F32), pltpu.VMEM((2, nb * tt, sw), F32),
            pltpu.VMEM((nb, n_state), F32), pltpu.VMEM((nb, n_state), F32),
            pltpu.VMEM((nb * tt, D_MODEL), BF16), pltpu.VMEM((nb * tt, D_MODEL), F32),
        ],
        compiler_params=pltpu.CompilerParams(dimension_semantics=("arbitrary",), vmem_limit_bytes=VMEM_LIMIT),
        name="s5_glu",
    )(x, h0_re, h0_im, wre, wim, cre, cim, ab_re, ab_im, d_skip, wa, wb, ln, *w_ops)
    return outs[:3], [o.reshape(s) for o, s in zip(outs[3:], w_final)]


def _block_diag_slices(m, rows_per_group, cols_per_group):
    gps = LANES // S5_GROUP
    m = m.reshape(S5_GROUPS // gps, gps, rows_per_group, cols_per_group)
    eye = jnp.eye(gps, dtype=m.dtype)
    out = m[:, :, :, None, :] * eye[None, :, None, :, None]
    return out.reshape(S5_GROUPS // gps, gps * rows_per_group, gps * cols_per_group)


def _prepare(p):
    w = p['w_in'][0]
    win = jnp.concatenate(
        [w[:, 0:1536], w[:, 1544:2056], w[:, 2056:3592], w[:, 3600:4112], w[:, 1536:1544], w[:, 3592:3600],
         jnp.zeros((D_MODEL, D_IN_PAD - 4112), w.dtype)], axis=1).astype(BF16)
    at_lane = lambda v, lane0: jnp.pad(v, (lane0, LANES - lane0 - v.shape[0]))
    gp = jnp.stack([at_lane(p['gdn_A_log'][0], G_DEC), at_lane(p['gdn_dt_bias'][0], G_DEC),
                    at_lane(p['ml_b_i'][0], G_IN), at_lane(p['ml_b_f'][0], G_FG),
                    p['gdn_norm_w'][0], p['ml_norm_w'][0], jnp.zeros((LANES,), F32), jnp.zeros((LANES,), F32)])
    ab_re, ab_im, bb_re, bb_im = _s5_prep(p['s5_A_re'][0], p['s5_A_im'][0], p['s5_log_dt'][0],
                                          p['s5_B_re'][0], p['s5_B_im'][0])
    to_in = lambda bb: _block_diag_slices(
        bb.reshape(S5_GROUPS, S5_STATE, S5_GROUP).transpose(0, 2, 1), S5_GROUP, S5_STATE).astype(BF16)
    to_out = lambda c: _block_diag_slices(c.transpose(0, 2, 1), S5_STATE, S5_GROUP).astype(BF16)
    return dict(
        win=win, wout=p['w_out'][0].astype(BF16), convw=p['gdn_conv_w'][0], gp=gp,
        ln_mix=[jnp.stack([p['ln_mix_g'][l], p['ln_mix_b'][l]]) for l in range(DEPTH)],
        ln_ffn=[jnp.stack([p['ln_ffn_g'][l], p['ln_ffn_b'][l]]) for l in range(DEPTH)],
        rw_t=p['router_w'].T, rb_col=p['router_b'][:, None],
        s5_wre=to_in(bb_re), s5_wim=to_in(bb_im),
        s5_cre=to_out(p['s5_C_re'][0]), s5_cim=to_out(p['s5_C_im'][0]),
        s5_abre=ab_re.reshape(1, -1), s5_abim=ab_im.reshape(1, -1), s5_d=p['s5_D'][0][None, :],
        glu_a=p['s5_w_glu_a'][0].astype(BF16), glu_b=p['s5_w_glu_b'][0].astype(BF16),
    )


def _ffn(h, w, layer, out_row, experts=None, moe_w=None):
    meta, counts, order, cast = _router(h, w['rw_t'], w['rb_col'], moe_w if experts is None else None, layer)
    experts = cast if experts is None else experts
    return _moe(h, meta, counts, order, *experts, w['ln_ffn'][layer], out_row), experts


def _s5_layer(h, nb, h0_re, h0_im, w, moe_w=None):
    return _s5_scan(h, nb, h0_re, h0_im, w['s5_wre'], w['s5_wim'], w['s5_cre'], w['s5_cim'],
                    w['s5_abre'], w['s5_abim'], w['s5_d'], w['glu_a'], w['glu_b'], w['ln_mix'][1], moe_w)


def kernel(x_prompt, x_sample, state_gdn_conv, state_gdn_S, state_mlstm_C, state_mlstm_n, state_mlstm_m,
           state_s5_re, state_s5_im, w_in, gdn_conv_w, gdn_A_log, gdn_dt_bias, gdn_norm_w, ml_b_i, ml_b_f,
           ml_norm_w, w_out, s5_A_re, s5_A_im, s5_log_dt, s5_B_re, s5_B_im, s5_C_re, s5_C_im, s5_D,
           s5_w_glu_a, s5_w_glu_b, router_w, router_b, moe_w_gate, moe_w_up, moe_w_down,
           ln_mix_g, ln_mix_b, ln_ffn_g, ln_ffn_b):
    w = _prepare(dict(
        w_in=w_in, gdn_conv_w=gdn_conv_w, gdn_A_log=gdn_A_log, gdn_dt_bias=gdn_dt_bias, gdn_norm_w=gdn_norm_w,
        ml_b_i=ml_b_i, ml_b_f=ml_b_f, ml_norm_w=ml_norm_w, w_out=w_out, s5_A_re=s5_A_re, s5_A_im=s5_A_im,
        s5_log_dt=s5_log_dt, s5_B_re=s5_B_re, s5_B_im=s5_B_im, s5_C_re=s5_C_re, s5_C_im=s5_C_im, s5_D=s5_D,
        s5_w_glu_a=s5_w_glu_a, s5_w_glu_b=s5_w_glu_b, router_w=router_w, router_b=router_b,
        ln_mix_g=ln_mix_g, ln_mix_b=ln_mix_b, ln_ffn_g=ln_ffn_g, ln_ffn_b=ln_ffn_b))
    bp, lp, _ = x_prompt.shape
    bs, ls, _ = x_sample.shape
    assert ls == 1
    n_state = S5_GROUPS * S5_STATE

    moe_w = (moe_w_gate, moe_w_up, moe_w_down)
    (hs, s_cbuf, s_s, s_c, s_n, s_m), experts0 = _ab_decode(
        x_sample[:, 0], w['win'], w['wout'], w['convw'], w['gp'], w['ln_mix'][0],
        state_gdn_conv.reshape(bs, (CONV_W - 1) * A_CONV), state_gdn_S[:, 0], state_mlstm_C[:, 0],
        state_mlstm_n.reshape(bs, N_HEADS * HEAD_DIM), state_mlstm_m[:, 0], moe_w)

    h, p_hist, p_s, p_c, p_n, p_m = _ab_prompt(x_prompt, w['win'], w['wout'], w['convw'], w['gp'], w['ln_mix'][0])
    h = _ffn(h.reshape(bp * lp, D_MODEL), w, 0, lambda r: (r % lp) * bp + r // lp, experts=experts0)[0]
    zeros = jnp.zeros((bp, n_state), F32)
    (h, p_re, p_im), experts1 = _s5_layer(h, bp, zeros, zeros, w, moe_w)
    y_prompt = _ffn(h, w, 1, lambda r: (r % bp) * lp + r // bp, experts=experts1)[0].reshape(bp, lp, D_MODEL)

    hs = _ffn(hs, w, 0, lambda r: r, experts=experts0)[0]
    (hs, s_re, s_im), _ = _s5_layer(hs, bs, state_s5_re.reshape(bs, n_state), state_s5_im.reshape(bs, n_state), w)
    y_sample = _ffn(hs, w, 1, lambda r: r, experts=experts1)[0].reshape(bs, 1, D_MODEL)

    grp = lambda a, n: a.reshape(n, 1, S5_GROUPS, S5_STATE)
    return (
        y_prompt, y_sample,
        p_hist[:, None, SUBLANES - (CONV_W - 1):, :], p_s[:, None], p_c[:, None],
        p_n[:, None, :, 0, :], p_m[:, None, :, 0, 0], grp(p_re, bp), grp(p_im, bp),
        s_cbuf.reshape(bs, 1, CONV_W - 1, A_CONV), s_s[:, None], s_c[:, None],
        s_n.reshape(bs, 1, N_HEADS, HEAD_DIM), s_m[:, None], grp(s_re, bs), grp(s_im, bs),
    )
```

```python
import functools

import jax
import jax.numpy as jnp
from jax import lax
from jax.experimental import pallas as pl
from jax.experimental.pallas import tpu as pltpu

F32 = jnp.float32
BF16 = jnp.bfloat16
HIGHEST = lax.Precision.HIGHEST

D_MODEL = 1024
DEPTH = 2
N_HEADS = 4
HEAD_DIM = 128
CONV_W = 4
CHUNK = 64
A_CONV = 3 * N_HEADS * HEAD_DIM
S5_GROUP = 16
S5_GROUPS = D_MODEL // S5_GROUP
S5_STATE = 64
N_EXPERTS = 16
EXPERTS_PER_GROUP = 4
N_EXPERT_GROUPS = N_EXPERTS // EXPERTS_PER_GROUP
D_FF = 512
DN_ALPHA = (2 * DEPTH) ** 0.25
LN_EPS = 1e-5
RMS_EPS = 1e-6
NEG_BIG = -1e30

QA, KA, VA, ZA, QB, KB, VB, OB, GT = 0, 512, 1024, 1536, 2048, 2560, 3072, 3584, 4096
D_IN_PAD = 4224
G_DEC, G_BETA, G_IN, G_FG = 0, 4, 8, 12

S5_TT = 64

LANES = 128
SUBLANES = 8
VMEM_LIMIT = 56 * 1024 * 1024


def _bf(x):
    return x.astype(BF16)


def _sigmoid(x):
    return 1.0 / (1.0 + jnp.exp(-x))


def _softplus(x):
    return jnp.maximum(x, 0.0) + jnp.log(1.0 + jnp.exp(-jnp.abs(x)))


def _silu(x):
    return x * _sigmoid(x)


def _layer_norm(y, g, b):
    mu = jnp.mean(y, axis=-1, keepdims=True)
    yc = y - mu
    var = jnp.mean(yc * yc, axis=-1, keepdims=True)
    return yc * lax.rsqrt(var + LN_EPS) * g + b


def _rms(x, w):
    return x * lax.rsqrt(jnp.mean(x * x, axis=-1, keepdims=True) + RMS_EPS) * w


def _gate_transform(raw, gp):
    lane = lax.broadcasted_iota(jnp.int32, raw.shape, 1)
    dec = -jnp.exp(gp[0:1, :]) * _softplus(raw + gp[1:2, :])
    beta = _sigmoid(raw)
    ipre = raw + gp[2:3, :]
    logf = -_softplus(-(raw + gp[3:4, :]))
    return jnp.where(lane < G_BETA, dec,
                     jnp.where(lane < G_IN, beta,
                               jnp.where(lane < G_FG, ipre,
                                         jnp.where(lane < G_FG + N_HEADS, logf, 0.0))))


def _bnn(a, b):
    return lax.dot_general(_bf(a), _bf(b), (((2,), (1,)), ((0,), (0,))), preferred_element_type=F32)


def _bnt(a, b):
    return lax.dot_general(_bf(a), _bf(b), (((2,), (2,)), ((0,), (0,))), preferred_element_type=F32)


def _btn(a, b):
    return lax.dot_general(_bf(a), _bf(b), (((1,), (1,)), ((0,), (0,))), preferred_element_type=F32)


def _unit_lower_inverse_minus_eye(a, pack):
    p, c, _ = a.shape
    nb = p // pack
    w = pack * c
    r = -jnp.stack([jnp.concatenate([a[b * pack + j] for j in range(pack)], axis=1) for b in range(nb)])
    assert c & (c - 1) == 0
    block_of = lambda dim: lax.shift_right_logical(lax.broadcasted_iota(jnp.int32, (w, w), dim), c.bit_length() - 1)
    on_diag = block_of(0) == block_of(1)

    def block_diag(x):
        return jnp.where(on_diag, jnp.concatenate([x] * pack, axis=1), 0.0)

    steps = max(1, (c - 1).bit_length()) - 1
    q = _bnn(r, block_diag(r))
    for i in range(steps):
        bd = block_diag(q)
        if i + 1 < steps:
            both = _bnn(jnp.concatenate([r, q], axis=1), bd)
            rq, qq = both[:, :c, :], both[:, c:, :]
        else:
            rq, qq = _bnn(r, bd), None
        r = r + q + rq
        q = qq
    return jnp.stack([r[b][:, j * c:(j + 1) * c] for b in range(nb) for j in range(pack)])


def _ab_prompt_kernel(x_ref, win_ref, wout_ref, convw_ref, gp_ref, ln_ref,
                      h_ref, hist_ref, s_out, c_out, n_out, m_out,
                      proj, qkv, gates, gcum, merged, s_s, c_s, n_s, m_s,
                      u_s, w_s, attn_s, dlog_s, qk_s, dmax_s, kt_s, kbt_s, rows_s, *, tb, n_t):
    t = pl.program_id(1)
    nc = tb // CHUNK

    @pl.when(t == 0)
    def _():
        proj[0:SUBLANES, :] = jnp.zeros((SUBLANES, D_IN_PAD), F32)
        s_s[...] = jnp.zeros_like(s_s)
        c_s[...] = jnp.zeros_like(c_s)
        n_s[...] = jnp.zeros_like(n_s)
        m_s[...] = jnp.zeros_like(m_s)

    x = x_ref[...]
    xb = _bf(x)
    proj[SUBLANES:SUBLANES + tb, 0:A_CONV] = jnp.dot(xb, win_ref[:, 0:A_CONV], preferred_element_type=F32)
    proj[SUBLANES:SUBLANES + tb, A_CONV:] = jnp.dot(xb, win_ref[:, A_CONV:], preferred_element_type=F32)

    for blk in range(A_CONV // LANES):
        cs = slice(blk * LANES, (blk + 1) * LANES)
        acc = proj[SUBLANES:SUBLANES + tb, cs] * convw_ref[CONV_W - 1:CONV_W, cs]
        for j in range(1, CONV_W):
            acc = acc + proj[SUBLANES - j:SUBLANES - j + tb, cs] * convw_ref[CONV_W - 1 - j:CONV_W - j, cs]
        y = _silu(acc)
        if blk < 2 * N_HEADS:
            y = y * lax.rsqrt(jnp.sum(y * y, axis=-1, keepdims=True) + RMS_EPS)
            if blk < N_HEADS:
                y = y * HEAD_DIM ** -0.5
        qkv[:, cs] = y

    gt = _gate_transform(proj[SUBLANES:SUBLANES + tb, GT:GT + LANES], gp_ref[...])
    gates[...] = gt
    ri = lax.broadcasted_iota(jnp.int32, (tb, tb), 0)
    ci = lax.broadcasted_iota(jnp.int32, (tb, tb), 1)
    log2_chunk = CHUNK.bit_length() - 1
    same_chunk = lax.shift_right_logical(ri, log2_chunk) == lax.shift_right_logical(ci, log2_chunk)
    ltri = jnp.where(same_chunk, jnp.where(ri >= ci, 1.0, 0.0), 0.0)
    gcum[...] = jnp.dot(ltri, gt, preferred_element_type=F32, precision=HIGHEST)

    ii = lax.broadcasted_iota(jnp.int32, (CHUNK, CHUNK), 0)
    jj = lax.broadcasted_iota(jnp.int32, (CHUNK, CHUNK), 1)
    incl = ii >= jj
    strict = ii > jj
    gdn_w = gp_ref[4:5, :]
    ml_w = gp_ref[5:6, :]

    pairs = [(c, h) for c in range(nc) for h in range(N_HEADS)]

    def tile_heads(ref, row0, col0):
        return jnp.stack([ref[row0 + c * CHUNK:row0 + (c + 1) * CHUNK, col0 + h * HEAD_DIM:col0 + (h + 1) * HEAD_DIM]
                          for c, h in pairs])

    def tile_cols(ref, lane0):
        return jnp.stack([ref[c * CHUNK:(c + 1) * CHUNK, lane0 + h:lane0 + h + 1] for c, h in pairs])

    def tile_rows(transposed, lane0):
        return jnp.stack([transposed[c][lane0 + h:lane0 + h + 1, :] for c, h in pairs])

    cs_t = [gcum[c * CHUNK:(c + 1) * CHUNK, :].T for c in range(nc)]
    gt_t = [gates[c * CHUNK:(c + 1) * CHUNK, :].T for c in range(nc)]
    q3 = tile_heads(qkv, 0, QA)
    k3 = tile_heads(qkv, 0, KA)
    v3 = tile_heads(qkv, 0, VA)
    g_col3 = tile_cols(gcum, G_DEC)
    beta3 = tile_cols(gates, G_BETA)
    decay3 = jnp.where(incl, jnp.exp(jnp.where(incl, g_col3 - tile_rows(cs_t, G_DEC), 0.0)), 0.0)
    kb3 = k3 * beta3
    a_low3 = jnp.where(strict, _bnt(kb3, k3) * decay3, 0.0)
    attn_s[...] = _bnt(q3, k3) * decay3
    kbm3 = tile_heads(proj, SUBLANES, KB) * HEAD_DIM ** -0.5
    qk_s[...] = _bnt(tile_heads(proj, SUBLANES, QB), kbm3)
    kt_s[...] = jnp.stack([k3[p].T for p in range(len(pairs))])
    kbt_s[...] = jnp.stack([kbm3[p].T for p in range(len(pairs))])
    zero_row = jnp.zeros((1, CHUNK), F32)
    rows_s[...] = jnp.stack([jnp.concatenate(
        [cs_t[c][G_DEC + h:G_DEC + h + 1, :], cs_t[c][G_FG + h:G_FG + h + 1, :], gt_t[c][G_IN + h:G_IN + h + 1, :]]
        + [zero_row] * (SUBLANES - 3), axis=0) for c, h in pairs])
    r3 = _unit_lower_inverse_minus_eye(a_low3, N_HEADS)
    rhs3 = jnp.concatenate([v3 * beta3, kb3 * jnp.exp(g_col3)], axis=2)
    uw3 = rhs3 + _bnn(r3, rhs3)
    u_s[...] = uw3[:, :, :HEAD_DIM]
    w_s[...] = uw3[:, :, HEAD_DIM:]
    dlog3 = jnp.where(incl, tile_cols(gcum, G_FG) - tile_rows(cs_t, G_FG) + tile_rows(gt_t, G_IN), NEG_BIG)
    dlog_s[...] = dlog3
    dmax_s[...] = jnp.max(dlog3, axis=-1, keepdims=True)

    def chunk_body(c, carry):
        r0 = pl.multiple_of(c * CHUNK, CHUNK)
        rows = pl.ds(r0, CHUNK)
        prow = pl.ds(pl.multiple_of(r0 + SUBLANES, SUBLANES), CHUNK)
        last = pl.ds(r0 + CHUNK - 1, 1)
        pc = pl.ds(pl.multiple_of(c * N_HEADS, N_HEADS), N_HEADS)

        def heads(ref, rws, col0):
            return jnp.stack([ref[rws, col0 + h * HEAD_DIM:col0 + (h + 1) * HEAD_DIM] for h in range(N_HEADS)])

        def cols(ref, rws, lane0):
            return jnp.stack([ref[rws, lane0 + h:lane0 + h + 1] for h in range(N_HEADS)])

        q = heads(qkv, rows, QA)
        g_col = cols(gcum, rows, G_DEC)
        g_last = cols(gcum, last, G_DEC)
        qb = heads(proj, prow, QB)
        kbm = heads(proj, prow, KB) * HEAD_DIM ** -0.5
        vbm = heads(proj, prow, VB)
        b_col = cols(gcum, rows, G_FG)
        b_last = cols(gcum, last, G_FG)
        row_forms = rows_s[pc]
        g_row = row_forms[:, 0:1, :]
        b_row = row_forms[:, 1:2, :]
        i_row = row_forms[:, 2:3, :]
        s_old = s_s[...]
        c_old = c_s[...]
        n_old = n_s[...]
        m_old = m_s[:, 0:1, 0:1]
        inter = b_col + m_old
        mt = jnp.maximum(inter, dmax_s[pc])
        wts = jnp.exp(dlog_s[pc] - mt) * qk_s[pc]
        sc = jnp.exp(inter - mt)
        m_new = mt[:, CHUNK - 1:CHUNK, :]
        sd = jnp.exp(b_last + m_old - m_new)
        wk_row = jnp.exp(b_last - b_row + i_row - m_new)
        w_state = _bnn(w_s[pc], s_old)
        q_state = _bnn(q * jnp.exp(g_col), s_old)
        q_mem = _bnn(qb, c_old)
        w_val = _bnn(wts, vbm)
        kv = _bnn(kbt_s[pc] * wk_row, vbm)
        k_sum = _bnn(jnp.broadcast_to(wk_row, (N_HEADS, SUBLANES, CHUNK)), kbm)
        v_new = u_s[pc] - w_state
        o_a = q_state + _bnn(attn_s[pc], v_new)
        s_s[...] = s_old * jnp.exp(g_last) + _bnn(kt_s[pc] * jnp.exp(g_last - g_row), v_new)
        num = sc * q_mem + w_val
        den = sc * jnp.sum(qb * n_old[:, 0:1, :], axis=-1, keepdims=True) + jnp.sum(wts, axis=-1, keepdims=True)
        h_b = num / jnp.maximum(jnp.abs(den), jnp.exp(-mt))
        c_s[...] = sd * c_old + kv
        n_s[...] = sd * n_old + k_sum
        m_s[...] = jnp.broadcast_to(m_new, m_s.shape)
        o_n = _rms(o_a, gdn_w)
        h_n = _rms(h_b, ml_w)
        for h in range(N_HEADS):
            hs = slice(h * HEAD_DIM, (h + 1) * HEAD_DIM)
            hs2 = slice((N_HEADS + h) * HEAD_DIM, (N_HEADS + h + 1) * HEAD_DIM)
            merged[rows, hs] = o_n[h] * _silu(proj[prow, ZA + h * HEAD_DIM:ZA + (h + 1) * HEAD_DIM])
            merged[rows, hs2] = h_n[h] * _sigmoid(proj[prow, OB + h * HEAD_DIM:OB + (h + 1) * HEAD_DIM])
        return carry

    lax.fori_loop(0, nc, chunk_body, 0)

    mix = jnp.dot(_bf(merged[...]), wout_ref[...], preferred_element_type=F32)
    h_ref[...] = _layer_norm(DN_ALPHA * x + mix, ln_ref[0:1, :], ln_ref[1:2, :])

    proj[0:SUBLANES, 0:A_CONV] = proj[tb:tb + SUBLANES, 0:A_CONV]

    @pl.when(t == n_t - 1)
    def _():
        hist_ref[...] = proj[tb:tb + SUBLANES, 0:A_CONV]
        s_out[...] = s_s[...]
        c_out[...] = c_s[...]
        n_out[...] = n_s[...]
        m_out[...] = m_s[...]


def _expert_weight_cast_specs(moe_w, layer, n_steps, step_of):
    ops, in_specs, out_specs, out_shapes, shapes = [], [], [], [], []
    for w in moe_w:
        _, n_e, rows, cols = w.shape
        per_layer = n_e * rows
        blk = per_layer // n_steps
        assert per_layer % n_steps == 0 and blk % (2 * SUBLANES) == 0
        ops.append(w.reshape(w.shape[0] * per_layer, cols))
        in_specs.append(pl.BlockSpec((blk, cols), lambda *g: (layer * n_steps + step_of(*g), 0)))
        out_specs.append(pl.BlockSpec((blk, cols), lambda *g: (step_of(*g), 0)))
        out_shapes.append(jax.ShapeDtypeStruct((per_layer, cols), BF16))
        shapes.append((n_e, rows, cols))
    return ops, in_specs, out_specs, out_shapes, shapes


def _ab_prompt(x, win, wout, convw, gp, ln):
    bsz, length, _ = x.shape
    tb = min(512, length)
    assert length % tb == 0 and tb % CHUNK == 0 and length >= SUBLANES
    n_t = length // tb
    n_pairs = (tb // CHUNK) * N_HEADS
    const = lambda shape: pl.BlockSpec(shape, lambda b, t: (0,) * len(shape))
    per_b = lambda shape: pl.BlockSpec((None,) + shape, lambda b, t: (b,) + (0,) * len(shape))
    return pl.pallas_call(
        functools.partial(_ab_prompt_kernel, tb=tb, n_t=n_t),
        grid=(bsz, n_t),
        in_specs=[
            pl.BlockSpec((None, tb, D_MODEL), lambda b, t: (b, t, 0)),
            const((D_MODEL, D_IN_PAD)), const((2 * N_HEADS * HEAD_DIM, D_MODEL)),
            const((CONV_W, A_CONV)), const((SUBLANES, LANES)), const((2, D_MODEL)),
        ],
        out_specs=[
            pl.BlockSpec((None, tb, D_MODEL), lambda b, t: (b, t, 0)),
            per_b((SUBLANES, A_CONV)), per_b((N_HEADS, HEAD_DIM, HEAD_DIM)), per_b((N_HEADS, HEAD_DIM, HEAD_DIM)),
            per_b((N_HEADS, SUBLANES, LANES)), per_b((N_HEADS, SUBLANES, LANES)),
        ],
        out_shape=[
            jax.ShapeDtypeStruct((bsz, length, D_MODEL), F32),
            jax.ShapeDtypeStruct((bsz, SUBLANES, A_CONV), F32),
            jax.ShapeDtypeStruct((bsz, N_HEADS, HEAD_DIM, HEAD_DIM), F32),
            jax.ShapeDtypeStruct((bsz, N_HEADS, HEAD_DIM, HEAD_DIM), F32),
            jax.ShapeDtypeStruct((bsz, N_HEADS, SUBLANES, LANES), F32),
            jax.ShapeDtypeStruct((bsz, N_HEADS, SUBLANES, LANES), F32),
        ],
        scratch_shapes=[
            pltpu.VMEM((tb + SUBLANES, D_IN_PAD), F32),
            pltpu.VMEM((tb, A_CONV), F32),
            pltpu.VMEM((tb, LANES), F32),
            pltpu.VMEM((tb, LANES), F32),
            pltpu.VMEM((tb, 2 * N_HEADS * HEAD_DIM), F32),
            pltpu.VMEM((N_HEADS, HEAD_DIM, HEAD_DIM), F32),
            pltpu.VMEM((N_HEADS, HEAD_DIM, HEAD_DIM), F32),
            pltpu.VMEM((N_HEADS, SUBLANES, LANES), F32),
            pltpu.VMEM((N_HEADS, SUBLANES, LANES), F32),
            pltpu.VMEM((n_pairs, CHUNK, HEAD_DIM), F32),
            pltpu.VMEM((n_pairs, CHUNK, HEAD_DIM), F32),
            pltpu.VMEM((n_pairs, CHUNK, CHUNK), F32),
            pltpu.VMEM((n_pairs, CHUNK, CHUNK), F32),
            pltpu.VMEM((n_pairs, CHUNK, CHUNK), F32),
            pltpu.VMEM((n_pairs, CHUNK, 1), F32),
            pltpu.VMEM((n_pairs, HEAD_DIM, CHUNK), F32),
            pltpu.VMEM((n_pairs, HEAD_DIM, CHUNK), F32),
            pltpu.VMEM((n_pairs, SUBLANES, CHUNK), F32),
        ],
        compiler_params=pltpu.CompilerParams(
            dimension_semantics=("arbitrary", "arbitrary"), vmem_limit_bytes=VMEM_LIMIT),
        name="ab_prompt",
    )(x, win, wout, convw, gp, ln)


def _ab_decode_kernel(*refs, bb, n_steps, n_cast):
    x_ref, win_ref, wout_ref, convw_ref, gp_ref, ln_ref, cbuf_ref, s_in, c_in, n_in, m_in = refs[:11]
    y_ref, cbuf_out, s_out, c_out, n_out, m_out = refs[11 + n_cast:17 + n_cast]
    proj, merged = refs[17 + 2 * n_cast:]
    for src, dst in zip(refs[11:11 + n_cast], refs[17 + n_cast:17 + 2 * n_cast]):
        dst[...] = _bf(src[...])
    i = pl.program_id(0)

    @pl.when(i == 0)
    def _():
        proj[...] = jnp.dot(_bf(x_ref[...]), win_ref[...], preferred_element_type=F32)

    rows = pl.ds(pl.multiple_of(i * bb, bb), bb)
    raw = proj[rows, 0:A_CONV]
    cbuf = cbuf_ref[...]
    conv = raw * convw_ref[CONV_W - 1:CONV_W, :]
    for j in range(CONV_W - 1):
        conv = conv + cbuf[:, j * A_CONV:(j + 1) * A_CONV] * convw_ref[j:j + 1, :]
    cbuf_out[:, 0:(CONV_W - 2) * A_CONV] = cbuf[:, A_CONV:(CONV_W - 1) * A_CONV]
    cbuf_out[:, (CONV_W - 2) * A_CONV:(CONV_W - 1) * A_CONV] = raw
    act = _silu(conv)
    gt = _gate_transform(proj[rows, GT:GT + LANES], gp_ref[...])
    gdn_w = gp_ref[4:5, :]
    ml_w = gp_ref[5:6, :]
    m_all = m_in[...]

    for h in range(N_HEADS):
        def head(off):
            return act[:, off + h * HEAD_DIM:off + (h + 1) * HEAD_DIM]
        q = head(QA)
        q = q * lax.rsqrt(jnp.sum(q * q, axis=-1, keepdims=True) + RMS_EPS) * HEAD_DIM ** -0.5
        k = head(KA)
        k = k * lax.rsqrt(jnp.sum(k * k, axis=-1, keepdims=True) + RMS_EPS)
        v = head(VA)
        k_t = k.T
        qk = jnp.sum(q * k, axis=-1, keepdims=True)
        qb = proj[rows, QB + h * HEAD_DIM:QB + (h + 1) * HEAD_DIM]
        kbm = proj[rows, KB + h * HEAD_DIM:KB + (h + 1) * HEAD_DIM] * HEAD_DIM ** -0.5
        vbm = proj[rows, VB + h * HEAD_DIM:VB + (h + 1) * HEAD_DIM]
        kb_t = kbm.T
        pad = [jnp.zeros((1, HEAD_DIM), F32)] * (SUBLANES - 2)
        read_s = _bnn(jnp.stack([jnp.concatenate([k[b:b + 1], q[b:b + 1]] + pad, axis=0) for b in range(bb)]),
                      s_in[:, h])
        read_c = _bnn(jnp.stack([jnp.concatenate([qb[b:b + 1], qb[b:b + 1]] + pad, axis=0) for b in range(bb)]),
                      c_in[:, h])
        qkb = jnp.sum(qb * kbm, axis=-1, keepdims=True)
        n_old = n_in[:, h * HEAD_DIM:(h + 1) * HEAD_DIM]
        qn = jnp.sum(qb * n_old, axis=-1, keepdims=True)
        o_rows = []
        hb_rows = []
        n_rows = []
        m_rows = []
        for b in range(bb):
            s_old = s_in[b, h]
            k_c = k_t[:, b:b + 1]
            e_g = jnp.exp(gt[b:b + 1, G_DEC + h:G_DEC + h + 1])
            beta = gt[b:b + 1, G_BETA + h:G_BETA + h + 1]
            k_s = read_s[b, 0:1, :]
            q_s = read_s[b, 1:2, :]
            v_new = v[b:b + 1, :] * beta - (beta * e_g) * k_s
            o_rows.append(e_g * q_s + qk[b:b + 1, :] * v_new)
            s_out[b, h] = s_old * e_g + k_c * v_new
            c_old = c_in[b, h]
            m_old = m_all[b:b + 1, h:h + 1]
            i_pre = gt[b:b + 1, G_IN + h:G_IN + h + 1]
            logf = gt[b:b + 1, G_FG + h:G_FG + h + 1]
            inter = logf + m_old
            mt = jnp.maximum(inter, i_pre)
            w_in = jnp.exp(i_pre - mt)
            sc = jnp.exp(inter - mt)
            wts = w_in * qkb[b:b + 1, :]
            q_cm = read_c[b, 0:1, :]
            num = sc * q_cm + wts * vbm[b:b + 1, :]
            den = sc * qn[b:b + 1, :] + wts
            hb_rows.append(num / jnp.maximum(jnp.abs(den), jnp.exp(-mt)))
            c_out[b, h] = sc * c_old + (w_in * kb_t[:, b:b + 1]) * vbm[b:b + 1, :]
            n_rows.append(sc * n_old[b:b + 1, :] + w_in * kbm[b:b + 1, :])
            m_rows.append(mt)
        o_a = jnp.concatenate(o_rows, axis=0)
        h_b = jnp.concatenate(hb_rows, axis=0)
        n_out[:, h * HEAD_DIM:(h + 1) * HEAD_DIM] = jnp.concatenate(n_rows, axis=0)
        m_out[:, h:h + 1] = jnp.concatenate(m_rows, axis=0)
        z = proj[rows, ZA + h * HEAD_DIM:ZA + (h + 1) * HEAD_DIM]
        merged[rows, h * HEAD_DIM:(h + 1) * HEAD_DIM] = _rms(o_a, gdn_w) * _silu(z)
        o_gate = proj[rows, OB + h * HEAD_DIM:OB + (h + 1) * HEAD_DIM]
        merged[rows, N_HEADS * HEAD_DIM + h * HEAD_DIM:N_HEADS * HEAD_DIM + (h + 1) * HEAD_DIM] = (
            _rms(h_b, ml_w) * _sigmoid(o_gate))

    @pl.when(i == n_steps - 1)
    def _():
        mix = jnp.dot(_bf(merged[...]), wout_ref[...], preferred_element_type=F32)
        y_ref[...] = _layer_norm(DN_ALPHA * x_ref[...] + mix, ln_ref[0:1, :], ln_ref[1:2, :])


def _ab_decode(x, win, wout, convw, gp, ln, cbuf, s0, c0, n0, m0, moe_w):
    nb = x.shape[0]
    bb = SUBLANES
    assert nb % bb == 0
    n_steps = nb // bb
    const = lambda shape: pl.BlockSpec(shape, lambda i: (0,) * len(shape))
    blk = lambda shape: pl.BlockSpec((bb,) + shape, lambda i: (i,) + (0,) * len(shape))
    hist = (CONV_W - 1) * A_CONV
    width = N_HEADS * HEAD_DIM
    w_ops, w_in, w_out, w_shapes, w_final = _expert_weight_cast_specs(moe_w, 0, n_steps, lambda i: i)
    outs = pl.pallas_call(
        functools.partial(_ab_decode_kernel, bb=bb, n_steps=n_steps, n_cast=len(w_ops)),
        grid=(n_steps,),
        in_specs=[
            const((nb, D_MODEL)), const((D_MODEL, D_IN_PAD)), const((2 * width, D_MODEL)),
            const((CONV_W, A_CONV)), const((SUBLANES, LANES)), const((2, D_MODEL)),
            blk((hist,)), blk((N_HEADS, HEAD_DIM, HEAD_DIM)), blk((N_HEADS, HEAD_DIM, HEAD_DIM)),
            blk((width,)), blk((N_HEADS,)),
        ] + w_in,
        out_specs=[
            const((nb, D_MODEL)), blk((hist,)), blk((N_HEADS, HEAD_DIM, HEAD_DIM)),
            blk((N_HEADS, HEAD_DIM, HEAD_DIM)), blk((width,)), blk((N_HEADS,)),
        ] + w_out,
        out_shape=[
            jax.ShapeDtypeStruct((nb, D_MODEL), F32),
            jax.ShapeDtypeStruct((nb, hist), F32),
            jax.ShapeDtypeStruct((nb, N_HEADS, HEAD_DIM, HEAD_DIM), F32),
            jax.ShapeDtypeStruct((nb, N_HEADS, HEAD_DIM, HEAD_DIM), F32),
            jax.ShapeDtypeStruct((nb, width), F32),
            jax.ShapeDtypeStruct((nb, N_HEADS), F32),
        ] + w_shapes,
        scratch_shapes=[pltpu.VMEM((nb, D_IN_PAD), F32), pltpu.VMEM((nb, 2 * width), F32)],
        compiler_params=pltpu.CompilerParams(dimension_semantics=("arbitrary",), vmem_limit_bytes=VMEM_LIMIT),
        name="ab_decode",
    )(x, win, wout, convw, gp, ln, cbuf, s0, c0, n0, m0, *w_ops)
    return outs[:6], [o.reshape(s) for o, s in zip(outs[6:], w_final)]


def _second_largest_sum(a, b, c, d):
    hi1, lo1 = jnp.maximum(a, b), jnp.minimum(a, b)
    hi2, lo2 = jnp.maximum(c, d), jnp.minimum(c, d)
    return jnp.maximum(hi1, hi2) + jnp.maximum(jnp.minimum(hi1, hi2), jnp.maximum(lo1, lo2))


def _first_argmax(vals):
    best_v = vals[0]
    best_i = jnp.zeros(vals[0].shape, jnp.int32)
    for j in range(1, len(vals)):
        better = vals[j] > best_v
        best_v = jnp.where(better, vals[j], best_v)
        best_i = jnp.where(better, j, best_i)
    return best_i


def _router_kernel(*refs, n_steps, n_cast):
    x_ref, rw_ref, rb_ref = refs[:3]
    meta_ref, cnt_ref, order_ref = refs[3 + n_cast:6 + n_cast]
    carry = refs[6 + 2 * n_cast]
    for src, dst in zip(refs[3:3 + n_cast], refs[6 + n_cast:6 + 2 * n_cast]):
        dst[...] = _bf(src[...])
    i = pl.program_id(0)
    tm = x_ref.shape[0]

    @pl.when(i == 0)
    def _():
        carry[...] = jnp.zeros_like(carry)

    x = x_ref[...]
    rw = rw_ref[...]
    x_hi = _bf(x)
    x_lo = _bf(x - x_hi.astype(F32))
    w_hi = _bf(rw)
    w_lo = _bf(rw - w_hi.astype(F32))
    nt = lambda a, b: lax.dot_general(a, b, (((1,), (1,)), ((), ())), preferred_element_type=F32)
    logits = nt(w_hi, x_hi) + (nt(w_hi, x_lo) + nt(w_lo, x_hi))
    ex = jnp.exp(logits - jnp.max(logits, axis=0, keepdims=True))
    probs = ex / jnp.sum(ex, axis=0, keepdims=True)
    sel = probs + rb_ref[...]
    p = [probs[j:j + 1, :] for j in range(N_EXPERTS)]
    s = [sel[j:j + 1, :] for j in range(N_EXPERTS)]
    scores = [_second_largest_sum(*s[EXPERTS_PER_GROUP * g:EXPERTS_PER_GROUP * (g + 1)])
              for g in range(N_EXPERT_GROUPS)]
    best = _first_argmax(scores)
    masked = [jnp.where(best == j // EXPERTS_PER_GROUP, s[j], -jnp.inf) for j in range(N_EXPERTS)]
    i1 = _first_argmax(masked)
    i2 = _first_argmax([jnp.where(i1 == j, -jnp.inf, masked[j]) for j in range(N_EXPERTS)])
    zero = jnp.zeros_like(p[0])
    p1 = functools.reduce(lambda a, b: a + b, [jnp.where(i1 == j, p[j], zero) for j in range(N_EXPERTS)])
    p2 = functools.reduce(lambda a, b: a + b, [jnp.where(i2 == j, p[j], zero) for j in range(N_EXPERTS)])
    tot = p1 + p2
    rows = [jnp.where(i1 == j, p1 / tot, zero) + jnp.where(i2 == j, p2 / tot, zero) for j in range(N_EXPERTS)]
    in_group = [best == g for g in range(N_EXPERT_GROUPS)]
    local = [functools.reduce(lambda a, b: a + b,
                              [jnp.where(in_group[g], rows[EXPERTS_PER_GROUP * g + e], zero)
                               for g in range(N_EXPERT_GROUPS)])
             for e in range(EXPERTS_PER_GROUP)]
    onehot = jnp.concatenate([jnp.where(m, 1.0, 0.0) for m in in_group]
                             + [jnp.zeros((SUBLANES - N_EXPERT_GROUPS, tm), F32)], axis=0)
    ri = lax.broadcasted_iota(jnp.int32, (tm, tm), 0)
    ci = lax.broadcasted_iota(jnp.int32, (tm, tm), 1)
    incl = jnp.dot(_bf(onehot), _bf(jnp.where(ri <= ci, 1.0, 0.0)), preferred_element_type=F32)
    prev = carry[...]
    rank = jnp.sum(onehot * (incl - 1.0 + prev[:, 0:1]), axis=0, keepdims=True)
    carry[...] = prev + incl[:, tm - 1:tm]
    row_id = (i * tm + lax.broadcasted_iota(jnp.int32, (1, tm), 1)).astype(F32)
    meta_ref[...] = jnp.concatenate([row_id] + local + [best.astype(F32), rank, zero], axis=0).T
    order_ref[...] = jnp.concatenate([best.astype(F32), rank] + [zero] * (SUBLANES - 2), axis=0)

    @pl.when(i == n_steps - 1)
    def _():
        cnt_ref[...] = carry[...]


def _router(x, rw_t, rb_col, moe_w=None, layer=0):
    t = x.shape[0]
    tm = min(1024, t)
    assert t % tm == 0
    n_steps = t // tm
    if moe_w is None:
        w_ops, w_in, w_out, w_shapes, w_final = [], [], [], [], []
    else:
        w_ops, w_in, w_out, w_shapes, w_final = _expert_weight_cast_specs(moe_w, layer, n_steps, lambda i: i)
    outs = pl.pallas_call(
        functools.partial(_router_kernel, n_steps=n_steps, n_cast=len(w_ops)),
        grid=(n_steps,),
        in_specs=[pl.BlockSpec((tm, D_MODEL), lambda i: (i, 0)),
                  pl.BlockSpec((N_EXPERTS, D_MODEL), lambda i: (0, 0)),
                  pl.BlockSpec((N_EXPERTS, 1), lambda i: (0, 0))] + w_in,
        out_specs=[pl.BlockSpec((tm, SUBLANES), lambda i: (i, 0)),
                   pl.BlockSpec((SUBLANES, LANES), lambda i: (0, 0)),
                   pl.BlockSpec((SUBLANES, tm), lambda i: (0, i))] + w_out,
        out_shape=[jax.ShapeDtypeStruct((t, SUBLANES), F32), jax.ShapeDtypeStruct((SUBLANES, LANES), F32),
                   jax.ShapeDtypeStruct((SUBLANES, t), F32)] + w_shapes,
        scratch_shapes=[pltpu.VMEM((SUBLANES, LANES), F32)],
        compiler_params=pltpu.CompilerParams(dimension_semantics=("arbitrary",), vmem_limit_bytes=VMEM_LIMIT),
        name="router",
    )(x, rw_t, rb_col, *w_ops)
    return outs[0], outs[1], outs[2], [o.reshape(s) for o, s in zip(outs[3:], w_final)]


def _moe_kernel(tg_ref, nv_ref, src_cur, dst_cur, src_nxt, cw_ref, x_hbm, wg_ref, wu_ref, wd_ref, ln_ref,
                y_hbm, xg, stage, gsem, ssem, *, tm, n_tiles):
    i = pl.program_id(0)
    slot = lax.rem(i, 2)
    other = 1 - slot
    n_valid = nv_ref[i]
    n_prev = jnp.where(i >= 1, nv_ref[jnp.maximum(i - 1, 0)], 0)
    n_back2 = jnp.where(i >= 2, nv_ref[jnp.maximum(i - 2, 0)], 0)
    has_next = jnp.logical_and(i + 1 < n_tiles, nv_ref[jnp.minimum(i + 1, n_tiles - 1)] > 0)

    def gather_row(tok_ref, r, dst_slot):
        return pltpu.make_async_copy(x_hbm.at[pl.ds(tok_ref[0, r], 1), :], xg.at[dst_slot, pl.ds(r, 1), :],
                                     gsem.at[dst_slot])

    def scatter_row(tok_ref, r, src_slot):
        return pltpu.make_async_copy(stage.at[src_slot, pl.ds(r, 1), :], y_hbm.at[pl.ds(tok_ref[0, r], 1), :],
                                     ssem.at[src_slot])

    def scatter_wait(count, src_slot):
        for p in [1 << b for b in range(tm.bit_length())]:
            @pl.when(lax.bitwise_and(count, p) != 0)
            def _():
                pltpu.make_async_copy(stage.at[src_slot, pl.ds(0, p), :], y_hbm.at[pl.ds(0, p), :],
                                      ssem.at[src_slot]).wait()

    @pl.when(jnp.logical_and(i == 0, n_valid > 0))
    def _():
        def body(r, c):
            gather_row(src_cur, r, 0).start()
            return c
        lax.fori_loop(0, tm, body, 0, unroll=8)

    @pl.when(n_valid > 0)
    def _():
        pltpu.make_async_copy(x_hbm.at[pl.ds(0, tm), :], xg.at[slot], gsem.at[slot]).wait()

    for parity in range(2):
        @pl.when(jnp.logical_and(has_next, slot == parity))
        def _():
            for r in range(tm):
                gather_row(src_nxt, r, 1 - parity).start(priority=r % 2)

    @pl.when(i >= 2)
    def _():
        scatter_wait(n_back2, slot)

    def experts_on_rows(n_rows):
        rs = slice(0, n_rows)
        x = xg[slot, rs, :]
        xb = _bf(x)
        cw = cw_ref[rs, :]
        acc = jnp.zeros((n_rows, D_MODEL), F32)
        for e in range(EXPERTS_PER_GROUP):
            g = jnp.dot(xb, wg_ref[e], preferred_element_type=F32)
            u = jnp.dot(xb, wu_ref[e], preferred_element_type=F32)
            hid = _silu(g) * u * cw[:, 1 + e:2 + e]
            acc = acc + jnp.dot(_bf(hid), wd_ref[e], preferred_element_type=F32)
        stage[slot, rs, :] = _layer_norm(DN_ALPHA * x + acc, ln_ref[0:1, :], ln_ref[1:2, :])

    half = tm // 2 if tm >= 512 else 0

    @pl.when(n_valid > half)
    def _():
        experts_on_rows(tm)

    if half:
        @pl.when(jnp.logical_and(n_valid > 0, n_valid <= half))
        def _():
            experts_on_rows(half)

    for parity in range(2):
        @pl.when(jnp.logical_and(n_valid == tm, slot == parity))
        def _():
            for r in range(tm):
                scatter_row(dst_cur, r, parity).start(priority=r % 2)

    @pl.when(n_valid < tm)
    def _():
        for r in range(tm):
            @pl.when(r < n_valid)
            def _():
                scatter_row(dst_cur, r, slot).start(priority=r % 2)

    @pl.when(i == n_tiles - 1)
    def _():
        scatter_wait(n_prev, other)
        scatter_wait(n_valid, slot)


def _moe(x, meta, counts, order, wg, wu, wd, ln, out_row):
    t = x.shape[0]
    tm = min(512, t)
    assert t % tm == 0 and tm & (tm - 1) == 0
    n_tiles = t // tm + N_EXPERT_GROUPS
    n_slots = n_tiles * tm
    grp = order[0].astype(jnp.int32)
    rank = order[1].astype(jnp.int32)
    cnt = counts[:N_EXPERT_GROUPS, 0].astype(jnp.int32)
    padded = ((cnt + tm - 1) // tm) * tm
    g_end = jnp.cumsum(padded)
    g_off = g_end - padded
    pos = jnp.take(g_off, grp) + rank
    slot_meta = jnp.zeros((n_slots, SUBLANES), F32).at[pos].set(meta, unique_indices=True, mode='promise_in_bounds')
    tok_of_slot = slot_meta[:, 0].astype(jnp.int32)
    tile_start = jnp.arange(n_tiles, dtype=jnp.int32) * tm
    tile_grp = jnp.minimum(jnp.sum(tile_start[:, None] >= g_end[None, :], axis=1), N_EXPERT_GROUPS - 1).astype(jnp.int32)
    tile_valid = jnp.clip(jnp.take(g_off + cnt, tile_grp) - tile_start, 0, tm).astype(jnp.int32)
    src3 = tok_of_slot.reshape(n_tiles, 1, tm)
    dst3 = out_row(tok_of_slot).reshape(n_tiles, 1, tm)

    grid_spec = pltpu.PrefetchScalarGridSpec(
        num_scalar_prefetch=2,
        grid=(n_tiles,),
        in_specs=[
            pl.BlockSpec((None, 1, tm), lambda i, tg, nv: (i, 0, 0), memory_space=pltpu.SMEM),
            pl.BlockSpec((None, 1, tm), lambda i, tg, nv: (i, 0, 0), memory_space=pltpu.SMEM),
            pl.BlockSpec((None, 1, tm), lambda i, tg, nv: (jnp.minimum(i + 1, n_tiles - 1), 0, 0),
                         memory_space=pltpu.SMEM),
            pl.BlockSpec((tm, SUBLANES), lambda i, tg, nv: (i, 0)),
            pl.BlockSpec(memory_space=pl.ANY),
            pl.BlockSpec((EXPERTS_PER_GROUP, D_MODEL, D_FF), lambda i, tg, nv: (tg[i], 0, 0)),
            pl.BlockSpec((EXPERTS_PER_GROUP, D_MODEL, D_FF), lambda i, tg, nv: (tg[i], 0, 0)),
            pl.BlockSpec((EXPERTS_PER_GROUP, D_FF, D_MODEL), lambda i, tg, nv: (tg[i], 0, 0)),
            pl.BlockSpec((2, D_MODEL), lambda i, tg, nv: (0, 0)),
        ],
        out_specs=pl.BlockSpec(memory_space=pl.ANY),
        scratch_shapes=[pltpu.VMEM((2, tm, D_MODEL), F32), pltpu.VMEM((2, tm, D_MODEL), F32),
                        pltpu.SemaphoreType.DMA((2,)), pltpu.SemaphoreType.DMA((2,))],
    )
    return pl.pallas_call(
        functools.partial(_moe_kernel, tm=tm, n_tiles=n_tiles),
        grid_spec=grid_spec,
        out_shape=jax.ShapeDtypeStruct((t, D_MODEL), F32),
        compiler_params=pltpu.CompilerParams(dimension_semantics=("arbitrary",), vmem_limit_bytes=VMEM_LIMIT),
        name="moe",
    )(tile_grp, tile_valid, src3, dst3, src3, slot_meta, x, wg, wu, wd, ln)


def _s5_prep_kernel(are_ref, aim_ref, ldt_ref, bre_ref, bim_ref, abre_ref, abim_ref, bbre_ref, bbim_ref):
    a_r = are_ref[...]
    a_i = aim_ref[...]
    dt = jnp.exp(ldt_ref[...])
    mag = jnp.exp(dt * a_r)
    ab_re = mag * jnp.cos(dt * a_i)
    ab_im = mag * jnp.sin(dt * a_i)
    den = a_r * a_r + a_i * a_i
    nr = ab_re - 1.0
    z_re = (nr * a_r + ab_im * a_i) / den
    z_im = (ab_im * a_r - nr * a_i) / den
    abre_ref[...] = ab_re
    abim_ref[...] = ab_im
    bbre_ref[...] = z_re * bre_ref[...] - z_im * bim_ref[...]
    bbim_ref[...] = z_re * bim_ref[...] + z_im * bre_ref[...]


def _s5_prep(a_re, a_im, log_dt, b_re, b_im):
    n = S5_GROUPS * S5_STATE
    col = jax.ShapeDtypeStruct((n, 1), F32)
    mat = jax.ShapeDtypeStruct((n, S5_GROUP), F32)
    ldt = jnp.broadcast_to(log_dt[:, None], (S5_GROUPS, S5_STATE)).reshape(n, 1)
    return pl.pallas_call(_s5_prep_kernel, out_shape=[col, col, mat, mat], name="s5_prep")(
        a_re.reshape(n, 1), a_im.reshape(n, 1), ldt, b_re.reshape(n, S5_GROUP), b_im.reshape(n, S5_GROUP))


def _gelu_tanh(x):
    return 0.5 * x * (1.0 + jnp.tanh(0.7978845608028654 * (x + 0.044715 * (x * x * x))))


def _s5_kernel(*refs, nb, tt, n_t, n_cast):
    (x_ref, h0re_ref, h0im_ref, wre_ref, wim_ref, cre_ref, cim_ref, abre_ref, abim_ref, d_ref,
     wa_ref, wb_ref, ln_ref) = refs[:13]
    cast_in = refs[13:13 + n_cast]
    o_ref, hre_out, him_out = refs[13 + n_cast:16 + n_cast]
    cast_out = refs[16 + n_cast:16 + 2 * n_cast]
    bu_re, bu_im, hre_s, him_s, gy_ref, glu = refs[16 + 2 * n_cast:]
    t = pl.program_id(0)
    sw = wre_ref.shape[2]
    for src, dst in zip(cast_in, cast_out):
        dst[...] = _bf(src[...])

    @pl.when(t == 0)
    def _():
        hre_s[...] = h0re_ref[...]
        him_s[...] = h0im_ref[...]

    for k in range(D_MODEL // LANES):
        ls = slice(k * LANES, (k + 1) * LANES)
        ss = slice(k * sw, (k + 1) * sw)
        buf = k % 2
        xv = x_ref[:, ls]
        xb = _bf(xv)
        bu_re[buf] = jnp.dot(xb, wre_ref[k], preferred_element_type=F32)
        bu_im[buf] = jnp.dot(xb, wim_ref[k], preferred_element_type=F32)
        a_re = jnp.broadcast_to(abre_ref[:, ss], (nb, sw))
        a_im = jnp.broadcast_to(abim_ref[:, ss], (nb, sw))

        def step(s, carry, buf=buf, a_re=a_re, a_im=a_im):
            h_re, h_im = carry
            rows = pl.ds(pl.multiple_of(s * nb, nb), nb)
            n_re = a_re * h_re - a_im * h_im + bu_re[buf, rows, :]
            n_im = a_re * h_im + a_im * h_re + bu_im[buf, rows, :]
            bu_re[buf, rows, :] = n_re
            bu_im[buf, rows, :] = n_im
            return n_re, n_im

        h_re, h_im = lax.fori_loop(0, tt, step, (hre_s[:, ss], him_s[:, ss]), unroll=True)
        hre_s[:, ss] = h_re
        him_s[:, ss] = h_im
        y = (jnp.dot(_bf(bu_re[buf]), cre_ref[k], preferred_element_type=F32)
             - jnp.dot(_bf(bu_im[buf]), cim_ref[k], preferred_element_type=F32)
             + d_ref[:, ls] * xv)
        gy_ref[:, ls] = _gelu_tanh(y).astype(gy_ref.dtype)

    gy = gy_ref[...]
    width = 2 * LANES
    for n in range(D_MODEL // width):
        ns = slice(n * width, (n + 1) * width)
        a = jnp.dot(gy, wa_ref[:, ns], preferred_element_type=F32)
        b = jnp.dot(gy, wb_ref[:, ns], preferred_element_type=F32)
        glu[:, ns] = a * _sigmoid(b)
    o_ref[...] = _layer_norm(DN_ALPHA * x_ref[...] + glu[...], ln_ref[0:1, :], ln_ref[1:2, :])

    @pl.when(t == n_t - 1)
    def _():
        hre_out[...] = hre_s[...]
        him_out[...] = him_s[...]


def _s5_scan(x, nb, h0_re, h0_im, wre, wim, cre, cim, ab_re, ab_im, d_skip, wa, wb, ln, moe_w=None):
    length = x.shape[0] // nb
    tt = min(S5_TT, length)
    assert length % tt == 0 and nb % SUBLANES == 0
    n_t = length // tt
    n_k = D_MODEL // LANES
    sw = (LANES // S5_GROUP) * S5_STATE
    n_state = S5_GROUPS * S5_STATE
    const = lambda shape: pl.BlockSpec(shape, lambda t: (0,) * len(shape))
    if moe_w is None:
        w_ops, w_in, w_out, w_shapes, w_final = [], [], [], [], []
    else:
        w_ops, w_in, w_out, w_shapes, w_final = _expert_weight_cast_specs(moe_w, 1, n_t, lambda t: t)
    outs = pl.pallas_call(
        functools.partial(_s5_kernel, nb=nb, tt=tt, n_t=n_t, n_cast=len(w_ops)),
        grid=(n_t,),
        in_specs=[
            pl.BlockSpec((tt * nb, D_MODEL), lambda t: (t, 0)),
            const((nb, n_state)), const((nb, n_state)),
            const((n_k, LANES, sw)), const((n_k, LANES, sw)), const((n_k, sw, LANES)), const((n_k, sw, LANES)),
            const((1, n_state)), const((1, n_state)), const((1, D_MODEL)),
            const((D_MODEL, D_MODEL)), const((D_MODEL, D_MODEL)), const((2, D_MODEL)),
        ] + w_in,
        out_specs=[pl.BlockSpec((tt * nb, D_MODEL), lambda t: (t, 0)), const((nb, n_state)),
                   const((nb, n_state))] + w_out,
        out_shape=[
            jax.ShapeDtypeStruct((length * nb, D_MODEL), F32),
            jax.ShapeDtypeStruct((nb, n_state), F32),
            jax.ShapeDtypeStruct((nb, n_state), F32),
        ] + w_shapes,
        scratch_shapes=[
            pltpu.VMEM((2, nb * tt, sw), F32), pltpu.VMEM((2, nb * tt, sw), F32),
            pltpu.VMEM((nb, n_state), F32), pltpu.VMEM((nb, n_state), F32),
            pltpu.VMEM((nb * tt, D_MODEL), BF16), pltpu.VMEM((nb * tt, D_MODEL), F32),
        ],
        compiler_params=pltpu.CompilerParams(dimension_semantics=("arbitrary",), vmem_limit_bytes=VMEM_LIMIT),
        name="s5_glu",
    )(x, h0_re, h0_im, wre, wim, cre, cim, ab_re, ab_im, d_skip, wa, wb, ln, *w_ops)
    return outs[:3], [o.reshape(s) for o, s in zip(outs[3:], w_final)]


def _block_diag_slices(m, rows_per_group, cols_per_group):
    gps = LANES // S5_GROUP
    m = m.reshape(S5_GROUPS // gps, gps, rows_per_group, cols_per_group)
    eye = jnp.eye(gps, dtype=m.dtype)
    out = m[:, :, :, None, :] * eye[None, :, None, :, None]
    return out.reshape(S5_GROUPS // gps, gps * rows_per_group, gps * cols_per_group)


def _prepare(p):
    w = p['w_in'][0]
    win = jnp.concatenate(
        [w[:, 0:1536], w[:, 1544:2056], w[:, 2056:3592], w[:, 3600:4112], w[:, 1536:1544], w[:, 3592:3600],
         jnp.zeros((D_MODEL, D_IN_PAD - 4112), w.dtype)], axis=1).astype(BF16)
    at_lane = lambda v, lane0: jnp.pad(v, (lane0, LANES - lane0 - v.shape[0]))
    gp = jnp.stack([at_lane(p['gdn_A_log'][0], G_DEC), at_lane(p['gdn_dt_bias'][0], G_DEC),
                    at_lane(p['ml_b_i'][0], G_IN), at_lane(p['ml_b_f'][0], G_FG),
                    p['gdn_norm_w'][0], p['ml_norm_w'][0], jnp.zeros((LANES,), F32), jnp.zeros((LANES,), F32)])
    ab_re, ab_im, bb_re, bb_im = _s5_prep(p['s5_A_re'][0], p['s5_A_im'][0], p['s5_log_dt'][0],
                                          p['s5_B_re'][0], p['s5_B_im'][0])
    to_in = lambda bb: _block_diag_slices(
        bb.reshape(S5_GROUPS, S5_STATE, S5_GROUP).transpose(0, 2, 1), S5_GROUP, S5_STATE).astype(BF16)
    to_out = lambda c: _block_diag_slices(c.transpose(0, 2, 1), S5_STATE, S5_GROUP).astype(BF16)
    return dict(
        win=win, wout=p['w_out'][0].astype(BF16), convw=p['gdn_conv_w'][0], gp=gp,
        ln_mix=[jnp.stack([p['ln_mix_g'][l], p['ln_mix_b'][l]]) for l in range(DEPTH)],
        ln_ffn=[jnp.stack([p['ln_ffn_g'][l], p['ln_ffn_b'][l]]) for l in range(DEPTH)],
        rw_t=p['router_w'].T, rb_col=p['router_b'][:, None],
        s5_wre=to_in(bb_re), s5_wim=to_in(bb_im),
        s5_cre=to_out(p['s5_C_re'][0]), s5_cim=to_out(p['s5_C_im'][0]),
        s5_abre=ab_re.reshape(1, -1), s5_abim=ab_im.reshape(1, -1), s5_d=p['s5_D'][0][None, :],
        glu_a=p['s5_w_glu_a'][0].astype(BF16), glu_b=p['s5_w_glu_b'][0].astype(BF16),
    )


def _ffn(h, w, layer, out_row, experts=None, moe_w=None):
    meta, counts, order, cast = _router(h, w['rw_t'], w['rb_col'], moe_w if experts is None else None, layer)
    experts = cast if experts is None else experts
    return _moe(h, meta, counts, order, *experts, w['ln_ffn'][layer], out_row), experts


def _s5_layer(h, nb, h0_re, h0_im, w, moe_w=None):
    return _s5_scan(h, nb, h0_re, h0_im, w['s5_wre'], w['s5_wim'], w['s5_cre'], w['s5_cim'],
                    w['s5_abre'], w['s5_abim'], w['s5_d'], w['glu_a'], w['glu_b'], w['ln_mix'][1], moe_w)


def kernel(x_prompt, x_sample, state_gdn_conv, state_gdn_S, state_mlstm_C, state_mlstm_n, state_mlstm_m,
           state_s5_re, state_s5_im, w_in, gdn_conv_w, gdn_A_log, gdn_dt_bias, gdn_norm_w, ml_b_i, ml_b_f,
           ml_norm_w, w_out, s5_A_re, s5_A_im, s5_log_dt, s5_B_re, s5_B_im, s5_C_re, s5_C_im, s5_D,
           s5_w_glu_a, s5_w_glu_b, router_w, router_b, moe_w_gate, moe_w_up, moe_w_down,
           ln_mix_g, ln_mix_b, ln_ffn_g, ln_ffn_b):
    w = _prepare(dict(
        w_in=w_in, gdn_conv_w=gdn_conv_w, gdn_A_log=gdn_A_log, gdn_dt_bias=gdn_dt_bias, gdn_norm_w=gdn_norm_w,
        ml_b_i=ml_b_i, ml_b_f=ml_b_f, ml_norm_w=ml_norm_w, w_out=w_out, s5_A_re=s5_A_re, s5_A_im=s5_A_im,
        s5_log_dt=s5_log_dt, s5_B_re=s5_B_re, s5_B_im=s5_B_im, s5_C_re=s5_C_re, s5_C_im=s5_C_im, s5_D=s5_D,
        s5_w_glu_a=s5_w_glu_a, s5_w_glu_b=s5_w_glu_b, router_w=router_w, router_b=router_b,
        ln_mix_g=ln_mix_g, ln_mix_b=ln_mix_b, ln_ffn_g=ln_ffn_g, ln_ffn_b=ln_ffn_b))
    bp, lp, _ = x_prompt.shape
    bs, ls, _ = x_sample.shape
    assert ls == 1
    n_state = S5_GROUPS * S5_STATE

    moe_w = (moe_w_gate, moe_w_up, moe_w_down)
    (hs, s_cbuf, s_s, s_c, s_n, s_m), experts0 = _ab_decode(
        x_sample[:, 0], w['win'], w['wout'], w['convw'], w['gp'], w['ln_mix'][0],
        state_gdn_conv.reshape(bs, (CONV_W - 1) * A_CONV), state_gdn_S[:, 0], state_mlstm_C[:, 0],
        state_mlstm_n.reshape(bs, N_HEADS * HEAD_DIM), state_mlstm_m[:, 0], moe_w)

    h, p_hist, p_s, p_c, p_n, p_m = _ab_prompt(x_prompt, w['win'], w['wout'], w['convw'], w['gp'], w['ln_mix'][0])
    h = _ffn(h.reshape(bp * lp, D_MODEL), w, 0, lambda r: (r % lp) * bp + r // lp, experts=experts0)[0]
    zeros = jnp.zeros((bp, n_state), F32)
    (h, p_re, p_im), experts1 = _s5_layer(h, bp, zeros, zeros, w, moe_w)
    y_prompt = _ffn(h, w, 1, lambda r: (r % bp) * lp + r // bp, experts=experts1)[0].reshape(bp, lp, D_MODEL)

    hs = _ffn(hs, w, 0, lambda r: r, experts=experts0)[0]
    (hs, s_re, s_im), _ = _s5_layer(hs, bs, state_s5_re.reshape(bs, n_state), state_s5_im.reshape(bs, n_state), w)
    y_sample = _ffn(hs, w, 1, lambda r: r, experts=experts1)[0].reshape(bs, 1, D_MODEL)

    grp = lambda a, n: a.reshape(n, 1, S5_GROUPS, S5_STATE)
    return (
        y_prompt, y_sample,
        p_hist[:, None, SUBLANES - (CONV_W - 1):, :], p_s[:, None], p_c[:, None],
        p_n[:, None, :, 0, :], p_m[:, None, :, 0, 0], grp(p_re, bp), grp(p_im, bp),
        s_cbuf.reshape(bs, 1, CONV_W - 1, A_CONV), s_s[:, None], s_c[:, None],
        s_n.reshape(bs, 1, N_HEADS, HEAD_DIM), s_m[:, None], grp(s_re, bs), grp(s_im, bs),
    )
```
